```python
import math
import jax
import jax.numpy as jnp
from jax import lax
import numpy as np

D_MODEL = 2048
BATCH = 16
SEQ = 256
DEPTH = 4
DEC_BATCH = 2
DEC_SEQ = 4096
PAST_LEN = 256

GRID_W = 64
N_MOD = 9
D_FF = 5632
N_EVEN = (DEPTH + 1) // 2
N_ODD = DEPTH // 2
D_RWKV = D_MODEL // 2
RWKV_HEAD = 64
H_RWKV = D_RWKV // RWKV_HEAD
W_LORA = 64
A_LORA = 64
G_LORA = 160
RWKV_PROJ = 3 * D_RWKV + W_LORA + A_LORA + G_LORA
RWKV_SPLITS = [D_RWKV, 2 * D_RWKV, 3 * D_RWKV, 3 * D_RWKV + W_LORA, 3 * D_RWKV + W_LORA + A_LORA]
RWKV_GN_EPS = 64e-5
H_MLA = D_MODEL // 256
QK_NOPE = 128
QK_ROPE = 64
V_HEAD = 128
Q_RANK = D_MODEL // 4
KV_RANK = D_MODEL // 8
MLA_PROJ = Q_RANK + KV_RANK + QK_ROPE
IN_EVEN = RWKV_PROJ + MLA_PROJ
MIX_EVEN = D_RWKV + H_MLA * V_HEAD
ROPE_THETA = 10000.0
Q_BLOCK = 128
D_HYENA = D_MODEL
HYENA_ORDER = 2
POS_EMB = 33
FILT_HIDDEN = 64
HYENA_TARGET = 1e-2
FAST_DECAY_PCT = 0.3
SLOW_DECAY_PCT = 1.5

kernel_name = 'hybrid_rwkv7_mla_hyena_diffusion_step'


def rmsnorm(x, g, eps=1e-6):
    xf = x.astype(jnp.float32)
    y = xf * lax.rsqrt(jnp.mean(xf * xf, axis=-1, keepdims=True) + eps)
    return y.astype(x.dtype) * g


def modulate(h, shift, scale):
    return h * (1.0 + scale) + shift


def shift_prev(x):
    return jnp.pad(x[:, :-1], ((0, 0), (1, 0), (0, 0)))


def shift_next(x):
    return jnp.pad(x[:, 1:], ((0, 0), (0, 1), (0, 0)))


def swiglu(h, w_in, w_out):
    gate, up = jnp.split(h @ w_in, 2, axis=-1)
    return (jax.nn.silu(gate) * up) @ w_out


def grid_angles(L):
    rows = L // GRID_W
    row = jnp.repeat(jnp.arange(rows, dtype=jnp.float32), GRID_W)
    col = jnp.tile(jnp.arange(GRID_W, dtype=jnp.float32), rows)
    half = QK_ROPE // 2
    inv = 1.0 / (ROPE_THETA ** (jnp.arange(0, half, 2, dtype=jnp.float32) / half))
    return row[:, None] * inv[None, :], col[:, None] * inv[None, :]


def rope_half(x, ang):
    c = jnp.cos(ang)[None, :, None, :].astype(x.dtype)
    s = jnp.sin(ang)[None, :, None, :].astype(x.dtype)
    x1, x2 = jnp.split(x, 2, axis=-1)
    return jnp.concatenate([x1 * c - x2 * s, x2 * c + x1 * s], axis=-1)


def axial_rope(x):
    ang_r, ang_c = grid_angles(x.shape[1])
    xr, xc = jnp.split(x, 2, axis=-1)
    return jnp.concatenate([rope_half(xr, ang_r), rope_half(xc, ang_c)], axis=-1)


def attend(q, k, v, scale):
    B, Tq, H, dq = q.shape
    blk = min(Q_BLOCK, Tq)
    nb = Tq // blk
    qb = jnp.moveaxis(q.reshape(B, nb, blk, H, dq), 1, 0)

    def one_block(q_blk):
        s = jnp.einsum('bqhd,bkhd->bhqk', q_blk, k, preferred_element_type=jnp.float32) * scale
        p = jax.nn.softmax(s, axis=-1)
        return jnp.einsum('bhqk,bkhd->bqhd', p.astype(v.dtype), v)

    o = lax.map(one_block, qb)
    return jnp.moveaxis(o, 0, 1).reshape(B, Tq, H, v.shape[-1])


def rwkv_scan(S0, r, w, k, v, kk, a, reverse):
    seq = tuple(jnp.swapaxes(t, 0, 1) for t in (r, w, k, v, kk, a))

    def step(S, inp):
        r_t, w_t, k_t, v_t, kk_t, a_t = inp
        sa = jnp.einsum('bhvk,bhk->bhv', S, -kk_t)
        S = (S * w_t[:, :, None, :] + sa[..., None] * (kk_t * a_t)[:, :, None, :]
             + v_t[..., None] * k_t[:, :, None, :])
        return S, jnp.einsum('bhvk,bhk->bhv', S, r_t)

    S_fin, y = lax.scan(step, S0.astype(jnp.float32), seq, reverse=reverse)
    return jnp.swapaxes(y, 0, 1), S_fin


def rwkv_branch(z, p, S0):
    B, T, _ = z.shape
    f32 = jnp.float32
    z = z + p['mu_prev'] * (shift_prev(z) - z) + p['mu_next'] * (shift_next(z) - z)
    r, k, v, xw, xa, xg = jnp.split(z, RWKV_SPLITS, axis=-1)
    heads = lambda t: t.reshape(B, T, H_RWKV, RWKV_HEAD).astype(f32)
    kk = heads(k * p['k_k'])
    kk = kk / jnp.maximum(jnp.sqrt(jnp.sum(kk * kk, axis=-1, keepdims=True)), 1e-12)
    g = jax.nn.sigmoid(xg) @ p['g2']
    tw = jnp.tanh(xw)
    rh, vh = heads(r), heads(v)
    y_dirs, s_dirs, bonus = [], [], 0.0
    for d in range(2):
        wl = -jax.nn.softplus(-(p['w0'][d] + tw @ p['w2'][d]).astype(f32)) - 0.5
        decay = jnp.exp(-jnp.exp(wl))
        a = jax.nn.sigmoid(p['a0'][d] + xa @ p['a2'][d])
        kd = heads(k * (1.0 + (a - 1.0) * p['k_a']))
        y_d, s_d = rwkv_scan(S0[:, d], rh, heads(decay), kd, vh, kk, heads(a), reverse=(d == 1))
        y_dirs.append(y_d)
        s_dirs.append(s_d)
        bonus = bonus + jnp.sum(rh * kd * p['r_k'].astype(f32), axis=-1, keepdims=True) * vh
    y = y_dirs[0] + y_dirs[1]
    mu = jnp.mean(y, axis=-1, keepdims=True)
    var = jnp.mean(jnp.square(y - mu), axis=-1, keepdims=True)
    y = ((y - mu) * lax.rsqrt(var + RWKV_GN_EPS)).reshape(B, T, D_RWKV)
    y = y * p['gn_w'] + p['gn_b'] + bonus.reshape(B, T, D_RWKV)
    return y.astype(z.dtype) * g, jnp.stack(s_dirs, axis=1)


def mla_branch(z, p, ckv_ctx, kr_ctx):
    B, T, _ = z.shape
    cq, ckv, kr = jnp.split(z, [Q_RANK, Q_RANK + KV_RANK], axis=-1)
    cq = rmsnorm(cq, p['q_norm'])
    ckv = rmsnorm(ckv, p['kv_norm'])
    q = (cq @ p['w_qb']).reshape(B, T, H_MLA, QK_NOPE + QK_ROPE)
    q_nope, q_rope = jnp.split(q, [QK_NOPE], axis=-1)
    kr_h = kr[:, :, None, :]
    if ckv_ctx is None:
        keys_ckv, keys_kr = ckv, kr_h
    else:
        q_rope = axial_rope(q_rope)
        keys_ckv = jnp.concatenate([ckv, ckv_ctx], axis=1)
        keys_kr = jnp.concatenate([axial_rope(kr_h), kr_ctx[:, :, None, :]], axis=1)
    Tk = keys_ckv.shape[1]
    kv = (keys_ckv @ p['w_kvb']).reshape(B, Tk, H_MLA, QK_NOPE + V_HEAD)
    k_nope, v = jnp.split(kv, [QK_NOPE], axis=-1)
    k = jnp.concatenate([k_nope, jnp.broadcast_to(keys_kr, (B, Tk, H_MLA, QK_ROPE))], axis=-1)
    o = attend(jnp.concatenate([q_nope, q_rope], axis=-1), k, v, (QK_NOPE + QK_ROPE) ** -0.5)
    return o.reshape(B, T, H_MLA * V_HEAD), ckv, kr


def even_mixer(h, p, ctx):
    z = h @ p['w_in']
    z_rwkv, z_mla = jnp.split(z, [RWKV_PROJ], axis=-1)
    if ctx is None:
        S0 = jnp.zeros((h.shape[0], 2, H_RWKV, RWKV_HEAD, RWKV_HEAD), jnp.float32)
        y_r, S = rwkv_branch(z_rwkv, p, S0)
        y_m, ckv, kr = mla_branch(z_mla, p, None, None)
        state = (S, ckv, kr)
    else:
        ckv_ctx, kr_ctx, S0 = ctx
        y_r, _ = rwkv_branch(z_rwkv, p, S0)
        y_m, _, _ = mla_branch(z_mla, p, ckv_ctx, kr_ctx)
        state = None
    return jnp.concatenate([y_r, y_m], axis=-1) @ p['w_out'], state


def hyena_filters(L, p):
    f32 = jnp.float32
    t = jnp.arange(L, dtype=f32)
    t_norm = t / max(L - 1, 1)
    bands = (POS_EMB - 1) // 2
    freqs = jnp.linspace(1e-4, bands - 1, bands, dtype=f32)
    ang = (2.0 * math.pi / L) * t[:, None] * freqs[None, :]
    z = jnp.concatenate([t_norm[:, None], jnp.cos(ang), -jnp.sin(ang)], axis=-1).astype(p['filt_w1'].dtype)
    h = jnp.sin(p['filt_freq'][0] * (z @ p['filt_w1'] + p['filt_b1']))
    h = jnp.sin(p['filt_freq'][1] * (h @ p['filt_w2'] + p['filt_b2']))
    h = (h @ p['filt_w3']).astype(f32).reshape(L, HYENA_ORDER, 2, D_HYENA)
    deltas = jnp.linspace(abs(math.log(HYENA_TARGET)) / SLOW_DECAY_PCT,
                          abs(math.log(HYENA_TARGET)) / FAST_DECAY_PCT, D_HYENA, dtype=f32)
    h = h * jnp.exp(-t_norm[:, None] * deltas[None, :])[:, None, None, :]
    kern = jnp.concatenate([h[:, :, 0], jnp.zeros((1, HYENA_ORDER, D_HYENA), f32), h[1:, :, 1][::-1]], axis=0)
    kern = kern / jnp.sum(jnp.abs(kern), axis=0, keepdims=True)
    return jnp.fft.rfft(kern, axis=0)


def fftconv(u, K, bias):
    T = u.shape[1]
    uf = u.astype(jnp.float32)
    U = jnp.fft.rfft(uf, n=2 * T, axis=1)
    y = jnp.fft.irfft(U * K[None], n=2 * T, axis=1)[:, :T]
    return (y + uf * bias.astype(jnp.float32)).astype(u.dtype)


def hyena_mixer(h, p):
    z = h @ p['w_in']
    cw = p['conv_w']
    z = cw[0] * shift_prev(z) + cw[1] * z + cw[2] * shift_next(z) + p['conv_b']
    x1, x2, v = jnp.split(z, 3, axis=-1)
    K = hyena_filters(h.shape[1], p)
    y = v
    for n, gate in enumerate((x1, x2)):
        y = gate * fftconv(y, K[:, n], p['bias'][n])
    return y @ p['w_out'], None


def trunk_layer(x, mod, norm_g, ffn_in, ffn_out, mixer):
    sh1, sc1, g1, sh2, sc2, g2, sh3, sc3, g3 = jnp.split(mod, N_MOD, axis=-1)
    x = x + 0.5 * g1 * swiglu(modulate(rmsnorm(x, norm_g[0]), sh1, sc1), ffn_in[0], ffn_out[0])
    y, state = mixer(modulate(rmsnorm(x, norm_g[1]), sh2, sc2))
    x = x + g2 * y
    x = x + 0.5 * g3 * swiglu(modulate(rmsnorm(x, norm_g[2]), sh3, sc3), ffn_in[1], ffn_out[1])
    return x, state


def setup_inputs(seed: int = 0) -> dict:
    key = jax.random.key(seed)
    ks = iter(jax.random.split(key, 64))
    f32 = jnp.float32

    def nrm(shape, scale):
        return jax.random.normal(next(ks), shape, f32) * scale

    def unif(shape, lo, hi):
        return jax.random.uniform(next(ks), shape, f32, lo, hi)

    D = D_MODEL
    return {
        'x_prompt': nrm((BATCH, SEQ, D), 1.0),
        'x_sample': nrm((DEC_BATCH, DEC_SEQ, D), 1.0),
        'cache_mla_ckv': nrm((DEC_BATCH, N_EVEN, PAST_LEN, KV_RANK), 1.0),
        'cache_mla_krope': nrm((DEC_BATCH, N_EVEN, PAST_LEN, QK_ROPE), 1.0),
        'state_rwkv': nrm((DEC_BATCH, N_EVEN, 2, H_RWKV, RWKV_HEAD, RWKV_HEAD), 0.3),
        'c': nrm((DEC_BATCH, D), 1.0),
        'c_ctx': nrm((D,), 1.0),
        'w_mod': nrm((DEPTH, D, N_MOD * D), 0.5 * D ** -0.5),
        'b_mod': nrm((DEPTH, N_MOD * D), 0.02),
        'norm_g': 1.0 + nrm((DEPTH, 3, D), 0.05),
        'w_ffn_in': nrm((DEPTH, 2, D, 2 * D_FF), D ** -0.5),
        'w_ffn_out': nrm((DEPTH, 2, D_FF, D), D_FF ** -0.5),
        'final_norm_g': 1.0 + nrm((D,), 0.05),
        'w_in_even': nrm((N_EVEN, D, IN_EVEN), D ** -0.5),
        'mu_prev': unif((N_EVEN, RWKV_PROJ), 0.0, 0.5),
        'mu_next': unif((N_EVEN, RWKV_PROJ), 0.0, 0.5),
        'rwkv_w0': unif((N_EVEN, 2, D_RWKV), -6.0, -1.0),
        'rwkv_w2': nrm((N_EVEN, 2, W_LORA, D_RWKV), 0.5 * W_LORA ** -0.5),
        'rwkv_a0': nrm((N_EVEN, 2, D_RWKV), 0.5),
        'rwkv_a2': nrm((N_EVEN, 2, A_LORA, D_RWKV), 0.5 * A_LORA ** -0.5),
        'rwkv_g2': nrm((N_EVEN, G_LORA, D_RWKV), G_LORA ** -0.5),
        'rwkv_kk': 0.85 + nrm((N_EVEN, D_RWKV), 0.05),
        'rwkv_ka': 1.0 + nrm((N_EVEN, D_RWKV), 0.05),
        'rwkv_rk': nrm((N_EVEN, H_RWKV, RWKV_HEAD), 0.1),
        'rwkv_gn_w': 1.0 + nrm((N_EVEN, D_RWKV), 0.05),
        'rwkv_gn_b': nrm((N_EVEN, D_RWKV), 0.02),
        'mla_q_norm': 1.0 + nrm((N_EVEN, Q_RANK), 0.05),
        'mla_kv_norm': 1.0 + nrm((N_EVEN, KV_RANK), 0.05),
        'mla_w_qb': nrm((N_EVEN, Q_RANK, H_MLA * (QK_NOPE + QK_ROPE)), Q_RANK ** -0.5),
        'mla_w_kvb': nrm((N_EVEN, KV_RANK, H_MLA * (QK_NOPE + V_HEAD)), KV_RANK ** -0.5),
        'w_out_even': nrm((N_EVEN, MIX_EVEN, D), MIX_EVEN ** -0.5),
        'w_in_odd': nrm((N_ODD, D, 3 * D_HYENA), D ** -0.5),
        'hy_conv_w': nrm((N_ODD, 3, 3 * D_HYENA), 3.0 ** -0.5),
        'hy_conv_b': nrm((N_ODD, 3 * D_HYENA), 0.02),
        'hy_filt_w1': nrm((N_ODD, POS_EMB, FILT_HIDDEN), POS_EMB ** -0.5),
        'hy_filt_b1': nrm((N_ODD, FILT_HIDDEN), 0.1),
        'hy_filt_w2': nrm((N_ODD, FILT_HIDDEN, FILT_HIDDEN), FILT_HIDDEN ** -0.5),
        'hy_filt_b2': nrm((N_ODD, FILT_HIDDEN), 0.1),
        'hy_filt_w3': nrm((N_ODD, FILT_HIDDEN, HYENA_ORDER * 2 * D_HYENA), FILT_HIDDEN ** -0.5),
        'hy_filt_freq': 1.0 + nrm((N_ODD, 2, FILT_HIDDEN), 0.1),
        'hy_bias': nrm((N_ODD, HYENA_ORDER, D_HYENA), 0.5),
        'w_out_odd': nrm((N_ODD, D_HYENA, D), D_HYENA ** -0.5),
    }


def reference(x_prompt, x_sample, cache_mla_ckv, cache_mla_krope, state_rwkv, c, c_ctx,
              w_mod, b_mod, norm_g, w_ffn_in, w_ffn_out, final_norm_g,
              w_in_even, mu_prev, mu_next, rwkv_w0, rwkv_w2, rwkv_a0, rwkv_a2, rwkv_g2,
              rwkv_kk, rwkv_ka, rwkv_rk, rwkv_gn_w, rwkv_gn_b,
              mla_q_norm, mla_kv_norm, mla_w_qb, mla_w_kvb, w_out_even,
              w_in_odd, hy_conv_w, hy_conv_b, hy_filt_w1, hy_filt_b1, hy_filt_w2, hy_filt_b2,
              hy_filt_w3, hy_filt_freq, hy_bias, w_out_odd):
    xp, xs = x_prompt, x_sample
    new_ckv, new_kr, new_s = [], [], []
    for l in range(DEPTH):
        mod_p = (jax.nn.silu(c_ctx) @ w_mod[l] + b_mod[l])[None, None, :]
        mod_s = (jax.nn.silu(c) @ w_mod[l] + b_mod[l])[:, None, :]
        ffn = (norm_g[l], w_ffn_in[l], w_ffn_out[l])
        if l % 2 == 0:
            e = l // 2
            p = {'w_in': w_in_even[e], 'mu_prev': mu_prev[e], 'mu_next': mu_next[e],
                 'w0': rwkv_w0[e], 'w2': rwkv_w2[e], 'a0': rwkv_a0[e], 'a2': rwkv_a2[e],
                 'g2': rwkv_g2[e], 'k_k': rwkv_kk[e], 'k_a': rwkv_ka[e], 'r_k': rwkv_rk[e],
                 'gn_w': rwkv_gn_w[e], 'gn_b': rwkv_gn_b[e],
                 'q_norm': mla_q_norm[e], 'kv_norm': mla_kv_norm[e],
                 'w_qb': mla_w_qb[e], 'w_kvb': mla_w_kvb[e], 'w_out': w_out_even[e]}
            ctx = (cache_mla_ckv[:, e], cache_mla_krope[:, e], state_rwkv[:, e])
            xp, st = trunk_layer(xp, mod_p, *ffn, lambda h: even_mixer(h, p, None))
            xs, _ = trunk_layer(xs, mod_s, *ffn, lambda h: even_mixer(h, p, ctx))
            s_rwkv, ckv, kr = st
            new_s.append(s_rwkv.astype(x_prompt.dtype))
            new_ckv.append(ckv)
            new_kr.append(kr)
        else:
            o = l // 2
            p = {'w_in': w_in_odd[o], 'conv_w': hy_conv_w[o], 'conv_b': hy_conv_b[o],
                 'filt_w1': hy_filt_w1[o], 'filt_b1': hy_filt_b1[o],
                 'filt_w2': hy_filt_w2[o], 'filt_b2': hy_filt_b2[o],
                 'filt_w3': hy_filt_w3[o], 'filt_freq': hy_filt_freq[o],
                 'bias': hy_bias[o], 'w_out': w_out_odd[o]}
            xp, _ = trunk_layer(xp, mod_p, *ffn, lambda h: hyena_mixer(h, p))
            xs, _ = trunk_layer(xs, mod_s, *ffn, lambda h: hyena_mixer(h, p))
    y_prompt = rmsnorm(xp, final_norm_g)
    y_sample = rmsnorm(xs, final_norm_g)
    new_mla_ckv = jnp.stack(new_ckv, axis=1)
    new_mla_krope = jnp.stack(new_kr, axis=1)
    new_rwkv_state = jnp.stack(new_s, axis=1)
    return (y_prompt, y_sample, new_mla_ckv, new_mla_krope, new_rwkv_state)
```

```python
import functools
import math

import jax
import jax.numpy as jnp
from jax import lax
from jax.experimental import pallas as pl
from jax.experimental.pallas import tpu as pltpu

F32 = jnp.float32
BF16 = jnp.bfloat16

D_MODEL = 2048
N_MOD = 9
D_FF = 5632
D_RWKV = 1024
RWKV_HEAD = 64
H_RWKV = 16
W_LORA = 64
A_LORA = 64
G_LORA = 160
RWKV_SMALL = 384
RWKV_COLS = 3 * D_RWKV + RWKV_SMALL
RWKV_GN_EPS = 64e-5
H_MLA = 8
QK_NOPE = 128
QK_ROPE = 64
V_HEAD = 128
Q_RANK = 512
KV_RANK = 256
MLA_COLS = Q_RANK + KV_RANK + 2 * QK_ROPE
HEAD_SLOT = 256
ROPE_THETA = 10000.0
GRID_W = 64
D_HYENA = 2048
POS_EMB = 33
POS_PAD = 128
FILT_HIDDEN = 64
HYENA_TARGET = 1e-2
FAST_DECAY_PCT = 0.3
SLOW_DECAY_PCT = 1.5
LANE = 128
SUBLANE = 8
MXU_DIM = 256
MIB = 1024 * 1024


def _params(sem, vmem_mib):
    return pltpu.CompilerParams(dimension_semantics=sem, vmem_limit_bytes=vmem_mib * MIB)


def _sigmoid(x):
    return 1.0 / (1.0 + jnp.exp(-x))


def _softplus(x):
    return jnp.maximum(x, 0.0) + jnp.log(1.0 + jnp.exp(-jnp.abs(x)))


def _dot(a, b):
    return jnp.dot(a, b, preferred_element_type=F32)


def _norm_mod(x, gamma, shift, scale):
    xn = x * lax.rsqrt(jnp.mean(x * x, axis=-1, keepdims=True) + 1e-6)
    return (xn * gamma) * (1.0 + scale) + shift


def _mod_kernel(c_ref, w_ref, b_ref, o_ref):
    c = c_ref[...]
    s = c * _sigmoid(c)
    o_ref[0] = _dot(s.astype(BF16), w_ref[0].astype(BF16)) + b_ref[0]


def _mod_call(cvec, w_mod, b_mod):
    L, Dm, N = w_mod.shape
    tn = 1024
    out = pl.pallas_call(
        _mod_kernel,
        grid=(L, N // tn),
        in_specs=[pl.BlockSpec((SUBLANE, Dm), lambda l, j: (0, 0)),
                  pl.BlockSpec((1, Dm, tn), lambda l, j: (l, 0, j)),
                  pl.BlockSpec((1, 1, tn), lambda l, j: (l, 0, j))],
        out_specs=pl.BlockSpec((1, SUBLANE, tn), lambda l, j: (l, 0, j)),
        out_shape=jax.ShapeDtypeStruct((L, SUBLANE, N), F32),
        compiler_params=_params(("arbitrary", "arbitrary"), 40),
        name="adaln_mod",
    )(cvec, w_mod, b_mod.reshape(L, 1, N))
    return out.reshape(L, SUBLANE, N_MOD, Dm)


def _mod_spec(goff, tiles_per_group, nargs):
    if nargs == 1:
        return pl.BlockSpec((None, N_MOD, D_MODEL), lambda i: (goff + i // tiles_per_group, 0, 0))
    return pl.BlockSpec((None, N_MOD, D_MODEL), lambda i, j: (goff + i // tiles_per_group, 0, 0))


def _ffn_kernel(x_ref, mod_ref, g_ref, wg_ref, wu_ref, wo_ref, o_ref, h_sc, acc_sc, *, sub):
    f = pl.program_id(1)

    @pl.when(f == 0)
    def _():
        h = _norm_mod(x_ref[...], g_ref[...], mod_ref[3 * sub:3 * sub + 1, :],
                      mod_ref[3 * sub + 1:3 * sub + 2, :])
        h_sc[...] = h.astype(BF16)
        acc_sc[...] = jnp.zeros_like(acc_sc)

    h = h_sc[...]
    a = _dot(h, wg_ref[...])
    u = _dot(h, wu_ref[...])
    act = (a * _sigmoid(a)) * u
    acc_sc[...] += _dot(act.astype(BF16), wo_ref[...])

    @pl.when(f == pl.num_programs(1) - 1)
    def _():
        o_ref[...] = x_ref[...] + 0.5 * mod_ref[3 * sub + 2:3 * sub + 3, :] * acc_sc[...]


def _ffn_call(x, mod_l, goff, group_tokens, gamma, w_in, w_out, sub):
    T = x.shape[0]
    tm = min(512, group_tokens)
    tf = 512
    nf = D_FF // tf
    return pl.pallas_call(
        functools.partial(_ffn_kernel, sub=sub),
        grid=(T // tm, nf),
        in_specs=[pl.BlockSpec((tm, D_MODEL), lambda i, f: (i, 0)),
                  _mod_spec(goff, group_tokens // tm, 2),
                  pl.BlockSpec((1, D_MODEL), lambda i, f: (0, 0)),
                  pl.BlockSpec((D_MODEL, tf), lambda i, f: (0, f)),
                  pl.BlockSpec((D_MODEL, tf), lambda i, f: (0, f + nf)),
                  pl.BlockSpec((tf, D_MODEL), lambda i, f: (f, 0))],
        out_specs=pl.BlockSpec((tm, D_MODEL), lambda i, f: (i, 0)),
        out_shape=jax.ShapeDtypeStruct((T, D_MODEL), F32),
        scratch_shapes=[pltpu.VMEM((tm, D_MODEL), BF16), pltpu.VMEM((tm, D_MODEL), F32)],
        compiler_params=_params(("parallel", "arbitrary"), 52),
        name="ffn_swiglu",
    )(x, mod_l, gamma, w_in, w_in, w_out)


def _inproj_kernel(x_ref, mod_ref, g_ref, w_ref, o_ref, h_sc):
    @pl.when(pl.program_id(1) == 0)
    def _():
        h = _norm_mod(x_ref[...], g_ref[...], mod_ref[3:4, :], mod_ref[4:5, :])
        h_sc[...] = h.astype(BF16)

    o_ref[...] = _dot(h_sc[...], w_ref[...])


def _inproj_call(x, mod_l, goff, group_tokens, gamma, w, tn):
    T = x.shape[0]
    N = w.shape[1]
    tm = min(512, group_tokens)
    return pl.pallas_call(
        _inproj_kernel,
        grid=(T // tm, N // tn),
        in_specs=[pl.BlockSpec((tm, D_MODEL), lambda i, j: (i, 0)),
                  _mod_spec(goff, group_tokens // tm, 2),
                  pl.BlockSpec((1, D_MODEL), lambda i, j: (0, 0)),
                  pl.BlockSpec((D_MODEL, tn), lambda i, j: (0, j))],
        out_specs=pl.BlockSpec((tm, tn), lambda i, j: (i, j)),
        out_shape=jax.ShapeDtypeStruct((T, N), F32),
        scratch_shapes=[pltpu.VMEM((tm, D_MODEL), BF16)],
        compiler_params=_params(("parallel", "arbitrary"), 48),
        name="mixer_inproj",
    )(x, mod_l, gamma, w)


def _outproj_kernel(x_ref, mod_ref, a1_ref, a2_ref, w1_ref, w2_ref, o_ref):
    y = _dot(a1_ref[...], w1_ref[...]) + _dot(a2_ref[...], w2_ref[...])
    o_ref[...] = x_ref[...] + mod_ref[5:6, :] * y


def _outproj_call(x, mod_l, goff, group_tokens, a1, a2, w, split_a):
    T = x.shape[0]
    tm = min(512, group_tokens)
    half = D_MODEL // 2
    if split_a:
        a_specs = [pl.BlockSpec((tm, half), lambda i: (i, 0)), pl.BlockSpec((tm, half), lambda i: (i, 1))]
    else:
        a_specs = [pl.BlockSpec((tm, half), lambda i: (i, 0)), pl.BlockSpec((tm, half), lambda i: (i, 0))]
    return pl.pallas_call(
        _outproj_kernel,
        grid=(T // tm,),
        in_specs=[pl.BlockSpec((tm, D_MODEL), lambda i: (i, 0)),
                  _mod_spec(goff, group_tokens // tm, 1)] + a_specs +
                 [pl.BlockSpec((half, D_MODEL), lambda i: (0, 0)),
                  pl.BlockSpec((half, D_MODEL), lambda i: (1, 0))],
        out_specs=pl.BlockSpec((tm, D_MODEL), lambda i: (i, 0)),
        out_shape=jax.ShapeDtypeStruct((T, D_MODEL), F32),
        compiler_params=_params(("parallel",), 48),
        name="mixer_outproj",
    )(x, mod_l, a1, a2, w, w)


def _final_norm_kernel(x_ref, g_ref, o_ref):
    x = x_ref[...]
    o_ref[...] = (x * lax.rsqrt(jnp.mean(x * x, axis=-1, keepdims=True) + 1e-6)) * g_ref[...]


def _final_norm_call(x, gamma):
    T = x.shape[0]
    tm = 512
    return pl.pallas_call(
        _final_norm_kernel,
        grid=(T // tm,),
        in_specs=[pl.BlockSpec((tm, D_MODEL), lambda i: (i, 0)),
                  pl.BlockSpec((1, D_MODEL), lambda i: (0, 0))],
        out_specs=pl.BlockSpec((tm, D_MODEL), lambda i: (i, 0)),
        out_shape=jax.ShapeDtypeStruct((T, D_MODEL), F32),
        compiler_params=_params(("parallel",), 32),
        name="final_norm",
    )(x, gamma)


def _shift_prev_next(cur, halo_prev, halo_next, row0, seq_len):
    tt = cur.shape[0]
    rid = lax.broadcasted_iota(jnp.int32, (tt, 1), 0)
    pos = jnp.bitwise_and(rid + row0, seq_len - 1)
    prev = pltpu.roll(cur, 1, 0)
    prev = jnp.where(rid == 0, halo_prev[SUBLANE - 1:SUBLANE, :], prev)
    prev = jnp.where(pos == 0, 0.0, prev)
    nxt = pltpu.roll(cur, tt - 1, 0)
    nxt = jnp.where(rid == tt - 1, halo_next[0:1, :], nxt)
    nxt = jnp.where(pos == seq_len - 1, 0.0, nxt)
    return prev, nxt


def _halo_specs(tt, width, col, total_rows):
    per = tt // SUBLANE
    last = total_rows // SUBLANE - 1
    return [pl.BlockSpec((tt, width), lambda i: (i, col)),
            pl.BlockSpec((SUBLANE, width), lambda i: (jnp.maximum(i * per - 1, 0), col)),
            pl.BlockSpec((SUBLANE, width), lambda i: (jnp.minimum((i + 1) * per, last), col))]


def _segsum(x, ones_blk):
    hi = x.astype(BF16)
    lo = (x - hi.astype(F32)).astype(BF16)
    outs = []
    for g in range(x.shape[1] // MXU_DIM):
        sl = slice(g * MXU_DIM, (g + 1) * MXU_DIM)
        outs.append(_dot(hi[:, sl], ones_blk) + _dot(lo[:, sl], ones_blk))
    return jnp.concatenate(outs, axis=1)


def _rwkv_prep_kernel(z_ref, zp_ref, zn_ref, mup_ref, mun_ref, kk_ref, ka_ref, rk_ref,
                      w0_ref, w2_ref, a0_ref, a2_ref, g2_ref, ones_ref,
                      r_o, v_o, c_o, w0_o, b0_o, k0_o, w1_o, b1_o, k1_o, bonus_o, g_o,
                      *, tt, seq_len):
    row0 = pl.program_id(0) * tt
    cur = z_ref[...]
    prev, nxt = _shift_prev_next(cur, zp_ref[...], zn_ref[...], row0, seq_len)
    zs = cur + mup_ref[...] * (prev - cur) + mun_ref[...] * (nxt - cur)
    r = zs[:, 0:D_RWKV]
    k = zs[:, D_RWKV:2 * D_RWKV]
    v = zs[:, 2 * D_RWKV:3 * D_RWKV]
    small = zs[:, 3 * D_RWKV:RWKV_COLS]
    ones_blk = ones_ref[...]

    kk = k * kk_ref[...]
    kk = kk / jnp.maximum(jnp.sqrt(_segsum(kk * kk, ones_blk)), 1e-12)
    tw = jnp.tanh(small).astype(BF16)
    sg = _sigmoid(small).astype(BF16)
    xs = small.astype(BF16)
    r_o[...] = r
    v_o[...] = v
    c_o[...] = -kk
    g_o[...] = _dot(sg, g2_ref[...])

    bonus = jnp.zeros_like(r)
    outs = ((w0_o, b0_o, k0_o), (w1_o, b1_o, k1_o))
    for d in range(2):
        wl = -_softplus(-(w0_ref[d:d + 1, :] + _dot(tw, w2_ref[d]))) - 0.5
        a = _sigmoid(a0_ref[d:d + 1, :] + _dot(xs, a2_ref[d]))
        kd = k * (1.0 + (a - 1.0) * ka_ref[...])
        w_o, b_o, k_o = outs[d]
        w_o[...] = jnp.exp(-jnp.exp(wl))
        b_o[...] = kk * a
        k_o[...] = kd
        bonus = bonus + _segsum(r * kd * rk_ref[...], ones_blk) * v
    bonus_o[...] = bonus


def _rwkv_prep_call(z_r, seq_len, p):
    T = z_r.shape[0]
    tt = min(256, seq_len)
    row = lambda n: pl.BlockSpec((1, n), lambda i: (0, 0))
    full2 = lambda a, b: pl.BlockSpec((a, b), lambda i: (0, 0))
    full3 = lambda a, b, c: pl.BlockSpec((a, b, c), lambda i: (0, 0, 0))
    in_specs = _halo_specs(tt, RWKV_COLS, 0, T) + [
        row(RWKV_COLS), row(RWKV_COLS), row(D_RWKV), row(D_RWKV), row(D_RWKV),
        full2(2, D_RWKV), full3(2, RWKV_SMALL, D_RWKV), full2(2, D_RWKV), full3(2, RWKV_SMALL, D_RWKV),
        full2(RWKV_SMALL, D_RWKV), full2(MXU_DIM, MXU_DIM)]
    out_spec = pl.BlockSpec((tt, D_RWKV), lambda i: (i, 0))
    out_shape = jax.ShapeDtypeStruct((T, D_RWKV), F32)
    return pl.pallas_call(
        functools.partial(_rwkv_prep_kernel, tt=tt, seq_len=seq_len),
        grid=(T // tt,),
        in_specs=in_specs,
        out_specs=[out_spec] * 11,
        out_shape=[out_shape] * 11,
        compiler_params=_params(("parallel",), 52),
        name="rwkv_prep",
    )(z_r, z_r, z_r, p['mu_prev'], p['mu_next'], p['k_k'], p['k_a'], p['r_k'],
      p['w0'], p['w2'], p['a0'], p['a2'], p['g2'], p['ones_blk'])


N_GRP = D_RWKV // MXU_DIM


def _scan_kernel(rf, vf, cf, wf, bf, kf, rb, vb, cb, wb, bb, kb, s0_ref, ones_ref, eye_ref,
                 y0_ref, y1_ref, sfin_ref, st, *, tc):
    j = pl.program_id(1)

    @pl.when(j == 0)
    def _():
        st[...] = s0_ref[...]

    ones_blk = ones_ref[...]
    eye = eye_ref[...]
    dirs = ((rf, vf, cf, wf, bf, kf, y0_ref), (rb, vb, cb, wb, bb, kb, y1_ref))

    def body(i, carry):
        chains = []
        for b in range(2):
            for d in range(2):
                t = i if d == 0 else tc - 1 - i
                chains.append((b, d, t))
        lhs = []
        states = []
        for (b, d, t) in chains:
            r_, v_, c_, w_, b_, k_, y_ = dirs[d]
            crow = c_[b, pl.ds(t, 1), :]
            vrow = v_[b, pl.ds(t, 1), :]
            for g in range(N_GRP):
                sl = slice(g * MXU_DIM, (g + 1) * MXU_DIM)
                m = st[b, d, g]
                states.append(m)
                lhs.append((m * crow[:, sl]).astype(BF16))
                lhs.append((eye * vrow[:, sl]).astype(BF16))
        res = _dot(jnp.concatenate(lhs, axis=0), ones_blk)
        lhs2 = []
        idx = 0
        for (b, d, t) in chains:
            r_, v_, c_, w_, b_, k_, y_ = dirs[d]
            wrow = w_[b, pl.ds(t, 1), :]
            brow = b_[b, pl.ds(t, 1), :]
            krow = k_[b, pl.ds(t, 1), :]
            rrow = r_[b, pl.ds(t, 1), :]
            for g in range(N_GRP):
                sl = slice(g * MXU_DIM, (g + 1) * MXU_DIM)
                ub = res[idx * 2 * RWKV_HEAD:(idx * 2 + 1) * RWKV_HEAD]
                vc = res[(idx * 2 + 1) * RWKV_HEAD:(idx * 2 + 2) * RWKV_HEAD]
                mn = states[idx] * wrow[:, sl] + ub * brow[:, sl] + vc * krow[:, sl]
                st[b, d, g] = mn
                lhs2.append((mn * rrow[:, sl]).astype(BF16))
                idx += 1
        res2 = _dot(jnp.concatenate(lhs2, axis=0), ones_blk)
        idx = 0
        for (b, d, t) in chains:
            y_ = dirs[d][6]
            for g in range(N_GRP):
                sl = slice(g * MXU_DIM, (g + 1) * MXU_DIM)
                yb = res2[idx * RWKV_HEAD:(idx + 1) * RWKV_HEAD]
                y_[b, pl.ds(t, 1), sl] = jnp.sum(yb * eye, axis=0, keepdims=True)
                idx += 1
        return carry

    lax.fori_loop(0, tc, body, 0)

    @pl.when(j == pl.num_programs(1) - 1)
    def _():
        sfin_ref[...] = st[...]


def _scan_call(pre, s0, ones_blk, eye):
    B, T, _ = pre['r'].shape
    tc = min(128, T)
    nj = T // tc
    fwd = pl.BlockSpec((2, tc, D_RWKV), lambda bi, j: (bi, j, 0))
    bwd = pl.BlockSpec((2, tc, D_RWKV), lambda bi, j: (bi, nj - 1 - j, 0))
    st_spec = pl.BlockSpec((2, 2, N_GRP, RWKV_HEAD, MXU_DIM), lambda bi, j: (bi, 0, 0, 0, 0))
    return pl.pallas_call(
        functools.partial(_scan_kernel, tc=tc),
        grid=(B // 2, nj),
        in_specs=[fwd] * 6 + [bwd] * 6 + [
            st_spec,
            pl.BlockSpec((MXU_DIM, MXU_DIM), lambda bi, j: (0, 0)),
            pl.BlockSpec((RWKV_HEAD, MXU_DIM), lambda bi, j: (0, 0))],
        out_specs=[fwd, bwd, st_spec],
        out_shape=[jax.ShapeDtypeStruct((B, T, D_RWKV), F32),
                   jax.ShapeDtypeStruct((B, T, D_RWKV), F32),
                   jax.ShapeDtypeStruct((B, 2, N_GRP, RWKV_HEAD, MXU_DIM), F32)],
        scratch_shapes=[pltpu.VMEM((2, 2, N_GRP, RWKV_HEAD, MXU_DIM), F32)],
        compiler_params=_params(("arbitrary", "arbitrary"), 48),
        name="rwkv_scan",
    )(pre['r'], pre['v'], pre['c'], pre['w0'], pre['b0'], pre['k0'],
      pre['r'], pre['v'], pre['c'], pre['w1'], pre['b1'], pre['k1'], s0, ones_blk, eye)


def _rwkv_post_kernel(y0_ref, y1_ref, bonus_ref, g_ref, gw_ref, gb_ref, ones_ref, o_ref):
    ones_blk = ones_ref[...]
    y = y0_ref[...] + y1_ref[...]
    mu = _segsum(y, ones_blk) * (1.0 / RWKV_HEAD)
    yc = y - mu
    var = _segsum(yc * yc, ones_blk) * (1.0 / RWKV_HEAD)
    yn = yc * lax.rsqrt(var + RWKV_GN_EPS)
    out = (yn * gw_ref[...] + gb_ref[...] + bonus_ref[...]) * g_ref[...]
    o_ref[...] = out.astype(BF16)


def _rwkv_post_call(y0, y1, bonus, g, gn_w, gn_b, ones_blk):
    T = y0.shape[0]
    tt = 512
    blk = pl.BlockSpec((tt, D_RWKV), lambda i: (i, 0))
    row = pl.BlockSpec((1, D_RWKV), lambda i: (0, 0))
    return pl.pallas_call(
        _rwkv_post_kernel,
        grid=(T // tt,),
        in_specs=[blk, blk, blk, blk, row, row, pl.BlockSpec((MXU_DIM, MXU_DIM), lambda i: (0, 0))],
        out_specs=blk,
        out_shape=jax.ShapeDtypeStruct((T, D_RWKV), BF16),
        compiler_params=_params(("parallel",), 40),
        name="rwkv_post",
    )(y0, y1, bonus, g, gn_w, gn_b, ones_blk)


def _rope128(x, cos_t, sin_t):
    return x * cos_t + pltpu.roll(x, QK_ROPE, 1) * sin_t


def _pack_kv(kv, kr_rot, k_o, v_o):
    for h in range(H_MLA):
        k_o[:, h * HEAD_SLOT:h * HEAD_SLOT + QK_NOPE] = kv[:, h * HEAD_SLOT:h * HEAD_SLOT + QK_NOPE].astype(BF16)
        k_o[:, h * HEAD_SLOT + QK_NOPE:(h + 1) * HEAD_SLOT] = kr_rot.astype(BF16)
        v_o[:, h * V_HEAD:(h + 1) * V_HEAD] = kv[:, h * HEAD_SLOT + QK_NOPE:(h + 1) * HEAD_SLOT].astype(BF16)


def _mla_prep_kernel(z_ref, cos_ref, sin_ref, qn_ref, kvn_ref, wq_ref, wkv_ref,
                     q_o, k_o, v_o, ckv_o, kr_o):
    z = z_ref[...]
    cq = z[:, 0:Q_RANK]
    ckv = z[:, Q_RANK:Q_RANK + KV_RANK]
    krp = z[:, Q_RANK + KV_RANK:MLA_COLS]
    cos_t = cos_ref[...]
    sin_t = sin_ref[...]
    cq = (cq * lax.rsqrt(jnp.mean(cq * cq, axis=-1, keepdims=True) + 1e-6)) * qn_ref[...]
    ckv = (ckv * lax.rsqrt(jnp.mean(ckv * ckv, axis=-1, keepdims=True) + 1e-6)) * kvn_ref[...]
    ckv_o[...] = ckv
    kr_o[...] = krp
    q = _dot(cq.astype(BF16), wq_ref[...])
    for h in range(H_MLA):
        q_o[:, h * HEAD_SLOT:h * HEAD_SLOT + QK_NOPE] = q[:, h * HEAD_SLOT:h * HEAD_SLOT + QK_NOPE].astype(BF16)
        q_o[:, h * HEAD_SLOT + QK_NOPE:(h + 1) * HEAD_SLOT] = _rope128(
            q[:, h * HEAD_SLOT + QK_NOPE:(h + 1) * HEAD_SLOT], cos_t, sin_t).astype(BF16)
    kv = _dot(ckv.astype(BF16), wkv_ref[...])
    _pack_kv(kv, _rope128(krp, cos_t, sin_t), k_o, v_o)


def _mla_prep_call(z_m, cos_t, sin_t, seq_len, p):
    T = z_m.shape[0]
    tm = min(512, seq_len)
    per_seq = seq_len // tm
    row = lambda n: pl.BlockSpec((1, n), lambda i: (0, 0))
    blk = lambda n: pl.BlockSpec((tm, n), lambda i: (i, 0))
    tab = pl.BlockSpec((tm, LANE), lambda i: (i % per_seq, 0))
    return pl.pallas_call(
        _mla_prep_kernel,
        grid=(T // tm,),
        in_specs=[blk(MLA_COLS), tab, tab, row(Q_RANK), row(KV_RANK),
                  pl.BlockSpec((Q_RANK, H_MLA * HEAD_SLOT), lambda i: (0, 0)),
                  pl.BlockSpec((KV_RANK, H_MLA * HEAD_SLOT), lambda i: (0, 0))],
        out_specs=[blk(H_MLA * HEAD_SLOT), blk(H_MLA * HEAD_SLOT), blk(H_MLA * V_HEAD),
                   blk(KV_RANK), blk(LANE)],
        out_shape=[jax.ShapeDtypeStruct((T, H_MLA * HEAD_SLOT), BF16),
                   jax.ShapeDtypeStruct((T, H_MLA * HEAD_SLOT), BF16),
                   jax.ShapeDtypeStruct((T, H_MLA * V_HEAD), BF16),
                   jax.ShapeDtypeStruct((T, KV_RANK), F32),
                   jax.ShapeDtypeStruct((T, LANE), F32)],
        compiler_params=_params(("parallel",), 48),
        name="mla_prep",
    )(z_m, cos_t, sin_t, p['q_norm'], p['kv_norm'], p['w_qb'], p['w_kvb'])


def _ctx_kv_kernel(ckv_ref, kr_ref, wkv_ref, k_o, v_o):
    kv = _dot(ckv_ref[...].astype(BF16), wkv_ref[...])
    _pack_kv(kv, kr_ref[...], k_o, v_o)


def _ctx_kv_call(ckv_ctx, kr_ctx_pad, w_kvb):
    T = ckv_ctx.shape[0]
    tm = min(512, T)
    blk = lambda n: pl.BlockSpec((tm, n), lambda i: (i, 0))
    return pl.pallas_call(
        _ctx_kv_kernel,
        grid=(T // tm,),
        in_specs=[blk(KV_RANK), blk(LANE), pl.BlockSpec((KV_RANK, H_MLA * HEAD_SLOT), lambda i: (0, 0))],
        out_specs=[blk(H_MLA * HEAD_SLOT), blk(H_MLA * V_HEAD)],
        out_shape=[jax.ShapeDtypeStruct((T, H_MLA * HEAD_SLOT), BF16),
                   jax.ShapeDtypeStruct((T, H_MLA * V_HEAD), BF16)],
        compiler_params=_params(("parallel",), 32),
        name="mla_ctx_kv",
    )(ckv_ctx, kr_ctx_pad, w_kvb)


def _attn_kernel(q_ref, k_ref, v_ref, o_ref, *, scale):
    s = lax.dot_general(q_ref[...], k_ref[...], (((1,), (1,)), ((), ())),
                        preferred_element_type=F32) * scale
    m = jnp.max(s, axis=-1, keepdims=True)
    p = jnp.exp(s - m)
    l = jnp.sum(p, axis=-1, keepdims=True)
    o = _dot(p.astype(BF16), v_ref[...])
    o_ref[...] = (o / l).astype(BF16)


def _attn_call(q, k, v):
    B, Tq, _ = q.shape
    Tk = k.shape[1]
    tq = min(512, Tq)
    scale = (QK_NOPE + QK_ROPE) ** -0.5
    return pl.pallas_call(
        functools.partial(_attn_kernel, scale=scale),
        grid=(B, H_MLA, Tq // tq),
        in_specs=[pl.BlockSpec((None, tq, HEAD_SLOT), lambda b, h, i: (b, i, h)),
                  pl.BlockSpec((None, Tk, HEAD_SLOT), lambda b, h, i: (b, 0, h)),
                  pl.BlockSpec((None, Tk, V_HEAD), lambda b, h, i: (b, 0, h))],
        out_specs=pl.BlockSpec((None, tq, V_HEAD), lambda b, h, i: (b, i, h)),
        out_shape=jax.ShapeDtypeStruct((B, Tq, H_MLA * V_HEAD), BF16),
        compiler_params=_params(("parallel", "parallel", "arbitrary"), 48),
        name="mla_attention",
    )(q, k, v)


def _conv3_kernel(*refs, tt, seq_len):
    ins, (cw_refs, cb_refs), outs = refs[0:9], (refs[9:12], refs[12:15]), refs[15:]
    row0 = pl.program_id(0) * tt
    for s in range(3):
        cur = ins[3 * s][...]
        prev, nxt = _shift_prev_next(cur, ins[3 * s + 1][...], ins[3 * s + 2][...], row0, seq_len)
        cw = cw_refs[s][...]
        y = cw[0:1, :] * prev + cw[1:2, :] * cur + cw[2:3, :] * nxt + cb_refs[s][...]
        outs[s][...] = y
        if s == 2:
            outs[3][...] = y.astype(BF16)


def _conv3_call(z, seq_len, conv_w, conv_b):
    T = z.shape[0]
    tt = min(256, seq_len)
    in_specs = []
    for s in range(3):
        in_specs += _halo_specs(tt, D_HYENA, s, T)
    in_specs += [pl.BlockSpec((3, D_HYENA), lambda i, s=s: (0, s)) for s in range(3)]
    in_specs += [pl.BlockSpec((1, D_HYENA), lambda i, s=s: (0, s)) for s in range(3)]
    blk = pl.BlockSpec((tt, D_HYENA), lambda i: (i, 0))
    f32s = jax.ShapeDtypeStruct((T, D_HYENA), F32)
    return pl.pallas_call(
        functools.partial(_conv3_kernel, tt=tt, seq_len=seq_len),
        grid=(T // tt,),
        in_specs=in_specs,
        out_specs=[blk] * 4,
        out_shape=[f32s, f32s, f32s, jax.ShapeDtypeStruct((T, D_HYENA), BF16)],
        compiler_params=_params(("parallel",), 48),
        name="hyena_conv3",
    )(*([z] * 9), conv_w, conv_w, conv_w, conv_b, conv_b, conv_b)


def _filt_mlp_kernel(z_ref, w1_ref, b1_ref, w2_ref, b2_ref, fr_ref, o_ref):
    h = jnp.sin(fr_ref[0:1, :] * (_dot(z_ref[...].astype(BF16), w1_ref[...].astype(BF16)) + b1_ref[...]))
    h = jnp.sin(fr_ref[1:2, :] * (_dot(h.astype(BF16), w2_ref[...].astype(BF16)) + b2_ref[...]))
    o_ref[...] = h.astype(BF16)


def _filt_mlp_call(zpos, w1p, b1, w2, b2, freq):
    L = zpos.shape[0]
    return pl.pallas_call(
        _filt_mlp_kernel,
        out_shape=jax.ShapeDtypeStruct((L, FILT_HIDDEN), BF16),
        compiler_params=pltpu.CompilerParams(vmem_limit_bytes=32 * MIB),
        name="hyena_filter_mlp",
    )(zpos, w1p, b1, w2, b2, freq)


def _filt_gen_kernel(h_ref, tn_ref, dl_ref, w00, w01, w10, w11, o_ref):
    h = h_ref[...]
    L = h.shape[0]
    win = jnp.exp(-tn_ref[...] * dl_ref[...])
    not_first = lax.broadcasted_iota(jnp.int32, (L, 1), 0) > 0
    ws = ((w00, w01), (w10, w11))
    for n in range(2):
        causal = _dot(h, ws[n][0][...].astype(BF16)) * win
        anti = jnp.where(not_first, _dot(h, ws[n][1][...].astype(BF16)) * win, 0.0)
        norm = (jnp.sum(jnp.abs(causal), axis=0, keepdims=True)
                + jnp.sum(jnp.abs(anti), axis=0, keepdims=True))
        o_ref[2 * n] = (causal / norm).astype(BF16)
        o_ref[2 * n + 1] = (anti / norm).astype(BF16)


def _filt_gen_call(h2, tnorm, deltas, w3):
    L = h2.shape[0]
    tc = 128
    nc = D_HYENA // tc
    wspec = lambda k: pl.BlockSpec((FILT_HIDDEN, tc), lambda j, k=k: (0, k * nc + j))
    return pl.pallas_call(
        _filt_gen_kernel,
        grid=(nc,),
        in_specs=[pl.BlockSpec((L, FILT_HIDDEN), lambda j: (0, 0)),
                  pl.BlockSpec((L, 1), lambda j: (0, 0)),
                  pl.BlockSpec((1, tc), lambda j: (0, j)),
                  wspec(0), wspec(1), wspec(2), wspec(3)],
        out_specs=pl.BlockSpec((4, L, tc), lambda j: (0, 0, j)),
        out_shape=jax.ShapeDtypeStruct((4, L, D_HYENA), BF16),
        compiler_params=_params(("parallel",), 48),
        name="hyena_filter_gen",
    )(h2, tnorm, deltas, w3, w3, w3, w3)


def _dft_fwd_kernel(f_ref, u_ref, o_ref):
    o_ref[...] = _dot(f_ref[...], u_ref[...])


def _dft_fwd_call(fmat, u):
    B, L, C = u.shape
    tm = 512
    tn = 512 if L > 512 else C
    return pl.pallas_call(
        _dft_fwd_kernel,
        grid=(B, C // tn, 2 * L // tm),
        in_specs=[pl.BlockSpec((tm, L), lambda b, j, i: (i, 0)),
                  pl.BlockSpec((None, L, tn), lambda b, j, i: (b, 0, j))],
        out_specs=pl.BlockSpec((None, tm, tn), lambda b, j, i: (b, i, j)),
        out_shape=jax.ShapeDtypeStruct((B, 2 * L, C), F32),
        compiler_params=_params(("parallel", "parallel", "arbitrary"), 40),
        name="hyena_dft_fwd",
    )(fmat, u)


def _spec_mul_kernel(u_ref, k_ref, o_ref, *, tr):
    a = u_ref[0]
    b = u_ref[1]
    ka = k_ref[0, 0] + k_ref[1, 0]
    kb_diff = k_ref[0, 1] - k_ref[1, 1]
    kb_sum = k_ref[0, 1] + k_ref[1, 1]
    first = (lax.broadcasted_iota(jnp.int32, (tr, 1), 0) + pl.program_id(1) * tr) == 0
    kb = jnp.where(first, kb_sum, kb_diff)
    top = jnp.where(first, a * ka, a * ka - b * kb)
    bot = jnp.where(first, b * kb, a * kb + b * ka)
    o_ref[0] = top.astype(BF16)
    o_ref[1] = bot.astype(BF16)


def _spec_mul_call(u_spec, k_spec, order):
    B, L2, C = u_spec.shape
    L = L2 // 2
    tr = min(256, L)
    tc = 512
    out = pl.pallas_call(
        functools.partial(_spec_mul_kernel, tr=tr),
        grid=(B, L // tr, C // tc),
        in_specs=[pl.BlockSpec((None, 2, tr, tc), lambda b, i, j: (b, 0, i, j)),
                  pl.BlockSpec((None, 2, 2, tr, tc), lambda b, i, j: (order, 0, 0, i, j))],
        out_specs=pl.BlockSpec((None, 2, tr, tc), lambda b, i, j: (b, 0, i, j)),
        out_shape=jax.ShapeDtypeStruct((B, 2, L, C), BF16),
        compiler_params=_params(("parallel", "parallel", "parallel"), 40),
        name="hyena_spectral_mul",
    )(u_spec.reshape(B, 2, L, C), k_spec.reshape(2, 2, 2, L, C))
    return out.reshape(B, L2, C)


def _dft_inv_kernel(f_ref, y_ref, gate_ref, u_ref, bias_ref, o_ref, ob_ref):
    conv = _dot(f_ref[...], y_ref[...])
    out = gate_ref[...] * (conv + u_ref[...] * bias_ref[...])
    o_ref[...] = out
    ob_ref[...] = out.astype(BF16)


def _dft_inv_call(imat, y_spec, gate, u, bias):
    B, L, C = u.shape
    tm = min(256, L)
    tn = 512
    blk = pl.BlockSpec((None, tm, tn), lambda b, j, i: (b, i, j))
    return pl.pallas_call(
        _dft_inv_kernel,
        grid=(B, C // tn, L // tm),
        in_specs=[pl.BlockSpec((tm, 2 * L), lambda b, j, i: (i, 0)),
                  pl.BlockSpec((None, 2 * L, tn), lambda b, j, i: (b, 0, j)),
                  blk, blk, pl.BlockSpec((1, tn), lambda b, j, i: (0, j))],
        out_specs=[blk, blk],
        out_shape=[jax.ShapeDtypeStruct((B, L, C), F32), jax.ShapeDtypeStruct((B, L, C), BF16)],
        compiler_params=_params(("parallel", "parallel", "arbitrary"), 48),
        name="hyena_dft_inv",
    )(imat, y_spec, gate, u, bias)


def _dft_tables(L):
    N = 2 * L
    lo = min(64, L)
    hi = L // lo
    f = jnp.arange(L, dtype=jnp.int32)
    theta = 2.0 * math.pi / N
    ang_hi = ((f[:, None] * (jnp.arange(hi, dtype=jnp.int32) * lo)[None, :]) % N).astype(F32) * theta
    ang_lo = ((f[:, None] * jnp.arange(lo, dtype=jnp.int32)[None, :]) % N).astype(F32) * theta
    ch, sh, cl, sl = jnp.cos(ang_hi), jnp.sin(ang_hi), jnp.cos(ang_lo), jnp.sin(ang_lo)
    cos_m = (ch[:, :, None] * cl[:, None, :] - sh[:, :, None] * sl[:, None, :]).reshape(L, L)
    sin_m = (sh[:, :, None] * cl[:, None, :] + ch[:, :, None] * sl[:, None, :]).reshape(L, L)
    nyq = jnp.where(jnp.arange(L) % 2 == 0, 1.0, -1.0).astype(F32)
    sin_m = jnp.where((f == 0)[:, None], nyq[None, :], sin_m)
    fwd = jnp.concatenate([cos_m, sin_m], axis=0)
    wts = jnp.where(f == 0, 1.0 / N, 2.0 / N).astype(F32)
    inv = jnp.concatenate([cos_m * wts[:, None], sin_m * wts[:, None]], axis=0).T
    return fwd.astype(BF16), inv.astype(BF16)


def _filter_positions(L):
    t = jnp.arange(L, dtype=F32)
    t_norm = t / max(L - 1, 1)
    bands = (POS_EMB - 1) // 2
    freqs = jnp.linspace(1e-4, bands - 1, bands, dtype=F32)
    ang = (2.0 * math.pi / L) * t[:, None] * freqs[None, :]
    z = jnp.concatenate([t_norm[:, None], jnp.cos(ang), -jnp.sin(ang)], axis=-1)
    return jnp.pad(z, ((0, 0), (0, POS_PAD - POS_EMB))), t_norm[:, None]


def _hyena_deltas():
    return jnp.linspace(abs(math.log(HYENA_TARGET)) / SLOW_DECAY_PCT,
                        abs(math.log(HYENA_TARGET)) / FAST_DECAY_PCT, D_HYENA, dtype=F32)[None, :]


def _hyena_filter_spectrum(L, fwd, p):
    zpos, tnorm = _filter_positions(L)
    h2 = _filt_mlp_call(zpos, p['filt_w1'], p['filt_b1'], p['filt_w2'], p['filt_b2'], p['filt_freq'])
    filt = _filt_gen_call(h2, tnorm, _hyena_deltas(), p['filt_w3'])
    return _dft_fwd_call(fwd, filt)


def _hyena_mixer(z, B, L, tables, k_spec, p):
    fwd, inv = tables
    x1, x2, v, vb = _conv3_call(z, L, p['conv_w'], p['conv_b'])
    shp = (B, L, D_HYENA)
    u, ub = v.reshape(shp), vb.reshape(shp)
    for n, gate in enumerate((x1, x2)):
        spec = _dft_fwd_call(fwd, ub)
        yspec = _spec_mul_call(spec, k_spec, n)
        u, ub = _dft_inv_call(inv, yspec, gate.reshape(shp), u, p['bias'][n:n + 1])
    return ub.reshape(B * L, D_HYENA)


def _rope_swap(w):
    q = QK_ROPE // 4
    return jnp.concatenate([w[..., q:2 * q], w[..., 0:q], w[..., 3 * q:4 * q], w[..., 2 * q:3 * q]], axis=-1)


def _pad_cols(w, n):
    return jnp.pad(w, [(0, 0)] * (w.ndim - 1) + [(0, n - w.shape[-1])])


def _pack_even(e, w_in_even, mu_prev, mu_next, rwkv_w0, rwkv_w2, rwkv_a0, rwkv_a2, rwkv_g2,
               rwkv_kk, rwkv_ka, rwkv_rk, rwkv_gn_w, rwkv_gn_b, mla_q_norm, mla_kv_norm,
               mla_w_qb, mla_w_kvb, w_out_even):
    n_r = 3 * D_RWKV + W_LORA + A_LORA + G_LORA
    w_in = w_in_even[e]
    w_r = _pad_cols(w_in[:, :n_r], RWKV_COLS).astype(BF16)
    w_m = w_in[:, n_r:]
    kr_cols = w_m[:, Q_RANK + KV_RANK:]
    w_m = jnp.concatenate([w_m, _rope_swap(kr_cols)], axis=-1).astype(BF16)
    small_rows = lambda w, off: jnp.pad(w, [(0, 0)] * (w.ndim - 2)
                                        + [(off, RWKV_SMALL - off - w.shape[-2]), (0, 0)]).astype(BF16)
    wq = mla_w_qb[e].reshape(Q_RANK, H_MLA, QK_NOPE + QK_ROPE)
    wq = jnp.concatenate([wq, _rope_swap(wq[..., QK_NOPE:])], axis=-1).reshape(Q_RANK, H_MLA * HEAD_SLOT)
    blk = jnp.arange(MXU_DIM) // RWKV_HEAD
    return {
        'w_r': w_r, 'w_m': w_m,
        'mu_prev': _pad_cols(mu_prev[e][None, :], RWKV_COLS),
        'mu_next': _pad_cols(mu_next[e][None, :], RWKV_COLS),
        'k_k': rwkv_kk[e][None, :], 'k_a': rwkv_ka[e][None, :],
        'r_k': rwkv_rk[e].reshape(1, D_RWKV),
        'w0': rwkv_w0[e], 'w2': small_rows(rwkv_w2[e], 0),
        'a0': rwkv_a0[e], 'a2': small_rows(rwkv_a2[e], W_LORA),
        'g2': small_rows(rwkv_g2[e], W_LORA + A_LORA),
        'gn_w': rwkv_gn_w[e][None, :], 'gn_b': rwkv_gn_b[e][None, :],
        'q_norm': mla_q_norm[e][None, :], 'kv_norm': mla_kv_norm[e][None, :],
        'w_qb': wq.astype(BF16), 'w_kvb': mla_w_kvb[e].astype(BF16),
        'w_out': w_out_even[e].astype(BF16),
        'ones_blk': (blk[:, None] == blk[None, :]).astype(BF16),
        'eye': (jnp.arange(RWKV_HEAD)[:, None] == (jnp.arange(MXU_DIM) % RWKV_HEAD)[None, :]).astype(F32),
    }


def _rope_tables(L):
    rows = L // GRID_W
    row = jnp.repeat(jnp.arange(rows, dtype=F32), GRID_W)
    col = jnp.tile(jnp.arange(GRID_W, dtype=F32), rows)
    half = QK_ROPE // 2
    inv = 1.0 / (ROPE_THETA ** (jnp.arange(0, half, 2, dtype=F32) / half))
    ar, ac = row[:, None] * inv[None, :], col[:, None] * inv[None, :]
    cos_t = jnp.concatenate([jnp.cos(ar), jnp.cos(ar), jnp.cos(ac), jnp.cos(ac)], axis=-1)
    sin_t = jnp.concatenate([-jnp.sin(ar), jnp.sin(ar), -jnp.sin(ac), jnp.sin(ac)], axis=-1)
    return _pad_cols(cos_t, LANE), _pad_cols(sin_t, LANE)


def _state_to_groups(s):
    B = s.shape[0]
    s = s.reshape(B, 2, N_GRP, H_RWKV // N_GRP, RWKV_HEAD, RWKV_HEAD)
    return jnp.swapaxes(s, 3, 4).reshape(B, 2, N_GRP, RWKV_HEAD, MXU_DIM)


def _groups_to_state(s):
    B = s.shape[0]
    s = s.reshape(B, 2, N_GRP, RWKV_HEAD, H_RWKV // N_GRP, RWKV_HEAD)
    return jnp.swapaxes(s, 3, 4).reshape(B, 2, H_RWKV, RWKV_HEAD, RWKV_HEAD)


def _even_mixer(x, mod_l, goff, B, L, gamma, p, rope, ctx):
    group_tokens = x.shape[0] if ctx is None else L
    z_r = _inproj_call(x, mod_l, goff, group_tokens, gamma, p['w_r'], RWKV_COLS // 3)
    z_m = _inproj_call(x, mod_l, goff, group_tokens, gamma, p['w_m'], MLA_COLS)
    names = ('r', 'v', 'c', 'w0', 'b0', 'k0', 'w1', 'b1', 'k1', 'bonus', 'g')
    pre = dict(zip(names, _rwkv_prep_call(z_r, L, p)))
    seq = {n: pre[n].reshape(B, L, D_RWKV) for n in names[:9]}
    if ctx is None:
        s0 = jnp.zeros((B, 2, N_GRP, RWKV_HEAD, MXU_DIM), F32)
    else:
        s0 = _state_to_groups(ctx[2].astype(F32))
    y0, y1, s_fin = _scan_call(seq, s0, p['ones_blk'], p['eye'])
    y_r = _rwkv_post_call(y0.reshape(B * L, D_RWKV), y1.reshape(B * L, D_RWKV), pre['bonus'], pre['g'],
                          p['gn_w'], p['gn_b'], p['ones_blk'])

    q, k, v, ckv, krp = _mla_prep_call(z_m, rope[0], rope[1], L, p)
    q = q.reshape(B, L, H_MLA * HEAD_SLOT)
    k = k.reshape(B, L, H_MLA * HEAD_SLOT)
    v = v.reshape(B, L, H_MLA * V_HEAD)
    if ctx is not None:
        P = ctx[0].shape[1]
        k_ctx, v_ctx = _ctx_kv_call(ctx[0].reshape(B * P, KV_RANK),
                                    _pad_cols(ctx[1].reshape(B * P, QK_ROPE), LANE), p['w_kvb'])
        k = jnp.concatenate([k, k_ctx.reshape(B, P, H_MLA * HEAD_SLOT)], axis=1)
        v = jnp.concatenate([v, v_ctx.reshape(B, P, H_MLA * V_HEAD)], axis=1)
    y_m = _attn_call(q, k, v).reshape(B * L, H_MLA * V_HEAD)
    x = _outproj_call(x, mod_l, goff, group_tokens, y_r, y_m, p['w_out'], False)
    state = (_groups_to_state(s_fin), ckv.reshape(B, L, KV_RANK), krp[:, :QK_ROPE].reshape(B, L, QK_ROPE))
    return x, state


def _odd_mixer(x, mod_l, goff, group_tokens, B, L, gamma, tables, k_spec, p):
    z = _inproj_call(x, mod_l, goff, group_tokens, gamma, p['w_in'], 1536)
    y = _hyena_mixer(z, B, L, tables, k_spec, p)
    return _outproj_call(x, mod_l, goff, group_tokens, y, y, p['w_out'], True)


def kernel(x_prompt, x_sample, cache_mla_ckv, cache_mla_krope, state_rwkv, c, c_ctx,
           w_mod, b_mod, norm_g, w_ffn_in, w_ffn_out, final_norm_g,
           w_in_even, mu_prev, mu_next, rwkv_w0, rwkv_w2, rwkv_a0, rwkv_a2, rwkv_g2,
           rwkv_kk, rwkv_ka, rwkv_rk, rwkv_gn_w, rwkv_gn_b,
           mla_q_norm, mla_kv_norm, mla_w_qb, mla_w_kvb, w_out_even,
           w_in_odd, hy_conv_w, hy_conv_b, hy_filt_w1, hy_filt_b1, hy_filt_w2, hy_filt_b2,
           hy_filt_w3, hy_filt_freq, hy_bias, w_out_odd):
    Bp, Lp, D = x_prompt.shape
    Bs, Ls, _ = x_sample.shape
    depth = w_mod.shape[0]
    xp = x_prompt.reshape(Bp * Lp, D)
    xs = x_sample.reshape(Bs * Ls, D)
    Tp = Bp * Lp

    cvec = jnp.concatenate([c_ctx[None, :], c, jnp.zeros((SUBLANE - 1 - Bs, D), F32)], axis=0)
    mod = _mod_call(cvec, w_mod, b_mod)

    rope_p = (_pad_cols(jnp.ones((Lp, QK_ROPE), F32), LANE), jnp.zeros((Lp, LANE), F32))
    rope_s = _rope_tables(Ls)
    tabs_p = tabs_s = None
    new_ckv, new_kr, new_s = [], [], []
    for l in range(depth):
        mod_l = mod[l]
        gam = [norm_g[l, s][None, :] for s in range(3)]
        w_in = [w_ffn_in[l, s].astype(BF16) for s in range(2)]
        w_out = [w_ffn_out[l, s].astype(BF16) for s in range(2)]
        xp = _ffn_call(xp, mod_l, 0, Tp, gam[0], w_in[0], w_out[0], 0)
        xs = _ffn_call(xs, mod_l, 1, Ls, gam[0], w_in[0], w_out[0], 0)
        if l % 2 == 0:
            e = l // 2
            p = _pack_even(e, w_in_even, mu_prev, mu_next, rwkv_w0, rwkv_w2, rwkv_a0, rwkv_a2, rwkv_g2,
                           rwkv_kk, rwkv_ka, rwkv_rk, rwkv_gn_w, rwkv_gn_b, mla_q_norm, mla_kv_norm,
                           mla_w_qb, mla_w_kvb, w_out_even)
            ctx = (cache_mla_ckv[:, e], cache_mla_krope[:, e], state_rwkv[:, e])
            xp, st = _even_mixer(xp, mod_l, 0, Bp, Lp, gam[1], p, rope_p, None)
            xs, _ = _even_mixer(xs, mod_l, 1, Bs, Ls, gam[1], p, rope_s, ctx)
            new_s.append(st[0].astype(x_prompt.dtype))
            new_ckv.append(st[1])
            new_kr.append(st[2])
        else:
            o = l // 2
            p = {'w_in': w_in_odd[o].astype(BF16), 'conv_w': hy_conv_w[o], 'conv_b': hy_conv_b[o][None, :],
                 'filt_w1': jnp.pad(hy_filt_w1[o], ((0, POS_PAD - POS_EMB), (0, 0))),
                 'filt_b1': hy_filt_b1[o][None, :], 'filt_w2': hy_filt_w2[o],
                 'filt_b2': hy_filt_b2[o][None, :], 'filt_w3': hy_filt_w3[o],
                 'filt_freq': hy_filt_freq[o], 'bias': hy_bias[o], 'w_out': w_out_odd[o].astype(BF16)}
            if tabs_p is None:
                tabs_p, tabs_s = _dft_tables(Lp), _dft_tables(Ls)
            ks_p = _hyena_filter_spectrum(Lp, tabs_p[0], p)
            ks_s = _hyena_filter_spectrum(Ls, tabs_s[0], p)
            xp = _odd_mixer(xp, mod_l, 0, Tp, Bp, Lp, gam[1], tabs_p, ks_p, p)
            xs = _odd_mixer(xs, mod_l, 1, Ls, Bs, Ls, gam[1], tabs_s, ks_s, p)
        xp = _ffn_call(xp, mod_l, 0, Tp, gam[2], w_in[1], w_out[1], 2)
        xs = _ffn_call(xs, mod_l, 1, Ls, gam[2], w_in[1], w_out[1], 2)

    fg = final_norm_g[None, :]
    y_prompt = _final_norm_call(xp, fg).reshape(Bp, Lp, D)
    y_sample = _final_norm_call(xs, fg).reshape(Bs, Ls, D)
    return (y_prompt, y_sample, jnp.stack(new_ckv, axis=1), jnp.stack(new_kr, axis=1),
            jnp.stack(new_s, axis=1))
```

```python
import functools
import math

import jax
import jax.numpy as jnp
from jax import lax
from jax.experimental import pallas as pl
from jax.experimental.pallas import tpu as pltpu

F32 = jnp.float32
BF16 = jnp.bfloat16

D_MODEL = 2048
N_MOD = 9
D_FF = 5632
D_RWKV = 1024
RWKV_HEAD = 64
H_RWKV = 16
W_LORA = 64
A_LORA = 64
G_LORA = 160
RWKV_SMALL = 384
RWKV_COLS = 3 * D_RWKV + RWKV_SMALL
RWKV_GN_EPS = 64e-5
H_MLA = 8
QK_NOPE = 128
QK_ROPE = 64
V_HEAD = 128
Q_RANK = 512
KV_RANK = 256
MLA_COLS = Q_RANK + KV_RANK + 2 * QK_ROPE
HEAD_SLOT = 256
ROPE_THETA = 10000.0
GRID_W = 64
D_HYENA = 2048
POS_EMB = 33
POS_PAD = 128
FILT_HIDDEN = 64
HYENA_TARGET = 1e-2
FAST_DECAY_PCT = 0.3
SLOW_DECAY_PCT = 1.5
LANE = 128
SUBLANE = 8
MXU_DIM = 256
MIB = 1024 * 1024


def _params(sem, vmem_mib):
    return pltpu.CompilerParams(dimension_semantics=sem, vmem_limit_bytes=vmem_mib * MIB)


def _sigmoid(x):
    return 1.0 / (1.0 + jnp.exp(-x))


def _softplus(x):
    return jnp.maximum(x, 0.0) + jnp.log(1.0 + jnp.exp(-jnp.abs(x)))


def _dot(a, b):
    return jnp.dot(a, b, preferred_element_type=F32)


def _norm_mod(x, gamma, shift, scale):
    xn = x * lax.rsqrt(jnp.mean(x * x, axis=-1, keepdims=True) + 1e-6)
    return (xn * gamma) * (1.0 + scale) + shift


def _mod_kernel(c_ref, w_ref, b_ref, o_ref):
    c = c_ref[...]
    s = c * _sigmoid(c)
    o_ref[0] = _dot(s.astype(BF16), w_ref[0].astype(BF16)) + b_ref[0]


def _mod_call(cvec, w_mod, b_mod):
    L, Dm, N = w_mod.shape
    tn = 1024
    out = pl.pallas_call(
        _mod_kernel,
        grid=(L, N // tn),
        in_specs=[pl.BlockSpec((SUBLANE, Dm), lambda l, j: (0, 0)),
                  pl.BlockSpec((1, Dm, tn), lambda l, j: (l, 0, j)),
                  pl.BlockSpec((1, 1, tn), lambda l, j: (l, 0, j))],
        out_specs=pl.BlockSpec((1, SUBLANE, tn), lambda l, j: (l, 0, j)),
        out_shape=jax.ShapeDtypeStruct((L, SUBLANE, N), F32),
        compiler_params=_params(("arbitrary", "arbitrary"), 40),
        name="adaln_mod",
    )(cvec, w_mod, b_mod.reshape(L, 1, N))
    return out.reshape(L, SUBLANE, N_MOD, Dm)


def _mod_spec(goff, tiles_per_group, nargs):
    if nargs == 1:
        return pl.BlockSpec((None, N_MOD, D_MODEL), lambda i: (goff + i // tiles_per_group, 0, 0))
    return pl.BlockSpec((None, N_MOD, D_MODEL), lambda i, j: (goff + i // tiles_per_group, 0, 0))


def _ffn_kernel(x_ref, mod_ref, g_ref, wg_ref, wu_ref, wo_ref, o_ref, h_sc, acc_sc, *, sub):
    f = pl.program_id(1)

    @pl.when(f == 0)
    def _():
        h = _norm_mod(x_ref[...], g_ref[...], mod_ref[3 * sub:3 * sub + 1, :],
                      mod_ref[3 * sub + 1:3 * sub + 2, :])
        h_sc[...] = h.astype(BF16)
        acc_sc[...] = jnp.zeros_like(acc_sc)

    h = h_sc[...]
    a = _dot(h, wg_ref[...])
    u = _dot(h, wu_ref[...])
    act = (a * _sigmoid(a)) * u
    acc_sc[...] += _dot(act.astype(BF16), wo_ref[...])

    @pl.when(f == pl.num_programs(1) - 1)
    def _():
        o_ref[...] = x_ref[...] + 0.5 * mod_ref[3 * sub + 2:3 * sub + 3, :] * acc_sc[...]


def _ffn_call(x, mod_l, goff, group_tokens, gamma, w_in, w_out, sub):
    T = x.shape[0]
    tm = min(512, group_tokens)
    tf = 512
    nf = D_FF // tf
    return pl.pallas_call(
        functools.partial(_ffn_kernel, sub=sub),
        grid=(T // tm, nf),
        in_specs=[pl.BlockSpec((tm, D_MODEL), lambda i, f: (i, 0)),
                  _mod_spec(goff, group_tokens // tm, 2),
                  pl.BlockSpec((1, D_MODEL), lambda i, f: (0, 0)),
                  pl.BlockSpec((D_MODEL, tf), lambda i, f: (0, f)),
                  pl.BlockSpec((D_MODEL, tf), lambda i, f: (0, f + nf)),
                  pl.BlockSpec((tf, D_MODEL), lambda i, f: (f, 0))],
        out_specs=pl.BlockSpec((tm, D_MODEL), lambda i, f: (i, 0)),
        out_shape=jax.ShapeDtypeStruct((T, D_MODEL), F32),
        scratch_shapes=[pltpu.VMEM((tm, D_MODEL), BF16), pltpu.VMEM((tm, D_MODEL), F32)],
        compiler_params=_params(("parallel", "arbitrary"), 52),
        name="ffn_swiglu",
    )(x, mod_l, gamma, w_in, w_in, w_out)


def _inproj_kernel(x_ref, mod_ref, g_ref, w_ref, o_ref, h_sc):
    @pl.when(pl.program_id(1) == 0)
    def _():
        h = _norm_mod(x_ref[...], g_ref[...], mod_ref[3:4, :], mod_ref[4:5, :])
        h_sc[...] = h.astype(BF16)

    o_ref[...] = _dot(h_sc[...], w_ref[...])


def _inproj_call(x, mod_l, goff, group_tokens, gamma, w, tn):
    T = x.shape[0]
    N = w.shape[1]
    tm = min(512, group_tokens)
    return pl.pallas_call(
        _inproj_kernel,
        grid=(T // tm, N // tn),
        in_specs=[pl.BlockSpec((tm, D_MODEL), lambda i, j: (i, 0)),
                  _mod_spec(goff, group_tokens // tm, 2),
                  pl.BlockSpec((1, D_MODEL), lambda i, j: (0, 0)),
                  pl.BlockSpec((D_MODEL, tn), lambda i, j: (0, j))],
        out_specs=pl.BlockSpec((tm, tn), lambda i, j: (i, j)),
        out_shape=jax.ShapeDtypeStruct((T, N), F32),
        scratch_shapes=[pltpu.VMEM((tm, D_MODEL), BF16)],
        compiler_params=_params(("parallel", "arbitrary"), 48),
        name="mixer_inproj",
    )(x, mod_l, gamma, w)


def _outproj_kernel(x_ref, mod_ref, a1_ref, a2_ref, w1_ref, w2_ref, o_ref):
    y = _dot(a1_ref[...], w1_ref[...]) + _dot(a2_ref[...], w2_ref[...])
    o_ref[...] = x_ref[...] + mod_ref[5:6, :] * y


def _outproj_call(x, mod_l, goff, group_tokens, a1, a2, w, split_a):
    T = x.shape[0]
    tm = min(512, group_tokens)
    half = D_MODEL // 2
    if split_a:
        a_specs = [pl.BlockSpec((tm, half), lambda i: (i, 0)), pl.BlockSpec((tm, half), lambda i: (i, 1))]
    else:
        a_specs = [pl.BlockSpec((tm, half), lambda i: (i, 0)), pl.BlockSpec((tm, half), lambda i: (i, 0))]
    return pl.pallas_call(
        _outproj_kernel,
        grid=(T // tm,),
        in_specs=[pl.BlockSpec((tm, D_MODEL), lambda i: (i, 0)),
                  _mod_spec(goff, group_tokens // tm, 1)] + a_specs +
                 [pl.BlockSpec((half, D_MODEL), lambda i: (0, 0)),
                  pl.BlockSpec((half, D_MODEL), lambda i: (1, 0))],
        out_specs=pl.BlockSpec((tm, D_MODEL), lambda i: (i, 0)),
        out_shape=jax.ShapeDtypeStruct((T, D_MODEL), F32),
        compiler_params=_params(("parallel",), 48),
        name="mixer_outproj",
    )(x, mod_l, a1, a2, w, w)


def _final_norm_kernel(x_ref, g_ref, o_ref):
    x = x_ref[...]
    o_ref[...] = (x * lax.rsqrt(jnp.mean(x * x, axis=-1, keepdims=True) + 1e-6)) * g_ref[...]


def _final_norm_call(x, gamma):
    T = x.shape[0]
    tm = 512
    return pl.pallas_call(
        _final_norm_kernel,
        grid=(T // tm,),
        in_specs=[pl.BlockSpec((tm, D_MODEL), lambda i: (i, 0)),
                  pl.BlockSpec((1, D_MODEL), lambda i: (0, 0))],
        out_specs=pl.BlockSpec((tm, D_MODEL), lambda i: (i, 0)),
        out_shape=jax.ShapeDtypeStruct((T, D_MODEL), F32),
        compiler_params=_params(("parallel",), 32),
        name="final_norm",
    )(x, gamma)


def _shift_prev_next(cur, halo_prev, halo_next, row0, seq_len):
    tt = cur.shape[0]
    rid = lax.broadcasted_iota(jnp.int32, (tt, 1), 0)
    pos = jnp.bitwise_and(rid + row0, seq_len - 1)
    prev = pltpu.roll(cur, 1, 0)
    prev = jnp.where(rid == 0, halo_prev[SUBLANE - 1:SUBLANE, :], prev)
    prev = jnp.where(pos == 0, 0.0, prev)
    nxt = pltpu.roll(cur, tt - 1, 0)
    nxt = jnp.where(rid == tt - 1, halo_next[0:1, :], nxt)
    nxt = jnp.where(pos == seq_len - 1, 0.0, nxt)
    return prev, nxt


def _halo_specs(tt, width, col, total_rows):
    per = tt // SUBLANE
    last = total_rows // SUBLANE - 1
    return [pl.BlockSpec((tt, width), lambda i: (i, col)),
            pl.BlockSpec((SUBLANE, width), lambda i: (jnp.maximum(i * per - 1, 0), col)),
            pl.BlockSpec((SUBLANE, width), lambda i: (jnp.minimum((i + 1) * per, last), col))]


def _segsum(x, ones_blk):
    hi = x.astype(BF16)
    lo = (x - hi.astype(F32)).astype(BF16)
    outs = []
    for g in range(x.shape[1] // MXU_DIM):
        sl = slice(g * MXU_DIM, (g + 1) * MXU_DIM)
        outs.append(_dot(hi[:, sl], ones_blk) + _dot(lo[:, sl], ones_blk))
    return jnp.concatenate(outs, axis=1)


def _rwkv_prep_kernel(z_ref, zp_ref, zn_ref, mup_ref, mun_ref, kk_ref, ka_ref, rk_ref,
                      w0_ref, w2_ref, a0_ref, a2_ref, g2_ref, ones_ref,
                      r_o, v_o, c_o, w0_o, b0_o, k0_o, w1_o, b1_o, k1_o, bonus_o, g_o,
                      *, tt, seq_len):
    row0 = pl.program_id(0) * tt
    cur = z_ref[...]
    prev, nxt = _shift_prev_next(cur, zp_ref[...], zn_ref[...], row0, seq_len)
    zs = cur + mup_ref[...] * (prev - cur) + mun_ref[...] * (nxt - cur)
    r = zs[:, 0:D_RWKV]
    k = zs[:, D_RWKV:2 * D_RWKV]
    v = zs[:, 2 * D_RWKV:3 * D_RWKV]
    small = zs[:, 3 * D_RWKV:RWKV_COLS]
    ones_blk = ones_ref[...]

    kk = k * kk_ref[...]
    kk = kk / jnp.maximum(jnp.sqrt(_segsum(kk * kk, ones_blk)), 1e-12)
    tw = jnp.tanh(small).astype(BF16)
    sg = _sigmoid(small).astype(BF16)
    xs = small.astype(BF16)
    r_o[...] = r
    v_o[...] = v
    c_o[...] = -kk
    g_o[...] = _dot(sg, g2_ref[...])

    bonus = jnp.zeros_like(r)
    outs = ((w0_o, b0_o, k0_o), (w1_o, b1_o, k1_o))
    for d in range(2):
        wl = -_softplus(-(w0_ref[d:d + 1, :] + _dot(tw, w2_ref[d]))) - 0.5
        a = _sigmoid(a0_ref[d:d + 1, :] + _dot(xs, a2_ref[d]))
        kd = k * (1.0 + (a - 1.0) * ka_ref[...])
        w_o, b_o, k_o = outs[d]
        w_o[...] = jnp.exp(-jnp.exp(wl))
        b_o[...] = kk * a
        k_o[...] = kd
        bonus = bonus + _segsum(r * kd * rk_ref[...], ones_blk) * v
    bonus_o[...] = bonus


def _rwkv_prep_call(z_r, seq_len, p):
    T = z_r.shape[0]
    tt = min(256, seq_len)
    row = lambda n: pl.BlockSpec((1, n), lambda i: (0, 0))
    full2 = lambda a, b: pl.BlockSpec((a, b), lambda i: (0, 0))
    full3 = lambda a, b, c: pl.BlockSpec((a, b, c), lambda i: (0, 0, 0))
    in_specs = _halo_specs(tt, RWKV_COLS, 0, T) + [
        row(RWKV_COLS), row(RWKV_COLS), row(D_RWKV), row(D_RWKV), row(D_RWKV),
        full2(2, D_RWKV), full3(2, RWKV_SMALL, D_RWKV), full2(2, D_RWKV), full3(2, RWKV_SMALL, D_RWKV),
        full2(RWKV_SMALL, D_RWKV), full2(MXU_DIM, MXU_DIM)]
    out_spec = pl.BlockSpec((tt, D_RWKV), lambda i: (i, 0))
    out_shape = jax.ShapeDtypeStruct((T, D_RWKV), F32)
    return pl.pallas_call(
        functools.partial(_rwkv_prep_kernel, tt=tt, seq_len=seq_len),
        grid=(T // tt,),
        in_specs=in_specs,
        out_specs=[out_spec] * 11,
        out_shape=[out_shape] * 11,
        compiler_params=_params(("parallel",), 52),
        name="rwkv_prep",
    )(z_r, z_r, z_r, p['mu_prev'], p['mu_next'], p['k_k'], p['k_a'], p['r_k'],
      p['w0'], p['w2'], p['a0'], p['a2'], p['g2'], p['ones_blk'])


N_GRP = D_RWKV // MXU_DIM
HEADS_PER_GRP = H_RWKV // N_GRP


SCAN_ROWS = 16


def _scan_head_masks():
    row = jnp.arange(SCAN_ROWS)[None, :, None]
    head = (jnp.arange(MXU_DIM) // RWKV_HEAD)[None, None, :]
    s = jnp.arange(3)[:, None, None]
    return (row == HEADS_PER_GRP * s + head).astype(F32)


def _scan_kernel(rf, vf, cf, wf, bf, kf, rb, vb, cb, wb, bb, kb, s0_ref, eye_ref, hm_ref,
                 y0_ref, y1_ref, sfin_ref, st, *, tc):
    j = pl.program_id(1)

    @pl.when(j == 0)
    def _():
        st[...] = s0_ref[...]

    eye = eye_ref[...]
    m0 = hm_ref[0]
    m1 = hm_ref[1]
    m2 = hm_ref[2]
    dirs = ((rf, vf, cf, wf, bf, kf, y0_ref), (rb, vb, cb, wb, bb, kb, y1_ref))
    nt = (((1,), (1,)), ((), ()))

    def reduce_heads(m, crow, vrow, rrow):
        wr = jnp.concatenate([crow * m0 + rrow * m2, vrow * m1], axis=1).astype(BF16)
        lhs = jnp.concatenate([m.astype(BF16), eye], axis=1)
        return lax.dot_general(lhs, wr, nt, preferred_element_type=F32)

    def store_y(y_, b, t, g, red):
        tr = red.T
        y_[b, pl.ds(t, 1), g * HEADS_PER_GRP:(g + 1) * HEADS_PER_GRP, :] = (
            tr[2 * HEADS_PER_GRP:3 * HEADS_PER_GRP, :].reshape(1, HEADS_PER_GRP, RWKV_HEAD))

    def body(i, carry):
        chains = []
        for b in range(2):
            chains.append((b, 0, i, jnp.maximum(i - 1, 0)))
            chains.append((b, 1, tc - 1 - i, jnp.minimum(tc - i, tc - 1)))
        reds = []
        for (b, d, t, tp) in chains:
            r_, v_, c_, w_, b_, k_, y_ = dirs[d]
            crow = c_[b, pl.ds(t, 1), :]
            vrow = v_[b, pl.ds(t, 1), :]
            rrow = r_[b, pl.ds(tp, 1), :]
            for g in range(N_GRP):
                sl = slice(g * MXU_DIM, (g + 1) * MXU_DIM)
                reds.append(reduce_heads(st[b, d, g], crow[:, sl], vrow[:, sl], rrow[:, sl]))
        idx = 0
        for (b, d, t, tp) in chains:
            r_, v_, c_, w_, b_, k_, y_ = dirs[d]
            wrow = w_[b, pl.ds(t, 1), :]
            brow = b_[b, pl.ds(t, 1), :]
            krow = k_[b, pl.ds(t, 1), :]
            for g in range(N_GRP):
                sl = slice(g * MXU_DIM, (g + 1) * MXU_DIM)
                red = reds[idx]
                w2 = (brow[:, sl] * m0 + krow[:, sl] * m1).astype(BF16)
                st[b, d, g] = st[b, d, g] * wrow[:, sl] + _dot(red.astype(BF16), w2)
                store_y(y_, b, tp, g, red)
                idx += 1
        return carry

    lax.fori_loop(0, tc, body, 0)

    for b in range(2):
        for d in range(2):
            t = tc - 1 if d == 0 else 0
            rrow = dirs[d][0][b, pl.ds(t, 1), :]
            for g in range(N_GRP):
                sl = slice(g * MXU_DIM, (g + 1) * MXU_DIM)
                red = reduce_heads(st[b, d, g], rrow[:, sl], rrow[:, sl], rrow[:, sl])
                store_y(dirs[d][6], b, t, g, red)

    @pl.when(j == pl.num_programs(1) - 1)
    def _():
        sfin_ref[...] = st[...]


def _scan_call(pre, s0, eye):
    B, T, _ = pre['r'].shape
    tc = min(128, T)
    nj = T // tc
    fwd = pl.BlockSpec((2, tc, D_RWKV), lambda bi, j: (bi, j, 0))
    bwd = pl.BlockSpec((2, tc, D_RWKV), lambda bi, j: (bi, nj - 1 - j, 0))
    yfwd = pl.BlockSpec((2, tc, H_RWKV, RWKV_HEAD), lambda bi, j: (bi, j, 0, 0))
    ybwd = pl.BlockSpec((2, tc, H_RWKV, RWKV_HEAD), lambda bi, j: (bi, nj - 1 - j, 0, 0))
    st_spec = pl.BlockSpec((2, 2, N_GRP, RWKV_HEAD, MXU_DIM), lambda bi, j: (bi, 0, 0, 0, 0))
    y_shape = jax.ShapeDtypeStruct((B, T, H_RWKV, RWKV_HEAD), F32)
    return pl.pallas_call(
        functools.partial(_scan_kernel, tc=tc),
        grid=(B // 2, nj),
        in_specs=[fwd] * 6 + [bwd] * 6 + [
            st_spec,
            pl.BlockSpec((RWKV_HEAD, MXU_DIM), lambda bi, j: (0, 0)),
            pl.BlockSpec((3, SCAN_ROWS, MXU_DIM), lambda bi, j: (0, 0, 0))],
        out_specs=[yfwd, ybwd, st_spec],
        out_shape=[y_shape, y_shape, jax.ShapeDtypeStruct((B, 2, N_GRP, RWKV_HEAD, MXU_DIM), F32)],
        scratch_shapes=[pltpu.VMEM((2, 2, N_GRP, RWKV_HEAD, MXU_DIM), F32)],
        compiler_params=_params(("arbitrary", "arbitrary"), 48),
        name="rwkv_scan",
    )(pre['r'], pre['v'], pre['c'], pre['w0'], pre['b0'], pre['k0'],
      pre['r'], pre['v'], pre['c'], pre['w1'], pre['b1'], pre['k1'], s0, eye.astype(BF16),
      _scan_head_masks())


def _rwkv_post_kernel(y0_ref, y1_ref, bonus_ref, g_ref, gw_ref, gb_ref, ones_ref, o_ref):
    ones_blk = ones_ref[...]
    y = y0_ref[...] + y1_ref[...]
    mu = _segsum(y, ones_blk) * (1.0 / RWKV_HEAD)
    yc = y - mu
    var = _segsum(yc * yc, ones_blk) * (1.0 / RWKV_HEAD)
    yn = yc * lax.rsqrt(var + RWKV_GN_EPS)
    out = (yn * gw_ref[...] + gb_ref[...] + bonus_ref[...]) * g_ref[...]
    o_ref[...] = out.astype(BF16)


def _rwkv_post_call(y0, y1, bonus, g, gn_w, gn_b, ones_blk):
    T = y0.shape[0]
    tt = 512
    blk = pl.BlockSpec((tt, D_RWKV), lambda i: (i, 0))
    row = pl.BlockSpec((1, D_RWKV), lambda i: (0, 0))
    return pl.pallas_call(
        _rwkv_post_kernel,
        grid=(T // tt,),
        in_specs=[blk, blk, blk, blk, row, row, pl.BlockSpec((MXU_DIM, MXU_DIM), lambda i: (0, 0))],
        out_specs=blk,
        out_shape=jax.ShapeDtypeStruct((T, D_RWKV), BF16),
        compiler_params=_params(("parallel",), 40),
        name="rwkv_post",
    )(y0, y1, bonus, g, gn_w, gn_b, ones_blk)


def _rope128(x, cos_t, sin_t):
    return x * cos_t + pltpu.roll(x, QK_ROPE, 1) * sin_t


def _pack_kv(kv, kr_rot, k_o, v_o):
    for h in range(H_MLA):
        k_o[:, h * HEAD_SLOT:h * HEAD_SLOT + QK_NOPE] = kv[:, h * HEAD_SLOT:h * HEAD_SLOT + QK_NOPE].astype(BF16)
        k_o[:, h * HEAD_SLOT + QK_NOPE:(h + 1) * HEAD_SLOT] = kr_rot.astype(BF16)
        v_o[:, h * V_HEAD:(h + 1) * V_HEAD] = kv[:, h * HEAD_SLOT + QK_NOPE:(h + 1) * HEAD_SLOT].astype(BF16)


def _mla_prep_kernel(z_ref, cos_ref, sin_ref, qn_ref, kvn_ref, wq_ref, wkv_ref,
                     q_o, k_o, v_o, ckv_o, kr_o):
    z = z_ref[...]
    cq = z[:, 0:Q_RANK]
    ckv = z[:, Q_RANK:Q_RANK + KV_RANK]
    krp = z[:, Q_RANK + KV_RANK:MLA_COLS]
    cos_t = cos_ref[...]
    sin_t = sin_ref[...]
    cq = (cq * lax.rsqrt(jnp.mean(cq * cq, axis=-1, keepdims=True) + 1e-6)) * qn_ref[...]
    ckv = (ckv * lax.rsqrt(jnp.mean(ckv * ckv, axis=-1, keepdims=True) + 1e-6)) * kvn_ref[...]
    ckv_o[...] = ckv
    kr_o[...] = krp
    q = _dot(cq.astype(BF16), wq_ref[...])
    for h in range(H_MLA):
        q_o[:, h * HEAD_SLOT:h * HEAD_SLOT + QK_NOPE] = q[:, h * HEAD_SLOT:h * HEAD_SLOT + QK_NOPE].astype(BF16)
        q_o[:, h * HEAD_SLOT + QK_NOPE:(h + 1) * HEAD_SLOT] = _rope128(
            q[:, h * HEAD_SLOT + QK_NOPE:(h + 1) * HEAD_SLOT], cos_t, sin_t).astype(BF16)
    kv = _dot(ckv.astype(BF16), wkv_ref[...])
    _pack_kv(kv, _rope128(krp, cos_t, sin_t), k_o, v_o)


def _mla_prep_call(z_m, cos_t, sin_t, seq_len, p):
    T = z_m.shape[0]
    tm = min(512, seq_len)
    per_seq = seq_len // tm
    row = lambda n: pl.BlockSpec((1, n), lambda i: (0, 0))
    blk = lambda n: pl.BlockSpec((tm, n), lambda i: (i, 0))
    tab = pl.BlockSpec((tm, LANE), lambda i: (i % per_seq, 0))
    return pl.pallas_call(
        _mla_prep_kernel,
        grid=(T // tm,),
        in_specs=[blk(MLA_COLS), tab, tab, row(Q_RANK), row(KV_RANK),
                  pl.BlockSpec((Q_RANK, H_MLA * HEAD_SLOT), lambda i: (0, 0)),
                  pl.BlockSpec((KV_RANK, H_MLA * HEAD_SLOT), lambda i: (0, 0))],
        out_specs=[blk(H_MLA * HEAD_SLOT), blk(H_MLA * HEAD_SLOT), blk(H_MLA * V_HEAD),
                   blk(KV_RANK), blk(LANE)],
        out_shape=[jax.ShapeDtypeStruct((T, H_MLA * HEAD_SLOT), BF16),
                   jax.ShapeDtypeStruct((T, H_MLA * HEAD_SLOT), BF16),
                   jax.ShapeDtypeStruct((T, H_MLA * V_HEAD), BF16),
                   jax.ShapeDtypeStruct((T, KV_RANK), F32),
                   jax.ShapeDtypeStruct((T, LANE), F32)],
        compiler_params=_params(("parallel",), 48),
        name="mla_prep",
    )(z_m, cos_t, sin_t, p['q_norm'], p['kv_norm'], p['w_qb'], p['w_kvb'])


def _ctx_kv_kernel(ckv_ref, kr_ref, wkv_ref, k_o, v_o):
    kv = _dot(ckv_ref[...].astype(BF16), wkv_ref[...])
    _pack_kv(kv, kr_ref[...], k_o, v_o)


def _ctx_kv_call(ckv_ctx, kr_ctx_pad, w_kvb):
    T = ckv_ctx.shape[0]
    tm = min(512, T)
    blk = lambda n: pl.BlockSpec((tm, n), lambda i: (i, 0))
    return pl.pallas_call(
        _ctx_kv_kernel,
        grid=(T // tm,),
        in_specs=[blk(KV_RANK), blk(LANE), pl.BlockSpec((KV_RANK, H_MLA * HEAD_SLOT), lambda i: (0, 0))],
        out_specs=[blk(H_MLA * HEAD_SLOT), blk(H_MLA * V_HEAD)],
        out_shape=[jax.ShapeDtypeStruct((T, H_MLA * HEAD_SLOT), BF16),
                   jax.ShapeDtypeStruct((T, H_MLA * V_HEAD), BF16)],
        compiler_params=_params(("parallel",), 32),
        name="mla_ctx_kv",
    )(ckv_ctx, kr_ctx_pad, w_kvb)


def _attn_kernel(q_ref, k_ref, v_ref, o_ref, *, scale):
    s = lax.dot_general(q_ref[...], k_ref[...], (((1,), (1,)), ((), ())),
                        preferred_element_type=F32) * scale
    m = jnp.max(s, axis=-1, keepdims=True)
    p = jnp.exp(s - m)
    l = jnp.sum(p, axis=-1, keepdims=True)
    o = _dot(p.astype(BF16), v_ref[...])
    o_ref[...] = (o / l).astype(BF16)


def _attn_call(q, k, v):
    B, Tq, _ = q.shape
    Tk = k.shape[1]
    tq = min(512, Tq)
    scale = (QK_NOPE + QK_ROPE) ** -0.5
    return pl.pallas_call(
        functools.partial(_attn_kernel, scale=scale),
        grid=(B, H_MLA, Tq // tq),
        in_specs=[pl.BlockSpec((None, tq, HEAD_SLOT), lambda b, h, i: (b, i, h)),
                  pl.BlockSpec((None, Tk, HEAD_SLOT), lambda b, h, i: (b, 0, h)),
                  pl.BlockSpec((None, Tk, V_HEAD), lambda b, h, i: (b, 0, h))],
        out_specs=pl.BlockSpec((None, tq, V_HEAD), lambda b, h, i: (b, i, h)),
        out_shape=jax.ShapeDtypeStruct((B, Tq, H_MLA * V_HEAD), BF16),
        compiler_params=_params(("parallel", "parallel", "arbitrary"), 48),
        name="mla_attention",
    )(q, k, v)


def _conv3_kernel(*refs, tt, seq_len):
    ins, (cw_refs, cb_refs), outs = refs[0:9], (refs[9:12], refs[12:15]), refs[15:]
    row0 = pl.program_id(0) * tt
    for s in range(3):
        cur = ins[3 * s][...]
        prev, nxt = _shift_prev_next(cur, ins[3 * s + 1][...], ins[3 * s + 2][...], row0, seq_len)
        cw = cw_refs[s][...]
        y = cw[0:1, :] * prev + cw[1:2, :] * cur + cw[2:3, :] * nxt + cb_refs[s][...]
        outs[s][...] = y
        if s == 2:
            outs[3][...] = y.astype(BF16)


def _conv3_call(z, seq_len, conv_w, conv_b):
    T = z.shape[0]
    tt = min(256, seq_len)
    in_specs = []
    for s in range(3):
        in_specs += _halo_specs(tt, D_HYENA, s, T)
    in_specs += [pl.BlockSpec((3, D_HYENA), lambda i, s=s: (0, s)) for s in range(3)]
    in_specs += [pl.BlockSpec((1, D_HYENA), lambda i, s=s: (0, s)) for s in range(3)]
    blk = pl.BlockSpec((tt, D_HYENA), lambda i: (i, 0))
    f32s = jax.ShapeDtypeStruct((T, D_HYENA), F32)
    return pl.pallas_call(
        functools.partial(_conv3_kernel, tt=tt, seq_len=seq_len),
        grid=(T // tt,),
        in_specs=in_specs,
        out_specs=[blk] * 4,
        out_shape=[f32s, f32s, f32s, jax.ShapeDtypeStruct((T, D_HYENA), BF16)],
        compiler_params=_params(("parallel",), 48),
        name="hyena_conv3",
    )(*([z] * 9), conv_w, conv_w, conv_w, conv_b, conv_b, conv_b)


def _filt_mlp_kernel(z_ref, w1_ref, b1_ref, w2_ref, b2_ref, fr_ref, o_ref):
    h = jnp.sin(fr_ref[0:1, :] * (_dot(z_ref[...].astype(BF16), w1_ref[...].astype(BF16)) + b1_ref[...]))
    h = jnp.sin(fr_ref[1:2, :] * (_dot(h.astype(BF16), w2_ref[...].astype(BF16)) + b2_ref[...]))
    o_ref[...] = h.astype(BF16)


def _filt_mlp_call(zpos, w1p, b1, w2, b2, freq):
    L = zpos.shape[0]
    return pl.pallas_call(
        _filt_mlp_kernel,
        out_shape=jax.ShapeDtypeStruct((L, FILT_HIDDEN), BF16),
        compiler_params=pltpu.CompilerParams(vmem_limit_bytes=32 * MIB),
        name="hyena_filter_mlp",
    )(zpos, w1p, b1, w2, b2, freq)


def _filt_gen_kernel(h_ref, tn_ref, dl_ref, w00, w01, w10, w11, o_ref):
    h = h_ref[...]
    L = h.shape[0]
    win = jnp.exp(-tn_ref[...] * dl_ref[...])
    not_first = lax.broadcasted_iota(jnp.int32, (L, 1), 0) > 0
    ws = ((w00, w01), (w10, w11))
    for n in range(2):
        causal = _dot(h, ws[n][0][...].astype(BF16)) * win
        anti = jnp.where(not_first, _dot(h, ws[n][1][...].astype(BF16)) * win, 0.0)
        norm = (jnp.sum(jnp.abs(causal), axis=0, keepdims=True)
                + jnp.sum(jnp.abs(anti), axis=0, keepdims=True))
        o_ref[2 * n] = (causal / norm).astype(BF16)
        o_ref[2 * n + 1] = (anti / norm).astype(BF16)


def _filt_gen_call(h2, tnorm, deltas, w3):
    L = h2.shape[0]
    tc = 128
    nc = D_HYENA // tc
    wspec = lambda k: pl.BlockSpec((FILT_HIDDEN, tc), lambda j, k=k: (0, k * nc + j))
    return pl.pallas_call(
        _filt_gen_kernel,
        grid=(nc,),
        in_specs=[pl.BlockSpec((L, FILT_HIDDEN), lambda j: (0, 0)),
                  pl.BlockSpec((L, 1), lambda j: (0, 0)),
                  pl.BlockSpec((1, tc), lambda j: (0, j)),
                  wspec(0), wspec(1), wspec(2), wspec(3)],
        out_specs=pl.BlockSpec((4, L, tc), lambda j: (0, 0, j)),
        out_shape=jax.ShapeDtypeStruct((4, L, D_HYENA), BF16),
        compiler_params=_params(("parallel",), 48),
        name="hyena_filter_gen",
    )(h2, tnorm, deltas, w3, w3, w3, w3)


def _dft_fwd_kernel(f_ref, u_ref, o_ref):
    o_ref[...] = _dot(f_ref[...], u_ref[...])


def _dft_fwd_call(fmat, u):
    B, L, C = u.shape
    tm = 512
    tn = 512 if L > 512 else C
    return pl.pallas_call(
        _dft_fwd_kernel,
        grid=(B, C // tn, 2 * L // tm),
        in_specs=[pl.BlockSpec((tm, L), lambda b, j, i: (i, 0)),
                  pl.BlockSpec((None, L, tn), lambda b, j, i: (b, 0, j))],
        out_specs=pl.BlockSpec((None, tm, tn), lambda b, j, i: (b, i, j)),
        out_shape=jax.ShapeDtypeStruct((B, 2 * L, C), F32),
        compiler_params=_params(("parallel", "parallel", "arbitrary"), 40),
        name="hyena_dft_fwd",
    )(fmat, u)


def _spec_mul_kernel(u_ref, k_ref, o_ref, *, tr):
    a = u_ref[0]
    b = u_ref[1]
    ka = k_ref[0, 0] + k_ref[1, 0]
    kb_diff = k_ref[0, 1] - k_ref[1, 1]
    kb_sum = k_ref[0, 1] + k_ref[1, 1]
    first = (lax.broadcasted_iota(jnp.int32, (tr, 1), 0) + pl.program_id(1) * tr) == 0
    kb = jnp.where(first, kb_sum, kb_diff)
    top = jnp.where(first, a * ka, a * ka - b * kb)
    bot = jnp.where(first, b * kb, a * kb + b * ka)
    o_ref[0] = top.astype(BF16)
    o_ref[1] = bot.astype(BF16)


def _spec_mul_call(u_spec, k_spec, order):
    B, L2, C = u_spec.shape
    L = L2 // 2
    tr = min(256, L)
    tc = 512
    out = pl.pallas_call(
        functools.partial(_spec_mul_kernel, tr=tr),
        grid=(B, L // tr, C // tc),
        in_specs=[pl.BlockSpec((None, 2, tr, tc), lambda b, i, j: (b, 0, i, j)),
                  pl.BlockSpec((None, 2, 2, tr, tc), lambda b, i, j: (order, 0, 0, i, j))],
        out_specs=pl.BlockSpec((None, 2, tr, tc), lambda b, i, j: (b, 0, i, j)),
        out_shape=jax.ShapeDtypeStruct((B, 2, L, C), BF16),
        compiler_params=_params(("parallel", "parallel", "parallel"), 40),
        name="hyena_spectral_mul",
    )(u_spec.reshape(B, 2, L, C), k_spec.reshape(2, 2, 2, L, C))
    return out.reshape(B, L2, C)


def _dft_inv_kernel(f_ref, y_ref, gate_ref, u_ref, bias_ref, o_ref, ob_ref):
    conv = _dot(f_ref[...], y_ref[...])
    out = gate_ref[...] * (conv + u_ref[...] * bias_ref[...])
    o_ref[...] = out
    ob_ref[...] = out.astype(BF16)


def _dft_inv_call(imat, y_spec, gate, u, bias):
    B, L, C = u.shape
    tm = min(256, L)
    tn = 512
    blk = pl.BlockSpec((None, tm, tn), lambda b, j, i: (b, i, j))
    return pl.pallas_call(
        _dft_inv_kernel,
        grid=(B, C // tn, L // tm),
        in_specs=[pl.BlockSpec((tm, 2 * L), lambda b, j, i: (i, 0)),
                  pl.BlockSpec((None, 2 * L, tn), lambda b, j, i: (b, 0, j)),
                  blk, blk, pl.BlockSpec((1, tn), lambda b, j, i: (0, j))],
        out_specs=[blk, blk],
        out_shape=[jax.ShapeDtypeStruct((B, L, C), F32), jax.ShapeDtypeStruct((B, L, C), BF16)],
        compiler_params=_params(("parallel", "parallel", "arbitrary"), 48),
        name="hyena_dft_inv",
    )(imat, y_spec, gate, u, bias)


def _dft_tables(L):
    N = 2 * L
    lo = min(64, L)
    hi = L // lo
    f = jnp.arange(L, dtype=jnp.int32)
    theta = 2.0 * math.pi / N
    ang_hi = ((f[:, None] * (jnp.arange(hi, dtype=jnp.int32) * lo)[None, :]) % N).astype(F32) * theta
    ang_lo = ((f[:, None] * jnp.arange(lo, dtype=jnp.int32)[None, :]) % N).astype(F32) * theta
    ch, sh, cl, sl = jnp.cos(ang_hi), jnp.sin(ang_hi), jnp.cos(ang_lo), jnp.sin(ang_lo)
    cos_m = (ch[:, :, None] * cl[:, None, :] - sh[:, :, None] * sl[:, None, :]).reshape(L, L)
    sin_m = (sh[:, :, None] * cl[:, None, :] + ch[:, :, None] * sl[:, None, :]).reshape(L, L)
    nyq = jnp.where(jnp.arange(L) % 2 == 0, 1.0, -1.0).astype(F32)
    sin_m = jnp.where((f == 0)[:, None], nyq[None, :], sin_m)
    fwd = jnp.concatenate([cos_m, sin_m], axis=0)
    wts = jnp.where(f == 0, 1.0 / N, 2.0 / N).astype(F32)
    inv = jnp.concatenate([cos_m * wts[:, None], sin_m * wts[:, None]], axis=0).T
    return fwd.astype(BF16), inv.astype(BF16)


def _filter_positions(L):
    t = jnp.arange(L, dtype=F32)
    t_norm = t / max(L - 1, 1)
    bands = (POS_EMB - 1) // 2
    freqs = jnp.linspace(1e-4, bands - 1, bands, dtype=F32)
    ang = (2.0 * math.pi / L) * t[:, None] * freqs[None, :]
    z = jnp.concatenate([t_norm[:, None], jnp.cos(ang), -jnp.sin(ang)], axis=-1)
    return jnp.pad(z, ((0, 0), (0, POS_PAD - POS_EMB))), t_norm[:, None]


def _hyena_deltas():
    return jnp.linspace(abs(math.log(HYENA_TARGET)) / SLOW_DECAY_PCT,
                        abs(math.log(HYENA_TARGET)) / FAST_DECAY_PCT, D_HYENA, dtype=F32)[None, :]


def _hyena_filter_spectrum(L, fwd, p):
    zpos, tnorm = _filter_positions(L)
    h2 = _filt_mlp_call(zpos, p['filt_w1'], p['filt_b1'], p['filt_w2'], p['filt_b2'], p['filt_freq'])
    filt = _filt_gen_call(h2, tnorm, _hyena_deltas(), p['filt_w3'])
    return _dft_fwd_call(fwd, filt)


def _hyena_mixer(z, B, L, tables, k_spec, p):
    fwd, inv = tables
    x1, x2, v, vb = _conv3_call(z, L, p['conv_w'], p['conv_b'])
    shp = (B, L, D_HYENA)
    u, ub = v.reshape(shp), vb.reshape(shp)
    for n, gate in enumerate((x1, x2)):
        spec = _dft_fwd_call(fwd, ub)
        yspec = _spec_mul_call(spec, k_spec, n)
        u, ub = _dft_inv_call(inv, yspec, gate.reshape(shp), u, p['bias'][n:n + 1])
    return ub.reshape(B * L, D_HYENA)


def _rope_swap(w):
    q = QK_ROPE // 4
    return jnp.concatenate([w[..., q:2 * q], w[..., 0:q], w[..., 3 * q:4 * q], w[..., 2 * q:3 * q]], axis=-1)


def _pad_cols(w, n):
    return jnp.pad(w, [(0, 0)] * (w.ndim - 1) + [(0, n - w.shape[-1])])


def _pack_even(e, w_in_even, mu_prev, mu_next, rwkv_w0, rwkv_w2, rwkv_a0, rwkv_a2, rwkv_g2,
               rwkv_kk, rwkv_ka, rwkv_rk, rwkv_gn_w, rwkv_gn_b, mla_q_norm, mla_kv_norm,
               mla_w_qb, mla_w_kvb, w_out_even):
    n_r = 3 * D_RWKV + W_LORA + A_LORA + G_LORA
    w_in = w_in_even[e]
    w_r = _pad_cols(w_in[:, :n_r], RWKV_COLS).astype(BF16)
    w_m = w_in[:, n_r:]
    kr_cols = w_m[:, Q_RANK + KV_RANK:]
    w_m = jnp.concatenate([w_m, _rope_swap(kr_cols)], axis=-1).astype(BF16)
    small_rows = lambda w, off: jnp.pad(w, [(0, 0)] * (w.ndim - 2)
                                        + [(off, RWKV_SMALL - off - w.shape[-2]), (0, 0)]).astype(BF16)
    wq = mla_w_qb[e].reshape(Q_RANK, H_MLA, QK_NOPE + QK_ROPE)
    wq = jnp.concatenate([wq, _rope_swap(wq[..., QK_NOPE:])], axis=-1).reshape(Q_RANK, H_MLA * HEAD_SLOT)
    blk = jnp.arange(MXU_DIM) // RWKV_HEAD
    return {
        'w_r': w_r, 'w_m': w_m,
        'mu_prev': _pad_cols(mu_prev[e][None, :], RWKV_COLS),
        'mu_next': _pad_cols(mu_next[e][None, :], RWKV_COLS),
        'k_k': rwkv_kk[e][None, :], 'k_a': rwkv_ka[e][None, :],
        'r_k': rwkv_rk[e].reshape(1, D_RWKV),
        'w0': rwkv_w0[e], 'w2': small_rows(rwkv_w2[e], 0),
        'a0': rwkv_a0[e], 'a2': small_rows(rwkv_a2[e], W_LORA),
        'g2': small_rows(rwkv_g2[e], W_LORA + A_LORA),
        'gn_w': rwkv_gn_w[e][None, :], 'gn_b': rwkv_gn_b[e][None, :],
        'q_norm': mla_q_norm[e][None, :], 'kv_norm': mla_kv_norm[e][None, :],
        'w_qb': wq.astype(BF16), 'w_kvb': mla_w_kvb[e].astype(BF16),
        'w_out': w_out_even[e].astype(BF16),
        'ones_blk': (blk[:, None] == blk[None, :]).astype(BF16),
        'eye': (jnp.arange(RWKV_HEAD)[:, None] == (jnp.arange(MXU_DIM) % RWKV_HEAD)[None, :]).astype(F32),
    }


def _rope_tables(L):
    rows = L // GRID_W
    row = jnp.repeat(jnp.arange(rows, dtype=F32), GRID_W)
    col = jnp.tile(jnp.arange(GRID_W, dtype=F32), rows)
    half = QK_ROPE // 2
    inv = 1.0 / (ROPE_THETA ** (jnp.arange(0, half, 2, dtype=F32) / half))
    ar, ac = row[:, None] * inv[None, :], col[:, None] * inv[None, :]
    cos_t = jnp.concatenate([jnp.cos(ar), jnp.cos(ar), jnp.cos(ac), jnp.cos(ac)], axis=-1)
    sin_t = jnp.concatenate([-jnp.sin(ar), jnp.sin(ar), -jnp.sin(ac), jnp.sin(ac)], axis=-1)
    return _pad_cols(cos_t, LANE), _pad_cols(sin_t, LANE)


def _state_to_groups(s):
    B = s.shape[0]
    s = s.reshape(B, 2, N_GRP, H_RWKV // N_GRP, RWKV_HEAD, RWKV_HEAD)
    return jnp.swapaxes(s, 3, 4).reshape(B, 2, N_GRP, RWKV_HEAD, MXU_DIM)


def _groups_to_state(s):
    B = s.shape[0]
    s = s.reshape(B, 2, N_GRP, RWKV_HEAD, H_RWKV // N_GRP, RWKV_HEAD)
    return jnp.swapaxes(s, 3, 4).reshape(B, 2, H_RWKV, RWKV_HEAD, RWKV_HEAD)


def _even_mixer(x, mod_l, goff, B, L, gamma, p, rope, ctx):
    group_tokens = x.shape[0] if ctx is None else L
    z_r = _inproj_call(x, mod_l, goff, group_tokens, gamma, p['w_r'], RWKV_COLS // 3)
    z_m = _inproj_call(x, mod_l, goff, group_tokens, gamma, p['w_m'], MLA_COLS)
    names = ('r', 'v', 'c', 'w0', 'b0', 'k0', 'w1', 'b1', 'k1', 'bonus', 'g')
    pre = dict(zip(names, _rwkv_prep_call(z_r, L, p)))
    seq = {n: pre[n].reshape(B, L, D_RWKV) for n in names[:9]}
    if ctx is None:
        s0 = jnp.zeros((B, 2, N_GRP, RWKV_HEAD, MXU_DIM), F32)
    else:
        s0 = _state_to_groups(ctx[2].astype(F32))
    y0, y1, s_fin = _scan_call(seq, s0, p['eye'])
    y_r = _rwkv_post_call(y0.reshape(B * L, D_RWKV), y1.reshape(B * L, D_RWKV), pre['bonus'], pre['g'],
                          p['gn_w'], p['gn_b'], p['ones_blk'])

    q, k, v, ckv, krp = _mla_prep_call(z_m, rope[0], rope[1], L, p)
    q = q.reshape(B, L, H_MLA * HEAD_SLOT)
    k = k.reshape(B, L, H_MLA * HEAD_SLOT)
    v = v.reshape(B, L, H_MLA * V_HEAD)
    if ctx is not None:
        P = ctx[0].shape[1]
        k_ctx, v_ctx = _ctx_kv_call(ctx[0].reshape(B * P, KV_RANK),
                                    _pad_cols(ctx[1].reshape(B * P, QK_ROPE), LANE), p['w_kvb'])
        k = jnp.concatenate([k, k_ctx.reshape(B, P, H_MLA * HEAD_SLOT)], axis=1)
        v = jnp.concatenate([v, v_ctx.reshape(B, P, H_MLA * V_HEAD)], axis=1)
    y_m = _attn_call(q, k, v).reshape(B * L, H_MLA * V_HEAD)
    x = _outproj_call(x, mod_l, goff, group_tokens, y_r, y_m, p['w_out'], False)
    state = (_groups_to_state(s_fin), ckv.reshape(B, L, KV_RANK), krp[:, :QK_ROPE].reshape(B, L, QK_ROPE))
    return x, state


def _odd_mixer(x, mod_l, goff, group_tokens, B, L, gamma, tables, k_spec, p):
    z = _inproj_call(x, mod_l, goff, group_tokens, gamma, p['w_in'], 1536)
    y = _hyena_mixer(z, B, L, tables, k_spec, p)
    return _outproj_call(x, mod_l, goff, group_tokens, y, y, p['w_out'], True)


def kernel(x_prompt, x_sample, cache_mla_ckv, cache_mla_krope, state_rwkv, c, c_ctx,
           w_mod, b_mod, norm_g, w_ffn_in, w_ffn_out, final_norm_g,
           w_in_even, mu_prev, mu_next, rwkv_w0, rwkv_w2, rwkv_a0, rwkv_a2, rwkv_g2,
           rwkv_kk, rwkv_ka, rwkv_rk, rwkv_gn_w, rwkv_gn_b,
           mla_q_norm, mla_kv_norm, mla_w_qb, mla_w_kvb, w_out_even,
           w_in_odd, hy_conv_w, hy_conv_b, hy_filt_w1, hy_filt_b1, hy_filt_w2, hy_filt_b2,
           hy_filt_w3, hy_filt_freq, hy_bias, w_out_odd):
    Bp, Lp, D = x_prompt.shape
    Bs, Ls, _ = x_sample.shape
    depth = w_mod.shape[0]
    xp = x_prompt.reshape(Bp * Lp, D)
    xs = x_sample.reshape(Bs * Ls, D)
    Tp = Bp * Lp

    cvec = jnp.concatenate([c_ctx[None, :], c, jnp.zeros((SUBLANE - 1 - Bs, D), F32)], axis=0)
    mod = _mod_call(cvec, w_mod, b_mod)

    rope_p = (_pad_cols(jnp.ones((Lp, QK_ROPE), F32), LANE), jnp.zeros((Lp, LANE), F32))
    rope_s = _rope_tables(Ls)
    tabs_p = tabs_s = None
    new_ckv, new_kr, new_s = [], [], []
    for l in range(depth):
        mod_l = mod[l]
        gam = [norm_g[l, s][None, :] for s in range(3)]
        w_in = [w_ffn_in[l, s].astype(BF16) for s in range(2)]
        w_out = [w_ffn_out[l, s].astype(BF16) for s in range(2)]
        xp = _ffn_call(xp, mod_l, 0, Tp, gam[0], w_in[0], w_out[0], 0)
        xs = _ffn_call(xs, mod_l, 1, Ls, gam[0], w_in[0], w_out[0], 0)
        if l % 2 == 0:
            e = l // 2
            p = _pack_even(e, w_in_even, mu_prev, mu_next, rwkv_w0, rwkv_w2, rwkv_a0, rwkv_a2, rwkv_g2,
                           rwkv_kk, rwkv_ka, rwkv_rk, rwkv_gn_w, rwkv_gn_b, mla_q_norm, mla_kv_norm,
                           mla_w_qb, mla_w_kvb, w_out_even)
            ctx = (cache_mla_ckv[:, e], cache_mla_krope[:, e], state_rwkv[:, e])
            xp, st = _even_mixer(xp, mod_l, 0, Bp, Lp, gam[1], p, rope_p, None)
            xs, _ = _even_mixer(xs, mod_l, 1, Bs, Ls, gam[1], p, rope_s, ctx)
            new_s.append(st[0].astype(x_prompt.dtype))
            new_ckv.append(st[1])
            new_kr.append(st[2])
        else:
            o = l // 2
            p = {'w_in': w_in_odd[o].astype(BF16), 'conv_w': hy_conv_w[o], 'conv_b': hy_conv_b[o][None, :],
                 'filt_w1': jnp.pad(hy_filt_w1[o], ((0, POS_PAD - POS_EMB), (0, 0))),
                 'filt_b1': hy_filt_b1[o][None, :], 'filt_w2': hy_filt_w2[o],
                 'filt_b2': hy_filt_b2[o][None, :], 'filt_w3': hy_filt_w3[o],
                 'filt_freq': hy_filt_freq[o], 'bias': hy_bias[o], 'w_out': w_out_odd[o].astype(BF16)}
            if tabs_p is None:
                tabs_p, tabs_s = _dft_tables(Lp), _dft_tables(Ls)
            ks_p = _hyena_filter_spectrum(Lp, tabs_p[0], p)
            ks_s = _hyena_filter_spectrum(Ls, tabs_s[0], p)
            xp = _odd_mixer(xp, mod_l, 0, Tp, Bp, Lp, gam[1], tabs_p, ks_p, p)
            xs = _odd_mixer(xs, mod_l, 1, Ls, Bs, Ls, gam[1], tabs_s, ks_s, p)
        xp = _ffn_call(xp, mod_l, 0, Tp, gam[2], w_in[1], w_out[1], 2)
        xs = _ffn_call(xs, mod_l, 1, Ls, gam[2], w_in[1], w_out[1], 2)

    fg = final_norm_g[None, :]
    y_prompt = _final_norm_call(xp, fg).reshape(Bp, Lp, D)
    y_sample = _final_norm_call(xs, fg).reshape(Bs, Ls, D)
    return (y_prompt, y_sample, jnp.stack(new_ckv, axis=1), jnp.stack(new_kr, axis=1),
            jnp.stack(new_s, axis=1))
```

```python
import functools
import math

import jax
import jax.numpy as jnp
from jax import lax
from jax.experimental import pallas as pl
from jax.experimental.pallas import tpu as pltpu

F32 = jnp.float32
BF16 = jnp.bfloat16

D_MODEL = 2048
N_MOD = 9
D_FF = 5632
D_RWKV = 1024
RWKV_HEAD = 64
H_RWKV = 16
W_LORA = 64
A_LORA = 64
G_LORA = 160
RWKV_SMALL = 384
RWKV_COLS = 3 * D_RWKV + RWKV_SMALL
RWKV_GN_EPS = 64e-5
H_MLA = 8
QK_NOPE = 128
QK_ROPE = 64
V_HEAD = 128
Q_RANK = 512
KV_RANK = 256
MLA_COLS = Q_RANK + KV_RANK + 2 * QK_ROPE
HEAD_SLOT = 256
ROPE_THETA = 10000.0
GRID_W = 64
D_HYENA = 2048
POS_EMB = 33
POS_PAD = 128
FILT_HIDDEN = 64
HYENA_TARGET = 1e-2
FAST_DECAY_PCT = 0.3
SLOW_DECAY_PCT = 1.5
LANE = 128
SUBLANE = 8
MXU_DIM = 256
MIB = 1024 * 1024


def _params(sem, vmem_mib):
    return pltpu.CompilerParams(dimension_semantics=sem, vmem_limit_bytes=vmem_mib * MIB)


def _sigmoid(x):
    return 1.0 / (1.0 + jnp.exp(-x))


def _softplus(x):
    return jnp.maximum(x, 0.0) + jnp.log(1.0 + jnp.exp(-jnp.abs(x)))


def _dot(a, b):
    return jnp.dot(a, b, preferred_element_type=F32)


def _norm_mod(x, gamma, shift, scale):
    xn = x * lax.rsqrt(jnp.mean(x * x, axis=-1, keepdims=True) + 1e-6)
    return (xn * gamma) * (1.0 + scale) + shift


def _mod_kernel(c_ref, w_ref, b_ref, o_ref):
    c = c_ref[...]
    s = c * _sigmoid(c)
    o_ref[0] = _dot(s.astype(BF16), w_ref[0].astype(BF16)) + b_ref[0]


def _mod_call(cvec, w_mod, b_mod):
    L, Dm, N = w_mod.shape
    tn = 1024
    out = pl.pallas_call(
        _mod_kernel,
        grid=(L, N // tn),
        in_specs=[pl.BlockSpec((SUBLANE, Dm), lambda l, j: (0, 0)),
                  pl.BlockSpec((1, Dm, tn), lambda l, j: (l, 0, j)),
                  pl.BlockSpec((1, 1, tn), lambda l, j: (l, 0, j))],
        out_specs=pl.BlockSpec((1, SUBLANE, tn), lambda l, j: (l, 0, j)),
        out_shape=jax.ShapeDtypeStruct((L, SUBLANE, N), F32),
        compiler_params=_params(("arbitrary", "arbitrary"), 40),
        name="adaln_mod",
    )(cvec, w_mod, b_mod.reshape(L, 1, N))
    return out.reshape(L, SUBLANE, N_MOD, Dm)


def _mod_spec(goff, tiles_per_group, nargs):
    if nargs == 1:
        return pl.BlockSpec((None, N_MOD, D_MODEL), lambda i: (goff + i // tiles_per_group, 0, 0))
    return pl.BlockSpec((None, N_MOD, D_MODEL), lambda i, j: (goff + i // tiles_per_group, 0, 0))


def _ffn_kernel(x_ref, mod_ref, g_ref, wg_ref, wu_ref, wo_ref, o_ref, h_sc, acc_sc, *, sub):
    f = pl.program_id(1)

    @pl.when(f == 0)
    def _():
        h = _norm_mod(x_ref[...], g_ref[...], mod_ref[3 * sub:3 * sub + 1, :],
                      mod_ref[3 * sub + 1:3 * sub + 2, :])
        h_sc[...] = h.astype(BF16)
        acc_sc[...] = jnp.zeros_like(acc_sc)

    h = h_sc[...]
    a = _dot(h, wg_ref[...])
    u = _dot(h, wu_ref[...])
    act = (a * _sigmoid(a)) * u
    acc_sc[...] += _dot(act.astype(BF16), wo_ref[...])

    @pl.when(f == pl.num_programs(1) - 1)
    def _():
        o_ref[...] = x_ref[...] + 0.5 * mod_ref[3 * sub + 2:3 * sub + 3, :] * acc_sc[...]


def _ffn_call(x, mod_l, goff, group_tokens, gamma, w_in, w_out, l, s, sub):
    T = x.shape[0]
    tm = min(512, group_tokens)
    tf = 512
    nf = D_FF // tf
    return pl.pallas_call(
        functools.partial(_ffn_kernel, sub=sub),
        grid=(T // tm, nf),
        in_specs=[pl.BlockSpec((tm, D_MODEL), lambda i, f: (i, 0)),
                  _mod_spec(goff, group_tokens // tm, 2),
                  pl.BlockSpec((1, D_MODEL), lambda i, f: (0, 0)),
                  pl.BlockSpec((None, None, D_MODEL, tf), lambda i, f: (l, s, 0, f)),
                  pl.BlockSpec((None, None, D_MODEL, tf), lambda i, f: (l, s, 0, f + nf)),
                  pl.BlockSpec((None, None, tf, D_MODEL), lambda i, f: (l, s, f, 0))],
        out_specs=pl.BlockSpec((tm, D_MODEL), lambda i, f: (i, 0)),
        out_shape=jax.ShapeDtypeStruct((T, D_MODEL), F32),
        scratch_shapes=[pltpu.VMEM((tm, D_MODEL), BF16), pltpu.VMEM((tm, D_MODEL), F32)],
        compiler_params=_params(("parallel", "arbitrary"), 52),
        name="ffn_swiglu",
    )(x, mod_l, gamma, w_in, w_in, w_out)


def _inproj_kernel(x_ref, mod_ref, g_ref, w_ref, o_ref, h_sc, *, col_axis):
    @pl.when(pl.program_id(col_axis) == 0)
    def _():
        h = _norm_mod(x_ref[...], g_ref[...], mod_ref[3:4, :], mod_ref[4:5, :])
        h_sc[...] = h.astype(BF16)

    o_ref[...] = _dot(h_sc[...], w_ref[...])


def _inproj_call(x, mod_l, goff, group_tokens, gamma, w, tn):
    T = x.shape[0]
    N = w.shape[1]
    tm = min(512, group_tokens)
    return pl.pallas_call(
        functools.partial(_inproj_kernel, col_axis=1),
        grid=(T // tm, N // tn),
        in_specs=[pl.BlockSpec((tm, D_MODEL), lambda i, j: (i, 0)),
                  _mod_spec(goff, group_tokens // tm, 2),
                  pl.BlockSpec((1, D_MODEL), lambda i, j: (0, 0)),
                  pl.BlockSpec((D_MODEL, tn), lambda i, j: (0, j))],
        out_specs=pl.BlockSpec((tm, tn), lambda i, j: (i, j)),
        out_shape=jax.ShapeDtypeStruct((T, N), F32),
        scratch_shapes=[pltpu.VMEM((tm, D_MODEL), BF16)],
        compiler_params=_params(("parallel", "arbitrary"), 48),
        name="mixer_inproj",
    )(x, mod_l, gamma, w)


def _inproj_parity_call(x2, mod_l, goff, group_rows, gamma, w, tn):
    T2 = x2.shape[0]
    N = w.shape[1]
    tm = min(512, group_rows)
    tpg = group_rows // tm
    return pl.pallas_call(
        functools.partial(_inproj_kernel, col_axis=2),
        grid=(2, T2 // tm, N // tn),
        in_specs=[pl.BlockSpec((tm, D_MODEL), lambda p, i, j: (i, p)),
                  pl.BlockSpec((None, N_MOD, D_MODEL), lambda p, i, j: (goff + i // tpg, 0, 0)),
                  pl.BlockSpec((1, D_MODEL), lambda p, i, j: (0, 0)),
                  pl.BlockSpec((D_MODEL, tn), lambda p, i, j: (0, j))],
        out_specs=pl.BlockSpec((None, tm, tn), lambda p, i, j: (p, i, j)),
        out_shape=jax.ShapeDtypeStruct((2, T2, N), F32),
        scratch_shapes=[pltpu.VMEM((tm, D_MODEL), BF16)],
        compiler_params=_params(("parallel", "parallel", "arbitrary"), 48),
        name="mixer_inproj_parity",
    )(x2, mod_l, gamma, w)


def _outproj_kernel(x_ref, mod_ref, a1_ref, a2_ref, w1_ref, w2_ref, o_ref):
    y = _dot(a1_ref[...], w1_ref[...]) + _dot(a2_ref[...], w2_ref[...])
    o_ref[...] = x_ref[...] + mod_ref[5:6, :] * y


def _outproj_call(x, mod_l, goff, group_tokens, a1, a2, w):
    T = x.shape[0]
    tm = min(512, group_tokens)
    half = D_MODEL // 2
    return pl.pallas_call(
        _outproj_kernel,
        grid=(T // tm,),
        in_specs=[pl.BlockSpec((tm, D_MODEL), lambda i: (i, 0)),
                  _mod_spec(goff, group_tokens // tm, 1),
                  pl.BlockSpec((tm, half), lambda i: (i, 0)),
                  pl.BlockSpec((tm, half), lambda i: (i, 0)),
                  pl.BlockSpec((half, D_MODEL), lambda i: (0, 0)),
                  pl.BlockSpec((half, D_MODEL), lambda i: (1, 0))],
        out_specs=pl.BlockSpec((tm, D_MODEL), lambda i: (i, 0)),
        out_shape=jax.ShapeDtypeStruct((T, D_MODEL), F32),
        compiler_params=_params(("parallel",), 48),
        name="mixer_outproj",
    )(x, mod_l, a1, a2, w, w)


def _outproj_parity_call(x2, mod_l, goff, group_rows, a, w):
    T2 = x2.shape[0]
    tm = min(512, group_rows)
    tpg = group_rows // tm
    half = D_MODEL // 2
    return pl.pallas_call(
        _outproj_kernel,
        grid=(2, T2 // tm),
        in_specs=[pl.BlockSpec((tm, D_MODEL), lambda p, i: (i, p)),
                  pl.BlockSpec((None, N_MOD, D_MODEL), lambda p, i: (goff + i // tpg, 0, 0)),
                  pl.BlockSpec((None, tm, half), lambda p, i: (p, i, 0)),
                  pl.BlockSpec((None, tm, half), lambda p, i: (p, i, 1)),
                  pl.BlockSpec((half, D_MODEL), lambda p, i: (0, 0)),
                  pl.BlockSpec((half, D_MODEL), lambda p, i: (1, 0))],
        out_specs=pl.BlockSpec((tm, D_MODEL), lambda p, i: (i, p)),
        out_shape=jax.ShapeDtypeStruct((T2, 2 * D_MODEL), F32),
        compiler_params=_params(("parallel", "parallel"), 48),
        name="mixer_outproj_parity",
    )(x2, mod_l, a, a, w, w)


def _final_norm_kernel(x_ref, g_ref, o_ref):
    x = x_ref[...]
    o_ref[...] = (x * lax.rsqrt(jnp.mean(x * x, axis=-1, keepdims=True) + 1e-6)) * g_ref[...]


def _final_norm_call(x, gamma):
    T = x.shape[0]
    tm = 512
    return pl.pallas_call(
        _final_norm_kernel,
        grid=(T // tm,),
        in_specs=[pl.BlockSpec((tm, D_MODEL), lambda i: (i, 0)),
                  pl.BlockSpec((1, D_MODEL), lambda i: (0, 0))],
        out_specs=pl.BlockSpec((tm, D_MODEL), lambda i: (i, 0)),
        out_shape=jax.ShapeDtypeStruct((T, D_MODEL), F32),
        compiler_params=_params(("parallel",), 32),
        name="final_norm",
    )(x, gamma)


def _shift_prev(cur, halo_prev, row0, seq_len):
    tt = cur.shape[0]
    rid = lax.broadcasted_iota(jnp.int32, (tt, 1), 0)
    pos = jnp.bitwise_and(rid + row0, seq_len - 1)
    prev = pltpu.roll(cur, 1, 0)
    prev = jnp.where(rid == 0, halo_prev[SUBLANE - 1:SUBLANE, :], prev)
    return jnp.where(pos == 0, 0.0, prev)


def _shift_next(cur, halo_next, row0, seq_len):
    tt = cur.shape[0]
    rid = lax.broadcasted_iota(jnp.int32, (tt, 1), 0)
    pos = jnp.bitwise_and(rid + row0, seq_len - 1)
    nxt = pltpu.roll(cur, tt - 1, 0)
    nxt = jnp.where(rid == tt - 1, halo_next[0:1, :], nxt)
    return jnp.where(pos == seq_len - 1, 0.0, nxt)


def _shift_prev_next(cur, halo_prev, halo_next, row0, seq_len):
    return _shift_prev(cur, halo_prev, row0, seq_len), _shift_next(cur, halo_next, row0, seq_len)


def _halo_specs(tt, width, col, total_rows):
    per = tt // SUBLANE
    last = total_rows // SUBLANE - 1
    return [pl.BlockSpec((tt, width), lambda i: (i, col)),
            pl.BlockSpec((SUBLANE, width), lambda i: (jnp.maximum(i * per - 1, 0), col)),
            pl.BlockSpec((SUBLANE, width), lambda i: (jnp.minimum((i + 1) * per, last), col))]


def _segsum(x, ones_blk):
    hi = x.astype(BF16)
    lo = (x - hi.astype(F32)).astype(BF16)
    outs = []
    for g in range(x.shape[1] // MXU_DIM):
        sl = slice(g * MXU_DIM, (g + 1) * MXU_DIM)
        outs.append(_dot(hi[:, sl], ones_blk) + _dot(lo[:, sl], ones_blk))
    return jnp.concatenate(outs, axis=1)


def _rwkv_prep_kernel(z_ref, zp_ref, zn_ref, mup_ref, mun_ref, kk_ref, ka_ref, rk_ref,
                      w0_ref, w2_ref, a0_ref, a2_ref, g2_ref, ones_ref,
                      r_o, v_o, c_o, w0_o, b0_o, k0_o, w1_o, b1_o, k1_o, bonus_o, g_o,
                      *, tt, seq_len):
    row0 = pl.program_id(0) * tt
    cur = z_ref[...]
    prev, nxt = _shift_prev_next(cur, zp_ref[...], zn_ref[...], row0, seq_len)
    zs = cur + mup_ref[...] * (prev - cur) + mun_ref[...] * (nxt - cur)
    r = zs[:, 0:D_RWKV]
    k = zs[:, D_RWKV:2 * D_RWKV]
    v = zs[:, 2 * D_RWKV:3 * D_RWKV]
    small = zs[:, 3 * D_RWKV:RWKV_COLS]
    ones_blk = ones_ref[...]

    kk = k * kk_ref[...]
    kk = kk / jnp.maximum(jnp.sqrt(_segsum(kk * kk, ones_blk)), 1e-12)
    tw = jnp.tanh(small).astype(BF16)
    sg = _sigmoid(small).astype(BF16)
    xs = small.astype(BF16)
    r_o[...] = r
    v_o[...] = v
    c_o[...] = -kk
    g_o[...] = _dot(sg, g2_ref[...])

    bonus = jnp.zeros_like(r)
    outs = ((w0_o, b0_o, k0_o), (w1_o, b1_o, k1_o))
    for d in range(2):
        wl = -_softplus(-(w0_ref[d:d + 1, :] + _dot(tw, w2_ref[d]))) - 0.5
        a = _sigmoid(a0_ref[d:d + 1, :] + _dot(xs, a2_ref[d]))
        kd = k * (1.0 + (a - 1.0) * ka_ref[...])
        w_o, b_o, k_o = outs[d]
        w_o[...] = jnp.exp(-jnp.exp(wl))
        b_o[...] = kk * a
        k_o[...] = kd
        bonus = bonus + _segsum(r * kd * rk_ref[...], ones_blk) * v
    bonus_o[...] = bonus


def _rwkv_prep_call(z_r, seq_len, p):
    T = z_r.shape[0]
    tt = min(256, seq_len)
    row = lambda n: pl.BlockSpec((1, n), lambda i: (0, 0))
    full2 = lambda a, b: pl.BlockSpec((a, b), lambda i: (0, 0))
    full3 = lambda a, b, c: pl.BlockSpec((a, b, c), lambda i: (0, 0, 0))
    in_specs = _halo_specs(tt, RWKV_COLS, 0, T) + [
        row(RWKV_COLS), row(RWKV_COLS), row(D_RWKV), row(D_RWKV), row(D_RWKV),
        full2(2, D_RWKV), full3(2, RWKV_SMALL, D_RWKV), full2(2, D_RWKV), full3(2, RWKV_SMALL, D_RWKV),
        full2(RWKV_SMALL, D_RWKV), full2(MXU_DIM, MXU_DIM)]
    out_spec = pl.BlockSpec((tt, D_RWKV), lambda i: (i, 0))
    out_shape = jax.ShapeDtypeStruct((T, D_RWKV), F32)
    return pl.pallas_call(
        functools.partial(_rwkv_prep_kernel, tt=tt, seq_len=seq_len),
        grid=(T // tt,),
        in_specs=in_specs,
        out_specs=[out_spec] * 11,
        out_shape=[out_shape] * 11,
        compiler_params=_params(("parallel",), 52),
        name="rwkv_prep",
    )(z_r, z_r, z_r, p['mu_prev'], p['mu_next'], p['k_k'], p['k_a'], p['r_k'],
      p['w0'], p['w2'], p['a0'], p['a2'], p['g2'], p['ones_blk'])


N_GRP = D_RWKV // MXU_DIM
HEADS_PER_GRP = H_RWKV // N_GRP


SCAN_ROWS = 16


def _scan_head_masks():
    row = jnp.arange(SCAN_ROWS)[None, :, None]
    head = (jnp.arange(MXU_DIM) // RWKV_HEAD)[None, None, :]
    s = jnp.arange(3)[:, None, None]
    return (row == HEADS_PER_GRP * s + head).astype(F32)


def _scan_kernel(rf, vf, cf, wf, bf, kf, rb, vb, cb, wb, bb, kb, s0_ref, eye_ref, hm_ref,
                 y0_ref, y1_ref, sfin_ref, st, *, tc):
    j = pl.program_id(1)

    @pl.when(j == 0)
    def _():
        st[...] = s0_ref[...]

    eye = eye_ref[...]
    m0 = hm_ref[0]
    m1 = hm_ref[1]
    m2 = hm_ref[2]
    dirs = ((rf, vf, cf, wf, bf, kf, y0_ref), (rb, vb, cb, wb, bb, kb, y1_ref))
    nt = (((1,), (1,)), ((), ()))

    def reduce_heads(m, crow, vrow, rrow):
        wr = jnp.concatenate([crow * m0 + rrow * m2, vrow * m1], axis=1).astype(BF16)
        lhs = jnp.concatenate([m.astype(BF16), eye], axis=1)
        return lax.dot_general(lhs, wr, nt, preferred_element_type=F32)

    def store_y(y_, b, t, g, red):
        tr = red.T
        y_[b, pl.ds(t, 1), g * HEADS_PER_GRP:(g + 1) * HEADS_PER_GRP, :] = (
            tr[2 * HEADS_PER_GRP:3 * HEADS_PER_GRP, :].reshape(1, HEADS_PER_GRP, RWKV_HEAD))

    def body(i, carry):
        chains = []
        for b in range(2):
            chains.append((b, 0, i, jnp.maximum(i - 1, 0)))
            chains.append((b, 1, tc - 1 - i, jnp.minimum(tc - i, tc - 1)))
        reds = []
        for (b, d, t, tp) in chains:
            r_, v_, c_, w_, b_, k_, y_ = dirs[d]
            crow = c_[b, pl.ds(t, 1), :]
            vrow = v_[b, pl.ds(t, 1), :]
            rrow = r_[b, pl.ds(tp, 1), :]
            for g in range(N_GRP):
                sl = slice(g * MXU_DIM, (g + 1) * MXU_DIM)
                reds.append(reduce_heads(st[b, d, g], crow[:, sl], vrow[:, sl], rrow[:, sl]))
        idx = 0
        for (b, d, t, tp) in chains:
            r_, v_, c_, w_, b_, k_, y_ = dirs[d]
            wrow = w_[b, pl.ds(t, 1), :]
            brow = b_[b, pl.ds(t, 1), :]
            krow = k_[b, pl.ds(t, 1), :]
            for g in range(N_GRP):
                sl = slice(g * MXU_DIM, (g + 1) * MXU_DIM)
                red = reds[idx]
                w2 = (brow[:, sl] * m0 + krow[:, sl] * m1).astype(BF16)
                st[b, d, g] = st[b, d, g] * wrow[:, sl] + _dot(red.astype(BF16), w2)
                store_y(y_, b, tp, g, red)
                idx += 1
        return carry

    lax.fori_loop(0, tc, body, 0)

    for b in range(2):
        for d in range(2):
            t = tc - 1 if d == 0 else 0
            rrow = dirs[d][0][b, pl.ds(t, 1), :]
            for g in range(N_GRP):
                sl = slice(g * MXU_DIM, (g + 1) * MXU_DIM)
                red = reduce_heads(st[b, d, g], rrow[:, sl], rrow[:, sl], rrow[:, sl])
                store_y(dirs[d][6], b, t, g, red)

    @pl.when(j == pl.num_programs(1) - 1)
    def _():
        sfin_ref[...] = st[...]


def _scan_call(pre, s0, eye):
    B, T, _ = pre['r'].shape
    tc = min(128, T)
    nj = T // tc
    fwd = pl.BlockSpec((2, tc, D_RWKV), lambda bi, j: (bi, j, 0))
    bwd = pl.BlockSpec((2, tc, D_RWKV), lambda bi, j: (bi, nj - 1 - j, 0))
    yfwd = pl.BlockSpec((2, tc, H_RWKV, RWKV_HEAD), lambda bi, j: (bi, j, 0, 0))
    ybwd = pl.BlockSpec((2, tc, H_RWKV, RWKV_HEAD), lambda bi, j: (bi, nj - 1 - j, 0, 0))
    st_spec = pl.BlockSpec((2, 2, N_GRP, RWKV_HEAD, MXU_DIM), lambda bi, j: (bi, 0, 0, 0, 0))
    y_shape = jax.ShapeDtypeStruct((B, T, H_RWKV, RWKV_HEAD), F32)
    return pl.pallas_call(
        functools.partial(_scan_kernel, tc=tc),
        grid=(B // 2, nj),
        in_specs=[fwd] * 6 + [bwd] * 6 + [
            st_spec,
            pl.BlockSpec((RWKV_HEAD, MXU_DIM), lambda bi, j: (0, 0)),
            pl.BlockSpec((3, SCAN_ROWS, MXU_DIM), lambda bi, j: (0, 0, 0))],
        out_specs=[yfwd, ybwd, st_spec],
        out_shape=[y_shape, y_shape, jax.ShapeDtypeStruct((B, 2, N_GRP, RWKV_HEAD, MXU_DIM), F32)],
        scratch_shapes=[pltpu.VMEM((2, 2, N_GRP, RWKV_HEAD, MXU_DIM), F32)],
        compiler_params=_params(("arbitrary", "arbitrary"), 48),
        name="rwkv_scan",
    )(pre['r'], pre['v'], pre['c'], pre['w0'], pre['b0'], pre['k0'],
      pre['r'], pre['v'], pre['c'], pre['w1'], pre['b1'], pre['k1'], s0, eye.astype(BF16),
      _scan_head_masks())


def _rwkv_post_kernel(y0_ref, y1_ref, bonus_ref, g_ref, gw_ref, gb_ref, ones_ref, o_ref):
    ones_blk = ones_ref[...]
    y = y0_ref[...] + y1_ref[...]
    mu = _segsum(y, ones_blk) * (1.0 / RWKV_HEAD)
    yc = y - mu
    var = _segsum(yc * yc, ones_blk) * (1.0 / RWKV_HEAD)
    yn = yc * lax.rsqrt(var + RWKV_GN_EPS)
    out = (yn * gw_ref[...] + gb_ref[...] + bonus_ref[...]) * g_ref[...]
    o_ref[...] = out.astype(BF16)


def _rwkv_post_call(y0, y1, bonus, g, gn_w, gn_b, ones_blk):
    T = y0.shape[0]
    tt = 512
    blk = pl.BlockSpec((tt, D_RWKV), lambda i: (i, 0))
    row = pl.BlockSpec((1, D_RWKV), lambda i: (0, 0))
    return pl.pallas_call(
        _rwkv_post_kernel,
        grid=(T // tt,),
        in_specs=[blk, blk, blk, blk, row, row, pl.BlockSpec((MXU_DIM, MXU_DIM), lambda i: (0, 0))],
        out_specs=blk,
        out_shape=jax.ShapeDtypeStruct((T, D_RWKV), BF16),
        compiler_params=_params(("parallel",), 40),
        name="rwkv_post",
    )(y0, y1, bonus, g, gn_w, gn_b, ones_blk)


def _rope128(x, cos_t, sin_t):
    return x * cos_t + pltpu.roll(x, QK_ROPE, 1) * sin_t


def _pack_kv(kv, kr_rot, k_o, v_o):
    for h in range(H_MLA):
        k_o[:, h * HEAD_SLOT:h * HEAD_SLOT + QK_NOPE] = kv[:, h * HEAD_SLOT:h * HEAD_SLOT + QK_NOPE].astype(BF16)
        k_o[:, h * HEAD_SLOT + QK_NOPE:(h + 1) * HEAD_SLOT] = kr_rot.astype(BF16)
        v_o[:, h * V_HEAD:(h + 1) * V_HEAD] = kv[:, h * HEAD_SLOT + QK_NOPE:(h + 1) * HEAD_SLOT].astype(BF16)


def _mla_prep_kernel(z_ref, cos_ref, sin_ref, qn_ref, kvn_ref, wq_ref, wkv_ref,
                     q_o, k_o, v_o, ckv_o, kr_o):
    z = z_ref[...]
    cq = z[:, 0:Q_RANK]
    ckv = z[:, Q_RANK:Q_RANK + KV_RANK]
    krp = z[:, Q_RANK + KV_RANK:MLA_COLS]
    cos_t = cos_ref[...]
    sin_t = sin_ref[...]
    cq = (cq * lax.rsqrt(jnp.mean(cq * cq, axis=-1, keepdims=True) + 1e-6)) * qn_ref[...]
    ckv = (ckv * lax.rsqrt(jnp.mean(ckv * ckv, axis=-1, keepdims=True) + 1e-6)) * kvn_ref[...]
    ckv_o[...] = ckv
    kr_o[...] = krp
    q = _dot(cq.astype(BF16), wq_ref[...])
    for h in range(H_MLA):
        q_o[:, h * HEAD_SLOT:h * HEAD_SLOT + QK_NOPE] = q[:, h * HEAD_SLOT:h * HEAD_SLOT + QK_NOPE].astype(BF16)
        q_o[:, h * HEAD_SLOT + QK_NOPE:(h + 1) * HEAD_SLOT] = _rope128(
            q[:, h * HEAD_SLOT + QK_NOPE:(h + 1) * HEAD_SLOT], cos_t, sin_t).astype(BF16)
    kv = _dot(ckv.astype(BF16), wkv_ref[...])
    _pack_kv(kv, _rope128(krp, cos_t, sin_t), k_o, v_o)


def _mla_prep_call(z_m, cos_t, sin_t, seq_len, p):
    T = z_m.shape[0]
    tm = min(512, seq_len)
    per_seq = seq_len // tm
    row = lambda n: pl.BlockSpec((1, n), lambda i: (0, 0))
    blk = lambda n: pl.BlockSpec((tm, n), lambda i: (i, 0))
    tab = pl.BlockSpec((tm, LANE), lambda i: (i % per_seq, 0))
    return pl.pallas_call(
        _mla_prep_kernel,
        grid=(T // tm,),
        in_specs=[blk(MLA_COLS), tab, tab, row(Q_RANK), row(KV_RANK),
                  pl.BlockSpec((Q_RANK, H_MLA * HEAD_SLOT), lambda i: (0, 0)),
                  pl.BlockSpec((KV_RANK, H_MLA * HEAD_SLOT), lambda i: (0, 0))],
        out_specs=[blk(H_MLA * HEAD_SLOT), blk(H_MLA * HEAD_SLOT), blk(H_MLA * V_HEAD),
                   blk(KV_RANK), blk(LANE)],
        out_shape=[jax.ShapeDtypeStruct((T, H_MLA * HEAD_SLOT), BF16),
                   jax.ShapeDtypeStruct((T, H_MLA * HEAD_SLOT), BF16),
                   jax.ShapeDtypeStruct((T, H_MLA * V_HEAD), BF16),
                   jax.ShapeDtypeStruct((T, KV_RANK), F32),
                   jax.ShapeDtypeStruct((T, LANE), F32)],
        compiler_params=_params(("parallel",), 48),
        name="mla_prep",
    )(z_m, cos_t, sin_t, p['q_norm'], p['kv_norm'], p['w_qb'], p['w_kvb'])


def _ctx_kv_kernel(ckv_ref, kr_ref, wkv_ref, k_o, v_o):
    kv = _dot(ckv_ref[...].astype(BF16), wkv_ref[...])
    _pack_kv(kv, kr_ref[...], k_o, v_o)


def _ctx_kv_call(ckv_ctx, kr_ctx_pad, w_kvb):
    T = ckv_ctx.shape[0]
    tm = min(512, T)
    blk = lambda n: pl.BlockSpec((tm, n), lambda i: (i, 0))
    return pl.pallas_call(
        _ctx_kv_kernel,
        grid=(T // tm,),
        in_specs=[blk(KV_RANK), blk(LANE), pl.BlockSpec((KV_RANK, H_MLA * HEAD_SLOT), lambda i: (0, 0))],
        out_specs=[blk(H_MLA * HEAD_SLOT), blk(H_MLA * V_HEAD)],
        out_shape=[jax.ShapeDtypeStruct((T, H_MLA * HEAD_SLOT), BF16),
                   jax.ShapeDtypeStruct((T, H_MLA * V_HEAD), BF16)],
        compiler_params=_params(("parallel",), 32),
        name="mla_ctx_kv",
    )(ckv_ctx, kr_ctx_pad, w_kvb)


def _attn_kernel(q_ref, k_ref, v_ref, o_ref, *, scale):
    s = lax.dot_general(q_ref[...], k_ref[...], (((1,), (1,)), ((), ())),
                        preferred_element_type=F32) * scale
    m = jnp.max(s, axis=-1, keepdims=True)
    p = jnp.exp(s - m)
    l = jnp.sum(p, axis=-1, keepdims=True)
    o = _dot(p.astype(BF16), v_ref[...])
    o_ref[...] = (o / l).astype(BF16)


def _attn_call(q, k, v):
    B, Tq, _ = q.shape
    Tk = k.shape[1]
    tq = min(512, Tq)
    scale = (QK_NOPE + QK_ROPE) ** -0.5
    return pl.pallas_call(
        functools.partial(_attn_kernel, scale=scale),
        grid=(B, H_MLA, Tq // tq),
        in_specs=[pl.BlockSpec((None, tq, HEAD_SLOT), lambda b, h, i: (b, i, h)),
                  pl.BlockSpec((None, Tk, HEAD_SLOT), lambda b, h, i: (b, 0, h)),
                  pl.BlockSpec((None, Tk, V_HEAD), lambda b, h, i: (b, 0, h))],
        out_specs=pl.BlockSpec((None, tq, V_HEAD), lambda b, h, i: (b, i, h)),
        out_shape=jax.ShapeDtypeStruct((B, Tq, H_MLA * V_HEAD), BF16),
        compiler_params=_params(("parallel", "parallel", "arbitrary"), 48),
        name="mla_attention",
    )(q, k, v)


def _conv3_kernel(*refs, tt, half_len):
    ins, (cw_refs, cb_refs), outs = refs[0:12], (refs[12:15], refs[15:18]), refs[18:]
    row0 = pl.program_id(0) * tt
    for s in range(3):
        even, odd = ins[4 * s][...], ins[4 * s + 1][...]
        odd_prev = _shift_prev(odd, ins[4 * s + 2][...], row0, half_len)
        even_next = _shift_next(even, ins[4 * s + 3][...], row0, half_len)
        cw = cw_refs[s][...]
        bias = cb_refs[s][...]
        y_even = cw[0:1, :] * odd_prev + cw[1:2, :] * even + cw[2:3, :] * odd + bias
        y_odd = cw[0:1, :] * even + cw[1:2, :] * odd + cw[2:3, :] * even_next + bias
        outs[s][0] = y_even
        outs[s][1] = y_odd
        if s == 2:
            outs[3][0] = y_even.astype(BF16)
            outs[3][1] = y_odd.astype(BF16)


def _conv3_call(z, half_len, conv_w, conv_b):
    T2 = z.shape[1]
    tt = min(128, half_len)
    per = tt // SUBLANE
    last = T2 // SUBLANE - 1
    in_specs = []
    for s in range(3):
        in_specs += [
            pl.BlockSpec((None, tt, D_HYENA), lambda i, s=s: (0, i, s)),
            pl.BlockSpec((None, tt, D_HYENA), lambda i, s=s: (1, i, s)),
            pl.BlockSpec((None, SUBLANE, D_HYENA), lambda i, s=s: (1, jnp.maximum(i * per - 1, 0), s)),
            pl.BlockSpec((None, SUBLANE, D_HYENA), lambda i, s=s: (0, jnp.minimum((i + 1) * per, last), s))]
    in_specs += [pl.BlockSpec((3, D_HYENA), lambda i, s=s: (0, s)) for s in range(3)]
    in_specs += [pl.BlockSpec((1, D_HYENA), lambda i, s=s: (0, s)) for s in range(3)]
    blk = pl.BlockSpec((2, tt, D_HYENA), lambda i: (0, i, 0))
    f32s = jax.ShapeDtypeStruct((2, T2, D_HYENA), F32)
    return pl.pallas_call(
        functools.partial(_conv3_kernel, tt=tt, half_len=half_len),
        grid=(T2 // tt,),
        in_specs=in_specs,
        out_specs=[blk] * 4,
        out_shape=[f32s, f32s, f32s, jax.ShapeDtypeStruct((2, T2, D_HYENA), BF16)],
        compiler_params=_params(("parallel",), 48),
        name="hyena_conv3",
    )(*([z] * 12), conv_w, conv_w, conv_w, conv_b, conv_b, conv_b)


def _filt_mlp_kernel(z_ref, w1_ref, b1_ref, w2_ref, b2_ref, fr_ref, o_ref):
    h = jnp.sin(fr_ref[0:1, :] * (_dot(z_ref[...].astype(BF16), w1_ref[...].astype(BF16)) + b1_ref[...]))
    h = jnp.sin(fr_ref[1:2, :] * (_dot(h.astype(BF16), w2_ref[...].astype(BF16)) + b2_ref[...]))
    o_ref[...] = h.astype(BF16)


def _filt_mlp_call(zpos, w1p, b1, w2, b2, freq):
    L = zpos.shape[0]
    return pl.pallas_call(
        _filt_mlp_kernel,
        out_shape=jax.ShapeDtypeStruct((L, FILT_HIDDEN), BF16),
        compiler_params=pltpu.CompilerParams(vmem_limit_bytes=32 * MIB),
        name="hyena_filter_mlp",
    )(zpos, w1p, b1, w2, b2, freq)


def _filt_gen_kernel(h_ref, tn_ref, dl_ref, w00, w01, w10, w11, o_ref):
    h = h_ref[...]
    L = h.shape[0]
    win = jnp.exp(-tn_ref[...] * dl_ref[...])
    not_first = lax.broadcasted_iota(jnp.int32, (L, 1), 0) > 0
    ws = ((w00, w01), (w10, w11))
    for n in range(2):
        causal = _dot(h, ws[n][0][...].astype(BF16)) * win
        anti = jnp.where(not_first, _dot(h, ws[n][1][...].astype(BF16)) * win, 0.0)
        norm = (jnp.sum(jnp.abs(causal), axis=0, keepdims=True)
                + jnp.sum(jnp.abs(anti), axis=0, keepdims=True))
        o_ref[2 * n] = (causal / norm).astype(BF16)
        o_ref[2 * n + 1] = (anti / norm).astype(BF16)


def _filt_gen_call(h2, tnorm, deltas, w3):
    L = h2.shape[0]
    tc = 128
    nc = D_HYENA // tc
    wspec = lambda k: pl.BlockSpec((FILT_HIDDEN, tc), lambda j, k=k: (0, k * nc + j))
    return pl.pallas_call(
        _filt_gen_kernel,
        grid=(nc,),
        in_specs=[pl.BlockSpec((L, FILT_HIDDEN), lambda j: (0, 0)),
                  pl.BlockSpec((L, 1), lambda j: (0, 0)),
                  pl.BlockSpec((1, tc), lambda j: (0, j)),
                  wspec(0), wspec(1), wspec(2), wspec(3)],
        out_specs=pl.BlockSpec((4, L, tc), lambda j: (0, 0, j)),
        out_shape=jax.ShapeDtypeStruct((4, L, D_HYENA), BF16),
        compiler_params=_params(("parallel",), 48),
        name="hyena_filter_gen",
    )(h2, tnorm, deltas, w3, w3, w3, w3)


def _dft_fwd_kernel(f_ref, u_ref, o_ref):
    o_ref[...] = _dot(f_ref[...], u_ref[...])


def _dft_fwd_call(fmat, u):
    B, K, C = u.shape
    M = fmat.shape[0]
    tm = min(512, M)
    tn = 512 if K > 256 else min(C, 2048)
    return pl.pallas_call(
        _dft_fwd_kernel,
        grid=(B, C // tn, M // tm),
        in_specs=[pl.BlockSpec((tm, K), lambda b, j, i: (i, 0)),
                  pl.BlockSpec((None, K, tn), lambda b, j, i: (b, 0, j))],
        out_specs=pl.BlockSpec((None, tm, tn), lambda b, j, i: (b, i, j)),
        out_shape=jax.ShapeDtypeStruct((B, M, C), F32),
        compiler_params=_params(("parallel", "parallel", "arbitrary"), 40),
        name="hyena_dft_fwd",
    )(fmat, u)


def _butterfly(gc, gs, hc, hs, tw_c, tw_s, first):
    tc = hc * tw_c - hs * tw_s
    ts = hs * tw_c + hc * tw_s
    p0 = gc + tc
    p2 = gc - tc
    p1 = jnp.where(first, gs, gs + ts)
    p3 = jnp.where(first, hs, ts - gs)
    return p0, p1, p2, p3


def _cmul(ac, a_s, bc, bs):
    return ac * bc - a_s * bs, ac * bs + a_s * bc


def _filt_planes_kernel(ge_ref, ho_ref, twc_ref, tws_ref, o_ref, *, tr):
    first = (lax.broadcasted_iota(jnp.int32, (tr, 1), 0) + pl.program_id(1) * tr) == 0
    tw_c, tw_s = twc_ref[...], tws_ref[...]
    a = _butterfly(ge_ref[0, 0], ge_ref[0, 1], ho_ref[0, 0], ho_ref[0, 1], tw_c, tw_s, first)
    b = _butterfly(ge_ref[1, 0], ge_ref[1, 1], ho_ref[1, 0], ho_ref[1, 1], tw_c, tw_s, first)
    o_ref[0] = a[0] + b[0]
    o_ref[1] = jnp.where(first, a[1] + b[1], a[1] - b[1])
    o_ref[2] = a[2] + b[2]
    o_ref[3] = a[3] - b[3]


def _filt_planes_call(raw, twc, tws):
    _, L, C2 = raw.shape
    H, C = L // 2, C2 // 2
    tr = min(256, H)
    tc = 512
    nc = C // tc
    blk = lambda off: pl.BlockSpec((None, 2, 2, tr, tc), lambda n, i, j: (n, 0, 0, i, j + off))
    tw = pl.BlockSpec((tr, 1), lambda n, i, j: (i, 0))
    return pl.pallas_call(
        functools.partial(_filt_planes_kernel, tr=tr),
        grid=(2, H // tr, nc),
        in_specs=[blk(0), blk(nc), tw, tw],
        out_specs=pl.BlockSpec((None, 4, tr, tc), lambda n, i, j: (n, 0, i, j)),
        out_shape=jax.ShapeDtypeStruct((2, 4, H, C), F32),
        compiler_params=_params(("parallel", "parallel", "parallel"), 40),
        name="hyena_filter_planes",
    )(raw.reshape(2, 2, 2, H, C2), raw.reshape(2, 2, 2, H, C2), twc, tws)


def _spec_mul_kernel(raw_ref, k_ref, twc_ref, tws_ref, o_ref, *, tr):
    first = (lax.broadcasted_iota(jnp.int32, (tr, 1), 0) + pl.program_id(1) * tr) == 0
    tw_c, tw_s = twc_ref[...], tws_ref[...]
    p0, p1, p2, p3 = _butterfly(raw_ref[0, 0], raw_ref[0, 1], raw_ref[1, 0], raw_ref[1, 1], tw_c, tw_s, first)
    k0, k1, k2, k3 = k_ref[0], k_ref[1], k_ref[2], k_ref[3]
    yac, yas = _cmul(p0, p1, k0, k1)
    ybc, ybs = _cmul(p2, p3, k2, k3)
    ymc, yms = _cmul(p1, p3, k1, k3)
    yac = jnp.where(first, p0 * k0, yac)
    ybc = jnp.where(first, p2 * k2, ybc)
    dc = yac - ybc
    ds = yas + ybs
    o_ref[0, 0] = (yac + ybc).astype(BF16)
    o_ref[0, 1] = jnp.where(first, ymc, yas - ybs).astype(BF16)
    o_ref[1, 0] = jnp.where(first, dc, dc * tw_c + ds * tw_s).astype(BF16)
    o_ref[1, 1] = jnp.where(first, yms, ds * tw_c - dc * tw_s).astype(BF16)


def _spec_mul_call(raw, k_planes, order, twc, tws):
    B2, L, C = raw.shape
    B, H = B2 // 2, L // 2
    tr = min(256, H)
    tc = 512
    tw = pl.BlockSpec((tr, 1), lambda b, i, j: (i, 0))
    out = pl.pallas_call(
        functools.partial(_spec_mul_kernel, tr=tr),
        grid=(B, H // tr, C // tc),
        in_specs=[pl.BlockSpec((2, None, 2, tr, tc), lambda b, i, j: (0, b, 0, i, j)),
                  pl.BlockSpec((None, 4, tr, tc), lambda b, i, j: (order, 0, i, j)),
                  tw, tw],
        out_specs=pl.BlockSpec((2, None, 2, tr, tc), lambda b, i, j: (0, b, 0, i, j)),
        out_shape=jax.ShapeDtypeStruct((2, B, 2, H, C), BF16),
        compiler_params=_params(("parallel", "parallel", "parallel"), 40),
        name="hyena_spectral_mul",
    )(raw.reshape(2, B, 2, H, C), k_planes, twc, tws)
    return out.reshape(B2, L, C)


def _dft_inv_kernel(f_ref, y_ref, gate_ref, u_ref, bias_ref, o_ref, ob_ref):
    conv = _dot(f_ref[...], y_ref[...])
    out = gate_ref[...] * (conv + u_ref[...] * bias_ref[...])
    o_ref[...] = out
    ob_ref[...] = out.astype(BF16)


def _dft_inv_call(imat, y_spec, gate, u, bias):
    B, M, C = u.shape
    K = imat.shape[1]
    tm = min(256, M)
    tn = 512
    blk = pl.BlockSpec((None, tm, tn), lambda b, j, i: (b, i, j))
    return pl.pallas_call(
        _dft_inv_kernel,
        grid=(B, C // tn, M // tm),
        in_specs=[pl.BlockSpec((tm, K), lambda b, j, i: (i, 0)),
                  pl.BlockSpec((None, K, tn), lambda b, j, i: (b, 0, j)),
                  blk, blk, pl.BlockSpec((1, tn), lambda b, j, i: (0, j))],
        out_specs=[blk, blk],
        out_shape=[jax.ShapeDtypeStruct((B, M, C), F32), jax.ShapeDtypeStruct((B, M, C), BF16)],
        compiler_params=_params(("parallel", "parallel", "arbitrary"), 48),
        name="hyena_dft_inv",
    )(imat, y_spec, gate, u, bias)


def _dft_tables(L):
    H = L // 2
    lo = min(64, H)
    hi = H // lo
    g = jnp.arange(H, dtype=jnp.int32)
    theta = 2.0 * math.pi / L
    ang_hi = ((g[:, None] * (jnp.arange(hi, dtype=jnp.int32) * lo)[None, :]) % L).astype(F32) * theta
    ang_lo = ((g[:, None] * jnp.arange(lo, dtype=jnp.int32)[None, :]) % L).astype(F32) * theta
    ch, sh, cl, sl = jnp.cos(ang_hi), jnp.sin(ang_hi), jnp.cos(ang_lo), jnp.sin(ang_lo)
    cos_m = (ch[:, :, None] * cl[:, None, :] - sh[:, :, None] * sl[:, None, :]).reshape(H, H)
    sin_m = (sh[:, :, None] * cl[:, None, :] + ch[:, :, None] * sl[:, None, :]).reshape(H, H)
    alt = jnp.where(jnp.arange(H) % 2 == 0, 1.0, -1.0).astype(F32)
    sin_m = jnp.where((g == 0)[:, None], alt[None, :], sin_m)
    fwd = jnp.concatenate([cos_m, sin_m], axis=0)
    n = 2.0 * L
    w_cos = jnp.where(g == 0, 1.0 / n, 2.0 / n).astype(F32)
    inv = jnp.concatenate([cos_m * w_cos[:, None], sin_m * (2.0 / n)], axis=0).T
    ang_tw = g.astype(F32) * (math.pi / L)
    return fwd.astype(BF16), inv.astype(BF16), jnp.cos(ang_tw)[:, None], jnp.sin(ang_tw)[:, None]


def _filter_positions(L):
    t = jnp.arange(L, dtype=F32)
    t_norm = t / max(L - 1, 1)
    bands = (POS_EMB - 1) // 2
    freqs = jnp.linspace(1e-4, bands - 1, bands, dtype=F32)
    ang = (2.0 * math.pi / L) * t[:, None] * freqs[None, :]
    z = jnp.concatenate([t_norm[:, None], jnp.cos(ang), -jnp.sin(ang)], axis=-1)
    return jnp.pad(z, ((0, 0), (0, POS_PAD - POS_EMB))), t_norm[:, None]


def _hyena_deltas():
    return jnp.linspace(abs(math.log(HYENA_TARGET)) / SLOW_DECAY_PCT,
                        abs(math.log(HYENA_TARGET)) / FAST_DECAY_PCT, D_HYENA, dtype=F32)[None, :]


def _hyena_filter_spectrum(L, tables, p):
    fwd, _, twc, tws = tables
    zpos, tnorm = _filter_positions(L)
    h2 = _filt_mlp_call(zpos, p['filt_w1'], p['filt_b1'], p['filt_w2'], p['filt_b2'], p['filt_freq'])
    filt = _filt_gen_call(h2, tnorm, _hyena_deltas(), p['filt_w3'])
    raw = _dft_fwd_call(fwd, filt.reshape(4, L // 2, 2 * D_HYENA))
    return _filt_planes_call(raw, twc, tws)


def _hyena_mixer(z, B, L, tables, k_planes, p):
    fwd, inv, twc, tws = tables
    H = L // 2
    x1, x2, v, vb = _conv3_call(z, H, p['conv_w'], p['conv_b'])
    shp = (2 * B, H, D_HYENA)
    u, ub = v.reshape(shp), vb.reshape(shp)
    for n, gate in enumerate((x1, x2)):
        raw = _dft_fwd_call(fwd, ub)
        yspec = _spec_mul_call(raw, k_planes, n, twc, tws)
        u, ub = _dft_inv_call(inv, yspec, gate.reshape(shp), u, p['bias'][n:n + 1])
    return ub.reshape(2, B * H, D_HYENA)


def _rope_swap(w):
    q = QK_ROPE // 4
    return jnp.concatenate([w[..., q:2 * q], w[..., 0:q], w[..., 3 * q:4 * q], w[..., 2 * q:3 * q]], axis=-1)


def _pad_cols(w, n):
    return jnp.pad(w, [(0, 0)] * (w.ndim - 1) + [(0, n - w.shape[-1])])


def _pack_even(e, w_in_even, mu_prev, mu_next, rwkv_w0, rwkv_w2, rwkv_a0, rwkv_a2, rwkv_g2,
               rwkv_kk, rwkv_ka, rwkv_rk, rwkv_gn_w, rwkv_gn_b, mla_q_norm, mla_kv_norm,
               mla_w_qb, mla_w_kvb, w_out_even):
    n_r = 3 * D_RWKV + W_LORA + A_LORA + G_LORA
    w_in = w_in_even[e]
    w_r = _pad_cols(w_in[:, :n_r], RWKV_COLS).astype(BF16)
    w_m = w_in[:, n_r:]
    kr_cols = w_m[:, Q_RANK + KV_RANK:]
    w_m = jnp.concatenate([w_m, _rope_swap(kr_cols)], axis=-1).astype(BF16)
    small_rows = lambda w, off: jnp.pad(w, [(0, 0)] * (w.ndim - 2)
                                        + [(off, RWKV_SMALL - off - w.shape[-2]), (0, 0)]).astype(BF16)
    wq = mla_w_qb[e].reshape(Q_RANK, H_MLA, QK_NOPE + QK_ROPE)
    wq = jnp.concatenate([wq, _rope_swap(wq[..., QK_NOPE:])], axis=-1).reshape(Q_RANK, H_MLA * HEAD_SLOT)
    blk = jnp.arange(MXU_DIM) // RWKV_HEAD
    return {
        'w_r': w_r, 'w_m': w_m,
        'mu_prev': _pad_cols(mu_prev[e][None, :], RWKV_COLS),
        'mu_next': _pad_cols(mu_next[e][None, :], RWKV_COLS),
        'k_k': rwkv_kk[e][None, :], 'k_a': rwkv_ka[e][None, :],
        'r_k': rwkv_rk[e].reshape(1, D_RWKV),
        'w0': rwkv_w0[e], 'w2': small_rows(rwkv_w2[e], 0),
        'a0': rwkv_a0[e], 'a2': small_rows(rwkv_a2[e], W_LORA),
        'g2': small_rows(rwkv_g2[e], W_LORA + A_LORA),
        'gn_w': rwkv_gn_w[e][None, :], 'gn_b': rwkv_gn_b[e][None, :],
        'q_norm': mla_q_norm[e][None, :], 'kv_norm': mla_kv_norm[e][None, :],
        'w_qb': wq.astype(BF16), 'w_kvb': mla_w_kvb[e].astype(BF16),
        'w_out': w_out_even[e].astype(BF16),
        'ones_blk': (blk[:, None] == blk[None, :]).astype(BF16),
        'eye': (jnp.arange(RWKV_HEAD)[:, None] == (jnp.arange(MXU_DIM) % RWKV_HEAD)[None, :]).astype(F32),
    }


def _rope_tables(L):
    rows = L // GRID_W
    row = jnp.repeat(jnp.arange(rows, dtype=F32), GRID_W)
    col = jnp.tile(jnp.arange(GRID_W, dtype=F32), rows)
    half = QK_ROPE // 2
    inv = 1.0 / (ROPE_THETA ** (jnp.arange(0, half, 2, dtype=F32) / half))
    ar, ac = row[:, None] * inv[None, :], col[:, None] * inv[None, :]
    cos_t = jnp.concatenate([jnp.cos(ar), jnp.cos(ar), jnp.cos(ac), jnp.cos(ac)], axis=-1)
    sin_t = jnp.concatenate([-jnp.sin(ar), jnp.sin(ar), -jnp.sin(ac), jnp.sin(ac)], axis=-1)
    return _pad_cols(cos_t, LANE), _pad_cols(sin_t, LANE)


def _state_to_groups(s):
    B = s.shape[0]
    s = s.reshape(B, 2, N_GRP, H_RWKV // N_GRP, RWKV_HEAD, RWKV_HEAD)
    return jnp.swapaxes(s, 3, 4).reshape(B, 2, N_GRP, RWKV_HEAD, MXU_DIM)


def _groups_to_state(s):
    B = s.shape[0]
    s = s.reshape(B, 2, N_GRP, RWKV_HEAD, H_RWKV // N_GRP, RWKV_HEAD)
    return jnp.swapaxes(s, 3, 4).reshape(B, 2, H_RWKV, RWKV_HEAD, RWKV_HEAD)


def _even_mixer(x, mod_l, goff, B, L, gamma, p, rope, ctx):
    group_tokens = x.shape[0] if ctx is None else L
    z_r = _inproj_call(x, mod_l, goff, group_tokens, gamma, p['w_r'], RWKV_COLS // 3)
    z_m = _inproj_call(x, mod_l, goff, group_tokens, gamma, p['w_m'], MLA_COLS)
    names = ('r', 'v', 'c', 'w0', 'b0', 'k0', 'w1', 'b1', 'k1', 'bonus', 'g')
    pre = dict(zip(names, _rwkv_prep_call(z_r, L, p)))
    seq = {n: pre[n].reshape(B, L, D_RWKV) for n in names[:9]}
    if ctx is None:
        s0 = jnp.zeros((B, 2, N_GRP, RWKV_HEAD, MXU_DIM), F32)
    else:
        s0 = _state_to_groups(ctx[2].astype(F32))
    y0, y1, s_fin = _scan_call(seq, s0, p['eye'])
    y_r = _rwkv_post_call(y0.reshape(B * L, D_RWKV), y1.reshape(B * L, D_RWKV), pre['bonus'], pre['g'],
                          p['gn_w'], p['gn_b'], p['ones_blk'])

    q, k, v, ckv, krp = _mla_prep_call(z_m, rope[0], rope[1], L, p)
    q = q.reshape(B, L, H_MLA * HEAD_SLOT)
    k = k.reshape(B, L, H_MLA * HEAD_SLOT)
    v = v.reshape(B, L, H_MLA * V_HEAD)
    if ctx is not None:
        P = ctx[0].shape[1]
        k_ctx, v_ctx = _ctx_kv_call(ctx[0].reshape(B * P, KV_RANK),
                                    _pad_cols(ctx[1].reshape(B * P, QK_ROPE), LANE), p['w_kvb'])
        k = jnp.concatenate([k, k_ctx.reshape(B, P, H_MLA * HEAD_SLOT)], axis=1)
        v = jnp.concatenate([v, v_ctx.reshape(B, P, H_MLA * V_HEAD)], axis=1)
    y_m = _attn_call(q, k, v).reshape(B * L, H_MLA * V_HEAD)
    x = _outproj_call(x, mod_l, goff, group_tokens, y_r, y_m, p['w_out'])
    state = (_groups_to_state(s_fin), ckv.reshape(B, L, KV_RANK), krp[:, :QK_ROPE].reshape(B, L, QK_ROPE))
    return x, state


def _odd_mixer(x, mod_l, goff, group_tokens, B, L, gamma, tables, k_planes, p):
    T = x.shape[0]
    x2 = x.reshape(T // 2, 2 * D_MODEL)
    z = _inproj_parity_call(x2, mod_l, goff, group_tokens // 2, gamma, p['w_in'], 1536)
    y = _hyena_mixer(z, B, L, tables, k_planes, p)
    return _outproj_parity_call(x2, mod_l, goff, group_tokens // 2, y, p['w_out']).reshape(T, D_MODEL)


def kernel(x_prompt, x_sample, cache_mla_ckv, cache_mla_krope, state_rwkv, c, c_ctx,
           w_mod, b_mod, norm_g, w_ffn_in, w_ffn_out, final_norm_g,
           w_in_even, mu_prev, mu_next, rwkv_w0, rwkv_w2, rwkv_a0, rwkv_a2, rwkv_g2,
           rwkv_kk, rwkv_ka, rwkv_rk, rwkv_gn_w, rwkv_gn_b,
           mla_q_norm, mla_kv_norm, mla_w_qb, mla_w_kvb, w_out_even,
           w_in_odd, hy_conv_w, hy_conv_b, hy_filt_w1, hy_filt_b1, hy_filt_w2, hy_filt_b2,
           hy_filt_w3, hy_filt_freq, hy_bias, w_out_odd):
    Bp, Lp, D = x_prompt.shape
    Bs, Ls, _ = x_sample.shape
    depth = w_mod.shape[0]
    xp = x_prompt.reshape(Bp * Lp, D)
    xs = x_sample.reshape(Bs * Ls, D)
    Tp = Bp * Lp

    cvec = jnp.concatenate([c_ctx[None, :], c, jnp.zeros((SUBLANE - 1 - Bs, D), F32)], axis=0)
    mod = _mod_call(cvec, w_mod, b_mod)

    rope_p = (_pad_cols(jnp.ones((Lp, QK_ROPE), F32), LANE), jnp.zeros((Lp, LANE), F32))
    rope_s = _rope_tables(Ls)
    tabs_p = tabs_s = None
    w_in = w_ffn_in.astype(BF16)
    w_out = w_ffn_out.astype(BF16)
    new_ckv, new_kr, new_s = [], [], []
    for l in range(depth):
        mod_l = mod[l]
        gam = [norm_g[l, s][None, :] for s in range(3)]
        xp = _ffn_call(xp, mod_l, 0, Tp, gam[0], w_in, w_out, l, 0, 0)
        xs = _ffn_call(xs, mod_l, 1, Ls, gam[0], w_in, w_out, l, 0, 0)
        if l % 2 == 0:
            e = l // 2
            p = _pack_even(e, w_in_even, mu_prev, mu_next, rwkv_w0, rwkv_w2, rwkv_a0, rwkv_a2, rwkv_g2,
                           rwkv_kk, rwkv_ka, rwkv_rk, rwkv_gn_w, rwkv_gn_b, mla_q_norm, mla_kv_norm,
                           mla_w_qb, mla_w_kvb, w_out_even)
            ctx = (cache_mla_ckv[:, e], cache_mla_krope[:, e], state_rwkv[:, e])
            xp, st = _even_mixer(xp, mod_l, 0, Bp, Lp, gam[1], p, rope_p, None)
            xs, _ = _even_mixer(xs, mod_l, 1, Bs, Ls, gam[1], p, rope_s, ctx)
            new_s.append(st[0].astype(x_prompt.dtype))
            new_ckv.append(st[1])
            new_kr.append(st[2])
        else:
            o = l // 2
            p = {'w_in': w_in_odd[o].astype(BF16), 'conv_w': hy_conv_w[o], 'conv_b': hy_conv_b[o][None, :],
                 'filt_w1': jnp.pad(hy_filt_w1[o], ((0, POS_PAD - POS_EMB), (0, 0))),
                 'filt_b1': hy_filt_b1[o][None, :], 'filt_w2': hy_filt_w2[o],
                 'filt_b2': hy_filt_b2[o][None, :], 'filt_w3': hy_filt_w3[o],
                 'filt_freq': hy_filt_freq[o], 'bias': hy_bias[o], 'w_out': w_out_odd[o].astype(BF16)}
            if tabs_p is None:
                tabs_p, tabs_s = _dft_tables(Lp), _dft_tables(Ls)
            ks_p = _hyena_filter_spectrum(Lp, tabs_p, p)
            ks_s = _hyena_filter_spectrum(Ls, tabs_s, p)
            xp = _odd_mixer(xp, mod_l, 0, Tp, Bp, Lp, gam[1], tabs_p, ks_p, p)
            xs = _odd_mixer(xs, mod_l, 1, Ls, Bs, Ls, gam[1], tabs_s, ks_s, p)
        xp = _ffn_call(xp, mod_l, 0, Tp, gam[2], w_in, w_out, l, 1, 2)
        xs = _ffn_call(xs, mod_l, 1, Ls, gam[2], w_in, w_out, l, 1, 2)

    fg = final_norm_g[None, :]
    y_prompt = _final_norm_call(xp, fg).reshape(Bp, Lp, D)
    y_sample = _final_norm_call(xs, fg).reshape(Bs, Ls, D)
    return (y_prompt, y_sample, jnp.stack(new_ckv, axis=1), jnp.stack(new_kr, axis=1),
            jnp.stack(new_s, axis=1))
```

```python
import functools
import math

import jax
import jax.numpy as jnp
from jax import lax
from jax.experimental import pallas as pl
from jax.experimental.pallas import tpu as pltpu

F32 = jnp.float32
BF16 = jnp.bfloat16

D_MODEL = 2048
N_MOD = 9
D_FF = 5632
D_RWKV = 1024
RWKV_HEAD = 64
H_RWKV = 16
W_LORA = 64
A_LORA = 64
G_LORA = 160
RWKV_SMALL = 384
RWKV_COLS = 3 * D_RWKV + RWKV_SMALL
RWKV_GN_EPS = 64e-5
H_MLA = 8
QK_NOPE = 128
QK_ROPE = 64
V_HEAD = 128
Q_RANK = 512
KV_RANK = 256
MLA_COLS = Q_RANK + KV_RANK + 2 * QK_ROPE
HEAD_SLOT = 256
ROPE_THETA = 10000.0
ATTN_SCALE = (QK_NOPE + QK_ROPE) ** -0.5
LOG2_E = 1.0 / math.log(2.0)
GRID_W = 64
D_HYENA = 2048
POS_EMB = 33
POS_PAD = 128
FILT_HIDDEN = 64
HYENA_TARGET = 1e-2
FAST_DECAY_PCT = 0.3
SLOW_DECAY_PCT = 1.5
LANE = 128
SUBLANE = 8
MXU_DIM = 256
MIB = 1024 * 1024


def _params(sem, vmem_mib):
    return pltpu.CompilerParams(dimension_semantics=sem, vmem_limit_bytes=vmem_mib * MIB)


def _sigmoid(x):
    return 1.0 / (1.0 + jnp.exp(-x))


def _softplus(x):
    return jnp.maximum(x, 0.0) + jnp.log(1.0 + jnp.exp(-jnp.abs(x)))


def _dot(a, b):
    return jnp.dot(a, b, preferred_element_type=F32)


def _norm_mod(x, gamma, shift, scale):
    xn = x * lax.rsqrt(jnp.mean(x * x, axis=-1, keepdims=True) + 1e-6)
    return (xn * gamma) * (1.0 + scale) + shift


def _mod_kernel(c_ref, w_ref, b_ref, o_ref):
    c = c_ref[...]
    s = c * _sigmoid(c)
    o_ref[0] = _dot(s.astype(BF16), w_ref[0].astype(BF16)) + b_ref[0]


def _mod_call(cvec, w_mod, b_mod):
    L, Dm, N = w_mod.shape
    tn = 1024
    out = pl.pallas_call(
        _mod_kernel,
        grid=(L, N // tn),
        in_specs=[pl.BlockSpec((SUBLANE, Dm), lambda l, j: (0, 0)),
                  pl.BlockSpec((1, Dm, tn), lambda l, j: (l, 0, j)),
                  pl.BlockSpec((1, 1, tn), lambda l, j: (l, 0, j))],
        out_specs=pl.BlockSpec((1, SUBLANE, tn), lambda l, j: (l, 0, j)),
        out_shape=jax.ShapeDtypeStruct((L, SUBLANE, N), F32),
        compiler_params=_params(("arbitrary", "arbitrary"), 40),
        name="adaln_mod",
    )(cvec, w_mod, b_mod.reshape(L, 1, N))
    return out.reshape(L, SUBLANE, N_MOD, Dm)


def _mod_spec(goff, tiles_per_group, nargs):
    if nargs == 1:
        return pl.BlockSpec((None, N_MOD, D_MODEL), lambda i: (goff + i // tiles_per_group, 0, 0))
    return pl.BlockSpec((None, N_MOD, D_MODEL), lambda i, j: (goff + i // tiles_per_group, 0, 0))


def _ffn_kernel(x_ref, mod_ref, g_ref, wg_ref, wu_ref, wo_ref, o_ref, h_sc, acc_sc, *, sub):
    f = pl.program_id(1)

    @pl.when(f == 0)
    def _():
        h = _norm_mod(x_ref[...], g_ref[...], mod_ref[3 * sub:3 * sub + 1, :],
                      mod_ref[3 * sub + 1:3 * sub + 2, :])
        h_sc[...] = h.astype(BF16)
        acc_sc[...] = jnp.zeros_like(acc_sc)

    h = h_sc[...]
    a = _dot(h, wg_ref[...])
    u = _dot(h, wu_ref[...])
    act = (a * _sigmoid(a)) * u
    acc_sc[...] += _dot(act.astype(BF16), wo_ref[...])

    @pl.when(f == pl.num_programs(1) - 1)
    def _():
        o_ref[...] = x_ref[...] + 0.5 * mod_ref[3 * sub + 2:3 * sub + 3, :] * acc_sc[...]


def _ffn_call(x, mod_l, goff, group_tokens, gamma, w_in, w_out, l, s, sub):
    T = x.shape[0]
    tm = min(512, group_tokens)
    tf = 512
    nf = D_FF // tf
    return pl.pallas_call(
        functools.partial(_ffn_kernel, sub=sub),
        grid=(T // tm, nf),
        in_specs=[pl.BlockSpec((tm, D_MODEL), lambda i, f: (i, 0)),
                  _mod_spec(goff, group_tokens // tm, 2),
                  pl.BlockSpec((1, D_MODEL), lambda i, f: (0, 0)),
                  pl.BlockSpec((None, None, D_MODEL, tf), lambda i, f: (l, s, 0, f)),
                  pl.BlockSpec((None, None, D_MODEL, tf), lambda i, f: (l, s, 0, f + nf)),
                  pl.BlockSpec((None, None, tf, D_MODEL), lambda i, f: (l, s, f, 0))],
        out_specs=pl.BlockSpec((tm, D_MODEL), lambda i, f: (i, 0)),
        out_shape=jax.ShapeDtypeStruct((T, D_MODEL), F32),
        scratch_shapes=[pltpu.VMEM((tm, D_MODEL), BF16), pltpu.VMEM((tm, D_MODEL), F32)],
        compiler_params=_params(("parallel", "arbitrary"), 52),
        name="ffn_swiglu",
    )(x, mod_l, gamma, w_in, w_in, w_out)


def _inproj_kernel(x_ref, mod_ref, g_ref, w_ref, o_ref, h_sc, *, col_axis):
    @pl.when(pl.program_id(col_axis) == 0)
    def _():
        h = _norm_mod(x_ref[...], g_ref[...], mod_ref[3:4, :], mod_ref[4:5, :])
        h_sc[...] = h.astype(BF16)

    o_ref[...] = _dot(h_sc[...], w_ref[...])


def _inproj_call(x, mod_l, goff, group_tokens, gamma, w, tn):
    T = x.shape[0]
    N = w.shape[1]
    tm = min(512, group_tokens)
    return pl.pallas_call(
        functools.partial(_inproj_kernel, col_axis=1),
        grid=(T // tm, N // tn),
        in_specs=[pl.BlockSpec((tm, D_MODEL), lambda i, j: (i, 0)),
                  _mod_spec(goff, group_tokens // tm, 2),
                  pl.BlockSpec((1, D_MODEL), lambda i, j: (0, 0)),
                  pl.BlockSpec((D_MODEL, tn), lambda i, j: (0, j))],
        out_specs=pl.BlockSpec((tm, tn), lambda i, j: (i, j)),
        out_shape=jax.ShapeDtypeStruct((T, N), F32),
        scratch_shapes=[pltpu.VMEM((tm, D_MODEL), BF16)],
        compiler_params=_params(("parallel", "arbitrary"), 48),
        name="mixer_inproj",
    )(x, mod_l, gamma, w)


def _rows_of_parity(lane_sc, x, par):
    rows = x.shape[0]
    parts = []
    for c in range(x.shape[1] // LANE):
        lane_sc[c] = x[:, c * LANE:(c + 1) * LANE]
        parts.append(lane_sc[c, pl.ds(par, rows // 2, stride=2), :])
    return jnp.concatenate(parts, axis=1)


def _interleave_rows(lane_sc, even, odd):
    half = even.shape[0]
    parts = []
    for c in range(even.shape[1] // LANE):
        lane_sc[c, pl.ds(0, half, stride=2), :] = even[:, c * LANE:(c + 1) * LANE]
        lane_sc[c, pl.ds(1, half, stride=2), :] = odd[:, c * LANE:(c + 1) * LANE]
        parts.append(lane_sc[c])
    return jnp.concatenate(parts, axis=1)


def _inproj_parity_kernel(x_ref, mod_ref, g_ref, w_ref, o_ref, h_sc, lane_sc, *, half):
    @pl.when(pl.program_id(1) == 0)
    def _():
        h = _norm_mod(x_ref[...], g_ref[...], mod_ref[3:4, :], mod_ref[4:5, :])
        for p in range(2):
            h_sc[p * half:(p + 1) * half, :] = _rows_of_parity(lane_sc, h, p).astype(BF16)

    z = _dot(h_sc[...], w_ref[...])
    o_ref[0] = z[:half]
    o_ref[1] = z[half:]


def _inproj_parity_call(x, mod_l, goff, group_tokens, gamma, w, tn):
    T = x.shape[0]
    N = w.shape[1]
    tm = min(512, group_tokens)
    half = tm // 2
    return pl.pallas_call(
        functools.partial(_inproj_parity_kernel, half=half),
        grid=(T // tm, N // tn),
        in_specs=[pl.BlockSpec((tm, D_MODEL), lambda i, j: (i, 0)),
                  _mod_spec(goff, group_tokens // tm, 2),
                  pl.BlockSpec((1, D_MODEL), lambda i, j: (0, 0)),
                  pl.BlockSpec((D_MODEL, tn), lambda i, j: (0, j))],
        out_specs=pl.BlockSpec((2, half, tn), lambda i, j: (0, i, j)),
        out_shape=jax.ShapeDtypeStruct((2, T // 2, N), F32),
        scratch_shapes=[pltpu.VMEM((tm, D_MODEL), BF16), pltpu.VMEM((D_MODEL // LANE, tm, LANE), F32)],
        compiler_params=_params(("parallel", "arbitrary"), 48),
        name="mixer_inproj_parity",
    )(x, mod_l, gamma, w)


def _outproj_kernel(x_ref, mod_ref, a1_ref, a2_ref, w1_ref, w2_ref, o_ref):
    y = _dot(a1_ref[...], w1_ref[...]) + _dot(a2_ref[...], w2_ref[...])
    o_ref[...] = x_ref[...] + mod_ref[5:6, :] * y


def _outproj_call(x, mod_l, goff, group_tokens, a1, a2, w):
    T = x.shape[0]
    tm = min(512, group_tokens)
    half = D_MODEL // 2
    return pl.pallas_call(
        _outproj_kernel,
        grid=(T // tm,),
        in_specs=[pl.BlockSpec((tm, D_MODEL), lambda i: (i, 0)),
                  _mod_spec(goff, group_tokens // tm, 1),
                  pl.BlockSpec((tm, half), lambda i: (i, 0)),
                  pl.BlockSpec((tm, half), lambda i: (i, 0)),
                  pl.BlockSpec((half, D_MODEL), lambda i: (0, 0)),
                  pl.BlockSpec((half, D_MODEL), lambda i: (1, 0))],
        out_specs=pl.BlockSpec((tm, D_MODEL), lambda i: (i, 0)),
        out_shape=jax.ShapeDtypeStruct((T, D_MODEL), F32),
        compiler_params=_params(("parallel",), 48),
        name="mixer_outproj",
    )(x, mod_l, a1, a2, w, w)


def _outproj_parity_kernel(x_ref, mod_ref, a1_ref, a2_ref, w1_ref, w2_ref, o_ref, lane_sc):
    ys = [_dot(a1_ref[p], w1_ref[...]) + _dot(a2_ref[p], w2_ref[...]) for p in range(2)]
    o_ref[...] = x_ref[...] + mod_ref[5:6, :] * _interleave_rows(lane_sc, ys[0], ys[1])


def _outproj_parity_call(x, mod_l, goff, group_tokens, a, w):
    T = x.shape[0]
    tm = min(512, group_tokens)
    half = tm // 2
    hd = D_MODEL // 2
    return pl.pallas_call(
        _outproj_parity_kernel,
        grid=(T // tm,),
        in_specs=[pl.BlockSpec((tm, D_MODEL), lambda i: (i, 0)),
                  _mod_spec(goff, group_tokens // tm, 1),
                  pl.BlockSpec((2, half, hd), lambda i: (0, i, 0)),
                  pl.BlockSpec((2, half, hd), lambda i: (0, i, 1)),
                  pl.BlockSpec((hd, D_MODEL), lambda i: (0, 0)),
                  pl.BlockSpec((hd, D_MODEL), lambda i: (1, 0))],
        out_specs=pl.BlockSpec((tm, D_MODEL), lambda i: (i, 0)),
        out_shape=jax.ShapeDtypeStruct((T, D_MODEL), F32),
        scratch_shapes=[pltpu.VMEM((D_MODEL // LANE, tm, LANE), F32)],
        compiler_params=_params(("parallel",), 48),
        name="mixer_outproj_parity",
    )(x, mod_l, a, a, w, w)


def _final_norm_kernel(x_ref, g_ref, o_ref):
    x = x_ref[...]
    o_ref[...] = (x * lax.rsqrt(jnp.mean(x * x, axis=-1, keepdims=True) + 1e-6)) * g_ref[...]


def _final_norm_call(x, gamma):
    T = x.shape[0]
    tm = 512
    return pl.pallas_call(
        _final_norm_kernel,
        grid=(T // tm,),
        in_specs=[pl.BlockSpec((tm, D_MODEL), lambda i: (i, 0)),
                  pl.BlockSpec((1, D_MODEL), lambda i: (0, 0))],
        out_specs=pl.BlockSpec((tm, D_MODEL), lambda i: (i, 0)),
        out_shape=jax.ShapeDtypeStruct((T, D_MODEL), F32),
        compiler_params=_params(("parallel",), 32),
        name="final_norm",
    )(x, gamma)


def _shift_prev(cur, halo_prev, row0, seq_len):
    tt = cur.shape[0]
    rid = lax.broadcasted_iota(jnp.int32, (tt, 1), 0)
    pos = jnp.bitwise_and(rid + row0, seq_len - 1)
    prev = pltpu.roll(cur, 1, 0)
    prev = jnp.where(rid == 0, halo_prev[SUBLANE - 1:SUBLANE, :], prev)
    return jnp.where(pos == 0, 0.0, prev)


def _shift_next(cur, halo_next, row0, seq_len):
    tt = cur.shape[0]
    rid = lax.broadcasted_iota(jnp.int32, (tt, 1), 0)
    pos = jnp.bitwise_and(rid + row0, seq_len - 1)
    nxt = pltpu.roll(cur, tt - 1, 0)
    nxt = jnp.where(rid == tt - 1, halo_next[0:1, :], nxt)
    return jnp.where(pos == seq_len - 1, 0.0, nxt)


def _shift_prev_next(cur, halo_prev, halo_next, row0, seq_len):
    return _shift_prev(cur, halo_prev, row0, seq_len), _shift_next(cur, halo_next, row0, seq_len)


def _halo_specs(tt, width, col, total_rows):
    per = tt // SUBLANE
    last = total_rows // SUBLANE - 1
    return [pl.BlockSpec((tt, width), lambda i: (i, col)),
            pl.BlockSpec((SUBLANE, width), lambda i: (jnp.maximum(i * per - 1, 0), col)),
            pl.BlockSpec((SUBLANE, width), lambda i: (jnp.minimum((i + 1) * per, last), col))]


def _segsum(x, ones_blk):
    hi = x.astype(BF16)
    lo = (x - hi.astype(F32)).astype(BF16)
    outs = []
    for g in range(x.shape[1] // MXU_DIM):
        sl = slice(g * MXU_DIM, (g + 1) * MXU_DIM)
        outs.append(_dot(hi[:, sl], ones_blk) + _dot(lo[:, sl], ones_blk))
    return jnp.concatenate(outs, axis=1)


def _rwkv_prep_kernel(z_ref, zp_ref, zn_ref, mup_ref, mun_ref, kk_ref, ka_ref, rk_ref,
                      w0_ref, w2_ref, a0_ref, a2_ref, g2_ref, ones_ref,
                      r_o, v_o, c_o, w0_o, b0_o, k0_o, w1_o, b1_o, k1_o, bonus_o, g_o,
                      *, tt, seq_len):
    row0 = pl.program_id(0) * tt
    cur = z_ref[...]
    prev, nxt = _shift_prev_next(cur, zp_ref[...], zn_ref[...], row0, seq_len)
    zs = cur + mup_ref[...] * (prev - cur) + mun_ref[...] * (nxt - cur)
    r = zs[:, 0:D_RWKV]
    k = zs[:, D_RWKV:2 * D_RWKV]
    v = zs[:, 2 * D_RWKV:3 * D_RWKV]
    small = zs[:, 3 * D_RWKV:RWKV_COLS]
    ones_blk = ones_ref[...]

    kk = k * kk_ref[...]
    kk = kk / jnp.maximum(jnp.sqrt(_segsum(kk * kk, ones_blk)), 1e-12)
    tw = jnp.tanh(small).astype(BF16)
    sg = _sigmoid(small).astype(BF16)
    xs = small.astype(BF16)
    r_o[...] = r
    v_o[...] = v
    c_o[...] = -kk
    g_o[...] = _dot(sg, g2_ref[...])

    bonus = jnp.zeros_like(r)
    outs = ((w0_o, b0_o, k0_o), (w1_o, b1_o, k1_o))
    for d in range(2):
        wl = -_softplus(-(w0_ref[d:d + 1, :] + _dot(tw, w2_ref[d]))) - 0.5
        a = _sigmoid(a0_ref[d:d + 1, :] + _dot(xs, a2_ref[d]))
        kd = k * (1.0 + (a - 1.0) * ka_ref[...])
        w_o, b_o, k_o = outs[d]
        w_o[...] = jnp.exp(-jnp.exp(wl))
        b_o[...] = kk * a
        k_o[...] = kd
        bonus = bonus + _segsum(r * kd * rk_ref[...], ones_blk) * v
    bonus_o[...] = bonus


def _rwkv_prep_call(z_r, seq_len, p):
    T = z_r.shape[0]
    tt = min(256, seq_len)
    row = lambda n: pl.BlockSpec((1, n), lambda i: (0, 0))
    full2 = lambda a, b: pl.BlockSpec((a, b), lambda i: (0, 0))
    full3 = lambda a, b, c: pl.BlockSpec((a, b, c), lambda i: (0, 0, 0))
    in_specs = _halo_specs(tt, RWKV_COLS, 0, T) + [
        row(RWKV_COLS), row(RWKV_COLS), row(D_RWKV), row(D_RWKV), row(D_RWKV),
        full2(2, D_RWKV), full3(2, RWKV_SMALL, D_RWKV), full2(2, D_RWKV), full3(2, RWKV_SMALL, D_RWKV),
        full2(RWKV_SMALL, D_RWKV), full2(MXU_DIM, MXU_DIM)]
    out_spec = pl.BlockSpec((tt, D_RWKV), lambda i: (i, 0))
    out_shape = jax.ShapeDtypeStruct((T, D_RWKV), F32)
    return pl.pallas_call(
        functools.partial(_rwkv_prep_kernel, tt=tt, seq_len=seq_len),
        grid=(T // tt,),
        in_specs=in_specs,
        out_specs=[out_spec] * 11,
        out_shape=[out_shape] * 11,
        compiler_params=_params(("parallel",), 52),
        name="rwkv_prep",
    )(z_r, z_r, z_r, p['mu_prev'], p['mu_next'], p['k_k'], p['k_a'], p['r_k'],
      p['w0'], p['w2'], p['a0'], p['a2'], p['g2'], p['ones_blk'])


N_GRP = D_RWKV // MXU_DIM
HEADS_PER_GRP = H_RWKV // N_GRP


SCAN_ROWS = 16


def _scan_head_masks():
    row = jnp.arange(SCAN_ROWS)[None, :, None]
    head = (jnp.arange(MXU_DIM) // RWKV_HEAD)[None, None, :]
    s = jnp.arange(3)[:, None, None]
    return (row == HEADS_PER_GRP * s + head).astype(F32)


def _scan_kernel(rf, vf, cf, wf, bf, kf, rb, vb, cb, wb, bb, kb, s0_ref, eye_ref, hm_ref,
                 y0_ref, y1_ref, sfin_ref, st, *, tc):
    j = pl.program_id(1)

    @pl.when(j == 0)
    def _():
        st[...] = s0_ref[...]

    eye = eye_ref[...]
    m0 = hm_ref[0]
    m1 = hm_ref[1]
    m2 = hm_ref[2]
    dirs = ((rf, vf, cf, wf, bf, kf, y0_ref), (rb, vb, cb, wb, bb, kb, y1_ref))
    nt = (((1,), (1,)), ((), ()))

    def reduce_heads(m, crow, vrow, rrow):
        wr = jnp.concatenate([crow * m0 + rrow * m2, vrow * m1], axis=1).astype(BF16)
        lhs = jnp.concatenate([m.astype(BF16), eye], axis=1)
        return lax.dot_general(lhs, wr, nt, preferred_element_type=F32)

    def store_y(y_, b, t, g, red):
        tr = red.T
        y_[b, pl.ds(t, 1), g * HEADS_PER_GRP:(g + 1) * HEADS_PER_GRP, :] = (
            tr[2 * HEADS_PER_GRP:3 * HEADS_PER_GRP, :].reshape(1, HEADS_PER_GRP, RWKV_HEAD))

    def body(i, carry):
        chains = []
        for b in range(2):
            chains.append((b, 0, i, jnp.maximum(i - 1, 0)))
            chains.append((b, 1, tc - 1 - i, jnp.minimum(tc - i, tc - 1)))
        reds = []
        for (b, d, t, tp) in chains:
            r_, v_, c_, w_, b_, k_, y_ = dirs[d]
            crow = c_[b, pl.ds(t, 1), :]
            vrow = v_[b, pl.ds(t, 1), :]
            rrow = r_[b, pl.ds(tp, 1), :]
            for g in range(N_GRP):
                sl = slice(g * MXU_DIM, (g + 1) * MXU_DIM)
                reds.append(reduce_heads(st[b, d, g], crow[:, sl], vrow[:, sl], rrow[:, sl]))
        idx = 0
        for (b, d, t, tp) in chains:
            r_, v_, c_, w_, b_, k_, y_ = dirs[d]
            wrow = w_[b, pl.ds(t, 1), :]
            brow = b_[b, pl.ds(t, 1), :]
            krow = k_[b, pl.ds(t, 1), :]
            for g in range(N_GRP):
                sl = slice(g * MXU_DIM, (g + 1) * MXU_DIM)
                red = reds[idx]
                w2 = (brow[:, sl] * m0 + krow[:, sl] * m1).astype(BF16)
                st[b, d, g] = st[b, d, g] * wrow[:, sl] + _dot(red.astype(BF16), w2)
                store_y(y_, b, tp, g, red)
                idx += 1
        return carry

    lax.fori_loop(0, tc, body, 0)

    for b in range(2):
        for d in range(2):
            t = tc - 1 if d == 0 else 0
            rrow = dirs[d][0][b, pl.ds(t, 1), :]
            for g in range(N_GRP):
                sl = slice(g * MXU_DIM, (g + 1) * MXU_DIM)
                red = reduce_heads(st[b, d, g], rrow[:, sl], rrow[:, sl], rrow[:, sl])
                store_y(dirs[d][6], b, t, g, red)

    @pl.when(j == pl.num_programs(1) - 1)
    def _():
        sfin_ref[...] = st[...]


def _scan_call(pre, s0, eye):
    B, T, _ = pre['r'].shape
    tc = min(128, T)
    nj = T // tc
    fwd = pl.BlockSpec((2, tc, D_RWKV), lambda bi, j: (bi, j, 0))
    bwd = pl.BlockSpec((2, tc, D_RWKV), lambda bi, j: (bi, nj - 1 - j, 0))
    yfwd = pl.BlockSpec((2, tc, H_RWKV, RWKV_HEAD), lambda bi, j: (bi, j, 0, 0))
    ybwd = pl.BlockSpec((2, tc, H_RWKV, RWKV_HEAD), lambda bi, j: (bi, nj - 1 - j, 0, 0))
    st_spec = pl.BlockSpec((2, 2, N_GRP, RWKV_HEAD, MXU_DIM), lambda bi, j: (bi, 0, 0, 0, 0))
    y_shape = jax.ShapeDtypeStruct((B, T, H_RWKV, RWKV_HEAD), F32)
    return pl.pallas_call(
        functools.partial(_scan_kernel, tc=tc),
        grid=(B // 2, nj),
        in_specs=[fwd] * 6 + [bwd] * 6 + [
            st_spec,
            pl.BlockSpec((RWKV_HEAD, MXU_DIM), lambda bi, j: (0, 0)),
            pl.BlockSpec((3, SCAN_ROWS, MXU_DIM), lambda bi, j: (0, 0, 0))],
        out_specs=[yfwd, ybwd, st_spec],
        out_shape=[y_shape, y_shape, jax.ShapeDtypeStruct((B, 2, N_GRP, RWKV_HEAD, MXU_DIM), F32)],
        scratch_shapes=[pltpu.VMEM((2, 2, N_GRP, RWKV_HEAD, MXU_DIM), F32)],
        compiler_params=_params(("arbitrary", "arbitrary"), 48),
        name="rwkv_scan",
    )(pre['r'], pre['v'], pre['c'], pre['w0'], pre['b0'], pre['k0'],
      pre['r'], pre['v'], pre['c'], pre['w1'], pre['b1'], pre['k1'], s0, eye.astype(BF16),
      _scan_head_masks())


def _rwkv_post_kernel(y0_ref, y1_ref, bonus_ref, g_ref, gw_ref, gb_ref, ones_ref, o_ref):
    ones_blk = ones_ref[...]
    y = y0_ref[...] + y1_ref[...]
    mu = _segsum(y, ones_blk) * (1.0 / RWKV_HEAD)
    yc = y - mu
    var = _segsum(yc * yc, ones_blk) * (1.0 / RWKV_HEAD)
    yn = yc * lax.rsqrt(var + RWKV_GN_EPS)
    out = (yn * gw_ref[...] + gb_ref[...] + bonus_ref[...]) * g_ref[...]
    o_ref[...] = out.astype(BF16)


def _rwkv_post_call(y0, y1, bonus, g, gn_w, gn_b, ones_blk):
    T = y0.shape[0]
    tt = 512
    blk = pl.BlockSpec((tt, D_RWKV), lambda i: (i, 0))
    row = pl.BlockSpec((1, D_RWKV), lambda i: (0, 0))
    return pl.pallas_call(
        _rwkv_post_kernel,
        grid=(T // tt,),
        in_specs=[blk, blk, blk, blk, row, row, pl.BlockSpec((MXU_DIM, MXU_DIM), lambda i: (0, 0))],
        out_specs=blk,
        out_shape=jax.ShapeDtypeStruct((T, D_RWKV), BF16),
        compiler_params=_params(("parallel",), 40),
        name="rwkv_post",
    )(y0, y1, bonus, g, gn_w, gn_b, ones_blk)


def _rope128(x, cos_t, sin_t):
    return x * cos_t + pltpu.roll(x, QK_ROPE, 1) * sin_t


def _pack_kv(kv, kr_rot, k_o, v_o):
    for h in range(H_MLA):
        k_o[:, h * HEAD_SLOT:h * HEAD_SLOT + QK_NOPE] = kv[:, h * HEAD_SLOT:h * HEAD_SLOT + QK_NOPE].astype(BF16)
        k_o[:, h * HEAD_SLOT + QK_NOPE:(h + 1) * HEAD_SLOT] = kr_rot.astype(BF16)
        v_o[:, h * V_HEAD:(h + 1) * V_HEAD] = kv[:, h * HEAD_SLOT + QK_NOPE:(h + 1) * HEAD_SLOT].astype(BF16)


def _mla_prep_kernel(z_ref, cos_ref, sin_ref, qn_ref, kvn_ref, wq_ref, wkv_ref,
                     q_o, k_o, v_o, ckv_o, kr_o):
    z = z_ref[...]
    cq = z[:, 0:Q_RANK]
    ckv = z[:, Q_RANK:Q_RANK + KV_RANK]
    krp = z[:, Q_RANK + KV_RANK:MLA_COLS]
    cos_t = cos_ref[...]
    sin_t = sin_ref[...]
    cq = (cq * lax.rsqrt(jnp.mean(cq * cq, axis=-1, keepdims=True) + 1e-6)) * qn_ref[...]
    ckv = (ckv * lax.rsqrt(jnp.mean(ckv * ckv, axis=-1, keepdims=True) + 1e-6)) * kvn_ref[...]
    ckv_o[...] = ckv
    kr_o[...] = krp
    q = _dot(cq.astype(BF16), wq_ref[...]) * (ATTN_SCALE * LOG2_E)
    for h in range(H_MLA):
        q_o[:, h * HEAD_SLOT:h * HEAD_SLOT + QK_NOPE] = q[:, h * HEAD_SLOT:h * HEAD_SLOT + QK_NOPE].astype(BF16)
        q_o[:, h * HEAD_SLOT + QK_NOPE:(h + 1) * HEAD_SLOT] = _rope128(
            q[:, h * HEAD_SLOT + QK_NOPE:(h + 1) * HEAD_SLOT], cos_t, sin_t).astype(BF16)
    kv = _dot(ckv.astype(BF16), wkv_ref[...])
    _pack_kv(kv, _rope128(krp, cos_t, sin_t), k_o, v_o)


def _mla_prep_call(z_m, cos_t, sin_t, seq_len, p):
    T = z_m.shape[0]
    tm = min(512, seq_len)
    per_seq = seq_len // tm
    row = lambda n: pl.BlockSpec((1, n), lambda i: (0, 0))
    blk = lambda n: pl.BlockSpec((tm, n), lambda i: (i, 0))
    tab = pl.BlockSpec((tm, LANE), lambda i: (i % per_seq, 0))
    return pl.pallas_call(
        _mla_prep_kernel,
        grid=(T // tm,),
        in_specs=[blk(MLA_COLS), tab, tab, row(Q_RANK), row(KV_RANK),
                  pl.BlockSpec((Q_RANK, H_MLA * HEAD_SLOT), lambda i: (0, 0)),
                  pl.BlockSpec((KV_RANK, H_MLA * HEAD_SLOT), lambda i: (0, 0))],
        out_specs=[blk(H_MLA * HEAD_SLOT), blk(H_MLA * HEAD_SLOT), blk(H_MLA * V_HEAD),
                   blk(KV_RANK), blk(LANE)],
        out_shape=[jax.ShapeDtypeStruct((T, H_MLA * HEAD_SLOT), BF16),
                   jax.ShapeDtypeStruct((T, H_MLA * HEAD_SLOT), BF16),
                   jax.ShapeDtypeStruct((T, H_MLA * V_HEAD), BF16),
                   jax.ShapeDtypeStruct((T, KV_RANK), F32),
                   jax.ShapeDtypeStruct((T, LANE), F32)],
        compiler_params=_params(("parallel",), 48),
        name="mla_prep",
    )(z_m, cos_t, sin_t, p['q_norm'], p['kv_norm'], p['w_qb'], p['w_kvb'])


def _ctx_kv_kernel(ckv_ref, kr_ref, wkv_ref, k_o, v_o):
    kv = _dot(ckv_ref[...].astype(BF16), wkv_ref[...])
    _pack_kv(kv, kr_ref[...], k_o, v_o)


def _ctx_kv_call(ckv_ctx, kr_ctx_pad, w_kvb):
    T = ckv_ctx.shape[0]
    tm = min(512, T)
    blk = lambda n: pl.BlockSpec((tm, n), lambda i: (i, 0))
    return pl.pallas_call(
        _ctx_kv_kernel,
        grid=(T // tm,),
        in_specs=[blk(KV_RANK), blk(LANE), pl.BlockSpec((KV_RANK, H_MLA * HEAD_SLOT), lambda i: (0, 0))],
        out_specs=[blk(H_MLA * HEAD_SLOT), blk(H_MLA * V_HEAD)],
        out_shape=[jax.ShapeDtypeStruct((T, H_MLA * HEAD_SLOT), BF16),
                   jax.ShapeDtypeStruct((T, H_MLA * V_HEAD), BF16)],
        compiler_params=_params(("parallel",), 32),
        name="mla_ctx_kv",
    )(ckv_ctx, kr_ctx_pad, w_kvb)


ATTN_ROWS = 64


def _attn_kernel(q_ref, k_ref, v_ref, o_ref, s_sc, p_sc, *, kc):
    tq, tk = s_sc.shape
    hq = tq // 2
    nt = (((1,), (1,)), ((), ()))

    def scores(h):
        q = q_ref[h * hq:(h + 1) * hq, :]
        for c in range(tk // kc):
            s_sc[h * hq:(h + 1) * hq, c * kc:(c + 1) * kc] = lax.dot_general(
                q, k_ref[c * kc:(c + 1) * kc, :], nt, preferred_element_type=F32)

    def softmax(h):
        sums = []
        for r in range(hq // ATTN_ROWS):
            rows = slice(h * hq + r * ATTN_ROWS, h * hq + (r + 1) * ATTN_ROWS)
            mpart = s_sc[rows, 0:LANE]
            for t in range(1, tk // LANE):
                mpart = jnp.maximum(mpart, s_sc[rows, t * LANE:(t + 1) * LANE])
            m = jnp.max(mpart, axis=-1, keepdims=True)
            lpart = jnp.zeros((ATTN_ROWS, LANE), F32)
            for t in range(tk // LANE):
                p = jnp.exp2(s_sc[rows, t * LANE:(t + 1) * LANE] - m)
                lpart = lpart + p
                p_sc[rows, t * LANE:(t + 1) * LANE] = p.astype(BF16)
            sums.append(jnp.sum(lpart, axis=-1, keepdims=True))
        return jnp.concatenate(sums, axis=0)

    def weighted_values(h, l):
        acc = jnp.zeros((hq, V_HEAD), F32)
        for c in range(tk // kc):
            acc = acc + _dot(p_sc[h * hq:(h + 1) * hq, c * kc:(c + 1) * kc], v_ref[c * kc:(c + 1) * kc, :])
        o_ref[h * hq:(h + 1) * hq, :] = (acc / l).astype(BF16)

    scores(0)
    scores(1)
    l0 = softmax(0)
    weighted_values(0, l0)
    l1 = softmax(1)
    weighted_values(1, l1)


def _attn_call(q, k, v):
    B, Tq, _ = q.shape
    Tk = k.shape[1]
    tq = min(512, Tq)
    return pl.pallas_call(
        functools.partial(_attn_kernel, kc=MXU_DIM),
        grid=(B, H_MLA, Tq // tq),
        in_specs=[pl.BlockSpec((None, tq, HEAD_SLOT), lambda b, h, i: (b, i, h)),
                  pl.BlockSpec((None, Tk, HEAD_SLOT), lambda b, h, i: (b, 0, h)),
                  pl.BlockSpec((None, Tk, V_HEAD), lambda b, h, i: (b, 0, h))],
        out_specs=pl.BlockSpec((None, tq, V_HEAD), lambda b, h, i: (b, i, h)),
        out_shape=jax.ShapeDtypeStruct((B, Tq, H_MLA * V_HEAD), BF16),
        scratch_shapes=[pltpu.VMEM((tq, Tk), F32), pltpu.VMEM((tq, Tk), BF16)],
        compiler_params=_params(("parallel", "parallel", "arbitrary"), 48),
        name="mla_attention",
    )(q, k, v)


def _conv3_kernel(*refs, tt, half_len):
    ins, (cw_refs, cb_refs), outs = refs[0:12], (refs[12:15], refs[15:18]), refs[18:]
    row0 = pl.program_id(0) * tt
    for s in range(3):
        even, odd = ins[4 * s][...], ins[4 * s + 1][...]
        odd_prev = _shift_prev(odd, ins[4 * s + 2][...], row0, half_len)
        even_next = _shift_next(even, ins[4 * s + 3][...], row0, half_len)
        cw = cw_refs[s][...]
        bias = cb_refs[s][...]
        y_even = cw[0:1, :] * odd_prev + cw[1:2, :] * even + cw[2:3, :] * odd + bias
        y_odd = cw[0:1, :] * even + cw[1:2, :] * odd + cw[2:3, :] * even_next + bias
        outs[s][0] = y_even
        outs[s][1] = y_odd
        if s == 2:
            outs[3][0] = y_even.astype(BF16)
            outs[3][1] = y_odd.astype(BF16)


def _conv3_call(z, half_len, conv_w, conv_b):
    T2 = z.shape[1]
    tt = min(128, half_len)
    per = tt // SUBLANE
    last = T2 // SUBLANE - 1
    in_specs = []
    for s in range(3):
        in_specs += [
            pl.BlockSpec((None, tt, D_HYENA), lambda i, s=s: (0, i, s)),
            pl.BlockSpec((None, tt, D_HYENA), lambda i, s=s: (1, i, s)),
            pl.BlockSpec((None, SUBLANE, D_HYENA), lambda i, s=s: (1, jnp.maximum(i * per - 1, 0), s)),
            pl.BlockSpec((None, SUBLANE, D_HYENA), lambda i, s=s: (0, jnp.minimum((i + 1) * per, last), s))]
    in_specs += [pl.BlockSpec((3, D_HYENA), lambda i, s=s: (0, s)) for s in range(3)]
    in_specs += [pl.BlockSpec((1, D_HYENA), lambda i, s=s: (0, s)) for s in range(3)]
    blk = pl.BlockSpec((2, tt, D_HYENA), lambda i: (0, i, 0))
    f32s = jax.ShapeDtypeStruct((2, T2, D_HYENA), F32)
    return pl.pallas_call(
        functools.partial(_conv3_kernel, tt=tt, half_len=half_len),
        grid=(T2 // tt,),
        in_specs=in_specs,
        out_specs=[blk] * 4,
        out_shape=[f32s, f32s, f32s, jax.ShapeDtypeStruct((2, T2, D_HYENA), BF16)],
        compiler_params=_params(("parallel",), 48),
        name="hyena_conv3",
    )(*([z] * 12), conv_w, conv_w, conv_w, conv_b, conv_b, conv_b)


def _filt_mlp_kernel(z_ref, w1_ref, b1_ref, w2_ref, b2_ref, fr_ref, o_ref):
    h = jnp.sin(fr_ref[0:1, :] * (_dot(z_ref[...].astype(BF16), w1_ref[...].astype(BF16)) + b1_ref[...]))
    h = jnp.sin(fr_ref[1:2, :] * (_dot(h.astype(BF16), w2_ref[...].astype(BF16)) + b2_ref[...]))
    o_ref[...] = h.astype(BF16)


def _filt_mlp_call(zpos, w1p, b1, w2, b2, freq):
    L = zpos.shape[0]
    return pl.pallas_call(
        _filt_mlp_kernel,
        out_shape=jax.ShapeDtypeStruct((L, FILT_HIDDEN), BF16),
        compiler_params=pltpu.CompilerParams(vmem_limit_bytes=32 * MIB),
        name="hyena_filter_mlp",
    )(zpos, w1p, b1, w2, b2, freq)


def _filt_gen_kernel(h_ref, tn_ref, dl_ref, w00, w01, w10, w11, o_ref, taps_sc):
    h = h_ref[...]
    L = h.shape[0]
    win = jnp.exp(-tn_ref[...] * dl_ref[...])
    not_first = lax.broadcasted_iota(jnp.int32, (L, 1), 0) > 0
    ws = ((w00, w01), (w10, w11))

    def emit(k, taps):
        taps_sc[...] = taps
        for par in range(2):
            o_ref[k, par] = taps_sc[pl.ds(par, L // 2, stride=2), :].astype(BF16)

    for n in range(2):
        causal = _dot(h, ws[n][0][...].astype(BF16)) * win
        anti = jnp.where(not_first, _dot(h, ws[n][1][...].astype(BF16)) * win, 0.0)
        norm = (jnp.sum(jnp.abs(causal), axis=0, keepdims=True)
                + jnp.sum(jnp.abs(anti), axis=0, keepdims=True))
        emit(2 * n, causal / norm)
        emit(2 * n + 1, anti / norm)


def _filt_gen_call(h2, tnorm, deltas, w3):
    L = h2.shape[0]
    tc = 128
    nc = D_HYENA // tc
    wspec = lambda k: pl.BlockSpec((FILT_HIDDEN, tc), lambda j, k=k: (0, k * nc + j))
    return pl.pallas_call(
        _filt_gen_kernel,
        grid=(nc,),
        in_specs=[pl.BlockSpec((L, FILT_HIDDEN), lambda j: (0, 0)),
                  pl.BlockSpec((L, 1), lambda j: (0, 0)),
                  pl.BlockSpec((1, tc), lambda j: (0, j)),
                  wspec(0), wspec(1), wspec(2), wspec(3)],
        out_specs=pl.BlockSpec((4, 2, L // 2, tc), lambda j: (0, 0, 0, j)),
        out_shape=jax.ShapeDtypeStruct((4, 2, L // 2, D_HYENA), BF16),
        scratch_shapes=[pltpu.VMEM((L, tc), F32)],
        compiler_params=_params(("parallel",), 48),
        name="hyena_filter_gen",
    )(h2, tnorm, deltas, w3, w3, w3, w3)


def _dft_fwd_kernel(f_ref, u_ref, o_ref):
    o_ref[...] = _dot(f_ref[...], u_ref[...])


def _dft_fwd_call(fmat, u):
    B, K, C = u.shape
    M = fmat.shape[0]
    tm = min(512, M)
    tn = min(C, 2048)
    return pl.pallas_call(
        _dft_fwd_kernel,
        grid=(B, C // tn, M // tm),
        in_specs=[pl.BlockSpec((tm, K), lambda b, j, i: (i, 0)),
                  pl.BlockSpec((None, K, tn), lambda b, j, i: (b, 0, j))],
        out_specs=pl.BlockSpec((None, tm, tn), lambda b, j, i: (b, i, j)),
        out_shape=jax.ShapeDtypeStruct((B, M, C), F32),
        compiler_params=_params(("parallel", "parallel", "arbitrary"), 40),
        name="hyena_dft_fwd",
    )(fmat, u)


def _butterfly(gc, gs, hc, hs, tw_c, tw_s, first):
    tc = hc * tw_c - hs * tw_s
    ts = hs * tw_c + hc * tw_s
    p0 = gc + tc
    p2 = gc - tc
    p1 = jnp.where(first, gs, gs + ts)
    p3 = jnp.where(first, hs, ts - gs)
    return p0, p1, p2, p3


def _cmul(ac, a_s, bc, bs):
    return ac * bc - a_s * bs, ac * bs + a_s * bc


def _filt_planes_kernel(ge_ref, ho_ref, twc_ref, tws_ref, o_ref, *, tr):
    first = (lax.broadcasted_iota(jnp.int32, (tr, 1), 0) + pl.program_id(1) * tr) == 0
    tw_c, tw_s = twc_ref[...], tws_ref[...]
    a = _butterfly(ge_ref[0, 0], ge_ref[0, 1], ho_ref[0, 0], ho_ref[0, 1], tw_c, tw_s, first)
    b = _butterfly(ge_ref[1, 0], ge_ref[1, 1], ho_ref[1, 0], ho_ref[1, 1], tw_c, tw_s, first)
    o_ref[0] = a[0] + b[0]
    o_ref[1] = jnp.where(first, a[1] + b[1], a[1] - b[1])
    o_ref[2] = a[2] + b[2]
    o_ref[3] = a[3] - b[3]


def _filt_planes_call(raw, twc, tws):
    _, L, C = raw.shape
    H = L // 2
    tr = min(256, H)
    tc = 512
    blk = lambda par: pl.BlockSpec((None, 2, None, 2, tr, tc), lambda n, i, j: (n, 0, par, 0, i, j))
    tw = pl.BlockSpec((tr, 1), lambda n, i, j: (i, 0))
    raw6 = raw.reshape(2, 2, 2, 2, H, C)
    return pl.pallas_call(
        functools.partial(_filt_planes_kernel, tr=tr),
        grid=(2, H // tr, C // tc),
        in_specs=[blk(0), blk(1), tw, tw],
        out_specs=pl.BlockSpec((None, 4, tr, tc), lambda n, i, j: (n, 0, i, j)),
        out_shape=jax.ShapeDtypeStruct((2, 4, H, C), F32),
        compiler_params=_params(("parallel", "parallel", "parallel"), 40),
        name="hyena_filter_planes",
    )(raw6, raw6, twc, tws)


def _spec_mul_kernel(raw_ref, k_ref, twc_ref, tws_ref, o_ref, *, tr):
    first = (lax.broadcasted_iota(jnp.int32, (tr, 1), 0) + pl.program_id(1) * tr) == 0
    tw_c, tw_s = twc_ref[...], tws_ref[...]
    p0, p1, p2, p3 = _butterfly(raw_ref[0, 0], raw_ref[0, 1], raw_ref[1, 0], raw_ref[1, 1], tw_c, tw_s, first)
    k0, k1, k2, k3 = k_ref[0], k_ref[1], k_ref[2], k_ref[3]
    yac, yas = _cmul(p0, p1, k0, k1)
    ybc, ybs = _cmul(p2, p3, k2, k3)
    ymc, yms = _cmul(p1, p3, k1, k3)
    yac = jnp.where(first, p0 * k0, yac)
    ybc = jnp.where(first, p2 * k2, ybc)
    dc = yac - ybc
    ds = yas + ybs
    o_ref[0, 0] = (yac + ybc).astype(BF16)
    o_ref[0, 1] = jnp.where(first, ymc, yas - ybs).astype(BF16)
    o_ref[1, 0] = jnp.where(first, dc, dc * tw_c + ds * tw_s).astype(BF16)
    o_ref[1, 1] = jnp.where(first, yms, ds * tw_c - dc * tw_s).astype(BF16)


def _spec_mul_call(raw, k_planes, order, twc, tws):
    B2, L, C = raw.shape
    B, H = B2 // 2, L // 2
    tr = min(256, H)
    tc = 512
    tw = pl.BlockSpec((tr, 1), lambda b, i, j: (i, 0))
    out = pl.pallas_call(
        functools.partial(_spec_mul_kernel, tr=tr),
        grid=(B, H // tr, C // tc),
        in_specs=[pl.BlockSpec((2, None, 2, tr, tc), lambda b, i, j: (0, b, 0, i, j)),
                  pl.BlockSpec((None, 4, tr, tc), lambda b, i, j: (order, 0, i, j)),
                  tw, tw],
        out_specs=pl.BlockSpec((2, None, 2, tr, tc), lambda b, i, j: (0, b, 0, i, j)),
        out_shape=jax.ShapeDtypeStruct((2, B, 2, H, C), BF16),
        compiler_params=_params(("parallel", "parallel", "parallel"), 40),
        name="hyena_spectral_mul",
    )(raw.reshape(2, B, 2, H, C), k_planes, twc, tws)
    return out.reshape(B2, L, C)


def _dft_inv_kernel(f_ref, y_ref, gate_ref, u_ref, bias_ref, o_ref, ob_ref):
    conv = _dot(f_ref[...], y_ref[...])
    out = gate_ref[...] * (conv + u_ref[...] * bias_ref[...])
    o_ref[...] = out
    ob_ref[...] = out.astype(BF16)


def _dft_inv_call(imat, y_spec, gate, u, bias):
    B, M, C = u.shape
    K = imat.shape[1]
    tm = min(512, M)
    tn = 1024 if K > 1024 else min(C, 2048)
    blk = pl.BlockSpec((None, tm, tn), lambda b, j, i: (b, i, j))
    return pl.pallas_call(
        _dft_inv_kernel,
        grid=(B, C // tn, M // tm),
        in_specs=[pl.BlockSpec((tm, K), lambda b, j, i: (i, 0)),
                  pl.BlockSpec((None, K, tn), lambda b, j, i: (b, 0, j)),
                  blk, blk, pl.BlockSpec((1, tn), lambda b, j, i: (0, j))],
        out_specs=[blk, blk],
        out_shape=[jax.ShapeDtypeStruct((B, M, C), F32), jax.ShapeDtypeStruct((B, M, C), BF16)],
        compiler_params=_params(("parallel", "parallel", "arbitrary"), 48),
        name="hyena_dft_inv",
    )(imat, y_spec, gate, u, bias)


def _dft_tables(L):
    H = L // 2
    lo = min(64, H)
    hi = H // lo
    g = jnp.arange(H, dtype=jnp.int32)
    theta = 2.0 * math.pi / L
    ang_hi = ((g[:, None] * (jnp.arange(hi, dtype=jnp.int32) * lo)[None, :]) % L).astype(F32) * theta
    ang_lo = ((g[:, None] * jnp.arange(lo, dtype=jnp.int32)[None, :]) % L).astype(F32) * theta
    ch, sh, cl, sl = jnp.cos(ang_hi), jnp.sin(ang_hi), jnp.cos(ang_lo), jnp.sin(ang_lo)
    cos_m = (ch[:, :, None] * cl[:, None, :] - sh[:, :, None] * sl[:, None, :]).reshape(H, H)
    sin_m = (sh[:, :, None] * cl[:, None, :] + ch[:, :, None] * sl[:, None, :]).reshape(H, H)
    alt = jnp.where(jnp.arange(H) % 2 == 0, 1.0, -1.0).astype(F32)
    sin_m = jnp.where((g == 0)[:, None], alt[None, :], sin_m)
    fwd = jnp.concatenate([cos_m, sin_m], axis=0)
    n = 2.0 * L
    w_cos = jnp.where(g == 0, 1.0 / n, 2.0 / n).astype(F32)
    inv = jnp.concatenate([cos_m * w_cos[:, None], sin_m * (2.0 / n)], axis=0).T
    ang_tw = g.astype(F32) * (math.pi / L)
    return fwd.astype(BF16), inv.astype(BF16), jnp.cos(ang_tw)[:, None], jnp.sin(ang_tw)[:, None]


def _filter_positions(L):
    t = jnp.arange(L, dtype=F32)
    t_norm = t / max(L - 1, 1)
    bands = (POS_EMB - 1) // 2
    freqs = jnp.linspace(1e-4, bands - 1, bands, dtype=F32)
    ang = (2.0 * math.pi / L) * t[:, None] * freqs[None, :]
    z = jnp.concatenate([t_norm[:, None], jnp.cos(ang), -jnp.sin(ang)], axis=-1)
    return jnp.pad(z, ((0, 0), (0, POS_PAD - POS_EMB))), t_norm[:, None]


def _hyena_deltas():
    return jnp.linspace(abs(math.log(HYENA_TARGET)) / SLOW_DECAY_PCT,
                        abs(math.log(HYENA_TARGET)) / FAST_DECAY_PCT, D_HYENA, dtype=F32)[None, :]


def _hyena_filter_spectrum(L, tables, p):
    fwd, _, twc, tws = tables
    zpos, tnorm = _filter_positions(L)
    h2 = _filt_mlp_call(zpos, p['filt_w1'], p['filt_b1'], p['filt_w2'], p['filt_b2'], p['filt_freq'])
    filt = _filt_gen_call(h2, tnorm, _hyena_deltas(), p['filt_w3'])
    raw = _dft_fwd_call(fwd, filt.reshape(8, L // 2, D_HYENA))
    return _filt_planes_call(raw, twc, tws)


def _hyena_mixer(z, B, L, tables, k_planes, p):
    fwd, inv, twc, tws = tables
    H = L // 2
    x1, x2, v, vb = _conv3_call(z, H, p['conv_w'], p['conv_b'])
    shp = (2 * B, H, D_HYENA)
    u, ub = v.reshape(shp), vb.reshape(shp)
    for n, gate in enumerate((x1, x2)):
        raw = _dft_fwd_call(fwd, ub)
        yspec = _spec_mul_call(raw, k_planes, n, twc, tws)
        u, ub = _dft_inv_call(inv, yspec, gate.reshape(shp), u, p['bias'][n:n + 1])
    return ub.reshape(2, B * H, D_HYENA)


def _rope_swap(w):
    q = QK_ROPE // 4
    return jnp.concatenate([w[..., q:2 * q], w[..., 0:q], w[..., 3 * q:4 * q], w[..., 2 * q:3 * q]], axis=-1)


def _pad_cols(w, n):
    return jnp.pad(w, [(0, 0)] * (w.ndim - 1) + [(0, n - w.shape[-1])])


def _pack_even(e, w_in_even, mu_prev, mu_next, rwkv_w0, rwkv_w2, rwkv_a0, rwkv_a2, rwkv_g2,
               rwkv_kk, rwkv_ka, rwkv_rk, rwkv_gn_w, rwkv_gn_b, mla_q_norm, mla_kv_norm,
               mla_w_qb, mla_w_kvb, w_out_even):
    n_r = 3 * D_RWKV + W_LORA + A_LORA + G_LORA
    w_in = w_in_even[e]
    w_r = _pad_cols(w_in[:, :n_r], RWKV_COLS).astype(BF16)
    w_m = w_in[:, n_r:]
    kr_cols = w_m[:, Q_RANK + KV_RANK:]
    w_m = jnp.concatenate([w_m, _rope_swap(kr_cols)], axis=-1).astype(BF16)
    small_rows = lambda w, off: jnp.pad(w, [(0, 0)] * (w.ndim - 2)
                                        + [(off, RWKV_SMALL - off - w.shape[-2]), (0, 0)]).astype(BF16)
    wq = mla_w_qb[e].reshape(Q_RANK, H_MLA, QK_NOPE + QK_ROPE)
    wq = jnp.concatenate([wq, _rope_swap(wq[..., QK_NOPE:])], axis=-1).reshape(Q_RANK, H_MLA * HEAD_SLOT)
    blk = jnp.arange(MXU_DIM) // RWKV_HEAD
    return {
        'w_r': w_r, 'w_m': w_m,
        'mu_prev': _pad_cols(mu_prev[e][None, :], RWKV_COLS),
        'mu_next': _pad_cols(mu_next[e][None, :], RWKV_COLS),
        'k_k': rwkv_kk[e][None, :], 'k_a': rwkv_ka[e][None, :],
        'r_k': rwkv_rk[e].reshape(1, D_RWKV),
        'w0': rwkv_w0[e], 'w2': small_rows(rwkv_w2[e], 0),
        'a0': rwkv_a0[e], 'a2': small_rows(rwkv_a2[e], W_LORA),
        'g2': small_rows(rwkv_g2[e], W_LORA + A_LORA),
        'gn_w': rwkv_gn_w[e][None, :], 'gn_b': rwkv_gn_b[e][None, :],
        'q_norm': mla_q_norm[e][None, :], 'kv_norm': mla_kv_norm[e][None, :],
        'w_qb': wq.astype(BF16), 'w_kvb': mla_w_kvb[e].astype(BF16),
        'w_out': w_out_even[e].astype(BF16),
        'ones_blk': (blk[:, None] == blk[None, :]).astype(BF16),
        'eye': (jnp.arange(RWKV_HEAD)[:, None] == (jnp.arange(MXU_DIM) % RWKV_HEAD)[None, :]).astype(F32),
    }


def _rope_tables(L):
    rows = L // GRID_W
    row = jnp.repeat(jnp.arange(rows, dtype=F32), GRID_W)
    col = jnp.tile(jnp.arange(GRID_W, dtype=F32), rows)
    half = QK_ROPE // 2
    inv = 1.0 / (ROPE_THETA ** (jnp.arange(0, half, 2, dtype=F32) / half))
    ar, ac = row[:, None] * inv[None, :], col[:, None] * inv[None, :]
    cos_t = jnp.concatenate([jnp.cos(ar), jnp.cos(ar), jnp.cos(ac), jnp.cos(ac)], axis=-1)
    sin_t = jnp.concatenate([-jnp.sin(ar), jnp.sin(ar), -jnp.sin(ac), jnp.sin(ac)], axis=-1)
    return _pad_cols(cos_t, LANE), _pad_cols(sin_t, LANE)


def _state_to_groups(s):
    B = s.shape[0]
    s = s.reshape(B, 2, N_GRP, H_RWKV // N_GRP, RWKV_HEAD, RWKV_HEAD)
    return jnp.swapaxes(s, 3, 4).reshape(B, 2, N_GRP, RWKV_HEAD, MXU_DIM)


def _groups_to_state(s):
    B = s.shape[0]
    s = s.reshape(B, 2, N_GRP, RWKV_HEAD, H_RWKV // N_GRP, RWKV_HEAD)
    return jnp.swapaxes(s, 3, 4).reshape(B, 2, H_RWKV, RWKV_HEAD, RWKV_HEAD)


def _even_mixer(x, mod_l, goff, B, L, gamma, p, rope, ctx):
    group_tokens = x.shape[0] if ctx is None else L
    z_r = _inproj_call(x, mod_l, goff, group_tokens, gamma, p['w_r'], RWKV_COLS // 3)
    z_m = _inproj_call(x, mod_l, goff, group_tokens, gamma, p['w_m'], MLA_COLS)
    names = ('r', 'v', 'c', 'w0', 'b0', 'k0', 'w1', 'b1', 'k1', 'bonus', 'g')
    pre = dict(zip(names, _rwkv_prep_call(z_r, L, p)))
    seq = {n: pre[n].reshape(B, L, D_RWKV) for n in names[:9]}
    if ctx is None:
        s0 = jnp.zeros((B, 2, N_GRP, RWKV_HEAD, MXU_DIM), F32)
    else:
        s0 = _state_to_groups(ctx[2].astype(F32))
    y0, y1, s_fin = _scan_call(seq, s0, p['eye'])
    y_r = _rwkv_post_call(y0.reshape(B * L, D_RWKV), y1.reshape(B * L, D_RWKV), pre['bonus'], pre['g'],
                          p['gn_w'], p['gn_b'], p['ones_blk'])

    q, k, v, ckv, krp = _mla_prep_call(z_m, rope[0], rope[1], L, p)
    q = q.reshape(B, L, H_MLA * HEAD_SLOT)
    k = k.reshape(B, L, H_MLA * HEAD_SLOT)
    v = v.reshape(B, L, H_MLA * V_HEAD)
    if ctx is not None:
        P = ctx[0].shape[1]
        k_ctx, v_ctx = _ctx_kv_call(ctx[0].reshape(B * P, KV_RANK),
                                    _pad_cols(ctx[1].reshape(B * P, QK_ROPE), LANE), p['w_kvb'])
        k = jnp.concatenate([k, k_ctx.reshape(B, P, H_MLA * HEAD_SLOT)], axis=1)
        v = jnp.concatenate([v, v_ctx.reshape(B, P, H_MLA * V_HEAD)], axis=1)
    y_m = _attn_call(q, k, v).reshape(B * L, H_MLA * V_HEAD)
    x = _outproj_call(x, mod_l, goff, group_tokens, y_r, y_m, p['w_out'])
    state = (_groups_to_state(s_fin), ckv.reshape(B, L, KV_RANK), krp[:, :QK_ROPE].reshape(B, L, QK_ROPE))
    return x, state


def _odd_mixer(x, mod_l, goff, group_tokens, B, L, gamma, tables, k_planes, p):
    z = _inproj_parity_call(x, mod_l, goff, group_tokens, gamma, p['w_in'], 1536)
    y = _hyena_mixer(z, B, L, tables, k_planes, p)
    return _outproj_parity_call(x, mod_l, goff, group_tokens, y, p['w_out'])


def kernel(x_prompt, x_sample, cache_mla_ckv, cache_mla_krope, state_rwkv, c, c_ctx,
           w_mod, b_mod, norm_g, w_ffn_in, w_ffn_out, final_norm_g,
           w_in_even, mu_prev, mu_next, rwkv_w0, rwkv_w2, rwkv_a0, rwkv_a2, rwkv_g2,
           rwkv_kk, rwkv_ka, rwkv_rk, rwkv_gn_w, rwkv_gn_b,
           mla_q_norm, mla_kv_norm, mla_w_qb, mla_w_kvb, w_out_even,
           w_in_odd, hy_conv_w, hy_conv_b, hy_filt_w1, hy_filt_b1, hy_filt_w2, hy_filt_b2,
           hy_filt_w3, hy_filt_freq, hy_bias, w_out_odd):
    Bp, Lp, D = x_prompt.shape
    Bs, Ls, _ = x_sample.shape
    depth = w_mod.shape[0]
    xp = x_prompt.reshape(Bp * Lp, D)
    xs = x_sample.reshape(Bs * Ls, D)
    Tp = Bp * Lp

    cvec = jnp.concatenate([c_ctx[None, :], c, jnp.zeros((SUBLANE - 1 - Bs, D), F32)], axis=0)
    mod = _mod_call(cvec, w_mod, b_mod)

    rope_p = (_pad_cols(jnp.ones((Lp, QK_ROPE), F32), LANE), jnp.zeros((Lp, LANE), F32))
    rope_s = _rope_tables(Ls)
    tabs_p = tabs_s = None
    w_in = w_ffn_in.astype(BF16)
    w_out = w_ffn_out.astype(BF16)
    new_ckv, new_kr, new_s = [], [], []
    for l in range(depth):
        mod_l = mod[l]
        gam = [norm_g[l, s][None, :] for s in range(3)]
        xp = _ffn_call(xp, mod_l, 0, Tp, gam[0], w_in, w_out, l, 0, 0)
        xs = _ffn_call(xs, mod_l, 1, Ls, gam[0], w_in, w_out, l, 0, 0)
        if l % 2 == 0:
            e = l // 2
            p = _pack_even(e, w_in_even, mu_prev, mu_next, rwkv_w0, rwkv_w2, rwkv_a0, rwkv_a2, rwkv_g2,
                           rwkv_kk, rwkv_ka, rwkv_rk, rwkv_gn_w, rwkv_gn_b, mla_q_norm, mla_kv_norm,
                           mla_w_qb, mla_w_kvb, w_out_even)
            ctx = (cache_mla_ckv[:, e], cache_mla_krope[:, e], state_rwkv[:, e])
            xp, st = _even_mixer(xp, mod_l, 0, Bp, Lp, gam[1], p, rope_p, None)
            xs, _ = _even_mixer(xs, mod_l, 1, Bs, Ls, gam[1], p, rope_s, ctx)
            new_s.append(st[0].astype(x_prompt.dtype))
            new_ckv.append(st[1])
            new_kr.append(st[2])
        else:
            o = l // 2
            p = {'w_in': w_in_odd[o].astype(BF16), 'conv_w': hy_conv_w[o], 'conv_b': hy_conv_b[o][None, :],
                 'filt_w1': jnp.pad(hy_filt_w1[o], ((0, POS_PAD - POS_EMB), (0, 0))),
                 'filt_b1': hy_filt_b1[o][None, :], 'filt_w2': hy_filt_w2[o],
                 'filt_b2': hy_filt_b2[o][None, :], 'filt_w3': hy_filt_w3[o],
                 'filt_freq': hy_filt_freq[o], 'bias': hy_bias[o], 'w_out': w_out_odd[o].astype(BF16)}
            if tabs_p is None:
                tabs_p, tabs_s = _dft_tables(Lp), _dft_tables(Ls)
            ks_p = _hyena_filter_spectrum(Lp, tabs_p, p)
            ks_s = _hyena_filter_spectrum(Ls, tabs_s, p)
            xp = _odd_mixer(xp, mod_l, 0, Tp, Bp, Lp, gam[1], tabs_p, ks_p, p)
            xs = _odd_mixer(xs, mod_l, 1, Ls, Bs, Ls, gam[1], tabs_s, ks_s, p)
        xp = _ffn_call(xp, mod_l, 0, Tp, gam[2], w_in, w_out, l, 1, 2)
        xs = _ffn_call(xs, mod_l, 1, Ls, gam[2], w_in, w_out, l, 1, 2)

    fg = final_norm_g[None, :]
    y_prompt = _final_norm_call(xp, fg).reshape(Bp, Lp, D)
    y_sample = _final_norm_call(xs, fg).reshape(Bs, Ls, D)
    return (y_prompt, y_sample, jnp.stack(new_ckv, axis=1), jnp.stack(new_kr, axis=1),
            jnp.stack(new_s, axis=1))
```

```python
import functools
import math

import jax
import jax.numpy as jnp
from jax import lax
from jax.experimental import pallas as pl
from jax.experimental.pallas import tpu as pltpu

F32 = jnp.float32
BF16 = jnp.bfloat16

D_MODEL = 2048
N_MOD = 9
D_FF = 5632
D_RWKV = 1024
RWKV_HEAD = 64
H_RWKV = 16
W_LORA = 64
A_LORA = 64
G_LORA = 160
RWKV_SMALL = 384
RWKV_COLS = 3 * D_RWKV + RWKV_SMALL
RWKV_GN_EPS = 64e-5
H_MLA = 8
QK_NOPE = 128
QK_ROPE = 64
V_HEAD = 128
Q_RANK = 512
KV_RANK = 256
MLA_COLS = Q_RANK + KV_RANK + 2 * QK_ROPE
HEAD_SLOT = 256
ROPE_THETA = 10000.0
ATTN_SCALE = (QK_NOPE + QK_ROPE) ** -0.5
LOG2_E = 1.0 / math.log(2.0)
GRID_W = 64
D_HYENA = 2048
POS_EMB = 33
POS_PAD = 128
FILT_HIDDEN = 64
HYENA_TARGET = 1e-2
FAST_DECAY_PCT = 0.3
SLOW_DECAY_PCT = 1.5
LANE = 128
SUBLANE = 8
MXU_DIM = 256
MIB = 1024 * 1024


def _params(sem, vmem_mib):
    return pltpu.CompilerParams(dimension_semantics=sem, vmem_limit_bytes=vmem_mib * MIB)


def _sigmoid(x):
    return 1.0 / (1.0 + jnp.exp(-x))


def _softplus(x):
    return jnp.maximum(x, 0.0) + jnp.log(1.0 + jnp.exp(-jnp.abs(x)))


def _dot(a, b):
    return jnp.dot(a, b, preferred_element_type=F32)


def _norm_mod(x, gamma, shift, scale):
    xn = x * lax.rsqrt(jnp.mean(x * x, axis=-1, keepdims=True) + 1e-6)
    return (xn * gamma) * (1.0 + scale) + shift


def _mod_kernel(c_ref, w_ref, b_ref, o_ref):
    c = c_ref[...]
    s = c * _sigmoid(c)
    o_ref[0] = _dot(s.astype(BF16), w_ref[0].astype(BF16)) + b_ref[0]


def _mod_call(cvec, w_mod, b_mod):
    L, Dm, N = w_mod.shape
    tn = 1024
    out = pl.pallas_call(
        _mod_kernel,
        grid=(L, N // tn),
        in_specs=[pl.BlockSpec((SUBLANE, Dm), lambda l, j: (0, 0)),
                  pl.BlockSpec((1, Dm, tn), lambda l, j: (l, 0, j)),
                  pl.BlockSpec((1, 1, tn), lambda l, j: (l, 0, j))],
        out_specs=pl.BlockSpec((1, SUBLANE, tn), lambda l, j: (l, 0, j)),
        out_shape=jax.ShapeDtypeStruct((L, SUBLANE, N), F32),
        compiler_params=_params(("arbitrary", "arbitrary"), 40),
        name="adaln_mod",
    )(cvec, w_mod, b_mod.reshape(L, 1, N))
    return out.reshape(L, SUBLANE, N_MOD, Dm)


def _mod_spec(goff, tiles_per_group, nargs):
    if nargs == 1:
        return pl.BlockSpec((None, N_MOD, D_MODEL), lambda i: (goff + i // tiles_per_group, 0, 0))
    return pl.BlockSpec((None, N_MOD, D_MODEL), lambda i, j: (goff + i // tiles_per_group, 0, 0))


def _ffn_kernel(x_ref, mod_ref, g_ref, wg_ref, wu_ref, wo_ref, o_ref, h_sc, acc_sc, *, sub):
    f = pl.program_id(1)

    @pl.when(f == 0)
    def _():
        h = _norm_mod(x_ref[...], g_ref[...], mod_ref[3 * sub:3 * sub + 1, :],
                      mod_ref[3 * sub + 1:3 * sub + 2, :])
        h_sc[...] = h.astype(BF16)
        acc_sc[...] = jnp.zeros_like(acc_sc)

    h = h_sc[...]
    a = _dot(h, wg_ref[...])
    u = _dot(h, wu_ref[...])
    act = (a * _sigmoid(a)) * u
    acc_sc[...] += _dot(act.astype(BF16), wo_ref[...])

    @pl.when(f == pl.num_programs(1) - 1)
    def _():
        o_ref[...] = x_ref[...] + 0.5 * mod_ref[3 * sub + 2:3 * sub + 3, :] * acc_sc[...]


def _ffn_call(x, mod_l, goff, group_tokens, gamma, w_in, w_out, l, s, sub):
    T = x.shape[0]
    tm = min(512, group_tokens)
    tf = 512
    nf = D_FF // tf
    return pl.pallas_call(
        functools.partial(_ffn_kernel, sub=sub),
        grid=(T // tm, nf),
        in_specs=[pl.BlockSpec((tm, D_MODEL), lambda i, f: (i, 0)),
                  _mod_spec(goff, group_tokens // tm, 2),
                  pl.BlockSpec((1, D_MODEL), lambda i, f: (0, 0)),
                  pl.BlockSpec((None, None, D_MODEL, tf), lambda i, f: (l, s, 0, f)),
                  pl.BlockSpec((None, None, D_MODEL, tf), lambda i, f: (l, s, 0, f + nf)),
                  pl.BlockSpec((None, None, tf, D_MODEL), lambda i, f: (l, s, f, 0))],
        out_specs=pl.BlockSpec((tm, D_MODEL), lambda i, f: (i, 0)),
        out_shape=jax.ShapeDtypeStruct((T, D_MODEL), F32),
        scratch_shapes=[pltpu.VMEM((tm, D_MODEL), BF16), pltpu.VMEM((tm, D_MODEL), F32)],
        compiler_params=_params(("parallel", "arbitrary"), 52),
        name="ffn_swiglu",
    )(x, mod_l, gamma, w_in, w_in, w_out)


def _inproj_kernel(x_ref, mod_ref, g_ref, w_ref, o_ref, h_sc, *, col_axis):
    @pl.when(pl.program_id(col_axis) == 0)
    def _():
        h = _norm_mod(x_ref[...], g_ref[...], mod_ref[3:4, :], mod_ref[4:5, :])
        h_sc[...] = h.astype(BF16)

    o_ref[...] = _dot(h_sc[...], w_ref[...])


def _inproj_call(x, mod_l, goff, group_tokens, gamma, w, tn):
    T = x.shape[0]
    N = w.shape[1]
    tm = min(512, group_tokens)
    return pl.pallas_call(
        functools.partial(_inproj_kernel, col_axis=1),
        grid=(T // tm, N // tn),
        in_specs=[pl.BlockSpec((tm, D_MODEL), lambda i, j: (i, 0)),
                  _mod_spec(goff, group_tokens // tm, 2),
                  pl.BlockSpec((1, D_MODEL), lambda i, j: (0, 0)),
                  pl.BlockSpec((D_MODEL, tn), lambda i, j: (0, j))],
        out_specs=pl.BlockSpec((tm, tn), lambda i, j: (i, j)),
        out_shape=jax.ShapeDtypeStruct((T, N), F32),
        scratch_shapes=[pltpu.VMEM((tm, D_MODEL), BF16)],
        compiler_params=_params(("parallel", "arbitrary"), 48),
        name="mixer_inproj",
    )(x, mod_l, gamma, w)


def _rows_of_parity(lane_sc, x, par):
    rows = x.shape[0]
    parts = []
    for c in range(x.shape[1] // LANE):
        lane_sc[c] = x[:, c * LANE:(c + 1) * LANE]
        parts.append(lane_sc[c, pl.ds(par, rows // 2, stride=2), :])
    return jnp.concatenate(parts, axis=1)


def _interleave_rows(lane_sc, even, odd):
    half = even.shape[0]
    parts = []
    for c in range(even.shape[1] // LANE):
        lane_sc[c, pl.ds(0, half, stride=2), :] = even[:, c * LANE:(c + 1) * LANE]
        lane_sc[c, pl.ds(1, half, stride=2), :] = odd[:, c * LANE:(c + 1) * LANE]
        parts.append(lane_sc[c])
    return jnp.concatenate(parts, axis=1)


def _inproj_parity_kernel(x_ref, mod_ref, g_ref, w_ref, o_ref, h_sc, lane_sc, *, half):
    @pl.when(pl.program_id(1) == 0)
    def _():
        h = _norm_mod(x_ref[...], g_ref[...], mod_ref[3:4, :], mod_ref[4:5, :])
        for p in range(2):
            h_sc[p * half:(p + 1) * half, :] = _rows_of_parity(lane_sc, h, p).astype(BF16)

    z = _dot(h_sc[...], w_ref[...])
    o_ref[0] = z[:half]
    o_ref[1] = z[half:]


def _inproj_parity_call(x, mod_l, goff, group_tokens, gamma, w, tn):
    T = x.shape[0]
    N = w.shape[1]
    tm = min(512, group_tokens)
    half = tm // 2
    return pl.pallas_call(
        functools.partial(_inproj_parity_kernel, half=half),
        grid=(T // tm, N // tn),
        in_specs=[pl.BlockSpec((tm, D_MODEL), lambda i, j: (i, 0)),
                  _mod_spec(goff, group_tokens // tm, 2),
                  pl.BlockSpec((1, D_MODEL), lambda i, j: (0, 0)),
                  pl.BlockSpec((D_MODEL, tn), lambda i, j: (0, j))],
        out_specs=pl.BlockSpec((2, half, tn), lambda i, j: (0, i, j)),
        out_shape=jax.ShapeDtypeStruct((2, T // 2, N), F32),
        scratch_shapes=[pltpu.VMEM((tm, D_MODEL), BF16), pltpu.VMEM((D_MODEL // LANE, tm, LANE), F32)],
        compiler_params=_params(("parallel", "arbitrary"), 48),
        name="mixer_inproj_parity",
    )(x, mod_l, gamma, w)


def _outproj_kernel(x_ref, mod_ref, a1_ref, a2_ref, w1_ref, w2_ref, o_ref):
    y = _dot(a1_ref[...], w1_ref[...]) + _dot(a2_ref[...], w2_ref[...])
    o_ref[...] = x_ref[...] + mod_ref[5:6, :] * y


def _outproj_call(x, mod_l, goff, group_tokens, a1, a2, w):
    T = x.shape[0]
    tm = min(512, group_tokens)
    half = D_MODEL // 2
    return pl.pallas_call(
        _outproj_kernel,
        grid=(T // tm,),
        in_specs=[pl.BlockSpec((tm, D_MODEL), lambda i: (i, 0)),
                  _mod_spec(goff, group_tokens // tm, 1),
                  pl.BlockSpec((tm, half), lambda i: (i, 0)),
                  pl.BlockSpec((tm, half), lambda i: (i, 0)),
                  pl.BlockSpec((half, D_MODEL), lambda i: (0, 0)),
                  pl.BlockSpec((half, D_MODEL), lambda i: (1, 0))],
        out_specs=pl.BlockSpec((tm, D_MODEL), lambda i: (i, 0)),
        out_shape=jax.ShapeDtypeStruct((T, D_MODEL), F32),
        compiler_params=_params(("parallel",), 48),
        name="mixer_outproj",
    )(x, mod_l, a1, a2, w, w)


def _outproj_parity_kernel(x_ref, mod_ref, a1_ref, a2_ref, w1_ref, w2_ref, o_ref, lane_sc):
    ys = [_dot(a1_ref[p], w1_ref[...]) + _dot(a2_ref[p], w2_ref[...]) for p in range(2)]
    o_ref[...] = x_ref[...] + mod_ref[5:6, :] * _interleave_rows(lane_sc, ys[0], ys[1])


def _outproj_parity_call(x, mod_l, goff, group_tokens, a, w):
    T = x.shape[0]
    tm = min(512, group_tokens)
    half = tm // 2
    hd = D_MODEL // 2
    return pl.pallas_call(
        _outproj_parity_kernel,
        grid=(T // tm,),
        in_specs=[pl.BlockSpec((tm, D_MODEL), lambda i: (i, 0)),
                  _mod_spec(goff, group_tokens // tm, 1),
                  pl.BlockSpec((2, half, hd), lambda i: (0, i, 0)),
                  pl.BlockSpec((2, half, hd), lambda i: (0, i, 1)),
                  pl.BlockSpec((hd, D_MODEL), lambda i: (0, 0)),
                  pl.BlockSpec((hd, D_MODEL), lambda i: (1, 0))],
        out_specs=pl.BlockSpec((tm, D_MODEL), lambda i: (i, 0)),
        out_shape=jax.ShapeDtypeStruct((T, D_MODEL), F32),
        scratch_shapes=[pltpu.VMEM((D_MODEL // LANE, tm, LANE), F32)],
        compiler_params=_params(("parallel",), 48),
        name="mixer_outproj_parity",
    )(x, mod_l, a, a, w, w)


def _final_norm_kernel(x_ref, g_ref, o_ref):
    x = x_ref[...]
    o_ref[...] = (x * lax.rsqrt(jnp.mean(x * x, axis=-1, keepdims=True) + 1e-6)) * g_ref[...]


def _final_norm_call(x, gamma):
    T = x.shape[0]
    tm = 512
    return pl.pallas_call(
        _final_norm_kernel,
        grid=(T // tm,),
        in_specs=[pl.BlockSpec((tm, D_MODEL), lambda i: (i, 0)),
                  pl.BlockSpec((1, D_MODEL), lambda i: (0, 0))],
        out_specs=pl.BlockSpec((tm, D_MODEL), lambda i: (i, 0)),
        out_shape=jax.ShapeDtypeStruct((T, D_MODEL), F32),
        compiler_params=_params(("parallel",), 32),
        name="final_norm",
    )(x, gamma)


def _shift_prev(cur, halo_prev, row0, seq_len):
    tt = cur.shape[0]
    rid = lax.broadcasted_iota(jnp.int32, (tt, 1), 0)
    pos = jnp.bitwise_and(rid + row0, seq_len - 1)
    prev = pltpu.roll(cur, 1, 0)
    prev = jnp.where(rid == 0, halo_prev[SUBLANE - 1:SUBLANE, :], prev)
    return jnp.where(pos == 0, 0.0, prev)


def _shift_next(cur, halo_next, row0, seq_len):
    tt = cur.shape[0]
    rid = lax.broadcasted_iota(jnp.int32, (tt, 1), 0)
    pos = jnp.bitwise_and(rid + row0, seq_len - 1)
    nxt = pltpu.roll(cur, tt - 1, 0)
    nxt = jnp.where(rid == tt - 1, halo_next[0:1, :], nxt)
    return jnp.where(pos == seq_len - 1, 0.0, nxt)


def _shift_prev_next(cur, halo_prev, halo_next, row0, seq_len):
    return _shift_prev(cur, halo_prev, row0, seq_len), _shift_next(cur, halo_next, row0, seq_len)


def _halo_specs(tt, width, col, total_rows):
    per = tt // SUBLANE
    last = total_rows // SUBLANE - 1
    return [pl.BlockSpec((tt, width), lambda i: (i, col)),
            pl.BlockSpec((SUBLANE, width), lambda i: (jnp.maximum(i * per - 1, 0), col)),
            pl.BlockSpec((SUBLANE, width), lambda i: (jnp.minimum((i + 1) * per, last), col))]


def _segsum(x, ones_blk):
    hi = x.astype(BF16)
    lo = (x - hi.astype(F32)).astype(BF16)
    outs = []
    for g in range(x.shape[1] // MXU_DIM):
        sl = slice(g * MXU_DIM, (g + 1) * MXU_DIM)
        outs.append(_dot(hi[:, sl], ones_blk) + _dot(lo[:, sl], ones_blk))
    return jnp.concatenate(outs, axis=1)


def _rwkv_prep_kernel(z_ref, zp_ref, zn_ref, mup_ref, mun_ref, kk_ref, ka_ref, rk_ref,
                      w0_ref, w2_ref, a0_ref, a2_ref, g2_ref, ones_ref,
                      r_o, v_o, c_o, w0_o, b0_o, k0_o, w1_o, b1_o, k1_o, bonus_o, g_o,
                      *, tt, seq_len):
    row0 = pl.program_id(0) * tt
    cur = z_ref[...]
    prev, nxt = _shift_prev_next(cur, zp_ref[...], zn_ref[...], row0, seq_len)
    zs = cur + mup_ref[...] * (prev - cur) + mun_ref[...] * (nxt - cur)
    r = zs[:, 0:D_RWKV]
    k = zs[:, D_RWKV:2 * D_RWKV]
    v = zs[:, 2 * D_RWKV:3 * D_RWKV]
    small = zs[:, 3 * D_RWKV:RWKV_COLS]
    ones_blk = ones_ref[...]

    kk = k * kk_ref[...]
    kk = kk / jnp.maximum(jnp.sqrt(_segsum(kk * kk, ones_blk)), 1e-12)
    tw = jnp.tanh(small).astype(BF16)
    sg = _sigmoid(small).astype(BF16)
    xs = small.astype(BF16)
    r_o[...] = r
    v_o[...] = v
    c_o[...] = -kk
    g_o[...] = _dot(sg, g2_ref[...])

    bonus = jnp.zeros_like(r)
    outs = ((w0_o, b0_o, k0_o), (w1_o, b1_o, k1_o))
    for d in range(2):
        wl = -_softplus(-(w0_ref[d:d + 1, :] + _dot(tw, w2_ref[d]))) - 0.5
        a = _sigmoid(a0_ref[d:d + 1, :] + _dot(xs, a2_ref[d]))
        kd = k * (1.0 + (a - 1.0) * ka_ref[...])
        w_o, b_o, k_o = outs[d]
        w_o[...] = jnp.exp(-jnp.exp(wl))
        b_o[...] = kk * a
        k_o[...] = kd
        bonus = bonus + _segsum(r * kd * rk_ref[...], ones_blk) * v
    bonus_o[...] = bonus


def _rwkv_prep_call(z_r, seq_len, p):
    T = z_r.shape[0]
    tt = min(256, seq_len)
    row = lambda n: pl.BlockSpec((1, n), lambda i: (0, 0))
    full2 = lambda a, b: pl.BlockSpec((a, b), lambda i: (0, 0))
    full3 = lambda a, b, c: pl.BlockSpec((a, b, c), lambda i: (0, 0, 0))
    in_specs = _halo_specs(tt, RWKV_COLS, 0, T) + [
        row(RWKV_COLS), row(RWKV_COLS), row(D_RWKV), row(D_RWKV), row(D_RWKV),
        full2(2, D_RWKV), full3(2, RWKV_SMALL, D_RWKV), full2(2, D_RWKV), full3(2, RWKV_SMALL, D_RWKV),
        full2(RWKV_SMALL, D_RWKV), full2(MXU_DIM, MXU_DIM)]
    out_spec = pl.BlockSpec((tt, D_RWKV), lambda i: (i, 0))
    out_shape = jax.ShapeDtypeStruct((T, D_RWKV), F32)
    return pl.pallas_call(
        functools.partial(_rwkv_prep_kernel, tt=tt, seq_len=seq_len),
        grid=(T // tt,),
        in_specs=in_specs,
        out_specs=[out_spec] * 11,
        out_shape=[out_shape] * 11,
        compiler_params=_params(("parallel",), 52),
        name="rwkv_prep",
    )(z_r, z_r, z_r, p['mu_prev'], p['mu_next'], p['k_k'], p['k_a'], p['r_k'],
      p['w0'], p['w2'], p['a0'], p['a2'], p['g2'], p['ones_blk'])


N_GRP = D_RWKV // MXU_DIM
HEADS_PER_GRP = H_RWKV // N_GRP


SCAN_ROWS = 16


def _scan_head_mask():
    return (jnp.arange(H_RWKV)[:, None] == (jnp.arange(D_RWKV) // RWKV_HEAD)[None, :]).astype(F32)


def _scan_value_rows(v):
    B, T, _ = v.shape
    tiled = jnp.tile(v.reshape(B, T, H_RWKV, RWKV_HEAD), (1, 1, 1, HEADS_PER_GRP))
    keep = (jnp.arange(H_RWKV)[:, None] // HEADS_PER_GRP) == (jnp.arange(MXU_DIM)[None, :] // RWKV_HEAD)
    return jnp.where(keep, tiled, 0.0).astype(BF16)


def _scan_kernel(rf, vtf, cf, wf, bf, kf, rb, vtb, cb, wb, bb, kb, s0_ref, eye_ref, hm_ref,
                 y0_ref, y1_ref, sfin_ref, st, *, tc):
    j = pl.program_id(1)

    @pl.when(j == 0)
    def _():
        st[...] = s0_ref[...]

    eye = eye_ref[...]
    hmask = hm_ref[...]
    dirs = ((rf, vtf, cf, wf, bf, kf, y0_ref), (rb, vtb, cb, wb, bb, kb, y1_ref))
    nt = (((1,), (1,)), ((), ()))
    zrows = jnp.zeros((SCAN_ROWS, D_RWKV), BF16)
    zeye = jnp.zeros((SCAN_ROWS, MXU_DIM), BF16)

    def head_rows(row):
        return (row * hmask).astype(BF16)

    def reduce_heads(b, d, crow, vrows, rrow):
        lhs = jnp.concatenate([st[b, d, g].astype(BF16) for g in range(N_GRP)] + [eye], axis=1)
        wr = jnp.concatenate([
            jnp.concatenate([head_rows(crow), zeye], axis=1),
            jnp.concatenate([zrows, vrows], axis=1),
            jnp.concatenate([head_rows(rrow), zeye], axis=1),
            jnp.concatenate([zrows, zeye], axis=1)], axis=0)
        return lax.dot_general(lhs, wr, nt, preferred_element_type=F32)

    def store_y(y_, b, t, red):
        tr = red.T
        y_[b, pl.ds(t, 1), :, :] = tr[2 * SCAN_ROWS:3 * SCAN_ROWS, :].reshape(1, H_RWKV, RWKV_HEAD)

    def time_of(d, s):
        s = jnp.clip(s, 0, tc - 1)
        return s if d == 0 else tc - 1 - s

    def first_stage(chains, s):
        reds = []
        for (b, d) in chains:
            r_, vt_, c_, w_, b_, k_, y_ = dirs[d]
            t, tp = time_of(d, s), time_of(d, s - 1)
            reds.append(reduce_heads(b, d, c_[b, pl.ds(t, 1), :], vt_[b, t], r_[b, pl.ds(tp, 1), :]))
        return reds

    def second_stage(chains, s, reds):
        for (b, d), red in zip(chains, reds):
            r_, vt_, c_, w_, b_, k_, y_ = dirs[d]
            t, tp = time_of(d, s), time_of(d, s - 1)
            wrow = w_[b, pl.ds(t, 1), :]
            w2 = jnp.concatenate([head_rows(b_[b, pl.ds(t, 1), :]), head_rows(k_[b, pl.ds(t, 1), :]),
                                  zrows, zrows], axis=0)
            upd = _dot(red.astype(BF16), w2)
            for g in range(N_GRP):
                sl = slice(g * MXU_DIM, (g + 1) * MXU_DIM)
                st[b, d, g] = st[b, d, g] * wrow[:, sl] + upd[:, sl]
            store_y(y_, b, tp, red)

    seqs = s0_ref.shape[0]
    lead = [(b, d) for b in range(seqs // 2) for d in range(2)]
    lag = [(b, d) for b in range(seqs // 2, seqs) for d in range(2)]

    def body(i, lag_reds):
        lead_reds = first_stage(lead, i)
        second_stage(lag, i, lag_reds)
        second_stage(lead, i, lead_reds)
        return tuple(first_stage(lag, i + 1))

    lag_reds = lax.fori_loop(0, tc, body, tuple(first_stage(lag, 0)))

    for (b, d), red in zip(lag, lag_reds):
        store_y(dirs[d][6], b, time_of(d, tc - 1), red)
    for (b, d), red in zip(lead, first_stage(lead, tc)):
        store_y(dirs[d][6], b, time_of(d, tc - 1), red)

    @pl.when(j == pl.num_programs(1) - 1)
    def _():
        sfin_ref[...] = st[...]


def _scan_call(pre, s0, eye):
    B, T, _ = pre['r'].shape
    nb = 4 if B % 4 == 0 else 2
    tc = min(128 if nb == 2 else 32, T)
    nj = T // tc
    fwd = pl.BlockSpec((nb, tc, D_RWKV), lambda bi, j: (bi, j, 0))
    bwd = pl.BlockSpec((nb, tc, D_RWKV), lambda bi, j: (bi, nj - 1 - j, 0))
    yfwd = pl.BlockSpec((nb, tc, H_RWKV, RWKV_HEAD), lambda bi, j: (bi, j, 0, 0))
    ybwd = pl.BlockSpec((nb, tc, H_RWKV, RWKV_HEAD), lambda bi, j: (bi, nj - 1 - j, 0, 0))
    st_spec = pl.BlockSpec((nb, 2, N_GRP, RWKV_HEAD, MXU_DIM), lambda bi, j: (bi, 0, 0, 0, 0))
    y_shape = jax.ShapeDtypeStruct((B, T, H_RWKV, RWKV_HEAD), F32)
    vfwd = pl.BlockSpec((nb, tc, H_RWKV, MXU_DIM), lambda bi, j: (bi, j, 0, 0))
    vbwd = pl.BlockSpec((nb, tc, H_RWKV, MXU_DIM), lambda bi, j: (bi, nj - 1 - j, 0, 0))
    vt = _scan_value_rows(pre['v'])
    return pl.pallas_call(
        functools.partial(_scan_kernel, tc=tc),
        grid=(B // nb, nj),
        in_specs=[fwd, vfwd] + [fwd] * 4 + [bwd, vbwd] + [bwd] * 4 + [
            st_spec,
            pl.BlockSpec((RWKV_HEAD, MXU_DIM), lambda bi, j: (0, 0)),
            pl.BlockSpec((H_RWKV, D_RWKV), lambda bi, j: (0, 0))],
        out_specs=[yfwd, ybwd, st_spec],
        out_shape=[y_shape, y_shape, jax.ShapeDtypeStruct((B, 2, N_GRP, RWKV_HEAD, MXU_DIM), F32)],
        scratch_shapes=[pltpu.VMEM((nb, 2, N_GRP, RWKV_HEAD, MXU_DIM), F32)],
        compiler_params=_params(("arbitrary", "arbitrary"), 48),
        name="rwkv_scan",
    )(pre['r'], vt, pre['c'], pre['w0'], pre['b0'], pre['k0'],
      pre['r'], vt, pre['c'], pre['w1'], pre['b1'], pre['k1'], s0, eye.astype(BF16),
      _scan_head_mask())


def _rwkv_post_kernel(y0_ref, y1_ref, bonus_ref, g_ref, gw_ref, gb_ref, ones_ref, o_ref):
    ones_blk = ones_ref[...]
    y = y0_ref[...] + y1_ref[...]
    mu = _segsum(y, ones_blk) * (1.0 / RWKV_HEAD)
    yc = y - mu
    var = _segsum(yc * yc, ones_blk) * (1.0 / RWKV_HEAD)
    yn = yc * lax.rsqrt(var + RWKV_GN_EPS)
    out = (yn * gw_ref[...] + gb_ref[...] + bonus_ref[...]) * g_ref[...]
    o_ref[...] = out.astype(BF16)


def _rwkv_post_call(y0, y1, bonus, g, gn_w, gn_b, ones_blk):
    T = y0.shape[0]
    tt = 512
    blk = pl.BlockSpec((tt, D_RWKV), lambda i: (i, 0))
    row = pl.BlockSpec((1, D_RWKV), lambda i: (0, 0))
    return pl.pallas_call(
        _rwkv_post_kernel,
        grid=(T // tt,),
        in_specs=[blk, blk, blk, blk, row, row, pl.BlockSpec((MXU_DIM, MXU_DIM), lambda i: (0, 0))],
        out_specs=blk,
        out_shape=jax.ShapeDtypeStruct((T, D_RWKV), BF16),
        compiler_params=_params(("parallel",), 40),
        name="rwkv_post",
    )(y0, y1, bonus, g, gn_w, gn_b, ones_blk)


def _rope128(x, cos_t, sin_t):
    return x * cos_t + pltpu.roll(x, QK_ROPE, 1) * sin_t


def _pack_kv(kv, kr_rot, k_o, v_o):
    for h in range(H_MLA):
        k_o[:, h * HEAD_SLOT:h * HEAD_SLOT + QK_NOPE] = kv[:, h * HEAD_SLOT:h * HEAD_SLOT + QK_NOPE].astype(BF16)
        k_o[:, h * HEAD_SLOT + QK_NOPE:(h + 1) * HEAD_SLOT] = kr_rot.astype(BF16)
        v_o[:, h * V_HEAD:(h + 1) * V_HEAD] = kv[:, h * HEAD_SLOT + QK_NOPE:(h + 1) * HEAD_SLOT].astype(BF16)


def _mla_prep_kernel(z_ref, cos_ref, sin_ref, qn_ref, kvn_ref, wq_ref, wkv_ref,
                     q_o, k_o, v_o, ckv_o, kr_o):
    z = z_ref[...]
    cq = z[:, 0:Q_RANK]
    ckv = z[:, Q_RANK:Q_RANK + KV_RANK]
    krp = z[:, Q_RANK + KV_RANK:MLA_COLS]
    cos_t = cos_ref[...]
    sin_t = sin_ref[...]
    cq = (cq * lax.rsqrt(jnp.mean(cq * cq, axis=-1, keepdims=True) + 1e-6)) * qn_ref[...]
    ckv = (ckv * lax.rsqrt(jnp.mean(ckv * ckv, axis=-1, keepdims=True) + 1e-6)) * kvn_ref[...]
    ckv_o[...] = ckv
    kr_o[...] = krp
    q = _dot(cq.astype(BF16), wq_ref[...]) * (ATTN_SCALE * LOG2_E)
    for h in range(H_MLA):
        q_o[:, h * HEAD_SLOT:h * HEAD_SLOT + QK_NOPE] = q[:, h * HEAD_SLOT:h * HEAD_SLOT + QK_NOPE].astype(BF16)
        q_o[:, h * HEAD_SLOT + QK_NOPE:(h + 1) * HEAD_SLOT] = _rope128(
            q[:, h * HEAD_SLOT + QK_NOPE:(h + 1) * HEAD_SLOT], cos_t, sin_t).astype(BF16)
    kv = _dot(ckv.astype(BF16), wkv_ref[...])
    _pack_kv(kv, _rope128(krp, cos_t, sin_t), k_o, v_o)


def _mla_prep_call(z_m, cos_t, sin_t, seq_len, p):
    T = z_m.shape[0]
    tm = min(512, seq_len)
    per_seq = seq_len // tm
    row = lambda n: pl.BlockSpec((1, n), lambda i: (0, 0))
    blk = lambda n: pl.BlockSpec((tm, n), lambda i: (i, 0))
    tab = pl.BlockSpec((tm, LANE), lambda i: (i % per_seq, 0))
    return pl.pallas_call(
        _mla_prep_kernel,
        grid=(T // tm,),
        in_specs=[blk(MLA_COLS), tab, tab, row(Q_RANK), row(KV_RANK),
                  pl.BlockSpec((Q_RANK, H_MLA * HEAD_SLOT), lambda i: (0, 0)),
                  pl.BlockSpec((KV_RANK, H_MLA * HEAD_SLOT), lambda i: (0, 0))],
        out_specs=[blk(H_MLA * HEAD_SLOT), blk(H_MLA * HEAD_SLOT), blk(H_MLA * V_HEAD),
                   blk(KV_RANK), blk(LANE)],
        out_shape=[jax.ShapeDtypeStruct((T, H_MLA * HEAD_SLOT), BF16),
                   jax.ShapeDtypeStruct((T, H_MLA * HEAD_SLOT), BF16),
                   jax.ShapeDtypeStruct((T, H_MLA * V_HEAD), BF16),
                   jax.ShapeDtypeStruct((T, KV_RANK), F32),
                   jax.ShapeDtypeStruct((T, LANE), F32)],
        compiler_params=_params(("parallel",), 48),
        name="mla_prep",
    )(z_m, cos_t, sin_t, p['q_norm'], p['kv_norm'], p['w_qb'], p['w_kvb'])


def _ctx_kv_kernel(ckv_ref, kr_ref, wkv_ref, k_o, v_o):
    kv = _dot(ckv_ref[...].astype(BF16), wkv_ref[...])
    _pack_kv(kv, kr_ref[...], k_o, v_o)


def _ctx_kv_call(ckv_ctx, kr_ctx_pad, w_kvb):
    T = ckv_ctx.shape[0]
    tm = min(512, T)
    blk = lambda n: pl.BlockSpec((tm, n), lambda i: (i, 0))
    return pl.pallas_call(
        _ctx_kv_kernel,
        grid=(T // tm,),
        in_specs=[blk(KV_RANK), blk(LANE), pl.BlockSpec((KV_RANK, H_MLA * HEAD_SLOT), lambda i: (0, 0))],
        out_specs=[blk(H_MLA * HEAD_SLOT), blk(H_MLA * V_HEAD)],
        out_shape=[jax.ShapeDtypeStruct((T, H_MLA * HEAD_SLOT), BF16),
                   jax.ShapeDtypeStruct((T, H_MLA * V_HEAD), BF16)],
        compiler_params=_params(("parallel",), 32),
        name="mla_ctx_kv",
    )(ckv_ctx, kr_ctx_pad, w_kvb)


ATTN_ROWS = 64


def _attn_kernel(q_ref, k_ref, v_ref, o_ref, s_sc, p_sc, *, kc):
    tq, tk = s_sc.shape
    hq = tq // 2
    nt = (((1,), (1,)), ((), ()))

    def scores(h):
        q = q_ref[h * hq:(h + 1) * hq, :]
        for c in range(tk // kc):
            s_sc[h * hq:(h + 1) * hq, c * kc:(c + 1) * kc] = lax.dot_general(
                q, k_ref[c * kc:(c + 1) * kc, :], nt, preferred_element_type=F32)

    def softmax(h):
        sums = []
        for r in range(hq // ATTN_ROWS):
            rows = slice(h * hq + r * ATTN_ROWS, h * hq + (r + 1) * ATTN_ROWS)
            mpart = s_sc[rows, 0:LANE]
            for t in range(1, tk // LANE):
                mpart = jnp.maximum(mpart, s_sc[rows, t * LANE:(t + 1) * LANE])
            m = jnp.max(mpart, axis=-1, keepdims=True)
            lpart = jnp.zeros((ATTN_ROWS, LANE), F32)
            for t in range(tk // LANE):
                p = jnp.exp2(s_sc[rows, t * LANE:(t + 1) * LANE] - m)
                lpart = lpart + p
                p_sc[rows, t * LANE:(t + 1) * LANE] = p.astype(BF16)
            sums.append(jnp.sum(lpart, axis=-1, keepdims=True))
        return jnp.concatenate(sums, axis=0)

    def weighted_values(h, l):
        acc = jnp.zeros((hq, V_HEAD), F32)
        for c in range(tk // kc):
            acc = acc + _dot(p_sc[h * hq:(h + 1) * hq, c * kc:(c + 1) * kc], v_ref[c * kc:(c + 1) * kc, :])
        o_ref[h * hq:(h + 1) * hq, :] = (acc / l).astype(BF16)

    scores(0)
    scores(1)
    l0 = softmax(0)
    weighted_values(0, l0)
    l1 = softmax(1)
    weighted_values(1, l1)


def _attn_call(q, k, v):
    B, Tq, _ = q.shape
    Tk = k.shape[1]
    tq = min(512, Tq)
    return pl.pallas_call(
        functools.partial(_attn_kernel, kc=MXU_DIM),
        grid=(B, H_MLA, Tq // tq),
        in_specs=[pl.BlockSpec((None, tq, HEAD_SLOT), lambda b, h, i: (b, i, h)),
                  pl.BlockSpec((None, Tk, HEAD_SLOT), lambda b, h, i: (b, 0, h)),
                  pl.BlockSpec((None, Tk, V_HEAD), lambda b, h, i: (b, 0, h))],
        out_specs=pl.BlockSpec((None, tq, V_HEAD), lambda b, h, i: (b, i, h)),
        out_shape=jax.ShapeDtypeStruct((B, Tq, H_MLA * V_HEAD), BF16),
        scratch_shapes=[pltpu.VMEM((tq, Tk), F32), pltpu.VMEM((tq, Tk), BF16)],
        compiler_params=_params(("parallel", "parallel", "arbitrary"), 48),
        name="mla_attention",
    )(q, k, v)


def _conv3_kernel(*refs, tt, half_len):
    ins, (cw_refs, cb_refs), outs = refs[0:12], (refs[12:15], refs[15:18]), refs[18:]
    row0 = pl.program_id(0) * tt
    for s in range(3):
        even, odd = ins[4 * s][...], ins[4 * s + 1][...]
        odd_prev = _shift_prev(odd, ins[4 * s + 2][...], row0, half_len)
        even_next = _shift_next(even, ins[4 * s + 3][...], row0, half_len)
        cw = cw_refs[s][...]
        bias = cb_refs[s][...]
        y_even = cw[0:1, :] * odd_prev + cw[1:2, :] * even + cw[2:3, :] * odd + bias
        y_odd = cw[0:1, :] * even + cw[1:2, :] * odd + cw[2:3, :] * even_next + bias
        outs[s][0] = y_even
        outs[s][1] = y_odd
        if s == 2:
            outs[3][0] = y_even.astype(BF16)
            outs[3][1] = y_odd.astype(BF16)


def _conv3_call(z, half_len, conv_w, conv_b):
    T2 = z.shape[1]
    tt = min(128, half_len)
    per = tt // SUBLANE
    last = T2 // SUBLANE - 1
    in_specs = []
    for s in range(3):
        in_specs += [
            pl.BlockSpec((None, tt, D_HYENA), lambda i, s=s: (0, i, s)),
            pl.BlockSpec((None, tt, D_HYENA), lambda i, s=s: (1, i, s)),
            pl.BlockSpec((None, SUBLANE, D_HYENA), lambda i, s=s: (1, jnp.maximum(i * per - 1, 0), s)),
            pl.BlockSpec((None, SUBLANE, D_HYENA), lambda i, s=s: (0, jnp.minimum((i + 1) * per, last), s))]
    in_specs += [pl.BlockSpec((3, D_HYENA), lambda i, s=s: (0, s)) for s in range(3)]
    in_specs += [pl.BlockSpec((1, D_HYENA), lambda i, s=s: (0, s)) for s in range(3)]
    blk = pl.BlockSpec((2, tt, D_HYENA), lambda i: (0, i, 0))
    f32s = jax.ShapeDtypeStruct((2, T2, D_HYENA), F32)
    return pl.pallas_call(
        functools.partial(_conv3_kernel, tt=tt, half_len=half_len),
        grid=(T2 // tt,),
        in_specs=in_specs,
        out_specs=[blk] * 4,
        out_shape=[f32s, f32s, f32s, jax.ShapeDtypeStruct((2, T2, D_HYENA), BF16)],
        compiler_params=_params(("parallel",), 48),
        name="hyena_conv3",
    )(*([z] * 12), conv_w, conv_w, conv_w, conv_b, conv_b, conv_b)


def _filt_mlp_kernel(z_ref, w1_ref, b1_ref, w2_ref, b2_ref, fr_ref, o_ref):
    h = jnp.sin(fr_ref[0:1, :] * (_dot(z_ref[...].astype(BF16), w1_ref[...].astype(BF16)) + b1_ref[...]))
    h = jnp.sin(fr_ref[1:2, :] * (_dot(h.astype(BF16), w2_ref[...].astype(BF16)) + b2_ref[...]))
    o_ref[...] = h.astype(BF16)


def _filt_mlp_call(zpos, w1p, b1, w2, b2, freq):
    L = zpos.shape[0]
    return pl.pallas_call(
        _filt_mlp_kernel,
        out_shape=jax.ShapeDtypeStruct((L, FILT_HIDDEN), BF16),
        compiler_params=pltpu.CompilerParams(vmem_limit_bytes=32 * MIB),
        name="hyena_filter_mlp",
    )(zpos, w1p, b1, w2, b2, freq)


def _filt_gen_kernel(h_ref, tn_ref, dl_ref, w00, w01, w10, w11, o_ref, taps_sc):
    h = h_ref[...]
    L = h.shape[0]
    win = jnp.exp(-tn_ref[...] * dl_ref[...])
    not_first = lax.broadcasted_iota(jnp.int32, (L, 1), 0) > 0
    ws = ((w00, w01), (w10, w11))

    def emit(k, taps):
        taps_sc[...] = taps
        for par in range(2):
            o_ref[k, par] = taps_sc[pl.ds(par, L // 2, stride=2), :].astype(BF16)

    for n in range(2):
        causal = _dot(h, ws[n][0][...].astype(BF16)) * win
        anti = jnp.where(not_first, _dot(h, ws[n][1][...].astype(BF16)) * win, 0.0)
        norm = (jnp.sum(jnp.abs(causal), axis=0, keepdims=True)
                + jnp.sum(jnp.abs(anti), axis=0, keepdims=True))
        emit(2 * n, causal / norm)
        emit(2 * n + 1, anti / norm)


def _filt_gen_call(h2, tnorm, deltas, w3):
    L = h2.shape[0]
    tc = 128
    nc = D_HYENA // tc
    wspec = lambda k: pl.BlockSpec((FILT_HIDDEN, tc), lambda j, k=k: (0, k * nc + j))
    return pl.pallas_call(
        _filt_gen_kernel,
        grid=(nc,),
        in_specs=[pl.BlockSpec((L, FILT_HIDDEN), lambda j: (0, 0)),
                  pl.BlockSpec((L, 1), lambda j: (0, 0)),
                  pl.BlockSpec((1, tc), lambda j: (0, j)),
                  wspec(0), wspec(1), wspec(2), wspec(3)],
        out_specs=pl.BlockSpec((4, 2, L // 2, tc), lambda j: (0, 0, 0, j)),
        out_shape=jax.ShapeDtypeStruct((4, 2, L // 2, D_HYENA), BF16),
        scratch_shapes=[pltpu.VMEM((L, tc), F32)],
        compiler_params=_params(("parallel",), 48),
        name="hyena_filter_gen",
    )(h2, tnorm, deltas, w3, w3, w3, w3)


def _dft_fwd_kernel(f_ref, u_ref, o_ref):
    o_ref[...] = _dot(f_ref[...], u_ref[...])


def _dft_fwd_call(fmat, u):
    B, K, C = u.shape
    M = fmat.shape[0]
    tm = min(512, M)
    tn = min(C, 2048)
    return pl.pallas_call(
        _dft_fwd_kernel,
        grid=(B, C // tn, M // tm),
        in_specs=[pl.BlockSpec((tm, K), lambda b, j, i: (i, 0)),
                  pl.BlockSpec((None, K, tn), lambda b, j, i: (b, 0, j))],
        out_specs=pl.BlockSpec((None, tm, tn), lambda b, j, i: (b, i, j)),
        out_shape=jax.ShapeDtypeStruct((B, M, C), F32),
        compiler_params=_params(("parallel", "parallel", "arbitrary"), 40),
        name="hyena_dft_fwd",
    )(fmat, u)


def _butterfly(gc, gs, hc, hs, tw_c, tw_s, first):
    tc = hc * tw_c - hs * tw_s
    ts = hs * tw_c + hc * tw_s
    p0 = gc + tc
    p2 = gc - tc
    p1 = jnp.where(first, gs, gs + ts)
    p3 = jnp.where(first, hs, ts - gs)
    return p0, p1, p2, p3


def _cmul(ac, a_s, bc, bs):
    return ac * bc - a_s * bs, ac * bs + a_s * bc


def _filt_planes_kernel(ge_ref, ho_ref, twc_ref, tws_ref, o_ref, *, tr):
    first = (lax.broadcasted_iota(jnp.int32, (tr, 1), 0) + pl.program_id(1) * tr) == 0
    tw_c, tw_s = twc_ref[...], tws_ref[...]
    a = _butterfly(ge_ref[0, 0], ge_ref[0, 1], ho_ref[0, 0], ho_ref[0, 1], tw_c, tw_s, first)
    b = _butterfly(ge_ref[1, 0], ge_ref[1, 1], ho_ref[1, 0], ho_ref[1, 1], tw_c, tw_s, first)
    o_ref[0] = a[0] + b[0]
    o_ref[1] = jnp.where(first, a[1] + b[1], a[1] - b[1])
    o_ref[2] = a[2] + b[2]
    o_ref[3] = a[3] - b[3]


def _filt_planes_call(raw, twc, tws):
    _, L, C = raw.shape
    H = L // 2
    tr = min(256, H)
    tc = 512
    blk = lambda par: pl.BlockSpec((None, 2, None, 2, tr, tc), lambda n, i, j: (n, 0, par, 0, i, j))
    tw = pl.BlockSpec((tr, 1), lambda n, i, j: (i, 0))
    raw6 = raw.reshape(2, 2, 2, 2, H, C)
    return pl.pallas_call(
        functools.partial(_filt_planes_kernel, tr=tr),
        grid=(2, H // tr, C // tc),
        in_specs=[blk(0), blk(1), tw, tw],
        out_specs=pl.BlockSpec((None, 4, tr, tc), lambda n, i, j: (n, 0, i, j)),
        out_shape=jax.ShapeDtypeStruct((2, 4, H, C), F32),
        compiler_params=_params(("parallel", "parallel", "parallel"), 40),
        name="hyena_filter_planes",
    )(raw6, raw6, twc, tws)


def _spec_mul_kernel(raw_ref, k_ref, twc_ref, tws_ref, o_ref, *, tr):
    first = (lax.broadcasted_iota(jnp.int32, (tr, 1), 0) + pl.program_id(1) * tr) == 0
    tw_c, tw_s = twc_ref[...], tws_ref[...]
    p0, p1, p2, p3 = _butterfly(raw_ref[0, 0], raw_ref[0, 1], raw_ref[1, 0], raw_ref[1, 1], tw_c, tw_s, first)
    k0, k1, k2, k3 = k_ref[0], k_ref[1], k_ref[2], k_ref[3]
    yac, yas = _cmul(p0, p1, k0, k1)
    ybc, ybs = _cmul(p2, p3, k2, k3)
    ymc, yms = _cmul(p1, p3, k1, k3)
    yac = jnp.where(first, p0 * k0, yac)
    ybc = jnp.where(first, p2 * k2, ybc)
    dc = yac - ybc
    ds = yas + ybs
    o_ref[0, 0] = (yac + ybc).astype(BF16)
    o_ref[0, 1] = jnp.where(first, ymc, yas - ybs).astype(BF16)
    o_ref[1, 0] = jnp.where(first, dc, dc * tw_c + ds * tw_s).astype(BF16)
    o_ref[1, 1] = jnp.where(first, yms, ds * tw_c - dc * tw_s).astype(BF16)


def _spec_mul_call(raw, k_planes, order, twc, tws):
    B2, L, C = raw.shape
    B, H = B2 // 2, L // 2
    tr = min(256, H)
    tc = 512
    tw = pl.BlockSpec((tr, 1), lambda b, i, j: (i, 0))
    out = pl.pallas_call(
        functools.partial(_spec_mul_kernel, tr=tr),
        grid=(B, H // tr, C // tc),
        in_specs=[pl.BlockSpec((2, None, 2, tr, tc), lambda b, i, j: (0, b, 0, i, j)),
                  pl.BlockSpec((None, 4, tr, tc), lambda b, i, j: (order, 0, i, j)),
                  tw, tw],
        out_specs=pl.BlockSpec((2, None, 2, tr, tc), lambda b, i, j: (0, b, 0, i, j)),
        out_shape=jax.ShapeDtypeStruct((2, B, 2, H, C), BF16),
        compiler_params=_params(("parallel", "parallel", "parallel"), 40),
        name="hyena_spectral_mul",
    )(raw.reshape(2, B, 2, H, C), k_planes, twc, tws)
    return out.reshape(B2, L, C)


def _dft_inv_kernel(f_ref, y_ref, gate_ref, u_ref, bias_ref, o_ref, ob_ref):
    conv = _dot(f_ref[...], y_ref[...])
    out = gate_ref[...] * (conv + u_ref[...] * bias_ref[...])
    o_ref[...] = out
    ob_ref[...] = out.astype(BF16)


def _dft_inv_call(imat, y_spec, gate, u, bias):
    B, M, C = u.shape
    K = imat.shape[1]
    tm = min(512, M)
    tn = 1024 if K > 1024 else min(C, 2048)
    blk = pl.BlockSpec((None, tm, tn), lambda b, j, i: (b, i, j))
    return pl.pallas_call(
        _dft_inv_kernel,
        grid=(B, C // tn, M // tm),
        in_specs=[pl.BlockSpec((tm, K), lambda b, j, i: (i, 0)),
                  pl.BlockSpec((None, K, tn), lambda b, j, i: (b, 0, j)),
                  blk, blk, pl.BlockSpec((1, tn), lambda b, j, i: (0, j))],
        out_specs=[blk, blk],
        out_shape=[jax.ShapeDtypeStruct((B, M, C), F32), jax.ShapeDtypeStruct((B, M, C), BF16)],
        compiler_params=_params(("parallel", "parallel", "arbitrary"), 48),
        name="hyena_dft_inv",
    )(imat, y_spec, gate, u, bias)


def _dft_tables(L):
    H = L // 2
    lo = min(64, H)
    hi = H // lo
    g = jnp.arange(H, dtype=jnp.int32)
    theta = 2.0 * math.pi / L
    ang_hi = ((g[:, None] * (jnp.arange(hi, dtype=jnp.int32) * lo)[None, :]) % L).astype(F32) * theta
    ang_lo = ((g[:, None] * jnp.arange(lo, dtype=jnp.int32)[None, :]) % L).astype(F32) * theta
    ch, sh, cl, sl = jnp.cos(ang_hi), jnp.sin(ang_hi), jnp.cos(ang_lo), jnp.sin(ang_lo)
    cos_m = (ch[:, :, None] * cl[:, None, :] - sh[:, :, None] * sl[:, None, :]).reshape(H, H)
    sin_m = (sh[:, :, None] * cl[:, None, :] + ch[:, :, None] * sl[:, None, :]).reshape(H, H)
    alt = jnp.where(jnp.arange(H) % 2 == 0, 1.0, -1.0).astype(F32)
    sin_m = jnp.where((g == 0)[:, None], alt[None, :], sin_m)
    fwd = jnp.concatenate([cos_m, sin_m], axis=0)
    n = 2.0 * L
    w_cos = jnp.where(g == 0, 1.0 / n, 2.0 / n).astype(F32)
    inv = jnp.concatenate([cos_m * w_cos[:, None], sin_m * (2.0 / n)], axis=0).T
    ang_tw = g.astype(F32) * (math.pi / L)
    return fwd.astype(BF16), inv.astype(BF16), jnp.cos(ang_tw)[:, None], jnp.sin(ang_tw)[:, None]


def _filter_positions(L):
    t = jnp.arange(L, dtype=F32)
    t_norm = t / max(L - 1, 1)
    bands = (POS_EMB - 1) // 2
    freqs = jnp.linspace(1e-4, bands - 1, bands, dtype=F32)
    ang = (2.0 * math.pi / L) * t[:, None] * freqs[None, :]
    z = jnp.concatenate([t_norm[:, None], jnp.cos(ang), -jnp.sin(ang)], axis=-1)
    return jnp.pad(z, ((0, 0), (0, POS_PAD - POS_EMB))), t_norm[:, None]


def _hyena_deltas():
    return jnp.linspace(abs(math.log(HYENA_TARGET)) / SLOW_DECAY_PCT,
                        abs(math.log(HYENA_TARGET)) / FAST_DECAY_PCT, D_HYENA, dtype=F32)[None, :]


def _hyena_filter_spectrum(L, tables, p):
    fwd, _, twc, tws = tables
    zpos, tnorm = _filter_positions(L)
    h2 = _filt_mlp_call(zpos, p['filt_w1'], p['filt_b1'], p['filt_w2'], p['filt_b2'], p['filt_freq'])
    filt = _filt_gen_call(h2, tnorm, _hyena_deltas(), p['filt_w3'])
    raw = _dft_fwd_call(fwd, filt.reshape(8, L // 2, D_HYENA))
    return _filt_planes_call(raw, twc, tws)


def _hyena_mixer(z, B, L, tables, k_planes, p):
    fwd, inv, twc, tws = tables
    H = L // 2
    x1, x2, v, vb = _conv3_call(z, H, p['conv_w'], p['conv_b'])
    shp = (2 * B, H, D_HYENA)
    u, ub = v.reshape(shp), vb.reshape(shp)
    for n, gate in enumerate((x1, x2)):
        raw = _dft_fwd_call(fwd, ub)
        yspec = _spec_mul_call(raw, k_planes, n, twc, tws)
        u, ub = _dft_inv_call(inv, yspec, gate.reshape(shp), u, p['bias'][n:n + 1])
    return ub.reshape(2, B * H, D_HYENA)


def _rope_swap(w):
    q = QK_ROPE // 4
    return jnp.concatenate([w[..., q:2 * q], w[..., 0:q], w[..., 3 * q:4 * q], w[..., 2 * q:3 * q]], axis=-1)


def _pad_cols(w, n):
    return jnp.pad(w, [(0, 0)] * (w.ndim - 1) + [(0, n - w.shape[-1])])


def _pack_even(e, w_in_even, mu_prev, mu_next, rwkv_w0, rwkv_w2, rwkv_a0, rwkv_a2, rwkv_g2,
               rwkv_kk, rwkv_ka, rwkv_rk, rwkv_gn_w, rwkv_gn_b, mla_q_norm, mla_kv_norm,
               mla_w_qb, mla_w_kvb, w_out_even):
    n_r = 3 * D_RWKV + W_LORA + A_LORA + G_LORA
    w_in = w_in_even[e]
    w_r = _pad_cols(w_in[:, :n_r], RWKV_COLS).astype(BF16)
    w_m = w_in[:, n_r:]
    kr_cols = w_m[:, Q_RANK + KV_RANK:]
    w_m = jnp.concatenate([w_m, _rope_swap(kr_cols)], axis=-1).astype(BF16)
    small_rows = lambda w, off: jnp.pad(w, [(0, 0)] * (w.ndim - 2)
                                        + [(off, RWKV_SMALL - off - w.shape[-2]), (0, 0)]).astype(BF16)
    wq = mla_w_qb[e].reshape(Q_RANK, H_MLA, QK_NOPE + QK_ROPE)
    wq = jnp.concatenate([wq, _rope_swap(wq[..., QK_NOPE:])], axis=-1).reshape(Q_RANK, H_MLA * HEAD_SLOT)
    blk = jnp.arange(MXU_DIM) // RWKV_HEAD
    return {
        'w_r': w_r, 'w_m': w_m,
        'mu_prev': _pad_cols(mu_prev[e][None, :], RWKV_COLS),
        'mu_next': _pad_cols(mu_next[e][None, :], RWKV_COLS),
        'k_k': rwkv_kk[e][None, :], 'k_a': rwkv_ka[e][None, :],
        'r_k': rwkv_rk[e].reshape(1, D_RWKV),
        'w0': rwkv_w0[e], 'w2': small_rows(rwkv_w2[e], 0),
        'a0': rwkv_a0[e], 'a2': small_rows(rwkv_a2[e], W_LORA),
        'g2': small_rows(rwkv_g2[e], W_LORA + A_LORA),
        'gn_w': rwkv_gn_w[e][None, :], 'gn_b': rwkv_gn_b[e][None, :],
        'q_norm': mla_q_norm[e][None, :], 'kv_norm': mla_kv_norm[e][None, :],
        'w_qb': wq.astype(BF16), 'w_kvb': mla_w_kvb[e].astype(BF16),
        'w_out': w_out_even[e].astype(BF16),
        'ones_blk': (blk[:, None] == blk[None, :]).astype(BF16),
        'eye': (jnp.arange(RWKV_HEAD)[:, None] == (jnp.arange(MXU_DIM) % RWKV_HEAD)[None, :]).astype(F32),
    }


def _rope_tables(L):
    rows = L // GRID_W
    row = jnp.repeat(jnp.arange(rows, dtype=F32), GRID_W)
    col = jnp.tile(jnp.arange(GRID_W, dtype=F32), rows)
    half = QK_ROPE // 2
    inv = 1.0 / (ROPE_THETA ** (jnp.arange(0, half, 2, dtype=F32) / half))
    ar, ac = row[:, None] * inv[None, :], col[:, None] * inv[None, :]
    cos_t = jnp.concatenate([jnp.cos(ar), jnp.cos(ar), jnp.cos(ac), jnp.cos(ac)], axis=-1)
    sin_t = jnp.concatenate([-jnp.sin(ar), jnp.sin(ar), -jnp.sin(ac), jnp.sin(ac)], axis=-1)
    return _pad_cols(cos_t, LANE), _pad_cols(sin_t, LANE)


def _state_to_groups(s):
    B = s.shape[0]
    s = s.reshape(B, 2, N_GRP, H_RWKV // N_GRP, RWKV_HEAD, RWKV_HEAD)
    return jnp.swapaxes(s, 3, 4).reshape(B, 2, N_GRP, RWKV_HEAD, MXU_DIM)


def _groups_to_state(s):
    B = s.shape[0]
    s = s.reshape(B, 2, N_GRP, RWKV_HEAD, H_RWKV // N_GRP, RWKV_HEAD)
    return jnp.swapaxes(s, 3, 4).reshape(B, 2, H_RWKV, RWKV_HEAD, RWKV_HEAD)


def _even_mixer(x, mod_l, goff, B, L, gamma, p, rope, ctx):
    group_tokens = x.shape[0] if ctx is None else L
    z_r = _inproj_call(x, mod_l, goff, group_tokens, gamma, p['w_r'], RWKV_COLS // 3)
    z_m = _inproj_call(x, mod_l, goff, group_tokens, gamma, p['w_m'], MLA_COLS)
    names = ('r', 'v', 'c', 'w0', 'b0', 'k0', 'w1', 'b1', 'k1', 'bonus', 'g')
    pre = dict(zip(names, _rwkv_prep_call(z_r, L, p)))
    seq = {n: pre[n].reshape(B, L, D_RWKV) for n in names[:9]}
    if ctx is None:
        s0 = jnp.zeros((B, 2, N_GRP, RWKV_HEAD, MXU_DIM), F32)
    else:
        s0 = _state_to_groups(ctx[2].astype(F32))
    y0, y1, s_fin = _scan_call(seq, s0, p['eye'])
    y_r = _rwkv_post_call(y0.reshape(B * L, D_RWKV), y1.reshape(B * L, D_RWKV), pre['bonus'], pre['g'],
                          p['gn_w'], p['gn_b'], p['ones_blk'])

    q, k, v, ckv, krp = _mla_prep_call(z_m, rope[0], rope[1], L, p)
    q = q.reshape(B, L, H_MLA * HEAD_SLOT)
    k = k.reshape(B, L, H_MLA * HEAD_SLOT)
    v = v.reshape(B, L, H_MLA * V_HEAD)
    if ctx is not None:
        P = ctx[0].shape[1]
        k_ctx, v_ctx = _ctx_kv_call(ctx[0].reshape(B * P, KV_RANK),
                                    _pad_cols(ctx[1].reshape(B * P, QK_ROPE), LANE), p['w_kvb'])
        k = jnp.concatenate([k, k_ctx.reshape(B, P, H_MLA * HEAD_SLOT)], axis=1)
        v = jnp.concatenate([v, v_ctx.reshape(B, P, H_MLA * V_HEAD)], axis=1)
    y_m = _attn_call(q, k, v).reshape(B * L, H_MLA * V_HEAD)
    x = _outproj_call(x, mod_l, goff, group_tokens, y_r, y_m, p['w_out'])
    state = (_groups_to_state(s_fin), ckv.reshape(B, L, KV_RANK), krp[:, :QK_ROPE].reshape(B, L, QK_ROPE))
    return x, state


def _odd_mixer(x, mod_l, goff, group_tokens, B, L, gamma, tables, k_planes, p):
    z = _inproj_parity_call(x, mod_l, goff, group_tokens, gamma, p['w_in'], 1536)
    y = _hyena_mixer(z, B, L, tables, k_planes, p)
    return _outproj_parity_call(x, mod_l, goff, group_tokens, y, p['w_out'])


def kernel(x_prompt, x_sample, cache_mla_ckv, cache_mla_krope, state_rwkv, c, c_ctx,
           w_mod, b_mod, norm_g, w_ffn_in, w_ffn_out, final_norm_g,
           w_in_even, mu_prev, mu_next, rwkv_w0, rwkv_w2, rwkv_a0, rwkv_a2, rwkv_g2,
           rwkv_kk, rwkv_ka, rwkv_rk, rwkv_gn_w, rwkv_gn_b,
           mla_q_norm, mla_kv_norm, mla_w_qb, mla_w_kvb, w_out_even,
           w_in_odd, hy_conv_w, hy_conv_b, hy_filt_w1, hy_filt_b1, hy_filt_w2, hy_filt_b2,
           hy_filt_w3, hy_filt_freq, hy_bias, w_out_odd):
    Bp, Lp, D = x_prompt.shape
    Bs, Ls, _ = x_sample.shape
    depth = w_mod.shape[0]
    xp = x_prompt.reshape(Bp * Lp, D)
    xs = x_sample.reshape(Bs * Ls, D)
    Tp = Bp * Lp

    cvec = jnp.concatenate([c_ctx[None, :], c, jnp.zeros((SUBLANE - 1 - Bs, D), F32)], axis=0)
    mod = _mod_call(cvec, w_mod, b_mod)

    rope_p = (_pad_cols(jnp.ones((Lp, QK_ROPE), F32), LANE), jnp.zeros((Lp, LANE), F32))
    rope_s = _rope_tables(Ls)
    tabs_p = tabs_s = None
    w_in = w_ffn_in.astype(BF16)
    w_out = w_ffn_out.astype(BF16)
    new_ckv, new_kr, new_s = [], [], []
    for l in range(depth):
        mod_l = mod[l]
        gam = [norm_g[l, s][None, :] for s in range(3)]
        xp = _ffn_call(xp, mod_l, 0, Tp, gam[0], w_in, w_out, l, 0, 0)
        xs = _ffn_call(xs, mod_l, 1, Ls, gam[0], w_in, w_out, l, 0, 0)
        if l % 2 == 0:
            e = l // 2
            p = _pack_even(e, w_in_even, mu_prev, mu_next, rwkv_w0, rwkv_w2, rwkv_a0, rwkv_a2, rwkv_g2,
                           rwkv_kk, rwkv_ka, rwkv_rk, rwkv_gn_w, rwkv_gn_b, mla_q_norm, mla_kv_norm,
                           mla_w_qb, mla_w_kvb, w_out_even)
            ctx = (cache_mla_ckv[:, e], cache_mla_krope[:, e], state_rwkv[:, e])
            xp, st = _even_mixer(xp, mod_l, 0, Bp, Lp, gam[1], p, rope_p, None)
            xs, _ = _even_mixer(xs, mod_l, 1, Bs, Ls, gam[1], p, rope_s, ctx)
            new_s.append(st[0].astype(x_prompt.dtype))
            new_ckv.append(st[1])
            new_kr.append(st[2])
        else:
            o = l // 2
            p = {'w_in': w_in_odd[o].astype(BF16), 'conv_w': hy_conv_w[o], 'conv_b': hy_conv_b[o][None, :],
                 'filt_w1': jnp.pad(hy_filt_w1[o], ((0, POS_PAD - POS_EMB), (0, 0))),
                 'filt_b1': hy_filt_b1[o][None, :], 'filt_w2': hy_filt_w2[o],
                 'filt_b2': hy_filt_b2[o][None, :], 'filt_w3': hy_filt_w3[o],
                 'filt_freq': hy_filt_freq[o], 'bias': hy_bias[o], 'w_out': w_out_odd[o].astype(BF16)}
            if tabs_p is None:
                tabs_p, tabs_s = _dft_tables(Lp), _dft_tables(Ls)
            ks_p = _hyena_filter_spectrum(Lp, tabs_p, p)
            ks_s = _hyena_filter_spectrum(Ls, tabs_s, p)
            xp = _odd_mixer(xp, mod_l, 0, Tp, Bp, Lp, gam[1], tabs_p, ks_p, p)
            xs = _odd_mixer(xs, mod_l, 1, Ls, Bs, Ls, gam[1], tabs_s, ks_s, p)
        xp = _ffn_call(xp, mod_l, 0, Tp, gam[2], w_in, w_out, l, 1, 2)
        xs = _ffn_call(xs, mod_l, 1, Ls, gam[2], w_in, w_out, l, 1, 2)

    fg = final_norm_g[None, :]
    y_prompt = _final_norm_call(xp, fg).reshape(Bp, Lp, D)
    y_sample = _final_norm_call(xs, fg).reshape(Bs, Ls, D)
    return (y_prompt, y_sample, jnp.stack(new_ckv, axis=1), jnp.stack(new_kr, axis=1),
            jnp.stack(new_s, axis=1))
```

```python
import functools
import math

import jax
import jax.numpy as jnp
from jax import lax
from jax.experimental import pallas as pl
from jax.experimental.pallas import tpu as pltpu

F32 = jnp.float32
BF16 = jnp.bfloat16

D_MODEL = 2048
N_MOD = 9
D_FF = 5632
D_RWKV = 1024
RWKV_HEAD = 64
H_RWKV = 16
W_LORA = 64
A_LORA = 64
G_LORA = 160
RWKV_SMALL = 384
RWKV_COLS = 3 * D_RWKV + RWKV_SMALL
RWKV_GN_EPS = 64e-5
H_MLA = 8
QK_NOPE = 128
QK_ROPE = 64
V_HEAD = 128
Q_RANK = 512
KV_RANK = 256
MLA_COLS = Q_RANK + KV_RANK + 2 * QK_ROPE
HEAD_SLOT = 256
ROPE_THETA = 10000.0
ATTN_SCALE = (QK_NOPE + QK_ROPE) ** -0.5
LOG2_E = 1.0 / math.log(2.0)
GRID_W = 64
D_HYENA = 2048
POS_EMB = 33
POS_PAD = 128
FILT_HIDDEN = 64
HYENA_TARGET = 1e-2
FAST_DECAY_PCT = 0.3
SLOW_DECAY_PCT = 1.5
LANE = 128
SUBLANE = 8
MXU_DIM = 256
MIB = 1024 * 1024


def _params(sem, vmem_mib):
    return pltpu.CompilerParams(dimension_semantics=sem, vmem_limit_bytes=vmem_mib * MIB)


def _sigmoid(x):
    return 1.0 / (1.0 + jnp.exp(-x))


def _softplus(x):
    return jnp.maximum(x, 0.0) + jnp.log(1.0 + jnp.exp(-jnp.abs(x)))


def _dot(a, b):
    return jnp.dot(a, b, preferred_element_type=F32)


def _norm_mod(x, gamma, shift, scale):
    xn = x * lax.rsqrt(jnp.mean(x * x, axis=-1, keepdims=True) + 1e-6)
    return (xn * gamma) * (1.0 + scale) + shift


def _mod_kernel(c_ref, w_ref, b_ref, o_ref):
    c = c_ref[...]
    s = c * _sigmoid(c)
    o_ref[0] = _dot(s.astype(BF16), w_ref[0].astype(BF16)) + b_ref[0]


def _mod_call(cvec, w_mod, b_mod):
    L, Dm, N = w_mod.shape
    tn = 1024
    out = pl.pallas_call(
        _mod_kernel,
        grid=(L, N // tn),
        in_specs=[pl.BlockSpec((SUBLANE, Dm), lambda l, j: (0, 0)),
                  pl.BlockSpec((1, Dm, tn), lambda l, j: (l, 0, j)),
                  pl.BlockSpec((1, 1, tn), lambda l, j: (l, 0, j))],
        out_specs=pl.BlockSpec((1, SUBLANE, tn), lambda l, j: (l, 0, j)),
        out_shape=jax.ShapeDtypeStruct((L, SUBLANE, N), F32),
        compiler_params=_params(("arbitrary", "arbitrary"), 40),
        name="adaln_mod",
    )(cvec, w_mod, b_mod.reshape(L, 1, N))
    return out.reshape(L, SUBLANE, N_MOD, Dm)


def _mod_spec(goff, tiles_per_group, nargs):
    if nargs == 1:
        return pl.BlockSpec((None, N_MOD, D_MODEL), lambda i: (goff + i // tiles_per_group, 0, 0))
    return pl.BlockSpec((None, N_MOD, D_MODEL), lambda i, j: (goff + i // tiles_per_group, 0, 0))


def _ffn_kernel(x_ref, mod_ref, g_ref, wg_ref, wu_ref, wo_ref, o_ref, h_sc, acc_sc, *, sub):
    f = pl.program_id(1)

    @pl.when(f == 0)
    def _():
        h = _norm_mod(x_ref[...], g_ref[...], mod_ref[3 * sub:3 * sub + 1, :],
                      mod_ref[3 * sub + 1:3 * sub + 2, :])
        h_sc[...] = h.astype(BF16)
        acc_sc[...] = jnp.zeros_like(acc_sc)

    h = h_sc[...]
    a = _dot(h, wg_ref[...])
    u = _dot(h, wu_ref[...])
    act = (a * _sigmoid(a)) * u
    acc_sc[...] += _dot(act.astype(BF16), wo_ref[...])

    @pl.when(f == pl.num_programs(1) - 1)
    def _():
        o_ref[...] = x_ref[...] + 0.5 * mod_ref[3 * sub + 2:3 * sub + 3, :] * acc_sc[...]


def _ffn_call(x, mod_l, goff, group_tokens, gamma, w_in, w_out, l, s, sub):
    T = x.shape[0]
    tm = min(512, group_tokens)
    tf = 512
    nf = D_FF // tf
    return pl.pallas_call(
        functools.partial(_ffn_kernel, sub=sub),
        grid=(T // tm, nf),
        in_specs=[pl.BlockSpec((tm, D_MODEL), lambda i, f: (i, 0)),
                  _mod_spec(goff, group_tokens // tm, 2),
                  pl.BlockSpec((1, D_MODEL), lambda i, f: (0, 0)),
                  pl.BlockSpec((None, None, D_MODEL, tf), lambda i, f: (l, s, 0, f)),
                  pl.BlockSpec((None, None, D_MODEL, tf), lambda i, f: (l, s, 0, f + nf)),
                  pl.BlockSpec((None, None, tf, D_MODEL), lambda i, f: (l, s, f, 0))],
        out_specs=pl.BlockSpec((tm, D_MODEL), lambda i, f: (i, 0)),
        out_shape=jax.ShapeDtypeStruct((T, D_MODEL), F32),
        scratch_shapes=[pltpu.VMEM((tm, D_MODEL), BF16), pltpu.VMEM((tm, D_MODEL), F32)],
        compiler_params=_params(("parallel", "arbitrary"), 52),
        name="ffn_swiglu",
    )(x, mod_l, gamma, w_in, w_in, w_out)


def _inproj_kernel(x_ref, mod_ref, g_ref, w_ref, o_ref, h_sc, *, col_axis):
    @pl.when(pl.program_id(col_axis) == 0)
    def _():
        h = _norm_mod(x_ref[...], g_ref[...], mod_ref[3:4, :], mod_ref[4:5, :])
        h_sc[...] = h.astype(BF16)

    o_ref[...] = _dot(h_sc[...], w_ref[...])


def _inproj_call(x, mod_l, goff, group_tokens, gamma, w, tn):
    T = x.shape[0]
    N = w.shape[1]
    tm = min(512, group_tokens)
    return pl.pallas_call(
        functools.partial(_inproj_kernel, col_axis=1),
        grid=(T // tm, N // tn),
        in_specs=[pl.BlockSpec((tm, D_MODEL), lambda i, j: (i, 0)),
                  _mod_spec(goff, group_tokens // tm, 2),
                  pl.BlockSpec((1, D_MODEL), lambda i, j: (0, 0)),
                  pl.BlockSpec((D_MODEL, tn), lambda i, j: (0, j))],
        out_specs=pl.BlockSpec((tm, tn), lambda i, j: (i, j)),
        out_shape=jax.ShapeDtypeStruct((T, N), F32),
        scratch_shapes=[pltpu.VMEM((tm, D_MODEL), BF16)],
        compiler_params=_params(("parallel", "arbitrary"), 48),
        name="mixer_inproj",
    )(x, mod_l, gamma, w)


def _rows_of_parity(lane_sc, x, par):
    rows = x.shape[0]
    parts = []
    for c in range(x.shape[1] // LANE):
        lane_sc[c] = x[:, c * LANE:(c + 1) * LANE]
        parts.append(lane_sc[c, pl.ds(par, rows // 2, stride=2), :])
    return jnp.concatenate(parts, axis=1)


def _interleave_rows(lane_sc, even, odd):
    half = even.shape[0]
    parts = []
    for c in range(even.shape[1] // LANE):
        lane_sc[c, pl.ds(0, half, stride=2), :] = even[:, c * LANE:(c + 1) * LANE]
        lane_sc[c, pl.ds(1, half, stride=2), :] = odd[:, c * LANE:(c + 1) * LANE]
        parts.append(lane_sc[c])
    return jnp.concatenate(parts, axis=1)


def _inproj_parity_kernel(x_ref, mod_ref, g_ref, w_ref, o_ref, h_sc, lane_sc, *, half):
    @pl.when(pl.program_id(1) == 0)
    def _():
        h = _norm_mod(x_ref[...], g_ref[...], mod_ref[3:4, :], mod_ref[4:5, :])
        for p in range(2):
            h_sc[p * half:(p + 1) * half, :] = _rows_of_parity(lane_sc, h, p).astype(BF16)

    z = _dot(h_sc[...], w_ref[...])
    o_ref[0] = z[:half]
    o_ref[1] = z[half:]


def _inproj_parity_call(x, mod_l, goff, group_tokens, gamma, w, tn):
    T = x.shape[0]
    N = w.shape[1]
    tm = min(512, group_tokens)
    half = tm // 2
    return pl.pallas_call(
        functools.partial(_inproj_parity_kernel, half=half),
        grid=(T // tm, N // tn),
        in_specs=[pl.BlockSpec((tm, D_MODEL), lambda i, j: (i, 0)),
                  _mod_spec(goff, group_tokens // tm, 2),
                  pl.BlockSpec((1, D_MODEL), lambda i, j: (0, 0)),
                  pl.BlockSpec((D_MODEL, tn), lambda i, j: (0, j))],
        out_specs=pl.BlockSpec((2, half, tn), lambda i, j: (0, i, j)),
        out_shape=jax.ShapeDtypeStruct((2, T // 2, N), F32),
        scratch_shapes=[pltpu.VMEM((tm, D_MODEL), BF16), pltpu.VMEM((D_MODEL // LANE, tm, LANE), F32)],
        compiler_params=_params(("parallel", "arbitrary"), 48),
        name="mixer_inproj_parity",
    )(x, mod_l, gamma, w)


def _outproj_kernel(x_ref, mod_ref, a1_ref, a2_ref, w1_ref, w2_ref, o_ref):
    y = _dot(a1_ref[...], w1_ref[...]) + _dot(a2_ref[...], w2_ref[...])
    o_ref[...] = x_ref[...] + mod_ref[5:6, :] * y


def _outproj_call(x, mod_l, goff, group_tokens, a1, a2, w):
    T = x.shape[0]
    tm = min(512, group_tokens)
    half = D_MODEL // 2
    return pl.pallas_call(
        _outproj_kernel,
        grid=(T // tm,),
        in_specs=[pl.BlockSpec((tm, D_MODEL), lambda i: (i, 0)),
                  _mod_spec(goff, group_tokens // tm, 1),
                  pl.BlockSpec((tm, half), lambda i: (i, 0)),
                  pl.BlockSpec((tm, half), lambda i: (i, 0)),
                  pl.BlockSpec((half, D_MODEL), lambda i: (0, 0)),
                  pl.BlockSpec((half, D_MODEL), lambda i: (1, 0))],
        out_specs=pl.BlockSpec((tm, D_MODEL), lambda i: (i, 0)),
        out_shape=jax.ShapeDtypeStruct((T, D_MODEL), F32),
        compiler_params=_params(("parallel",), 48),
        name="mixer_outproj",
    )(x, mod_l, a1, a2, w, w)


def _outproj_parity_kernel(x_ref, mod_ref, a1_ref, a2_ref, w1_ref, w2_ref, o_ref, lane_sc):
    ys = [_dot(a1_ref[p], w1_ref[...]) + _dot(a2_ref[p], w2_ref[...]) for p in range(2)]
    o_ref[...] = x_ref[...] + mod_ref[5:6, :] * _interleave_rows(lane_sc, ys[0], ys[1])


def _outproj_parity_call(x, mod_l, goff, group_tokens, a, w):
    T = x.shape[0]
    tm = min(512, group_tokens)
    half = tm // 2
    hd = D_MODEL // 2
    return pl.pallas_call(
        _outproj_parity_kernel,
        grid=(T // tm,),
        in_specs=[pl.BlockSpec((tm, D_MODEL), lambda i: (i, 0)),
                  _mod_spec(goff, group_tokens // tm, 1),
                  pl.BlockSpec((2, half, hd), lambda i: (0, i, 0)),
                  pl.BlockSpec((2, half, hd), lambda i: (0, i, 1)),
                  pl.BlockSpec((hd, D_MODEL), lambda i: (0, 0)),
                  pl.BlockSpec((hd, D_MODEL), lambda i: (1, 0))],
        out_specs=pl.BlockSpec((tm, D_MODEL), lambda i: (i, 0)),
        out_shape=jax.ShapeDtypeStruct((T, D_MODEL), F32),
        scratch_shapes=[pltpu.VMEM((D_MODEL // LANE, tm, LANE), F32)],
        compiler_params=_params(("parallel",), 48),
        name="mixer_outproj_parity",
    )(x, mod_l, a, a, w, w)


def _final_norm_kernel(x_ref, g_ref, o_ref):
    x = x_ref[...]
    o_ref[...] = (x * lax.rsqrt(jnp.mean(x * x, axis=-1, keepdims=True) + 1e-6)) * g_ref[...]


def _final_norm_call(x, gamma):
    T = x.shape[0]
    tm = 512
    return pl.pallas_call(
        _final_norm_kernel,
        grid=(T // tm,),
        in_specs=[pl.BlockSpec((tm, D_MODEL), lambda i: (i, 0)),
                  pl.BlockSpec((1, D_MODEL), lambda i: (0, 0))],
        out_specs=pl.BlockSpec((tm, D_MODEL), lambda i: (i, 0)),
        out_shape=jax.ShapeDtypeStruct((T, D_MODEL), F32),
        compiler_params=_params(("parallel",), 32),
        name="final_norm",
    )(x, gamma)


def _shift_prev(cur, halo_prev, row0, seq_len):
    tt = cur.shape[0]
    rid = lax.broadcasted_iota(jnp.int32, (tt, 1), 0)
    pos = jnp.bitwise_and(rid + row0, seq_len - 1)
    prev = pltpu.roll(cur, 1, 0)
    prev = jnp.where(rid == 0, halo_prev[SUBLANE - 1:SUBLANE, :], prev)
    return jnp.where(pos == 0, 0.0, prev)


def _shift_next(cur, halo_next, row0, seq_len):
    tt = cur.shape[0]
    rid = lax.broadcasted_iota(jnp.int32, (tt, 1), 0)
    pos = jnp.bitwise_and(rid + row0, seq_len - 1)
    nxt = pltpu.roll(cur, tt - 1, 0)
    nxt = jnp.where(rid == tt - 1, halo_next[0:1, :], nxt)
    return jnp.where(pos == seq_len - 1, 0.0, nxt)


def _shift_prev_next(cur, halo_prev, halo_next, row0, seq_len):
    return _shift_prev(cur, halo_prev, row0, seq_len), _shift_next(cur, halo_next, row0, seq_len)


def _halo_specs(tt, width, col, total_rows):
    per = tt // SUBLANE
    last = total_rows // SUBLANE - 1
    return [pl.BlockSpec((tt, width), lambda i: (i, col)),
            pl.BlockSpec((SUBLANE, width), lambda i: (jnp.maximum(i * per - 1, 0), col)),
            pl.BlockSpec((SUBLANE, width), lambda i: (jnp.minimum((i + 1) * per, last), col))]


def _segsum(x, ones_blk):
    hi = x.astype(BF16)
    lo = (x - hi.astype(F32)).astype(BF16)
    outs = []
    for g in range(x.shape[1] // MXU_DIM):
        sl = slice(g * MXU_DIM, (g + 1) * MXU_DIM)
        outs.append(_dot(hi[:, sl], ones_blk) + _dot(lo[:, sl], ones_blk))
    return jnp.concatenate(outs, axis=1)


def _rwkv_prep_kernel(z_ref, zp_ref, zn_ref, mup_ref, mun_ref, kk_ref, ka_ref, rk_ref,
                      w0_ref, w2_ref, a0_ref, a2_ref, g2_ref, ones_ref,
                      r_o, v_o, c_o, w0_o, b0_o, k0_o, w1_o, b1_o, k1_o, bonus_o, g_o,
                      *, tt, seq_len):
    row0 = pl.program_id(0) * tt
    cur = z_ref[...]
    prev, nxt = _shift_prev_next(cur, zp_ref[...], zn_ref[...], row0, seq_len)
    zs = cur + mup_ref[...] * (prev - cur) + mun_ref[...] * (nxt - cur)
    r = zs[:, 0:D_RWKV]
    k = zs[:, D_RWKV:2 * D_RWKV]
    v = zs[:, 2 * D_RWKV:3 * D_RWKV]
    small = zs[:, 3 * D_RWKV:RWKV_COLS]
    ones_blk = ones_ref[...]

    kk = k * kk_ref[...]
    kk = kk / jnp.maximum(jnp.sqrt(_segsum(kk * kk, ones_blk)), 1e-12)
    tw = jnp.tanh(small).astype(BF16)
    sg = _sigmoid(small).astype(BF16)
    xs = small.astype(BF16)
    r_o[...] = r
    v_o[...] = v
    c_o[...] = -kk
    g_o[...] = _dot(sg, g2_ref[...])

    bonus = jnp.zeros_like(r)
    outs = ((w0_o, b0_o, k0_o), (w1_o, b1_o, k1_o))
    for d in range(2):
        wl = -_softplus(-(w0_ref[d:d + 1, :] + _dot(tw, w2_ref[d]))) - 0.5
        a = _sigmoid(a0_ref[d:d + 1, :] + _dot(xs, a2_ref[d]))
        kd = k * (1.0 + (a - 1.0) * ka_ref[...])
        w_o, b_o, k_o = outs[d]
        w_o[...] = jnp.exp(-jnp.exp(wl))
        b_o[...] = kk * a
        k_o[...] = kd
        bonus = bonus + _segsum(r * kd * rk_ref[...], ones_blk) * v
    bonus_o[...] = bonus


def _rwkv_prep_call(z_r, seq_len, p):
    T = z_r.shape[0]
    tt = min(256, seq_len)
    row = lambda n: pl.BlockSpec((1, n), lambda i: (0, 0))
    full2 = lambda a, b: pl.BlockSpec((a, b), lambda i: (0, 0))
    full3 = lambda a, b, c: pl.BlockSpec((a, b, c), lambda i: (0, 0, 0))
    in_specs = _halo_specs(tt, RWKV_COLS, 0, T) + [
        row(RWKV_COLS), row(RWKV_COLS), row(D_RWKV), row(D_RWKV), row(D_RWKV),
        full2(2, D_RWKV), full3(2, RWKV_SMALL, D_RWKV), full2(2, D_RWKV), full3(2, RWKV_SMALL, D_RWKV),
        full2(RWKV_SMALL, D_RWKV), full2(MXU_DIM, MXU_DIM)]
    out_spec = pl.BlockSpec((tt, D_RWKV), lambda i: (i, 0))
    out_shape = jax.ShapeDtypeStruct((T, D_RWKV), F32)
    return pl.pallas_call(
        functools.partial(_rwkv_prep_kernel, tt=tt, seq_len=seq_len),
        grid=(T // tt,),
        in_specs=in_specs,
        out_specs=[out_spec] * 11,
        out_shape=[out_shape] * 11,
        compiler_params=_params(("parallel",), 52),
        name="rwkv_prep",
    )(z_r, z_r, z_r, p['mu_prev'], p['mu_next'], p['k_k'], p['k_a'], p['r_k'],
      p['w0'], p['w2'], p['a0'], p['a2'], p['g2'], p['ones_blk'])


N_GRP = D_RWKV // MXU_DIM
HEADS_PER_GRP = H_RWKV // N_GRP


SCAN_ROWS = 16
SCAN_UNROLL = 8


def _scan_head_mask():
    return (jnp.arange(H_RWKV)[:, None] == (jnp.arange(D_RWKV) // RWKV_HEAD)[None, :]).astype(F32)


def _scan_value_rows(v):
    B, T, _ = v.shape
    tiled = jnp.tile(v.reshape(B, T, H_RWKV, RWKV_HEAD), (1, 1, 1, HEADS_PER_GRP))
    keep = (jnp.arange(H_RWKV)[:, None] // HEADS_PER_GRP) == (jnp.arange(MXU_DIM)[None, :] // RWKV_HEAD)
    return jnp.where(keep, tiled, 0.0).astype(BF16)


def _look_kernel(c_ref, cp_ref, cn_ref, v_ref, w0_ref, b0_ref, k0_ref, w1_ref, b1_ref, k1_ref,
                 ones_ref, sel_ref, wc0_o, vk0_o, be0_o, wc1_o, vk1_o, be1_o, *, tt, seq_len):
    row0 = pl.program_id(0) * tt
    c = c_ref[...]
    c_prev, c_next = _shift_prev_next(c, cp_ref[...], cn_ref[...], row0, seq_len)
    ones_blk = ones_ref[...]
    sel = sel_ref[...]
    v = v_ref[...]
    for cn, (w_ref, b_ref, k_ref, wc_o, vk_o, be_o) in (
            (c_next, (w0_ref, b0_ref, k0_ref, wc0_o, vk0_o, be0_o)),
            (c_prev, (w1_ref, b1_ref, k1_ref, wc1_o, vk1_o, be1_o))):
        wc_o[...] = w_ref[...] * cn
        vk_o[...] = v * _segsum(k_ref[...] * cn, ones_blk)
        bc = b_ref[...] * cn
        hi = bc.astype(BF16)
        lo = (bc - hi.astype(F32)).astype(BF16)
        be_o[...] = _dot(hi, sel) + _dot(lo, sel)


def _look_call(pre, seq_len, ones_blk):
    T = pre['c'].shape[0]
    tt = min(256, seq_len)
    blk = pl.BlockSpec((tt, D_RWKV), lambda i: (i, 0))
    sel = (jnp.arange(D_RWKV)[:, None] // RWKV_HEAD == jnp.arange(LANE)[None, :]).astype(BF16)
    big = jax.ShapeDtypeStruct((T, D_RWKV), F32)
    small = jax.ShapeDtypeStruct((T, LANE), F32)
    sblk = pl.BlockSpec((tt, LANE), lambda i: (i, 0))
    outs = pl.pallas_call(
        functools.partial(_look_kernel, tt=tt, seq_len=seq_len),
        grid=(T // tt,),
        in_specs=_halo_specs(tt, D_RWKV, 0, T) + [blk] * 7 + [
            pl.BlockSpec((MXU_DIM, MXU_DIM), lambda i: (0, 0)),
            pl.BlockSpec((D_RWKV, LANE), lambda i: (0, 0))],
        out_specs=[blk, blk, sblk, blk, blk, sblk],
        out_shape=[big, big, small, big, big, small],
        compiler_params=_params(("parallel",), 48),
        name="rwkv_lookahead",
    )(pre['c'], pre['c'], pre['c'], pre['v'], pre['w0'], pre['b0'], pre['k0'],
      pre['w1'], pre['b1'], pre['k1'], ones_blk, sel)
    return outs[0:3], outs[3:6]


def _scan_kernel(rf, wcf, wf, bf, kf, vnf, vkf, bef, rb, wcb, wb, bb, kb, vnb, vkb, beb,
                 c0_ref, v0_ref, s0_ref, eye_ref, hm_ref, y0_ref, y1_ref, sfin_ref, st, uv, *, tc):
    j = pl.program_id(1)
    eye = eye_ref[...]
    hmask = hm_ref[...]
    dirs = ((rf, wcf, wf, bf, kf, vnf, vkf, bef, y0_ref), (rb, wcb, wb, bb, kb, vnb, vkb, beb, y1_ref))
    nt = (((1,), (1,)), ((), ()))
    zrows = jnp.zeros((SCAN_ROWS, D_RWKV), BF16)
    zeye = jnp.zeros((SCAN_ROWS, MXU_DIM), BF16)
    seqs = s0_ref.shape[0]
    chains = [(b, d) for b in range(seqs) for d in range(2)]

    def head_rows(row):
        return (row * hmask).astype(BF16)

    def state_products(b, d, rows0, eye0, eye1, rows2):
        lhs = jnp.concatenate([st[b, d, g].astype(BF16) for g in range(N_GRP)] + [eye], axis=1)
        wr = jnp.concatenate([
            jnp.concatenate([rows0, eye0], axis=1),
            jnp.concatenate([zrows, eye1], axis=1),
            jnp.concatenate([rows2, zeye], axis=1),
            jnp.concatenate([zrows, zeye], axis=1)], axis=0)
        return lax.dot_general(lhs, wr, nt, preferred_element_type=F32)

    def store_y(y_, b, t, prod):
        tr = prod.T
        y_[b, pl.ds(t, 1), :, :] = tr[2 * SCAN_ROWS:3 * SCAN_ROWS, :].reshape(1, H_RWKV, RWKV_HEAD)

    def time_of(d, s):
        s = jnp.clip(s, 0, tc - 1)
        return s if d == 0 else tc - 1 - s

    @pl.when(j == 0)
    def _():
        st[...] = s0_ref[...]
        for (b, d) in chains:
            uv[b, d] = state_products(b, d, head_rows(c0_ref[b, d:d + 1, :]), zeye, v0_ref[b, d], zrows)

    def issue(group, s):
        out = []
        for (b, d) in group:
            r_, wc_, w_, b_, k_, vn_, vk_, be_, y_ = dirs[d]
            t, tp = time_of(d, s), time_of(d, s - 1)
            prod = state_products(b, d, head_rows(wc_[b, pl.ds(t, 1), :]), vk_[b, t], vn_[b, t],
                                  head_rows(r_[b, pl.ds(tp, 1), :]))
            w2 = jnp.concatenate([head_rows(b_[b, pl.ds(t, 1), :]), head_rows(k_[b, pl.ds(t, 1), :]),
                                  zrows, zrows], axis=0)
            out.append((prod, _dot(uv[b, d].astype(BF16), w2)))
        return out

    def finish(group, s, results):
        for (b, d), (prod, upd) in zip(group, results):
            r_, wc_, w_, b_, k_, vn_, vk_, be_, y_ = dirs[d]
            t, tp = time_of(d, s), time_of(d, s - 1)
            wrow = w_[b, pl.ds(t, 1), :]
            for g in range(N_GRP):
                sl = slice(g * MXU_DIM, (g + 1) * MXU_DIM)
                st[b, d, g] = st[b, d, g] * wrow[:, sl] + upd[:, sl]
            store_y(y_, b, tp, prod)
            uv[b, d] = prod + uv[b, d] * be_[b, pl.ds(t, 1), 0:4 * SCAN_ROWS]

    lead, lag = chains[:seqs], chains[seqs:]

    def body(i, carry):
        for u in range(SCAN_UNROLL):
            s = i * SCAN_UNROLL + u
            lead_results = issue(lead, s)
            lag_results = issue(lag, s)
            finish(lead, s, lead_results)
            finish(lag, s, lag_results)
        return carry

    lax.fori_loop(0, tc // SCAN_UNROLL, body, 0)

    for (b, d) in chains:
        t = time_of(d, tc - 1)
        prod = state_products(b, d, zrows, zeye, zeye, head_rows(dirs[d][0][b, pl.ds(t, 1), :]))
        store_y(dirs[d][8], b, t, prod)

    @pl.when(j == pl.num_programs(1) - 1)
    def _():
        sfin_ref[...] = st[...]


def _scan_call(pre, s0, eye, ones_blk):
    B, T, _ = pre['r'].shape
    nb = 4 if B % 4 == 0 else 2
    tc = min(128 if nb == 2 else 32, T)
    nj = T // tc
    fwd = pl.BlockSpec((nb, tc, D_RWKV), lambda bi, j: (bi, j, 0))
    bwd = pl.BlockSpec((nb, tc, D_RWKV), lambda bi, j: (bi, nj - 1 - j, 0))
    yfwd = pl.BlockSpec((nb, tc, H_RWKV, RWKV_HEAD), lambda bi, j: (bi, j, 0, 0))
    ybwd = pl.BlockSpec((nb, tc, H_RWKV, RWKV_HEAD), lambda bi, j: (bi, nj - 1 - j, 0, 0))
    st_spec = pl.BlockSpec((nb, 2, N_GRP, RWKV_HEAD, MXU_DIM), lambda bi, j: (bi, 0, 0, 0, 0))
    y_shape = jax.ShapeDtypeStruct((B, T, H_RWKV, RWKV_HEAD), F32)
    vfwd = pl.BlockSpec((nb, tc, H_RWKV, MXU_DIM), lambda bi, j: (bi, j, 0, 0))
    vbwd = pl.BlockSpec((nb, tc, H_RWKV, MXU_DIM), lambda bi, j: (bi, nj - 1 - j, 0, 0))
    befwd = pl.BlockSpec((nb, tc, LANE), lambda bi, j: (bi, j, 0))
    bebwd = pl.BlockSpec((nb, tc, LANE), lambda bi, j: (bi, nj - 1 - j, 0))

    flat = {n: a.reshape(B * T, D_RWKV) for n, a in pre.items()}
    look = _look_call(flat, T, ones_blk)
    (wc0, vk0, be0), (wc1, vk1, be1) = [
        (wc.reshape(B, T, D_RWKV), _scan_value_rows(vk.reshape(B, T, D_RWKV)), be.reshape(B, T, LANE))
        for (wc, vk, be) in look]
    vt = _scan_value_rows(pre['v'])
    vn0 = jnp.roll(vt, -1, axis=1)
    vn1 = jnp.roll(vt, 1, axis=1)
    c = pre['c']
    c_first = jnp.concatenate([c[:, 0:1], c[:, T - 1:T], jnp.zeros((B, SUBLANE - 2, D_RWKV), F32)], axis=1)
    v_first = jnp.stack([vt[:, 0], vt[:, T - 1]], axis=1)
    return pl.pallas_call(
        functools.partial(_scan_kernel, tc=tc),
        grid=(B // nb, nj),
        in_specs=[fwd] * 5 + [vfwd, vfwd, befwd] + [bwd] * 5 + [vbwd, vbwd, bebwd] + [
            pl.BlockSpec((nb, SUBLANE, D_RWKV), lambda bi, j: (bi, 0, 0)),
            pl.BlockSpec((nb, 2, H_RWKV, MXU_DIM), lambda bi, j: (bi, 0, 0, 0)),
            st_spec,
            pl.BlockSpec((RWKV_HEAD, MXU_DIM), lambda bi, j: (0, 0)),
            pl.BlockSpec((H_RWKV, D_RWKV), lambda bi, j: (0, 0))],
        out_specs=[yfwd, ybwd, st_spec],
        out_shape=[y_shape, y_shape, jax.ShapeDtypeStruct((B, 2, N_GRP, RWKV_HEAD, MXU_DIM), F32)],
        scratch_shapes=[pltpu.VMEM((nb, 2, N_GRP, RWKV_HEAD, MXU_DIM), F32),
                        pltpu.VMEM((nb, 2, RWKV_HEAD, 4 * SCAN_ROWS), F32)],
        compiler_params=_params(("arbitrary", "arbitrary"), 52),
        name="rwkv_scan",
    )(pre['r'], wc0, pre['w0'], pre['b0'], pre['k0'], vn0, vk0, be0,
      pre['r'], wc1, pre['w1'], pre['b1'], pre['k1'], vn1, vk1, be1,
      c_first, v_first, s0, eye.astype(BF16), _scan_head_mask())


def _rwkv_post_kernel(y0_ref, y1_ref, bonus_ref, g_ref, gw_ref, gb_ref, ones_ref, o_ref):
    ones_blk = ones_ref[...]
    y = y0_ref[...] + y1_ref[...]
    mu = _segsum(y, ones_blk) * (1.0 / RWKV_HEAD)
    yc = y - mu
    var = _segsum(yc * yc, ones_blk) * (1.0 / RWKV_HEAD)
    yn = yc * lax.rsqrt(var + RWKV_GN_EPS)
    out = (yn * gw_ref[...] + gb_ref[...] + bonus_ref[...]) * g_ref[...]
    o_ref[...] = out.astype(BF16)


def _rwkv_post_call(y0, y1, bonus, g, gn_w, gn_b, ones_blk):
    T = y0.shape[0]
    tt = 512
    blk = pl.BlockSpec((tt, D_RWKV), lambda i: (i, 0))
    row = pl.BlockSpec((1, D_RWKV), lambda i: (0, 0))
    return pl.pallas_call(
        _rwkv_post_kernel,
        grid=(T // tt,),
        in_specs=[blk, blk, blk, blk, row, row, pl.BlockSpec((MXU_DIM, MXU_DIM), lambda i: (0, 0))],
        out_specs=blk,
        out_shape=jax.ShapeDtypeStruct((T, D_RWKV), BF16),
        compiler_params=_params(("parallel",), 40),
        name="rwkv_post",
    )(y0, y1, bonus, g, gn_w, gn_b, ones_blk)


def _rope128(x, cos_t, sin_t):
    return x * cos_t + pltpu.roll(x, QK_ROPE, 1) * sin_t


def _pack_kv(kv, kr_rot, k_o, v_o):
    for h in range(H_MLA):
        k_o[:, h * HEAD_SLOT:h * HEAD_SLOT + QK_NOPE] = kv[:, h * HEAD_SLOT:h * HEAD_SLOT + QK_NOPE].astype(BF16)
        k_o[:, h * HEAD_SLOT + QK_NOPE:(h + 1) * HEAD_SLOT] = kr_rot.astype(BF16)
        v_o[:, h * V_HEAD:(h + 1) * V_HEAD] = kv[:, h * HEAD_SLOT + QK_NOPE:(h + 1) * HEAD_SLOT].astype(BF16)


def _mla_prep_kernel(z_ref, cos_ref, sin_ref, qn_ref, kvn_ref, wq_ref, wkv_ref,
                     q_o, k_o, v_o, ckv_o, kr_o):
    z = z_ref[...]
    cq = z[:, 0:Q_RANK]
    ckv = z[:, Q_RANK:Q_RANK + KV_RANK]
    krp = z[:, Q_RANK + KV_RANK:MLA_COLS]
    cos_t = cos_ref[...]
    sin_t = sin_ref[...]
    cq = (cq * lax.rsqrt(jnp.mean(cq * cq, axis=-1, keepdims=True) + 1e-6)) * qn_ref[...]
    ckv = (ckv * lax.rsqrt(jnp.mean(ckv * ckv, axis=-1, keepdims=True) + 1e-6)) * kvn_ref[...]
    ckv_o[...] = ckv
    kr_o[...] = krp
    q = _dot(cq.astype(BF16), wq_ref[...]) * (ATTN_SCALE * LOG2_E)
    for h in range(H_MLA):
        q_o[:, h * HEAD_SLOT:h * HEAD_SLOT + QK_NOPE] = q[:, h * HEAD_SLOT:h * HEAD_SLOT + QK_NOPE].astype(BF16)
        q_o[:, h * HEAD_SLOT + QK_NOPE:(h + 1) * HEAD_SLOT] = _rope128(
            q[:, h * HEAD_SLOT + QK_NOPE:(h + 1) * HEAD_SLOT], cos_t, sin_t).astype(BF16)
    kv = _dot(ckv.astype(BF16), wkv_ref[...])
    _pack_kv(kv, _rope128(krp, cos_t, sin_t), k_o, v_o)


def _mla_prep_call(z_m, cos_t, sin_t, seq_len, p):
    T = z_m.shape[0]
    tm = min(512, seq_len)
    per_seq = seq_len // tm
    row = lambda n: pl.BlockSpec((1, n), lambda i: (0, 0))
    blk = lambda n: pl.BlockSpec((tm, n), lambda i: (i, 0))
    tab = pl.BlockSpec((tm, LANE), lambda i: (i % per_seq, 0))
    return pl.pallas_call(
        _mla_prep_kernel,
        grid=(T // tm,),
        in_specs=[blk(MLA_COLS), tab, tab, row(Q_RANK), row(KV_RANK),
                  pl.BlockSpec((Q_RANK, H_MLA * HEAD_SLOT), lambda i: (0, 0)),
                  pl.BlockSpec((KV_RANK, H_MLA * HEAD_SLOT), lambda i: (0, 0))],
        out_specs=[blk(H_MLA * HEAD_SLOT), blk(H_MLA * HEAD_SLOT), blk(H_MLA * V_HEAD),
                   blk(KV_RANK), blk(LANE)],
        out_shape=[jax.ShapeDtypeStruct((T, H_MLA * HEAD_SLOT), BF16),
                   jax.ShapeDtypeStruct((T, H_MLA * HEAD_SLOT), BF16),
                   jax.ShapeDtypeStruct((T, H_MLA * V_HEAD), BF16),
                   jax.ShapeDtypeStruct((T, KV_RANK), F32),
                   jax.ShapeDtypeStruct((T, LANE), F32)],
        compiler_params=_params(("parallel",), 48),
        name="mla_prep",
    )(z_m, cos_t, sin_t, p['q_norm'], p['kv_norm'], p['w_qb'], p['w_kvb'])


def _ctx_kv_kernel(ckv_ref, kr_ref, wkv_ref, k_o, v_o):
    kv = _dot(ckv_ref[...].astype(BF16), wkv_ref[...])
    _pack_kv(kv, kr_ref[...], k_o, v_o)


def _ctx_kv_call(ckv_ctx, kr_ctx_pad, w_kvb):
    T = ckv_ctx.shape[0]
    tm = min(512, T)
    blk = lambda n: pl.BlockSpec((tm, n), lambda i: (i, 0))
    return pl.pallas_call(
        _ctx_kv_kernel,
        grid=(T // tm,),
        in_specs=[blk(KV_RANK), blk(LANE), pl.BlockSpec((KV_RANK, H_MLA * HEAD_SLOT), lambda i: (0, 0))],
        out_specs=[blk(H_MLA * HEAD_SLOT), blk(H_MLA * V_HEAD)],
        out_shape=[jax.ShapeDtypeStruct((T, H_MLA * HEAD_SLOT), BF16),
                   jax.ShapeDtypeStruct((T, H_MLA * V_HEAD), BF16)],
        compiler_params=_params(("parallel",), 32),
        name="mla_ctx_kv",
    )(ckv_ctx, kr_ctx_pad, w_kvb)


ATTN_ROWS = 64


def _attn_kernel(q_ref, k_ref, v_ref, o_ref, s_sc, p_sc, *, kc):
    tq, tk = s_sc.shape
    hq = tq // 2
    nt = (((1,), (1,)), ((), ()))

    def scores(h):
        q = q_ref[h * hq:(h + 1) * hq, :]
        for c in range(tk // kc):
            s_sc[h * hq:(h + 1) * hq, c * kc:(c + 1) * kc] = lax.dot_general(
                q, k_ref[c * kc:(c + 1) * kc, :], nt, preferred_element_type=F32)

    def softmax(h):
        sums = []
        for r in range(hq // ATTN_ROWS):
            rows = slice(h * hq + r * ATTN_ROWS, h * hq + (r + 1) * ATTN_ROWS)
            mpart = s_sc[rows, 0:LANE]
            for t in range(1, tk // LANE):
                mpart = jnp.maximum(mpart, s_sc[rows, t * LANE:(t + 1) * LANE])
            m = jnp.max(mpart, axis=-1, keepdims=True)
            lpart = jnp.zeros((ATTN_ROWS, LANE), F32)
            for t in range(tk // LANE):
                p = jnp.exp2(s_sc[rows, t * LANE:(t + 1) * LANE] - m)
                lpart = lpart + p
                p_sc[rows, t * LANE:(t + 1) * LANE] = p.astype(BF16)
            sums.append(jnp.sum(lpart, axis=-1, keepdims=True))
        return jnp.concatenate(sums, axis=0)

    def weighted_values(h, l):
        acc = jnp.zeros((hq, V_HEAD), F32)
        for c in range(tk // kc):
            acc = acc + _dot(p_sc[h * hq:(h + 1) * hq, c * kc:(c + 1) * kc], v_ref[c * kc:(c + 1) * kc, :])
        o_ref[h * hq:(h + 1) * hq, :] = (acc / l).astype(BF16)

    scores(0)
    scores(1)
    l0 = softmax(0)
    weighted_values(0, l0)
    l1 = softmax(1)
    weighted_values(1, l1)


def _attn_call(q, k, v):
    B, Tq, _ = q.shape
    Tk = k.shape[1]
    tq = min(512, Tq)
    return pl.pallas_call(
        functools.partial(_attn_kernel, kc=MXU_DIM),
        grid=(B, H_MLA, Tq // tq),
        in_specs=[pl.BlockSpec((None, tq, HEAD_SLOT), lambda b, h, i: (b, i, h)),
                  pl.BlockSpec((None, Tk, HEAD_SLOT), lambda b, h, i: (b, 0, h)),
                  pl.BlockSpec((None, Tk, V_HEAD), lambda b, h, i: (b, 0, h))],
        out_specs=pl.BlockSpec((None, tq, V_HEAD), lambda b, h, i: (b, i, h)),
        out_shape=jax.ShapeDtypeStruct((B, Tq, H_MLA * V_HEAD), BF16),
        scratch_shapes=[pltpu.VMEM((tq, Tk), F32), pltpu.VMEM((tq, Tk), BF16)],
        compiler_params=_params(("parallel", "parallel", "arbitrary"), 48),
        name="mla_attention",
    )(q, k, v)


def _conv3_kernel(*refs, tt, half_len):
    ins, (cw_refs, cb_refs), outs = refs[0:12], (refs[12:15], refs[15:18]), refs[18:]
    row0 = pl.program_id(0) * tt
    for s in range(3):
        even, odd = ins[4 * s][...], ins[4 * s + 1][...]
        odd_prev = _shift_prev(odd, ins[4 * s + 2][...], row0, half_len)
        even_next = _shift_next(even, ins[4 * s + 3][...], row0, half_len)
        cw = cw_refs[s][...]
        bias = cb_refs[s][...]
        y_even = cw[0:1, :] * odd_prev + cw[1:2, :] * even + cw[2:3, :] * odd + bias
        y_odd = cw[0:1, :] * even + cw[1:2, :] * odd + cw[2:3, :] * even_next + bias
        outs[s][0] = y_even
        outs[s][1] = y_odd
        if s == 2:
            outs[3][0] = y_even.astype(BF16)
            outs[3][1] = y_odd.astype(BF16)


def _conv3_call(z, half_len, conv_w, conv_b):
    T2 = z.shape[1]
    tt = min(128, half_len)
    per = tt // SUBLANE
    last = T2 // SUBLANE - 1
    in_specs = []
    for s in range(3):
        in_specs += [
            pl.BlockSpec((None, tt, D_HYENA), lambda i, s=s: (0, i, s)),
            pl.BlockSpec((None, tt, D_HYENA), lambda i, s=s: (1, i, s)),
            pl.BlockSpec((None, SUBLANE, D_HYENA), lambda i, s=s: (1, jnp.maximum(i * per - 1, 0), s)),
            pl.BlockSpec((None, SUBLANE, D_HYENA), lambda i, s=s: (0, jnp.minimum((i + 1) * per, last), s))]
    in_specs += [pl.BlockSpec((3, D_HYENA), lambda i, s=s: (0, s)) for s in range(3)]
    in_specs += [pl.BlockSpec((1, D_HYENA), lambda i, s=s: (0, s)) for s in range(3)]
    blk = pl.BlockSpec((2, tt, D_HYENA), lambda i: (0, i, 0))
    f32s = jax.ShapeDtypeStruct((2, T2, D_HYENA), F32)
    return pl.pallas_call(
        functools.partial(_conv3_kernel, tt=tt, half_len=half_len),
        grid=(T2 // tt,),
        in_specs=in_specs,
        out_specs=[blk] * 4,
        out_shape=[f32s, f32s, f32s, jax.ShapeDtypeStruct((2, T2, D_HYENA), BF16)],
        compiler_params=_params(("parallel",), 48),
        name="hyena_conv3",
    )(*([z] * 12), conv_w, conv_w, conv_w, conv_b, conv_b, conv_b)


def _filt_mlp_kernel(z_ref, w1_ref, b1_ref, w2_ref, b2_ref, fr_ref, o_ref):
    h = jnp.sin(fr_ref[0:1, :] * (_dot(z_ref[...].astype(BF16), w1_ref[...].astype(BF16)) + b1_ref[...]))
    h = jnp.sin(fr_ref[1:2, :] * (_dot(h.astype(BF16), w2_ref[...].astype(BF16)) + b2_ref[...]))
    o_ref[...] = h.astype(BF16)


def _filt_mlp_call(zpos, w1p, b1, w2, b2, freq):
    L = zpos.shape[0]
    return pl.pallas_call(
        _filt_mlp_kernel,
        out_shape=jax.ShapeDtypeStruct((L, FILT_HIDDEN), BF16),
        compiler_params=pltpu.CompilerParams(vmem_limit_bytes=32 * MIB),
        name="hyena_filter_mlp",
    )(zpos, w1p, b1, w2, b2, freq)


def _filt_gen_kernel(h_ref, tn_ref, dl_ref, w00, w01, w10, w11, o_ref, taps_sc):
    h = h_ref[...]
    L = h.shape[0]
    win = jnp.exp(-tn_ref[...] * dl_ref[...])
    not_first = lax.broadcasted_iota(jnp.int32, (L, 1), 0) > 0
    ws = ((w00, w01), (w10, w11))

    def emit(k, taps):
        taps_sc[...] = taps
        for par in range(2):
            o_ref[k, par] = taps_sc[pl.ds(par, L // 2, stride=2), :].astype(BF16)

    for n in range(2):
        causal = _dot(h, ws[n][0][...].astype(BF16)) * win
        anti = jnp.where(not_first, _dot(h, ws[n][1][...].astype(BF16)) * win, 0.0)
        norm = (jnp.sum(jnp.abs(causal), axis=0, keepdims=True)
                + jnp.sum(jnp.abs(anti), axis=0, keepdims=True))
        emit(2 * n, causal / norm)
        emit(2 * n + 1, anti / norm)


def _filt_gen_call(h2, tnorm, deltas, w3):
    L = h2.shape[0]
    tc = 128
    nc = D_HYENA // tc
    wspec = lambda k: pl.BlockSpec((FILT_HIDDEN, tc), lambda j, k=k: (0, k * nc + j))
    return pl.pallas_call(
        _filt_gen_kernel,
        grid=(nc,),
        in_specs=[pl.BlockSpec((L, FILT_HIDDEN), lambda j: (0, 0)),
                  pl.BlockSpec((L, 1), lambda j: (0, 0)),
                  pl.BlockSpec((1, tc), lambda j: (0, j)),
                  wspec(0), wspec(1), wspec(2), wspec(3)],
        out_specs=pl.BlockSpec((4, 2, L // 2, tc), lambda j: (0, 0, 0, j)),
        out_shape=jax.ShapeDtypeStruct((4, 2, L // 2, D_HYENA), BF16),
        scratch_shapes=[pltpu.VMEM((L, tc), F32)],
        compiler_params=_params(("parallel",), 48),
        name="hyena_filter_gen",
    )(h2, tnorm, deltas, w3, w3, w3, w3)


def _dft_fwd_kernel(f_ref, u_ref, o_ref):
    o_ref[...] = _dot(f_ref[...], u_ref[...])


def _dft_fwd_call(fmat, u):
    B, K, C = u.shape
    M = fmat.shape[0]
    tm = min(512, M)
    tn = min(C, 2048)
    return pl.pallas_call(
        _dft_fwd_kernel,
        grid=(B, C // tn, M // tm),
        in_specs=[pl.BlockSpec((tm, K), lambda b, j, i: (i, 0)),
                  pl.BlockSpec((None, K, tn), lambda b, j, i: (b, 0, j))],
        out_specs=pl.BlockSpec((None, tm, tn), lambda b, j, i: (b, i, j)),
        out_shape=jax.ShapeDtypeStruct((B, M, C), F32),
        compiler_params=_params(("parallel", "parallel", "arbitrary"), 40),
        name="hyena_dft_fwd",
    )(fmat, u)


def _butterfly(gc, gs, hc, hs, tw_c, tw_s, first):
    tc = hc * tw_c - hs * tw_s
    ts = hs * tw_c + hc * tw_s
    p0 = gc + tc
    p2 = gc - tc
    p1 = jnp.where(first, gs, gs + ts)
    p3 = jnp.where(first, hs, ts - gs)
    return p0, p1, p2, p3


def _cmul(ac, a_s, bc, bs):
    return ac * bc - a_s * bs, ac * bs + a_s * bc


def _filt_planes_kernel(ge_ref, ho_ref, twc_ref, tws_ref, o_ref, *, tr):
    first = (lax.broadcasted_iota(jnp.int32, (tr, 1), 0) + pl.program_id(1) * tr) == 0
    tw_c, tw_s = twc_ref[...], tws_ref[...]
    a = _butterfly(ge_ref[0, 0], ge_ref[0, 1], ho_ref[0, 0], ho_ref[0, 1], tw_c, tw_s, first)
    b = _butterfly(ge_ref[1, 0], ge_ref[1, 1], ho_ref[1, 0], ho_ref[1, 1], tw_c, tw_s, first)
    o_ref[0] = a[0] + b[0]
    o_ref[1] = jnp.where(first, a[1] + b[1], a[1] - b[1])
    o_ref[2] = a[2] + b[2]
    o_ref[3] = a[3] - b[3]


def _filt_planes_call(raw, twc, tws):
    _, L, C = raw.shape
    H = L // 2
    tr = min(256, H)
    tc = 512
    blk = lambda par: pl.BlockSpec((None, 2, None, 2, tr, tc), lambda n, i, j: (n, 0, par, 0, i, j))
    tw = pl.BlockSpec((tr, 1), lambda n, i, j: (i, 0))
    raw6 = raw.reshape(2, 2, 2, 2, H, C)
    return pl.pallas_call(
        functools.partial(_filt_planes_kernel, tr=tr),
        grid=(2, H // tr, C // tc),
        in_specs=[blk(0), blk(1), tw, tw],
        out_specs=pl.BlockSpec((None, 4, tr, tc), lambda n, i, j: (n, 0, i, j)),
        out_shape=jax.ShapeDtypeStruct((2, 4, H, C), F32),
        compiler_params=_params(("parallel", "parallel", "parallel"), 40),
        name="hyena_filter_planes",
    )(raw6, raw6, twc, tws)


def _spec_mul_kernel(raw_ref, k_ref, twc_ref, tws_ref, o_ref, *, tr):
    first = (lax.broadcasted_iota(jnp.int32, (tr, 1), 0) + pl.program_id(1) * tr) == 0
    tw_c, tw_s = twc_ref[...], tws_ref[...]
    p0, p1, p2, p3 = _butterfly(raw_ref[0, 0], raw_ref[0, 1], raw_ref[1, 0], raw_ref[1, 1], tw_c, tw_s, first)
    k0, k1, k2, k3 = k_ref[0], k_ref[1], k_ref[2], k_ref[3]
    yac, yas = _cmul(p0, p1, k0, k1)
    ybc, ybs = _cmul(p2, p3, k2, k3)
    ymc, yms = _cmul(p1, p3, k1, k3)
    yac = jnp.where(first, p0 * k0, yac)
    ybc = jnp.where(first, p2 * k2, ybc)
    dc = yac - ybc
    ds = yas + ybs
    o_ref[0, 0] = (yac + ybc).astype(BF16)
    o_ref[0, 1] = jnp.where(first, ymc, yas - ybs).astype(BF16)
    o_ref[1, 0] = jnp.where(first, dc, dc * tw_c + ds * tw_s).astype(BF16)
    o_ref[1, 1] = jnp.where(first, yms, ds * tw_c - dc * tw_s).astype(BF16)


def _spec_mul_call(raw, k_planes, order, twc, tws):
    B2, L, C = raw.shape
    B, H = B2 // 2, L // 2
    tr = min(256, H)
    tc = 512
    tw = pl.BlockSpec((tr, 1), lambda b, i, j: (i, 0))
    out = pl.pallas_call(
        functools.partial(_spec_mul_kernel, tr=tr),
        grid=(B, H // tr, C // tc),
        in_specs=[pl.BlockSpec((2, None, 2, tr, tc), lambda b, i, j: (0, b, 0, i, j)),
                  pl.BlockSpec((None, 4, tr, tc), lambda b, i, j: (order, 0, i, j)),
                  tw, tw],
        out_specs=pl.BlockSpec((2, None, 2, tr, tc), lambda b, i, j: (0, b, 0, i, j)),
        out_shape=jax.ShapeDtypeStruct((2, B, 2, H, C), BF16),
        compiler_params=_params(("parallel", "parallel", "parallel"), 40),
        name="hyena_spectral_mul",
    )(raw.reshape(2, B, 2, H, C), k_planes, twc, tws)
    return out.reshape(B2, L, C)


def _dft_inv_kernel(f_ref, y_ref, gate_ref, u_ref, bias_ref, o_ref, ob_ref):
    conv = _dot(f_ref[...], y_ref[...])
    out = gate_ref[...] * (conv + u_ref[...] * bias_ref[...])
    o_ref[...] = out
    ob_ref[...] = out.astype(BF16)


def _dft_inv_call(imat, y_spec, gate, u, bias):
    B, M, C = u.shape
    K = imat.shape[1]
    tm = min(512, M)
    tn = 1024 if K > 1024 else min(C, 2048)
    blk = pl.BlockSpec((None, tm, tn), lambda b, j, i: (b, i, j))
    return pl.pallas_call(
        _dft_inv_kernel,
        grid=(B, C // tn, M // tm),
        in_specs=[pl.BlockSpec((tm, K), lambda b, j, i: (i, 0)),
                  pl.BlockSpec((None, K, tn), lambda b, j, i: (b, 0, j)),
                  blk, blk, pl.BlockSpec((1, tn), lambda b, j, i: (0, j))],
        out_specs=[blk, blk],
        out_shape=[jax.ShapeDtypeStruct((B, M, C), F32), jax.ShapeDtypeStruct((B, M, C), BF16)],
        compiler_params=_params(("parallel", "parallel", "arbitrary"), 48),
        name="hyena_dft_inv",
    )(imat, y_spec, gate, u, bias)


def _dft_tables(L):
    H = L // 2
    lo = min(64, H)
    hi = H // lo
    g = jnp.arange(H, dtype=jnp.int32)
    theta = 2.0 * math.pi / L
    ang_hi = ((g[:, None] * (jnp.arange(hi, dtype=jnp.int32) * lo)[None, :]) % L).astype(F32) * theta
    ang_lo = ((g[:, None] * jnp.arange(lo, dtype=jnp.int32)[None, :]) % L).astype(F32) * theta
    ch, sh, cl, sl = jnp.cos(ang_hi), jnp.sin(ang_hi), jnp.cos(ang_lo), jnp.sin(ang_lo)
    cos_m = (ch[:, :, None] * cl[:, None, :] - sh[:, :, None] * sl[:, None, :]).reshape(H, H)
    sin_m = (sh[:, :, None] * cl[:, None, :] + ch[:, :, None] * sl[:, None, :]).reshape(H, H)
    alt = jnp.where(jnp.arange(H) % 2 == 0, 1.0, -1.0).astype(F32)
    sin_m = jnp.where((g == 0)[:, None], alt[None, :], sin_m)
    fwd = jnp.concatenate([cos_m, sin_m], axis=0)
    n = 2.0 * L
    w_cos = jnp.where(g == 0, 1.0 / n, 2.0 / n).astype(F32)
    inv = jnp.concatenate([cos_m * w_cos[:, None], sin_m * (2.0 / n)], axis=0).T
    ang_tw = g.astype(F32) * (math.pi / L)
    return fwd.astype(BF16), inv.astype(BF16), jnp.cos(ang_tw)[:, None], jnp.sin(ang_tw)[:, None]


def _filter_positions(L):
    t = jnp.arange(L, dtype=F32)
    t_norm = t / max(L - 1, 1)
    bands = (POS_EMB - 1) // 2
    freqs = jnp.linspace(1e-4, bands - 1, bands, dtype=F32)
    ang = (2.0 * math.pi / L) * t[:, None] * freqs[None, :]
    z = jnp.concatenate([t_norm[:, None], jnp.cos(ang), -jnp.sin(ang)], axis=-1)
    return jnp.pad(z, ((0, 0), (0, POS_PAD - POS_EMB))), t_norm[:, None]


def _hyena_deltas():
    return jnp.linspace(abs(math.log(HYENA_TARGET)) / SLOW_DECAY_PCT,
                        abs(math.log(HYENA_TARGET)) / FAST_DECAY_PCT, D_HYENA, dtype=F32)[None, :]


def _hyena_filter_spectrum(L, tables, p):
    fwd, _, twc, tws = tables
    zpos, tnorm = _filter_positions(L)
    h2 = _filt_mlp_call(zpos, p['filt_w1'], p['filt_b1'], p['filt_w2'], p['filt_b2'], p['filt_freq'])
    filt = _filt_gen_call(h2, tnorm, _hyena_deltas(), p['filt_w3'])
    raw = _dft_fwd_call(fwd, filt.reshape(8, L // 2, D_HYENA))
    return _filt_planes_call(raw, twc, tws)


def _hyena_mixer(z, B, L, tables, k_planes, p):
    fwd, inv, twc, tws = tables
    H = L // 2
    x1, x2, v, vb = _conv3_call(z, H, p['conv_w'], p['conv_b'])
    shp = (2 * B, H, D_HYENA)
    u, ub = v.reshape(shp), vb.reshape(shp)
    for n, gate in enumerate((x1, x2)):
        raw = _dft_fwd_call(fwd, ub)
        yspec = _spec_mul_call(raw, k_planes, n, twc, tws)
        u, ub = _dft_inv_call(inv, yspec, gate.reshape(shp), u, p['bias'][n:n + 1])
    return ub.reshape(2, B * H, D_HYENA)


def _rope_swap(w):
    q = QK_ROPE // 4
    return jnp.concatenate([w[..., q:2 * q], w[..., 0:q], w[..., 3 * q:4 * q], w[..., 2 * q:3 * q]], axis=-1)


def _pad_cols(w, n):
    return jnp.pad(w, [(0, 0)] * (w.ndim - 1) + [(0, n - w.shape[-1])])


def _pack_even(e, w_in_even, mu_prev, mu_next, rwkv_w0, rwkv_w2, rwkv_a0, rwkv_a2, rwkv_g2,
               rwkv_kk, rwkv_ka, rwkv_rk, rwkv_gn_w, rwkv_gn_b, mla_q_norm, mla_kv_norm,
               mla_w_qb, mla_w_kvb, w_out_even):
    n_r = 3 * D_RWKV + W_LORA + A_LORA + G_LORA
    w_in = w_in_even[e]
    w_r = _pad_cols(w_in[:, :n_r], RWKV_COLS).astype(BF16)
    w_m = w_in[:, n_r:]
    kr_cols = w_m[:, Q_RANK + KV_RANK:]
    w_m = jnp.concatenate([w_m, _rope_swap(kr_cols)], axis=-1).astype(BF16)
    small_rows = lambda w, off: jnp.pad(w, [(0, 0)] * (w.ndim - 2)
                                        + [(off, RWKV_SMALL - off - w.shape[-2]), (0, 0)]).astype(BF16)
    wq = mla_w_qb[e].reshape(Q_RANK, H_MLA, QK_NOPE + QK_ROPE)
    wq = jnp.concatenate([wq, _rope_swap(wq[..., QK_NOPE:])], axis=-1).reshape(Q_RANK, H_MLA * HEAD_SLOT)
    blk = jnp.arange(MXU_DIM) // RWKV_HEAD
    return {
        'w_r': w_r, 'w_m': w_m,
        'mu_prev': _pad_cols(mu_prev[e][None, :], RWKV_COLS),
        'mu_next': _pad_cols(mu_next[e][None, :], RWKV_COLS),
        'k_k': rwkv_kk[e][None, :], 'k_a': rwkv_ka[e][None, :],
        'r_k': rwkv_rk[e].reshape(1, D_RWKV),
        'w0': rwkv_w0[e], 'w2': small_rows(rwkv_w2[e], 0),
        'a0': rwkv_a0[e], 'a2': small_rows(rwkv_a2[e], W_LORA),
        'g2': small_rows(rwkv_g2[e], W_LORA + A_LORA),
        'gn_w': rwkv_gn_w[e][None, :], 'gn_b': rwkv_gn_b[e][None, :],
        'q_norm': mla_q_norm[e][None, :], 'kv_norm': mla_kv_norm[e][None, :],
        'w_qb': wq.astype(BF16), 'w_kvb': mla_w_kvb[e].astype(BF16),
        'w_out': w_out_even[e].astype(BF16),
        'ones_blk': (blk[:, None] == blk[None, :]).astype(BF16),
        'eye': (jnp.arange(RWKV_HEAD)[:, None] == (jnp.arange(MXU_DIM) % RWKV_HEAD)[None, :]).astype(F32),
    }


def _rope_tables(L):
    rows = L // GRID_W
    row = jnp.repeat(jnp.arange(rows, dtype=F32), GRID_W)
    col = jnp.tile(jnp.arange(GRID_W, dtype=F32), rows)
    half = QK_ROPE // 2
    inv = 1.0 / (ROPE_THETA ** (jnp.arange(0, half, 2, dtype=F32) / half))
    ar, ac = row[:, None] * inv[None, :], col[:, None] * inv[None, :]
    cos_t = jnp.concatenate([jnp.cos(ar), jnp.cos(ar), jnp.cos(ac), jnp.cos(ac)], axis=-1)
    sin_t = jnp.concatenate([-jnp.sin(ar), jnp.sin(ar), -jnp.sin(ac), jnp.sin(ac)], axis=-1)
    return _pad_cols(cos_t, LANE), _pad_cols(sin_t, LANE)


def _state_to_groups(s):
    B = s.shape[0]
    s = s.reshape(B, 2, N_GRP, H_RWKV // N_GRP, RWKV_HEAD, RWKV_HEAD)
    return jnp.swapaxes(s, 3, 4).reshape(B, 2, N_GRP, RWKV_HEAD, MXU_DIM)


def _groups_to_state(s):
    B = s.shape[0]
    s = s.reshape(B, 2, N_GRP, RWKV_HEAD, H_RWKV // N_GRP, RWKV_HEAD)
    return jnp.swapaxes(s, 3, 4).reshape(B, 2, H_RWKV, RWKV_HEAD, RWKV_HEAD)


def _even_mixer(x, mod_l, goff, B, L, gamma, p, rope, ctx):
    group_tokens = x.shape[0] if ctx is None else L
    z_r = _inproj_call(x, mod_l, goff, group_tokens, gamma, p['w_r'], RWKV_COLS // 3)
    z_m = _inproj_call(x, mod_l, goff, group_tokens, gamma, p['w_m'], MLA_COLS)
    names = ('r', 'v', 'c', 'w0', 'b0', 'k0', 'w1', 'b1', 'k1', 'bonus', 'g')
    pre = dict(zip(names, _rwkv_prep_call(z_r, L, p)))
    seq = {n: pre[n].reshape(B, L, D_RWKV) for n in names[:9]}
    if ctx is None:
        s0 = jnp.zeros((B, 2, N_GRP, RWKV_HEAD, MXU_DIM), F32)
    else:
        s0 = _state_to_groups(ctx[2].astype(F32))
    y0, y1, s_fin = _scan_call(seq, s0, p['eye'], p['ones_blk'])
    y_r = _rwkv_post_call(y0.reshape(B * L, D_RWKV), y1.reshape(B * L, D_RWKV), pre['bonus'], pre['g'],
                          p['gn_w'], p['gn_b'], p['ones_blk'])

    q, k, v, ckv, krp = _mla_prep_call(z_m, rope[0], rope[1], L, p)
    q = q.reshape(B, L, H_MLA * HEAD_SLOT)
    k = k.reshape(B, L, H_MLA * HEAD_SLOT)
    v = v.reshape(B, L, H_MLA * V_HEAD)
    if ctx is not None:
        P = ctx[0].shape[1]
        k_ctx, v_ctx = _ctx_kv_call(ctx[0].reshape(B * P, KV_RANK),
                                    _pad_cols(ctx[1].reshape(B * P, QK_ROPE), LANE), p['w_kvb'])
        k = jnp.concatenate([k, k_ctx.reshape(B, P, H_MLA * HEAD_SLOT)], axis=1)
        v = jnp.concatenate([v, v_ctx.reshape(B, P, H_MLA * V_HEAD)], axis=1)
    y_m = _attn_call(q, k, v).reshape(B * L, H_MLA * V_HEAD)
    x = _outproj_call(x, mod_l, goff, group_tokens, y_r, y_m, p['w_out'])
    state = (_groups_to_state(s_fin), ckv.reshape(B, L, KV_RANK), krp[:, :QK_ROPE].reshape(B, L, QK_ROPE))
    return x, state


def _odd_mixer(x, mod_l, goff, group_tokens, B, L, gamma, tables, k_planes, p):
    z = _inproj_parity_call(x, mod_l, goff, group_tokens, gamma, p['w_in'], 1536)
    y = _hyena_mixer(z, B, L, tables, k_planes, p)
    return _outproj_parity_call(x, mod_l, goff, group_tokens, y, p['w_out'])


def kernel(x_prompt, x_sample, cache_mla_ckv, cache_mla_krope, state_rwkv, c, c_ctx,
           w_mod, b_mod, norm_g, w_ffn_in, w_ffn_out, final_norm_g,
           w_in_even, mu_prev, mu_next, rwkv_w0, rwkv_w2, rwkv_a0, rwkv_a2, rwkv_g2,
           rwkv_kk, rwkv_ka, rwkv_rk, rwkv_gn_w, rwkv_gn_b,
           mla_q_norm, mla_kv_norm, mla_w_qb, mla_w_kvb, w_out_even,
           w_in_odd, hy_conv_w, hy_conv_b, hy_filt_w1, hy_filt_b1, hy_filt_w2, hy_filt_b2,
           hy_filt_w3, hy_filt_freq, hy_bias, w_out_odd):
    Bp, Lp, D = x_prompt.shape
    Bs, Ls, _ = x_sample.shape
    depth = w_mod.shape[0]
    xp = x_prompt.reshape(Bp * Lp, D)
    xs = x_sample.reshape(Bs * Ls, D)
    Tp = Bp * Lp

    cvec = jnp.concatenate([c_ctx[None, :], c, jnp.zeros((SUBLANE - 1 - Bs, D), F32)], axis=0)
    mod = _mod_call(cvec, w_mod, b_mod)

    rope_p = (_pad_cols(jnp.ones((Lp, QK_ROPE), F32), LANE), jnp.zeros((Lp, LANE), F32))
    rope_s = _rope_tables(Ls)
    tabs_p = tabs_s = None
    w_in = w_ffn_in.astype(BF16)
    w_out = w_ffn_out.astype(BF16)
    new_ckv, new_kr, new_s = [], [], []
    for l in range(depth):
        mod_l = mod[l]
        gam = [norm_g[l, s][None, :] for s in range(3)]
        xp = _ffn_call(xp, mod_l, 0, Tp, gam[0], w_in, w_out, l, 0, 0)
        xs = _ffn_call(xs, mod_l, 1, Ls, gam[0], w_in, w_out, l, 0, 0)
        if l % 2 == 0:
            e = l // 2
            p = _pack_even(e, w_in_even, mu_prev, mu_next, rwkv_w0, rwkv_w2, rwkv_a0, rwkv_a2, rwkv_g2,
                           rwkv_kk, rwkv_ka, rwkv_rk, rwkv_gn_w, rwkv_gn_b, mla_q_norm, mla_kv_norm,
                           mla_w_qb, mla_w_kvb, w_out_even)
            ctx = (cache_mla_ckv[:, e], cache_mla_krope[:, e], state_rwkv[:, e])
            xp, st = _even_mixer(xp, mod_l, 0, Bp, Lp, gam[1], p, rope_p, None)
            xs, _ = _even_mixer(xs, mod_l, 1, Bs, Ls, gam[1], p, rope_s, ctx)
            new_s.append(st[0].astype(x_prompt.dtype))
            new_ckv.append(st[1])
            new_kr.append(st[2])
        else:
            o = l // 2
            p = {'w_in': w_in_odd[o].astype(BF16), 'conv_w': hy_conv_w[o], 'conv_b': hy_conv_b[o][None, :],
                 'filt_w1': jnp.pad(hy_filt_w1[o], ((0, POS_PAD - POS_EMB), (0, 0))),
                 'filt_b1': hy_filt_b1[o][None, :], 'filt_w2': hy_filt_w2[o],
                 'filt_b2': hy_filt_b2[o][None, :], 'filt_w3': hy_filt_w3[o],
                 'filt_freq': hy_filt_freq[o], 'bias': hy_bias[o], 'w_out': w_out_odd[o].astype(BF16)}
            if tabs_p is None:
                tabs_p, tabs_s = _dft_tables(Lp), _dft_tables(Ls)
            ks_p = _hyena_filter_spectrum(Lp, tabs_p, p)
            ks_s = _hyena_filter_spectrum(Ls, tabs_s, p)
            xp = _odd_mixer(xp, mod_l, 0, Tp, Bp, Lp, gam[1], tabs_p, ks_p, p)
            xs = _odd_mixer(xs, mod_l, 1, Ls, Bs, Ls, gam[1], tabs_s, ks_s, p)
        xp = _ffn_call(xp, mod_l, 0, Tp, gam[2], w_in, w_out, l, 1, 2)
        xs = _ffn_call(xs, mod_l, 1, Ls, gam[2], w_in, w_out, l, 1, 2)

    fg = final_norm_g[None, :]
    y_prompt = _final_norm_call(xp, fg).reshape(Bp, Lp, D)
    y_sample = _final_norm_call(xs, fg).reshape(Bs, Ls, D)
    return (y_prompt, y_sample, jnp.stack(new_ckv, axis=1), jnp.stack(new_kr, axis=1),
            jnp.stack(new_s, axis=1))
```

```python
import functools
import math

import jax
import jax.numpy as jnp
from jax import lax
from jax.experimental import pallas as pl
from jax.experimental.pallas import tpu as pltpu

F32 = jnp.float32
BF16 = jnp.bfloat16

D_MODEL = 2048
N_MOD = 9
D_FF = 5632
D_RWKV = 1024
RWKV_HEAD = 64
H_RWKV = 16
W_LORA = 64
A_LORA = 64
G_LORA = 160
RWKV_SMALL = 384
RWKV_COLS = 3 * D_RWKV + RWKV_SMALL
RWKV_GN_EPS = 64e-5
H_MLA = 8
QK_NOPE = 128
QK_ROPE = 64
V_HEAD = 128
Q_RANK = 512
KV_RANK = 256
MLA_COLS = Q_RANK + KV_RANK + 2 * QK_ROPE
HEAD_SLOT = 256
ROPE_THETA = 10000.0
ATTN_SCALE = (QK_NOPE + QK_ROPE) ** -0.5
LOG2_E = 1.0 / math.log(2.0)
GRID_W = 64
D_HYENA = 2048
POS_EMB = 33
POS_PAD = 128
FILT_HIDDEN = 64
HYENA_TARGET = 1e-2
FAST_DECAY_PCT = 0.3
SLOW_DECAY_PCT = 1.5
LANE = 128
SUBLANE = 8
MXU_DIM = 256
MIB = 1024 * 1024


def _params(sem, vmem_mib):
    return pltpu.CompilerParams(dimension_semantics=sem, vmem_limit_bytes=vmem_mib * MIB)


def _sigmoid(x):
    return 1.0 / (1.0 + jnp.exp(-x))


def _softplus(x):
    return jnp.maximum(x, 0.0) + jnp.log(1.0 + jnp.exp(-jnp.abs(x)))


def _dot(a, b):
    return jnp.dot(a, b, preferred_element_type=F32)


def _norm_mod(x, gamma, shift, scale):
    xn = x * lax.rsqrt(jnp.mean(x * x, axis=-1, keepdims=True) + 1e-6)
    return (xn * gamma) * (1.0 + scale) + shift


def _mod_kernel(c_ref, w_ref, b_ref, o_ref):
    c = c_ref[...]
    s = c * _sigmoid(c)
    o_ref[0] = _dot(s.astype(BF16), w_ref[0].astype(BF16)) + b_ref[0]


def _mod_call(cvec, w_mod, b_mod):
    L, Dm, N = w_mod.shape
    tn = 1024
    out = pl.pallas_call(
        _mod_kernel,
        grid=(L, N // tn),
        in_specs=[pl.BlockSpec((SUBLANE, Dm), lambda l, j: (0, 0)),
                  pl.BlockSpec((1, Dm, tn), lambda l, j: (l, 0, j)),
                  pl.BlockSpec((1, 1, tn), lambda l, j: (l, 0, j))],
        out_specs=pl.BlockSpec((1, SUBLANE, tn), lambda l, j: (l, 0, j)),
        out_shape=jax.ShapeDtypeStruct((L, SUBLANE, N), F32),
        compiler_params=_params(("arbitrary", "arbitrary"), 40),
        name="adaln_mod",
    )(cvec, w_mod, b_mod.reshape(L, 1, N))
    return out.reshape(L, SUBLANE, N_MOD, Dm)


def _mod_spec(goff, tiles_per_group, nargs):
    if nargs == 1:
        return pl.BlockSpec((None, N_MOD, D_MODEL), lambda i: (goff + i // tiles_per_group, 0, 0))
    return pl.BlockSpec((None, N_MOD, D_MODEL), lambda i, j: (goff + i // tiles_per_group, 0, 0))


def _ffn_kernel(x_ref, mod_ref, g_ref, wg_ref, wu_ref, wo_ref, o_ref, h_sc, acc_sc, *, sub):
    f = pl.program_id(1)

    @pl.when(f == 0)
    def _():
        h = _norm_mod(x_ref[...], g_ref[...], mod_ref[3 * sub:3 * sub + 1, :],
                      mod_ref[3 * sub + 1:3 * sub + 2, :])
        h_sc[...] = h.astype(BF16)
        acc_sc[...] = jnp.zeros_like(acc_sc)

    h = h_sc[...]
    a = _dot(h, wg_ref[...])
    u = _dot(h, wu_ref[...])
    act = (a * _sigmoid(a)) * u
    acc_sc[...] += _dot(act.astype(BF16), wo_ref[...])

    @pl.when(f == pl.num_programs(1) - 1)
    def _():
        o_ref[...] = x_ref[...] + 0.5 * mod_ref[3 * sub + 2:3 * sub + 3, :] * acc_sc[...]


def _ffn_call(x, mod_l, goff, group_tokens, gamma, w_in, w_out, l, s, sub):
    T = x.shape[0]
    tm = min(512, group_tokens)
    tf = 512
    nf = D_FF // tf
    return pl.pallas_call(
        functools.partial(_ffn_kernel, sub=sub),
        grid=(T // tm, nf),
        in_specs=[pl.BlockSpec((tm, D_MODEL), lambda i, f: (i, 0)),
                  _mod_spec(goff, group_tokens // tm, 2),
                  pl.BlockSpec((1, D_MODEL), lambda i, f: (0, 0)),
                  pl.BlockSpec((None, None, D_MODEL, tf), lambda i, f: (l, s, 0, f)),
                  pl.BlockSpec((None, None, D_MODEL, tf), lambda i, f: (l, s, 0, f + nf)),
                  pl.BlockSpec((None, None, tf, D_MODEL), lambda i, f: (l, s, f, 0))],
        out_specs=pl.BlockSpec((tm, D_MODEL), lambda i, f: (i, 0)),
        out_shape=jax.ShapeDtypeStruct((T, D_MODEL), F32),
        scratch_shapes=[pltpu.VMEM((tm, D_MODEL), BF16), pltpu.VMEM((tm, D_MODEL), F32)],
        compiler_params=_params(("parallel", "arbitrary"), 52),
        name="ffn_swiglu",
    )(x, mod_l, gamma, w_in, w_in, w_out)


def _inproj_kernel(x_ref, mod_ref, g_ref, w_ref, o_ref, h_sc, *, col_axis):
    @pl.when(pl.program_id(col_axis) == 0)
    def _():
        h = _norm_mod(x_ref[...], g_ref[...], mod_ref[3:4, :], mod_ref[4:5, :])
        h_sc[...] = h.astype(BF16)

    o_ref[...] = _dot(h_sc[...], w_ref[...])


def _inproj_call(x, mod_l, goff, group_tokens, gamma, w, tn):
    T = x.shape[0]
    N = w.shape[1]
    tm = min(512, group_tokens)
    return pl.pallas_call(
        functools.partial(_inproj_kernel, col_axis=1),
        grid=(T // tm, N // tn),
        in_specs=[pl.BlockSpec((tm, D_MODEL), lambda i, j: (i, 0)),
                  _mod_spec(goff, group_tokens // tm, 2),
                  pl.BlockSpec((1, D_MODEL), lambda i, j: (0, 0)),
                  pl.BlockSpec((D_MODEL, tn), lambda i, j: (0, j))],
        out_specs=pl.BlockSpec((tm, tn), lambda i, j: (i, j)),
        out_shape=jax.ShapeDtypeStruct((T, N), F32),
        scratch_shapes=[pltpu.VMEM((tm, D_MODEL), BF16)],
        compiler_params=_params(("parallel", "arbitrary"), 48),
        name="mixer_inproj",
    )(x, mod_l, gamma, w)


def _rows_of_parity(lane_sc, x, par):
    rows = x.shape[0]
    parts = []
    for c in range(x.shape[1] // LANE):
        lane_sc[c] = x[:, c * LANE:(c + 1) * LANE]
        parts.append(lane_sc[c, pl.ds(par, rows // 2, stride=2), :])
    return jnp.concatenate(parts, axis=1)


def _interleave_rows(lane_sc, even, odd):
    half = even.shape[0]
    parts = []
    for c in range(even.shape[1] // LANE):
        lane_sc[c, pl.ds(0, half, stride=2), :] = even[:, c * LANE:(c + 1) * LANE]
        lane_sc[c, pl.ds(1, half, stride=2), :] = odd[:, c * LANE:(c + 1) * LANE]
        parts.append(lane_sc[c])
    return jnp.concatenate(parts, axis=1)


def _inproj_parity_kernel(x_ref, mod_ref, g_ref, w_ref, o_ref, h_sc, lane_sc, *, half):
    @pl.when(pl.program_id(1) == 0)
    def _():
        h = _norm_mod(x_ref[...], g_ref[...], mod_ref[3:4, :], mod_ref[4:5, :])
        for p in range(2):
            h_sc[p * half:(p + 1) * half, :] = _rows_of_parity(lane_sc, h, p).astype(BF16)

    z = _dot(h_sc[...], w_ref[...])
    o_ref[0] = z[:half]
    o_ref[1] = z[half:]


def _inproj_parity_call(x, mod_l, goff, group_tokens, gamma, w, tn):
    T = x.shape[0]
    N = w.shape[1]
    tm = min(512, group_tokens)
    half = tm // 2
    return pl.pallas_call(
        functools.partial(_inproj_parity_kernel, half=half),
        grid=(T // tm, N // tn),
        in_specs=[pl.BlockSpec((tm, D_MODEL), lambda i, j: (i, 0)),
                  _mod_spec(goff, group_tokens // tm, 2),
                  pl.BlockSpec((1, D_MODEL), lambda i, j: (0, 0)),
                  pl.BlockSpec((D_MODEL, tn), lambda i, j: (0, j))],
        out_specs=pl.BlockSpec((2, half, tn), lambda i, j: (0, i, j)),
        out_shape=jax.ShapeDtypeStruct((2, T // 2, N), F32),
        scratch_shapes=[pltpu.VMEM((tm, D_MODEL), BF16), pltpu.VMEM((D_MODEL // LANE, tm, LANE), F32)],
        compiler_params=_params(("parallel", "arbitrary"), 48),
        name="mixer_inproj_parity",
    )(x, mod_l, gamma, w)


def _outproj_kernel(x_ref, mod_ref, a1_ref, a2_ref, w1_ref, w2_ref, o_ref):
    y = _dot(a1_ref[...], w1_ref[...]) + _dot(a2_ref[...], w2_ref[...])
    o_ref[...] = x_ref[...] + mod_ref[5:6, :] * y


def _outproj_call(x, mod_l, goff, group_tokens, a1, a2, w):
    T = x.shape[0]
    tm = min(512, group_tokens)
    half = D_MODEL // 2
    return pl.pallas_call(
        _outproj_kernel,
        grid=(T // tm,),
        in_specs=[pl.BlockSpec((tm, D_MODEL), lambda i: (i, 0)),
                  _mod_spec(goff, group_tokens // tm, 1),
                  pl.BlockSpec((tm, half), lambda i: (i, 0)),
                  pl.BlockSpec((tm, half), lambda i: (i, 0)),
                  pl.BlockSpec((half, D_MODEL), lambda i: (0, 0)),
                  pl.BlockSpec((half, D_MODEL), lambda i: (1, 0))],
        out_specs=pl.BlockSpec((tm, D_MODEL), lambda i: (i, 0)),
        out_shape=jax.ShapeDtypeStruct((T, D_MODEL), F32),
        compiler_params=_params(("parallel",), 48),
        name="mixer_outproj",
    )(x, mod_l, a1, a2, w, w)


def _outproj_parity_kernel(x_ref, mod_ref, a1_ref, a2_ref, w1_ref, w2_ref, o_ref, lane_sc):
    ys = [_dot(a1_ref[p], w1_ref[...]) + _dot(a2_ref[p], w2_ref[...]) for p in range(2)]
    o_ref[...] = x_ref[...] + mod_ref[5:6, :] * _interleave_rows(lane_sc, ys[0], ys[1])


def _outproj_parity_call(x, mod_l, goff, group_tokens, a, w):
    T = x.shape[0]
    tm = min(512, group_tokens)
    half = tm // 2
    hd = D_MODEL // 2
    return pl.pallas_call(
        _outproj_parity_kernel,
        grid=(T // tm,),
        in_specs=[pl.BlockSpec((tm, D_MODEL), lambda i: (i, 0)),
                  _mod_spec(goff, group_tokens // tm, 1),
                  pl.BlockSpec((2, half, hd), lambda i: (0, i, 0)),
                  pl.BlockSpec((2, half, hd), lambda i: (0, i, 1)),
                  pl.BlockSpec((hd, D_MODEL), lambda i: (0, 0)),
                  pl.BlockSpec((hd, D_MODEL), lambda i: (1, 0))],
        out_specs=pl.BlockSpec((tm, D_MODEL), lambda i: (i, 0)),
        out_shape=jax.ShapeDtypeStruct((T, D_MODEL), F32),
        scratch_shapes=[pltpu.VMEM((D_MODEL // LANE, tm, LANE), F32)],
        compiler_params=_params(("parallel",), 48),
        name="mixer_outproj_parity",
    )(x, mod_l, a, a, w, w)


def _final_norm_kernel(x_ref, g_ref, o_ref):
    x = x_ref[...]
    o_ref[...] = (x * lax.rsqrt(jnp.mean(x * x, axis=-1, keepdims=True) + 1e-6)) * g_ref[...]


def _final_norm_call(x, gamma):
    T = x.shape[0]
    tm = 512
    return pl.pallas_call(
        _final_norm_kernel,
        grid=(T // tm,),
        in_specs=[pl.BlockSpec((tm, D_MODEL), lambda i: (i, 0)),
                  pl.BlockSpec((1, D_MODEL), lambda i: (0, 0))],
        out_specs=pl.BlockSpec((tm, D_MODEL), lambda i: (i, 0)),
        out_shape=jax.ShapeDtypeStruct((T, D_MODEL), F32),
        compiler_params=_params(("parallel",), 32),
        name="final_norm",
    )(x, gamma)


def _shift_prev(cur, halo_prev, row0, seq_len):
    tt = cur.shape[0]
    rid = lax.broadcasted_iota(jnp.int32, (tt, 1), 0)
    pos = jnp.bitwise_and(rid + row0, seq_len - 1)
    prev = pltpu.roll(cur, 1, 0)
    prev = jnp.where(rid == 0, halo_prev[SUBLANE - 1:SUBLANE, :], prev)
    return jnp.where(pos == 0, 0.0, prev)


def _shift_next(cur, halo_next, row0, seq_len):
    tt = cur.shape[0]
    rid = lax.broadcasted_iota(jnp.int32, (tt, 1), 0)
    pos = jnp.bitwise_and(rid + row0, seq_len - 1)
    nxt = pltpu.roll(cur, tt - 1, 0)
    nxt = jnp.where(rid == tt - 1, halo_next[0:1, :], nxt)
    return jnp.where(pos == seq_len - 1, 0.0, nxt)


def _shift_prev_next(cur, halo_prev, halo_next, row0, seq_len):
    return _shift_prev(cur, halo_prev, row0, seq_len), _shift_next(cur, halo_next, row0, seq_len)


def _halo_specs(tt, width, col, total_rows):
    per = tt // SUBLANE
    last = total_rows // SUBLANE - 1
    return [pl.BlockSpec((tt, width), lambda i: (i, col)),
            pl.BlockSpec((SUBLANE, width), lambda i: (jnp.maximum(i * per - 1, 0), col)),
            pl.BlockSpec((SUBLANE, width), lambda i: (jnp.minimum((i + 1) * per, last), col))]


def _segsum(x, ones_blk):
    hi = x.astype(BF16)
    lo = (x - hi.astype(F32)).astype(BF16)
    outs = []
    for g in range(x.shape[1] // MXU_DIM):
        sl = slice(g * MXU_DIM, (g + 1) * MXU_DIM)
        outs.append(_dot(hi[:, sl], ones_blk) + _dot(lo[:, sl], ones_blk))
    return jnp.concatenate(outs, axis=1)


def _rwkv_prep_kernel(z_ref, zp_ref, zn_ref, mup_ref, mun_ref, kk_ref, ka_ref, rk_ref,
                      w0_ref, w2_ref, a0_ref, a2_ref, g2_ref, ones_ref,
                      r_o, v_o, c_o, w0_o, b0_o, k0_o, w1_o, b1_o, k1_o, bonus_o, g_o,
                      *, tt, seq_len):
    row0 = pl.program_id(0) * tt
    cur = z_ref[...]
    prev, nxt = _shift_prev_next(cur, zp_ref[...], zn_ref[...], row0, seq_len)
    zs = cur + mup_ref[...] * (prev - cur) + mun_ref[...] * (nxt - cur)
    r = zs[:, 0:D_RWKV]
    k = zs[:, D_RWKV:2 * D_RWKV]
    v = zs[:, 2 * D_RWKV:3 * D_RWKV]
    small = zs[:, 3 * D_RWKV:RWKV_COLS]
    ones_blk = ones_ref[...]

    kk = k * kk_ref[...]
    kk = kk / jnp.maximum(jnp.sqrt(_segsum(kk * kk, ones_blk)), 1e-12)
    tw = jnp.tanh(small).astype(BF16)
    sg = _sigmoid(small).astype(BF16)
    xs = small.astype(BF16)
    r_o[...] = r
    v_o[...] = v
    c_o[...] = -kk
    g_o[...] = _dot(sg, g2_ref[...])

    bonus = jnp.zeros_like(r)
    outs = ((w0_o, b0_o, k0_o), (w1_o, b1_o, k1_o))
    for d in range(2):
        wl = -_softplus(-(w0_ref[d:d + 1, :] + _dot(tw, w2_ref[d]))) - 0.5
        a = _sigmoid(a0_ref[d:d + 1, :] + _dot(xs, a2_ref[d]))
        kd = k * (1.0 + (a - 1.0) * ka_ref[...])
        w_o, b_o, k_o = outs[d]
        w_o[...] = jnp.exp(-jnp.exp(wl))
        b_o[...] = kk * a
        k_o[...] = kd
        bonus = bonus + _segsum(r * kd * rk_ref[...], ones_blk) * v
    bonus_o[...] = bonus


def _rwkv_prep_call(z_r, seq_len, p):
    T = z_r.shape[0]
    tt = min(256, seq_len)
    row = lambda n: pl.BlockSpec((1, n), lambda i: (0, 0))
    full2 = lambda a, b: pl.BlockSpec((a, b), lambda i: (0, 0))
    full3 = lambda a, b, c: pl.BlockSpec((a, b, c), lambda i: (0, 0, 0))
    in_specs = _halo_specs(tt, RWKV_COLS, 0, T) + [
        row(RWKV_COLS), row(RWKV_COLS), row(D_RWKV), row(D_RWKV), row(D_RWKV),
        full2(2, D_RWKV), full3(2, RWKV_SMALL, D_RWKV), full2(2, D_RWKV), full3(2, RWKV_SMALL, D_RWKV),
        full2(RWKV_SMALL, D_RWKV), full2(MXU_DIM, MXU_DIM)]
    out_spec = pl.BlockSpec((tt, D_RWKV), lambda i: (i, 0))
    out_shape = jax.ShapeDtypeStruct((T, D_RWKV), F32)
    return pl.pallas_call(
        functools.partial(_rwkv_prep_kernel, tt=tt, seq_len=seq_len),
        grid=(T // tt,),
        in_specs=in_specs,
        out_specs=[out_spec] * 11,
        out_shape=[out_shape] * 11,
        compiler_params=_params(("parallel",), 52),
        name="rwkv_prep",
    )(z_r, z_r, z_r, p['mu_prev'], p['mu_next'], p['k_k'], p['k_a'], p['r_k'],
      p['w0'], p['w2'], p['a0'], p['a2'], p['g2'], p['ones_blk'])


N_GRP = D_RWKV // MXU_DIM
HEADS_PER_GRP = H_RWKV // N_GRP


SCAN_ROWS = 16
SCAN_UNROLL = 8


def _scan_head_mask():
    return (jnp.arange(H_RWKV)[:, None] == (jnp.arange(D_RWKV) // RWKV_HEAD)[None, :]).astype(F32)


def _look_kernel(c_ref, cp_ref, cn_ref, v_ref, w0_ref, b0_ref, k0_ref, w1_ref, b1_ref, k1_ref,
                 ones_ref, sel_ref, wc0_o, vk0_o, be0_o, wc1_o, vk1_o, be1_o, *, tt, seq_len):
    row0 = pl.program_id(0) * tt
    c = c_ref[...]
    c_prev, c_next = _shift_prev_next(c, cp_ref[...], cn_ref[...], row0, seq_len)
    ones_blk = ones_ref[...]
    sel = sel_ref[...]
    v = v_ref[...]
    for cn, (w_ref, b_ref, k_ref, wc_o, vk_o, be_o) in (
            (c_next, (w0_ref, b0_ref, k0_ref, wc0_o, vk0_o, be0_o)),
            (c_prev, (w1_ref, b1_ref, k1_ref, wc1_o, vk1_o, be1_o))):
        wc_o[...] = w_ref[...] * cn
        vk_o[...] = v * _segsum(k_ref[...] * cn, ones_blk)
        bc = b_ref[...] * cn
        hi = bc.astype(BF16)
        lo = (bc - hi.astype(F32)).astype(BF16)
        be_o[...] = _dot(hi, sel) + _dot(lo, sel)


def _look_call(pre, seq_len, ones_blk):
    T = pre['c'].shape[0]
    tt = min(256, seq_len)
    blk = pl.BlockSpec((tt, D_RWKV), lambda i: (i, 0))
    sel = (jnp.arange(D_RWKV)[:, None] // RWKV_HEAD == jnp.arange(LANE)[None, :]).astype(BF16)
    big = jax.ShapeDtypeStruct((T, D_RWKV), F32)
    small = jax.ShapeDtypeStruct((T, LANE), F32)
    sblk = pl.BlockSpec((tt, LANE), lambda i: (i, 0))
    outs = pl.pallas_call(
        functools.partial(_look_kernel, tt=tt, seq_len=seq_len),
        grid=(T // tt,),
        in_specs=_halo_specs(tt, D_RWKV, 0, T) + [blk] * 7 + [
            pl.BlockSpec((MXU_DIM, MXU_DIM), lambda i: (0, 0)),
            pl.BlockSpec((D_RWKV, LANE), lambda i: (0, 0))],
        out_specs=[blk, blk, sblk, blk, blk, sblk],
        out_shape=[big, big, small, big, big, small],
        compiler_params=_params(("parallel",), 48),
        name="rwkv_lookahead",
    )(pre['c'], pre['c'], pre['c'], pre['v'], pre['w0'], pre['b0'], pre['k0'],
      pre['w1'], pre['b1'], pre['k1'], ones_blk, sel)
    return outs[0:3], outs[3:6]


def _scan_kernel(rf, wcf, wf, bf, kf, vf, vhf, vkf, bef, rb, wcb, wb, bb, kb, vb, vhb, vkb, beb,
                 c0_ref, v0_ref, s0_ref, eye_ref, hm_ref, y0_ref, y1_ref, sfin_ref, st, uv, *, tc):
    j = pl.program_id(1)
    eye = eye_ref[...]
    hmask = hm_ref[...]
    dirs = ((rf, wcf, wf, bf, kf, (vf, vhf), vkf, bef, y0_ref),
            (rb, wcb, wb, bb, kb, (vb, vhb), vkb, beb, y1_ref))
    nt = (((1,), (1,)), ((), ()))
    zrows = jnp.zeros((SCAN_ROWS, D_RWKV), BF16)
    zeye = jnp.zeros((SCAN_ROWS, MXU_DIM), BF16)
    seqs = s0_ref.shape[0]
    chains = [(b, d) for b in range(seqs) for d in range(2)]

    def head_rows(row):
        return (row * hmask).astype(BF16)

    def value_rows(row):
        x = row * hmask
        folded = x[:, 0:MXU_DIM]
        for g in range(1, N_GRP):
            folded = folded + x[:, g * MXU_DIM:(g + 1) * MXU_DIM]
        return folded.astype(BF16)

    def next_value_row(v_refs, b, d, t):
        v_, vh_ = v_refs
        if d == 0:
            inside = v_[b, pl.ds(jnp.minimum(t + 1, tc - 1), 1), :]
            return jnp.where(t == tc - 1, vh_[b, 0:1, :], inside)
        inside = v_[b, pl.ds(jnp.maximum(t - 1, 0), 1), :]
        return jnp.where(t == 0, vh_[b, SUBLANE - 1:SUBLANE, :], inside)

    def state_products(b, d, rows0, eye0, eye1, rows2):
        lhs = jnp.concatenate([st[b, d, g].astype(BF16) for g in range(N_GRP)] + [eye], axis=1)
        wr = jnp.concatenate([
            jnp.concatenate([rows0, eye0], axis=1),
            jnp.concatenate([zrows, eye1], axis=1),
            jnp.concatenate([rows2, zeye], axis=1),
            jnp.concatenate([zrows, zeye], axis=1)], axis=0)
        return lax.dot_general(lhs, wr, nt, preferred_element_type=F32)

    def store_y(y_, b, t, prod):
        tr = prod.T
        y_[b, pl.ds(t, 1), :, :] = tr[2 * SCAN_ROWS:3 * SCAN_ROWS, :].reshape(1, H_RWKV, RWKV_HEAD)

    def time_of(d, s):
        s = jnp.clip(s, 0, tc - 1)
        return s if d == 0 else tc - 1 - s

    @pl.when(j == 0)
    def _():
        st[...] = s0_ref[...]
        for (b, d) in chains:
            uv[b, d] = state_products(b, d, head_rows(c0_ref[b, d:d + 1, :]), zeye,
                                      value_rows(v0_ref[b, d:d + 1, :]), zrows)

    def issue(group, s):
        out = []
        for (b, d) in group:
            r_, wc_, w_, b_, k_, vn_, vk_, be_, y_ = dirs[d]
            t, tp = time_of(d, s), time_of(d, s - 1)
            prod = state_products(b, d, head_rows(wc_[b, pl.ds(t, 1), :]),
                                  value_rows(vk_[b, pl.ds(t, 1), :]),
                                  value_rows(next_value_row(vn_, b, d, t)),
                                  head_rows(r_[b, pl.ds(tp, 1), :]))
            w2 = jnp.concatenate([head_rows(b_[b, pl.ds(t, 1), :]), head_rows(k_[b, pl.ds(t, 1), :]),
                                  zrows, zrows], axis=0)
            out.append((prod, _dot(uv[b, d].astype(BF16), w2)))
        return out

    def finish(group, s, results):
        for (b, d), (prod, upd) in zip(group, results):
            r_, wc_, w_, b_, k_, vn_, vk_, be_, y_ = dirs[d]
            t, tp = time_of(d, s), time_of(d, s - 1)
            wrow = w_[b, pl.ds(t, 1), :]
            for g in range(N_GRP):
                sl = slice(g * MXU_DIM, (g + 1) * MXU_DIM)
                st[b, d, g] = st[b, d, g] * wrow[:, sl] + upd[:, sl]
            store_y(y_, b, tp, prod)
            uv[b, d] = prod + uv[b, d] * be_[b, pl.ds(t, 1), 0:4 * SCAN_ROWS]

    lead, lag = chains[:seqs], chains[seqs:]

    def body(i, carry):
        for u in range(SCAN_UNROLL):
            s = i * SCAN_UNROLL + u
            lead_results = issue(lead, s)
            lag_results = issue(lag, s)
            finish(lead, s, lead_results)
            finish(lag, s, lag_results)
        return carry

    lax.fori_loop(0, tc // SCAN_UNROLL, body, 0)

    for (b, d) in chains:
        t = time_of(d, tc - 1)
        prod = state_products(b, d, zrows, zeye, zeye, head_rows(dirs[d][0][b, pl.ds(t, 1), :]))
        store_y(dirs[d][8], b, t, prod)

    @pl.when(j == pl.num_programs(1) - 1)
    def _():
        sfin_ref[...] = st[...]


def _scan_call(pre, s0, eye, ones_blk):
    B, T, _ = pre['r'].shape
    nb = 4 if B % 4 == 0 else 2
    tc = min(128 if nb == 2 else 32, T)
    nj = T // tc
    fwd = pl.BlockSpec((nb, tc, D_RWKV), lambda bi, j: (bi, j, 0))
    bwd = pl.BlockSpec((nb, tc, D_RWKV), lambda bi, j: (bi, nj - 1 - j, 0))
    yfwd = pl.BlockSpec((nb, tc, H_RWKV, RWKV_HEAD), lambda bi, j: (bi, j, 0, 0))
    ybwd = pl.BlockSpec((nb, tc, H_RWKV, RWKV_HEAD), lambda bi, j: (bi, nj - 1 - j, 0, 0))
    st_spec = pl.BlockSpec((nb, 2, N_GRP, RWKV_HEAD, MXU_DIM), lambda bi, j: (bi, 0, 0, 0, 0))
    y_shape = jax.ShapeDtypeStruct((B, T, H_RWKV, RWKV_HEAD), F32)
    befwd = pl.BlockSpec((nb, tc, LANE), lambda bi, j: (bi, j, 0))
    bebwd = pl.BlockSpec((nb, tc, LANE), lambda bi, j: (bi, nj - 1 - j, 0))
    per = tc // SUBLANE
    last8 = T // SUBLANE - 1
    hfwd = pl.BlockSpec((nb, SUBLANE, D_RWKV), lambda bi, j: (bi, jnp.minimum((j + 1) * per, last8), 0))
    hbwd = pl.BlockSpec((nb, SUBLANE, D_RWKV), lambda bi, j: (bi, jnp.maximum((nj - 1 - j) * per - 1, 0), 0))
    first = pl.BlockSpec((nb, SUBLANE, D_RWKV), lambda bi, j: (bi, 0, 0))

    flat = {n: a.reshape(B * T, D_RWKV) for n, a in pre.items()}
    (wc0, vk0, be0), (wc1, vk1, be1) = [
        (wc.reshape(B, T, D_RWKV), vk.reshape(B, T, D_RWKV), be.reshape(B, T, LANE))
        for (wc, vk, be) in _look_call(flat, T, ones_blk)]
    v = pre['v']
    pad = jnp.zeros((B, SUBLANE - 2, D_RWKV), F32)
    c_first = jnp.concatenate([pre['c'][:, 0:1], pre['c'][:, T - 1:T], pad], axis=1)
    v_first = jnp.concatenate([v[:, 0:1], v[:, T - 1:T], pad], axis=1)
    return pl.pallas_call(
        functools.partial(_scan_kernel, tc=tc),
        grid=(B // nb, nj),
        in_specs=[fwd] * 6 + [hfwd, fwd, befwd] + [bwd] * 6 + [hbwd, bwd, bebwd] + [
            first, first,
            st_spec,
            pl.BlockSpec((RWKV_HEAD, MXU_DIM), lambda bi, j: (0, 0)),
            pl.BlockSpec((H_RWKV, D_RWKV), lambda bi, j: (0, 0))],
        out_specs=[yfwd, ybwd, st_spec],
        out_shape=[y_shape, y_shape, jax.ShapeDtypeStruct((B, 2, N_GRP, RWKV_HEAD, MXU_DIM), F32)],
        scratch_shapes=[pltpu.VMEM((nb, 2, N_GRP, RWKV_HEAD, MXU_DIM), F32),
                        pltpu.VMEM((nb, 2, RWKV_HEAD, 4 * SCAN_ROWS), F32)],
        compiler_params=_params(("arbitrary", "arbitrary"), 52),
        name="rwkv_scan",
    )(pre['r'], wc0, pre['w0'], pre['b0'], pre['k0'], v, v, vk0, be0,
      pre['r'], wc1, pre['w1'], pre['b1'], pre['k1'], v, v, vk1, be1,
      c_first, v_first, s0, eye.astype(BF16), _scan_head_mask())


def _rwkv_post_kernel(y0_ref, y1_ref, bonus_ref, g_ref, gw_ref, gb_ref, ones_ref, o_ref):
    ones_blk = ones_ref[...]
    y = y0_ref[...] + y1_ref[...]
    mu = _segsum(y, ones_blk) * (1.0 / RWKV_HEAD)
    yc = y - mu
    var = _segsum(yc * yc, ones_blk) * (1.0 / RWKV_HEAD)
    yn = yc * lax.rsqrt(var + RWKV_GN_EPS)
    out = (yn * gw_ref[...] + gb_ref[...] + bonus_ref[...]) * g_ref[...]
    o_ref[...] = out.astype(BF16)


def _rwkv_post_call(y0, y1, bonus, g, gn_w, gn_b, ones_blk):
    T = y0.shape[0]
    tt = 512
    blk = pl.BlockSpec((tt, D_RWKV), lambda i: (i, 0))
    row = pl.BlockSpec((1, D_RWKV), lambda i: (0, 0))
    return pl.pallas_call(
        _rwkv_post_kernel,
        grid=(T // tt,),
        in_specs=[blk, blk, blk, blk, row, row, pl.BlockSpec((MXU_DIM, MXU_DIM), lambda i: (0, 0))],
        out_specs=blk,
        out_shape=jax.ShapeDtypeStruct((T, D_RWKV), BF16),
        compiler_params=_params(("parallel",), 40),
        name="rwkv_post",
    )(y0, y1, bonus, g, gn_w, gn_b, ones_blk)


def _rope128(x, cos_t, sin_t):
    return x * cos_t + pltpu.roll(x, QK_ROPE, 1) * sin_t


def _pack_kv(kv, kr_rot, k_o, v_o):
    for h in range(H_MLA):
        k_o[:, h * HEAD_SLOT:h * HEAD_SLOT + QK_NOPE] = kv[:, h * HEAD_SLOT:h * HEAD_SLOT + QK_NOPE].astype(BF16)
        k_o[:, h * HEAD_SLOT + QK_NOPE:(h + 1) * HEAD_SLOT] = kr_rot.astype(BF16)
        v_o[:, h * V_HEAD:(h + 1) * V_HEAD] = kv[:, h * HEAD_SLOT + QK_NOPE:(h + 1) * HEAD_SLOT].astype(BF16)


def _mla_prep_kernel(z_ref, cos_ref, sin_ref, qn_ref, kvn_ref, wq_ref, wkv_ref,
                     q_o, k_o, v_o, ckv_o, kr_o):
    z = z_ref[...]
    cq = z[:, 0:Q_RANK]
    ckv = z[:, Q_RANK:Q_RANK + KV_RANK]
    krp = z[:, Q_RANK + KV_RANK:MLA_COLS]
    cos_t = cos_ref[...]
    sin_t = sin_ref[...]
    cq = (cq * lax.rsqrt(jnp.mean(cq * cq, axis=-1, keepdims=True) + 1e-6)) * qn_ref[...]
    ckv = (ckv * lax.rsqrt(jnp.mean(ckv * ckv, axis=-1, keepdims=True) + 1e-6)) * kvn_ref[...]
    ckv_o[...] = ckv
    kr_o[...] = krp
    q = _dot(cq.astype(BF16), wq_ref[...]) * (ATTN_SCALE * LOG2_E)
    for h in range(H_MLA):
        q_o[:, h * HEAD_SLOT:h * HEAD_SLOT + QK_NOPE] = q[:, h * HEAD_SLOT:h * HEAD_SLOT + QK_NOPE].astype(BF16)
        q_o[:, h * HEAD_SLOT + QK_NOPE:(h + 1) * HEAD_SLOT] = _rope128(
            q[:, h * HEAD_SLOT + QK_NOPE:(h + 1) * HEAD_SLOT], cos_t, sin_t).astype(BF16)
    kv = _dot(ckv.astype(BF16), wkv_ref[...])
    _pack_kv(kv, _rope128(krp, cos_t, sin_t), k_o, v_o)


def _mla_prep_call(z_m, cos_t, sin_t, seq_len, p):
    T = z_m.shape[0]
    tm = min(512, seq_len)
    per_seq = seq_len // tm
    row = lambda n: pl.BlockSpec((1, n), lambda i: (0, 0))
    blk = lambda n: pl.BlockSpec((tm, n), lambda i: (i, 0))
    tab = pl.BlockSpec((tm, LANE), lambda i: (i % per_seq, 0))
    return pl.pallas_call(
        _mla_prep_kernel,
        grid=(T // tm,),
        in_specs=[blk(MLA_COLS), tab, tab, row(Q_RANK), row(KV_RANK),
                  pl.BlockSpec((Q_RANK, H_MLA * HEAD_SLOT), lambda i: (0, 0)),
                  pl.BlockSpec((KV_RANK, H_MLA * HEAD_SLOT), lambda i: (0, 0))],
        out_specs=[blk(H_MLA * HEAD_SLOT), blk(H_MLA * HEAD_SLOT), blk(H_MLA * V_HEAD),
                   blk(KV_RANK), blk(LANE)],
        out_shape=[jax.ShapeDtypeStruct((T, H_MLA * HEAD_SLOT), BF16),
                   jax.ShapeDtypeStruct((T, H_MLA * HEAD_SLOT), BF16),
                   jax.ShapeDtypeStruct((T, H_MLA * V_HEAD), BF16),
                   jax.ShapeDtypeStruct((T, KV_RANK), F32),
                   jax.ShapeDtypeStruct((T, LANE), F32)],
        compiler_params=_params(("parallel",), 48),
        name="mla_prep",
    )(z_m, cos_t, sin_t, p['q_norm'], p['kv_norm'], p['w_qb'], p['w_kvb'])


def _ctx_kv_kernel(ckv_ref, kr_ref, wkv_ref, k_o, v_o):
    kv = _dot(ckv_ref[...].astype(BF16), wkv_ref[...])
    _pack_kv(kv, kr_ref[...], k_o, v_o)


def _ctx_kv_call(ckv_ctx, kr_ctx_pad, w_kvb):
    T = ckv_ctx.shape[0]
    tm = min(512, T)
    blk = lambda n: pl.BlockSpec((tm, n), lambda i: (i, 0))
    return pl.pallas_call(
        _ctx_kv_kernel,
        grid=(T // tm,),
        in_specs=[blk(KV_RANK), blk(LANE), pl.BlockSpec((KV_RANK, H_MLA * HEAD_SLOT), lambda i: (0, 0))],
        out_specs=[blk(H_MLA * HEAD_SLOT), blk(H_MLA * V_HEAD)],
        out_shape=[jax.ShapeDtypeStruct((T, H_MLA * HEAD_SLOT), BF16),
                   jax.ShapeDtypeStruct((T, H_MLA * V_HEAD), BF16)],
        compiler_params=_params(("parallel",), 32),
        name="mla_ctx_kv",
    )(ckv_ctx, kr_ctx_pad, w_kvb)


ATTN_ROWS = 64


def _attn_kernel(q_ref, k_ref, v_ref, o_ref, s_sc, p_sc, *, kc):
    tq, tk = s_sc.shape
    hq = tq // 2
    nt = (((1,), (1,)), ((), ()))

    def scores(h):
        q = q_ref[h * hq:(h + 1) * hq, :]
        for c in range(tk // kc):
            s_sc[h * hq:(h + 1) * hq, c * kc:(c + 1) * kc] = lax.dot_general(
                q, k_ref[c * kc:(c + 1) * kc, :], nt, preferred_element_type=F32)

    def softmax(h):
        sums = []
        for r in range(hq // ATTN_ROWS):
            rows = slice(h * hq + r * ATTN_ROWS, h * hq + (r + 1) * ATTN_ROWS)
            mpart = s_sc[rows, 0:LANE]
            for t in range(1, tk // LANE):
                mpart = jnp.maximum(mpart, s_sc[rows, t * LANE:(t + 1) * LANE])
            m = jnp.max(mpart, axis=-1, keepdims=True)
            lpart = jnp.zeros((ATTN_ROWS, LANE), F32)
            for t in range(tk // LANE):
                p = jnp.exp2(s_sc[rows, t * LANE:(t + 1) * LANE] - m)
                lpart = lpart + p
                p_sc[rows, t * LANE:(t + 1) * LANE] = p.astype(BF16)
            sums.append(jnp.sum(lpart, axis=-1, keepdims=True))
        return jnp.concatenate(sums, axis=0)

    def weighted_values(h, l):
        acc = jnp.zeros((hq, V_HEAD), F32)
        for c in range(tk // kc):
            acc = acc + _dot(p_sc[h * hq:(h + 1) * hq, c * kc:(c + 1) * kc], v_ref[c * kc:(c + 1) * kc, :])
        o_ref[h * hq:(h + 1) * hq, :] = (acc / l).astype(BF16)

    scores(0)
    scores(1)
    l0 = softmax(0)
    weighted_values(0, l0)
    l1 = softmax(1)
    weighted_values(1, l1)


def _attn_call(q, k, v):
    B, Tq, _ = q.shape
    Tk = k.shape[1]
    tq = min(512, Tq)
    return pl.pallas_call(
        functools.partial(_attn_kernel, kc=MXU_DIM),
        grid=(B, H_MLA, Tq // tq),
        in_specs=[pl.BlockSpec((None, tq, HEAD_SLOT), lambda b, h, i: (b, i, h)),
                  pl.BlockSpec((None, Tk, HEAD_SLOT), lambda b, h, i: (b, 0, h)),
                  pl.BlockSpec((None, Tk, V_HEAD), lambda b, h, i: (b, 0, h))],
        out_specs=pl.BlockSpec((None, tq, V_HEAD), lambda b, h, i: (b, i, h)),
        out_shape=jax.ShapeDtypeStruct((B, Tq, H_MLA * V_HEAD), BF16),
        scratch_shapes=[pltpu.VMEM((tq, Tk), F32), pltpu.VMEM((tq, Tk), BF16)],
        compiler_params=_params(("parallel", "parallel", "arbitrary"), 48),
        name="mla_attention",
    )(q, k, v)


def _conv3_kernel(*refs, tt, half_len):
    ins, (cw_refs, cb_refs), outs = refs[0:12], (refs[12:15], refs[15:18]), refs[18:]
    row0 = pl.program_id(0) * tt
    for s in range(3):
        even, odd = ins[4 * s][...], ins[4 * s + 1][...]
        odd_prev = _shift_prev(odd, ins[4 * s + 2][...], row0, half_len)
        even_next = _shift_next(even, ins[4 * s + 3][...], row0, half_len)
        cw = cw_refs[s][...]
        bias = cb_refs[s][...]
        y_even = cw[0:1, :] * odd_prev + cw[1:2, :] * even + cw[2:3, :] * odd + bias
        y_odd = cw[0:1, :] * even + cw[1:2, :] * odd + cw[2:3, :] * even_next + bias
        outs[s][0] = y_even
        outs[s][1] = y_odd
        if s == 2:
            outs[3][0] = y_even.astype(BF16)
            outs[3][1] = y_odd.astype(BF16)


def _conv3_call(z, half_len, conv_w, conv_b):
    T2 = z.shape[1]
    tt = min(128, half_len)
    per = tt // SUBLANE
    last = T2 // SUBLANE - 1
    in_specs = []
    for s in range(3):
        in_specs += [
            pl.BlockSpec((None, tt, D_HYENA), lambda i, s=s: (0, i, s)),
            pl.BlockSpec((None, tt, D_HYENA), lambda i, s=s: (1, i, s)),
            pl.BlockSpec((None, SUBLANE, D_HYENA), lambda i, s=s: (1, jnp.maximum(i * per - 1, 0), s)),
            pl.BlockSpec((None, SUBLANE, D_HYENA), lambda i, s=s: (0, jnp.minimum((i + 1) * per, last), s))]
    in_specs += [pl.BlockSpec((3, D_HYENA), lambda i, s=s: (0, s)) for s in range(3)]
    in_specs += [pl.BlockSpec((1, D_HYENA), lambda i, s=s: (0, s)) for s in range(3)]
    blk = pl.BlockSpec((2, tt, D_HYENA), lambda i: (0, i, 0))
    f32s = jax.ShapeDtypeStruct((2, T2, D_HYENA), F32)
    return pl.pallas_call(
        functools.partial(_conv3_kernel, tt=tt, half_len=half_len),
        grid=(T2 // tt,),
        in_specs=in_specs,
        out_specs=[blk] * 4,
        out_shape=[f32s, f32s, f32s, jax.ShapeDtypeStruct((2, T2, D_HYENA), BF16)],
        compiler_params=_params(("parallel",), 48),
        name="hyena_conv3",
    )(*([z] * 12), conv_w, conv_w, conv_w, conv_b, conv_b, conv_b)


def _filt_mlp_kernel(z_ref, w1_ref, b1_ref, w2_ref, b2_ref, fr_ref, o_ref):
    h = jnp.sin(fr_ref[0:1, :] * (_dot(z_ref[...].astype(BF16), w1_ref[...].astype(BF16)) + b1_ref[...]))
    h = jnp.sin(fr_ref[1:2, :] * (_dot(h.astype(BF16), w2_ref[...].astype(BF16)) + b2_ref[...]))
    o_ref[...] = h.astype(BF16)


def _filt_mlp_call(zpos, w1p, b1, w2, b2, freq):
    L = zpos.shape[0]
    return pl.pallas_call(
        _filt_mlp_kernel,
        out_shape=jax.ShapeDtypeStruct((L, FILT_HIDDEN), BF16),
        compiler_params=pltpu.CompilerParams(vmem_limit_bytes=32 * MIB),
        name="hyena_filter_mlp",
    )(zpos, w1p, b1, w2, b2, freq)


def _filt_gen_kernel(h_ref, tn_ref, dl_ref, w00, w01, w10, w11, o_ref, taps_sc):
    h = h_ref[...]
    L = h.shape[0]
    win = jnp.exp(-tn_ref[...] * dl_ref[...])
    not_first = lax.broadcasted_iota(jnp.int32, (L, 1), 0) > 0
    ws = ((w00, w01), (w10, w11))

    def emit(k, taps):
        taps_sc[...] = taps
        for par in range(2):
            o_ref[k, par] = taps_sc[pl.ds(par, L // 2, stride=2), :].astype(BF16)

    for n in range(2):
        causal = _dot(h, ws[n][0][...].astype(BF16)) * win
        anti = jnp.where(not_first, _dot(h, ws[n][1][...].astype(BF16)) * win, 0.0)
        norm = (jnp.sum(jnp.abs(causal), axis=0, keepdims=True)
                + jnp.sum(jnp.abs(anti), axis=0, keepdims=True))
        emit(2 * n, causal / norm)
        emit(2 * n + 1, anti / norm)


def _filt_gen_call(h2, tnorm, deltas, w3):
    L = h2.shape[0]
    tc = 128
    nc = D_HYENA // tc
    wspec = lambda k: pl.BlockSpec((FILT_HIDDEN, tc), lambda j, k=k: (0, k * nc + j))
    return pl.pallas_call(
        _filt_gen_kernel,
        grid=(nc,),
        in_specs=[pl.BlockSpec((L, FILT_HIDDEN), lambda j: (0, 0)),
                  pl.BlockSpec((L, 1), lambda j: (0, 0)),
                  pl.BlockSpec((1, tc), lambda j: (0, j)),
                  wspec(0), wspec(1), wspec(2), wspec(3)],
        out_specs=pl.BlockSpec((4, 2, L // 2, tc), lambda j: (0, 0, 0, j)),
        out_shape=jax.ShapeDtypeStruct((4, 2, L // 2, D_HYENA), BF16),
        scratch_shapes=[pltpu.VMEM((L, tc), F32)],
        compiler_params=_params(("parallel",), 48),
        name="hyena_filter_gen",
    )(h2, tnorm, deltas, w3, w3, w3, w3)


def _dft_fwd_kernel(f_ref, u_ref, o_ref):
    o_ref[...] = _dot(f_ref[...], u_ref[...])


def _dft_fwd_call(fmat, u):
    B, K, C = u.shape
    M = fmat.shape[0]
    tm = min(512, M)
    tn = min(C, 2048)
    return pl.pallas_call(
        _dft_fwd_kernel,
        grid=(B, C // tn, M // tm),
        in_specs=[pl.BlockSpec((tm, K), lambda b, j, i: (i, 0)),
                  pl.BlockSpec((None, K, tn), lambda b, j, i: (b, 0, j))],
        out_specs=pl.BlockSpec((None, tm, tn), lambda b, j, i: (b, i, j)),
        out_shape=jax.ShapeDtypeStruct((B, M, C), F32),
        compiler_params=_params(("parallel", "parallel", "arbitrary"), 40),
        name="hyena_dft_fwd",
    )(fmat, u)


def _butterfly(gc, gs, hc, hs, tw_c, tw_s, first):
    tc = hc * tw_c - hs * tw_s
    ts = hs * tw_c + hc * tw_s
    p0 = gc + tc
    p2 = gc - tc
    p1 = jnp.where(first, gs, gs + ts)
    p3 = jnp.where(first, hs, ts - gs)
    return p0, p1, p2, p3


def _cmul(ac, a_s, bc, bs):
    return ac * bc - a_s * bs, ac * bs + a_s * bc


def _filt_planes_kernel(ge_ref, ho_ref, twc_ref, tws_ref, o_ref, *, tr):
    first = (lax.broadcasted_iota(jnp.int32, (tr, 1), 0) + pl.program_id(1) * tr) == 0
    tw_c, tw_s = twc_ref[...], tws_ref[...]
    a = _butterfly(ge_ref[0, 0], ge_ref[0, 1], ho_ref[0, 0], ho_ref[0, 1], tw_c, tw_s, first)
    b = _butterfly(ge_ref[1, 0], ge_ref[1, 1], ho_ref[1, 0], ho_ref[1, 1], tw_c, tw_s, first)
    o_ref[0] = a[0] + b[0]
    o_ref[1] = jnp.where(first, a[1] + b[1], a[1] - b[1])
    o_ref[2] = a[2] + b[2]
    o_ref[3] = a[3] - b[3]


def _filt_planes_call(raw, twc, tws):
    _, L, C = raw.shape
    H = L // 2
    tr = min(256, H)
    tc = 512
    blk = lambda par: pl.BlockSpec((None, 2, None, 2, tr, tc), lambda n, i, j: (n, 0, par, 0, i, j))
    tw = pl.BlockSpec((tr, 1), lambda n, i, j: (i, 0))
    raw6 = raw.reshape(2, 2, 2, 2, H, C)
    return pl.pallas_call(
        functools.partial(_filt_planes_kernel, tr=tr),
        grid=(2, H // tr, C // tc),
        in_specs=[blk(0), blk(1), tw, tw],
        out_specs=pl.BlockSpec((None, 4, tr, tc), lambda n, i, j: (n, 0, i, j)),
        out_shape=jax.ShapeDtypeStruct((2, 4, H, C), F32),
        compiler_params=_params(("parallel", "parallel", "parallel"), 40),
        name="hyena_filter_planes",
    )(raw6, raw6, twc, tws)


def _spec_mul_kernel(raw_ref, k_ref, twc_ref, tws_ref, o_ref, *, tr):
    first = (lax.broadcasted_iota(jnp.int32, (tr, 1), 0) + pl.program_id(1) * tr) == 0
    tw_c, tw_s = twc_ref[...], tws_ref[...]
    p0, p1, p2, p3 = _butterfly(raw_ref[0, 0], raw_ref[0, 1], raw_ref[1, 0], raw_ref[1, 1], tw_c, tw_s, first)
    k0, k1, k2, k3 = k_ref[0], k_ref[1], k_ref[2], k_ref[3]
    yac, yas = _cmul(p0, p1, k0, k1)
    ybc, ybs = _cmul(p2, p3, k2, k3)
    ymc, yms = _cmul(p1, p3, k1, k3)
    yac = jnp.where(first, p0 * k0, yac)
    ybc = jnp.where(first, p2 * k2, ybc)
    dc = yac - ybc
    ds = yas + ybs
    o_ref[0, 0] = (yac + ybc).astype(BF16)
    o_ref[0, 1] = jnp.where(first, ymc, yas - ybs).astype(BF16)
    o_ref[1, 0] = jnp.where(first, dc, dc * tw_c + ds * tw_s).astype(BF16)
    o_ref[1, 1] = jnp.where(first, yms, ds * tw_c - dc * tw_s).astype(BF16)


def _spec_mul_call(raw, k_planes, order, twc, tws):
    B2, L, C = raw.shape
    B, H = B2 // 2, L // 2
    tr = min(256, H)
    tc = 512
    tw = pl.BlockSpec((tr, 1), lambda b, i, j: (i, 0))
    out = pl.pallas_call(
        functools.partial(_spec_mul_kernel, tr=tr),
        grid=(B, H // tr, C // tc),
        in_specs=[pl.BlockSpec((2, None, 2, tr, tc), lambda b, i, j: (0, b, 0, i, j)),
                  pl.BlockSpec((None, 4, tr, tc), lambda b, i, j: (order, 0, i, j)),
                  tw, tw],
        out_specs=pl.BlockSpec((2, None, 2, tr, tc), lambda b, i, j: (0, b, 0, i, j)),
        out_shape=jax.ShapeDtypeStruct((2, B, 2, H, C), BF16),
        compiler_params=_params(("parallel", "parallel", "parallel"), 40),
        name="hyena_spectral_mul",
    )(raw.reshape(2, B, 2, H, C), k_planes, twc, tws)
    return out.reshape(B2, L, C)


def _dft_inv_kernel(f_ref, y_ref, gate_ref, u_ref, bias_ref, o_ref, ob_ref):
    conv = _dot(f_ref[...], y_ref[...])
    out = gate_ref[...] * (conv + u_ref[...] * bias_ref[...])
    o_ref[...] = out
    ob_ref[...] = out.astype(BF16)


def _dft_inv_call(imat, y_spec, gate, u, bias):
    B, M, C = u.shape
    K = imat.shape[1]
    tm = min(512, M)
    tn = 1024 if K > 1024 else min(C, 2048)
    blk = pl.BlockSpec((None, tm, tn), lambda b, j, i: (b, i, j))
    return pl.pallas_call(
        _dft_inv_kernel,
        grid=(B, C // tn, M // tm),
        in_specs=[pl.BlockSpec((tm, K), lambda b, j, i: (i, 0)),
                  pl.BlockSpec((None, K, tn), lambda b, j, i: (b, 0, j)),
                  blk, blk, pl.BlockSpec((1, tn), lambda b, j, i: (0, j))],
        out_specs=[blk, blk],
        out_shape=[jax.ShapeDtypeStruct((B, M, C), F32), jax.ShapeDtypeStruct((B, M, C), BF16)],
        compiler_params=_params(("parallel", "parallel", "arbitrary"), 48),
        name="hyena_dft_inv",
    )(imat, y_spec, gate, u, bias)


def _dft_tables(L):
    H = L // 2
    lo = min(64, H)
    hi = H // lo
    g = jnp.arange(H, dtype=jnp.int32)
    theta = 2.0 * math.pi / L
    ang_hi = ((g[:, None] * (jnp.arange(hi, dtype=jnp.int32) * lo)[None, :]) % L).astype(F32) * theta
    ang_lo = ((g[:, None] * jnp.arange(lo, dtype=jnp.int32)[None, :]) % L).astype(F32) * theta
    ch, sh, cl, sl = jnp.cos(ang_hi), jnp.sin(ang_hi), jnp.cos(ang_lo), jnp.sin(ang_lo)
    cos_m = (ch[:, :, None] * cl[:, None, :] - sh[:, :, None] * sl[:, None, :]).reshape(H, H)
    sin_m = (sh[:, :, None] * cl[:, None, :] + ch[:, :, None] * sl[:, None, :]).reshape(H, H)
    alt = jnp.where(jnp.arange(H) % 2 == 0, 1.0, -1.0).astype(F32)
    sin_m = jnp.where((g == 0)[:, None], alt[None, :], sin_m)
    fwd = jnp.concatenate([cos_m, sin_m], axis=0)
    n = 2.0 * L
    w_cos = jnp.where(g == 0, 1.0 / n, 2.0 / n).astype(F32)
    inv = jnp.concatenate([cos_m * w_cos[:, None], sin_m * (2.0 / n)], axis=0).T
    ang_tw = g.astype(F32) * (math.pi / L)
    return fwd.astype(BF16), inv.astype(BF16), jnp.cos(ang_tw)[:, None], jnp.sin(ang_tw)[:, None]


def _filter_positions(L):
    t = jnp.arange(L, dtype=F32)
    t_norm = t / max(L - 1, 1)
    bands = (POS_EMB - 1) // 2
    freqs = jnp.linspace(1e-4, bands - 1, bands, dtype=F32)
    ang = (2.0 * math.pi / L) * t[:, None] * freqs[None, :]
    z = jnp.concatenate([t_norm[:, None], jnp.cos(ang), -jnp.sin(ang)], axis=-1)
    return jnp.pad(z, ((0, 0), (0, POS_PAD - POS_EMB))), t_norm[:, None]


def _hyena_deltas():
    return jnp.linspace(abs(math.log(HYENA_TARGET)) / SLOW_DECAY_PCT,
                        abs(math.log(HYENA_TARGET)) / FAST_DECAY_PCT, D_HYENA, dtype=F32)[None, :]


def _hyena_filter_spectrum(L, tables, p):
    fwd, _, twc, tws = tables
    zpos, tnorm = _filter_positions(L)
    h2 = _filt_mlp_call(zpos, p['filt_w1'], p['filt_b1'], p['filt_w2'], p['filt_b2'], p['filt_freq'])
    filt = _filt_gen_call(h2, tnorm, _hyena_deltas(), p['filt_w3'])
    raw = _dft_fwd_call(fwd, filt.reshape(8, L // 2, D_HYENA))
    return _filt_planes_call(raw, twc, tws)


def _hyena_mixer(z, B, L, tables, k_planes, p):
    fwd, inv, twc, tws = tables
    H = L // 2
    x1, x2, v, vb = _conv3_call(z, H, p['conv_w'], p['conv_b'])
    shp = (2 * B, H, D_HYENA)
    u, ub = v.reshape(shp), vb.reshape(shp)
    for n, gate in enumerate((x1, x2)):
        raw = _dft_fwd_call(fwd, ub)
        yspec = _spec_mul_call(raw, k_planes, n, twc, tws)
        u, ub = _dft_inv_call(inv, yspec, gate.reshape(shp), u, p['bias'][n:n + 1])
    return ub.reshape(2, B * H, D_HYENA)


def _rope_swap(w):
    q = QK_ROPE // 4
    return jnp.concatenate([w[..., q:2 * q], w[..., 0:q], w[..., 3 * q:4 * q], w[..., 2 * q:3 * q]], axis=-1)


def _pad_cols(w, n):
    return jnp.pad(w, [(0, 0)] * (w.ndim - 1) + [(0, n - w.shape[-1])])


def _pack_even(e, w_in_even, mu_prev, mu_next, rwkv_w0, rwkv_w2, rwkv_a0, rwkv_a2, rwkv_g2,
               rwkv_kk, rwkv_ka, rwkv_rk, rwkv_gn_w, rwkv_gn_b, mla_q_norm, mla_kv_norm,
               mla_w_qb, mla_w_kvb, w_out_even):
    n_r = 3 * D_RWKV + W_LORA + A_LORA + G_LORA
    w_in = w_in_even[e]
    w_r = _pad_cols(w_in[:, :n_r], RWKV_COLS).astype(BF16)
    w_m = w_in[:, n_r:]
    kr_cols = w_m[:, Q_RANK + KV_RANK:]
    w_m = jnp.concatenate([w_m, _rope_swap(kr_cols)], axis=-1).astype(BF16)
    small_rows = lambda w, off: jnp.pad(w, [(0, 0)] * (w.ndim - 2)
                                        + [(off, RWKV_SMALL - off - w.shape[-2]), (0, 0)]).astype(BF16)
    wq = mla_w_qb[e].reshape(Q_RANK, H_MLA, QK_NOPE + QK_ROPE)
    wq = jnp.concatenate([wq, _rope_swap(wq[..., QK_NOPE:])], axis=-1).reshape(Q_RANK, H_MLA * HEAD_SLOT)
    blk = jnp.arange(MXU_DIM) // RWKV_HEAD
    return {
        'w_r': w_r, 'w_m': w_m,
        'mu_prev': _pad_cols(mu_prev[e][None, :], RWKV_COLS),
        'mu_next': _pad_cols(mu_next[e][None, :], RWKV_COLS),
        'k_k': rwkv_kk[e][None, :], 'k_a': rwkv_ka[e][None, :],
        'r_k': rwkv_rk[e].reshape(1, D_RWKV),
        'w0': rwkv_w0[e], 'w2': small_rows(rwkv_w2[e], 0),
        'a0': rwkv_a0[e], 'a2': small_rows(rwkv_a2[e], W_LORA),
        'g2': small_rows(rwkv_g2[e], W_LORA + A_LORA),
        'gn_w': rwkv_gn_w[e][None, :], 'gn_b': rwkv_gn_b[e][None, :],
        'q_norm': mla_q_norm[e][None, :], 'kv_norm': mla_kv_norm[e][None, :],
        'w_qb': wq.astype(BF16), 'w_kvb': mla_w_kvb[e].astype(BF16),
        'w_out': w_out_even[e].astype(BF16),
        'ones_blk': (blk[:, None] == blk[None, :]).astype(BF16),
        'eye': (jnp.arange(RWKV_HEAD)[:, None] == (jnp.arange(MXU_DIM) % RWKV_HEAD)[None, :]).astype(F32),
    }


def _rope_tables(L):
    rows = L // GRID_W
    row = jnp.repeat(jnp.arange(rows, dtype=F32), GRID_W)
    col = jnp.tile(jnp.arange(GRID_W, dtype=F32), rows)
    half = QK_ROPE // 2
    inv = 1.0 / (ROPE_THETA ** (jnp.arange(0, half, 2, dtype=F32) / half))
    ar, ac = row[:, None] * inv[None, :], col[:, None] * inv[None, :]
    cos_t = jnp.concatenate([jnp.cos(ar), jnp.cos(ar), jnp.cos(ac), jnp.cos(ac)], axis=-1)
    sin_t = jnp.concatenate([-jnp.sin(ar), jnp.sin(ar), -jnp.sin(ac), jnp.sin(ac)], axis=-1)
    return _pad_cols(cos_t, LANE), _pad_cols(sin_t, LANE)


def _state_to_groups(s):
    B = s.shape[0]
    s = s.reshape(B, 2, N_GRP, H_RWKV // N_GRP, RWKV_HEAD, RWKV_HEAD)
    return jnp.swapaxes(s, 3, 4).reshape(B, 2, N_GRP, RWKV_HEAD, MXU_DIM)


def _groups_to_state(s):
    B = s.shape[0]
    s = s.reshape(B, 2, N_GRP, RWKV_HEAD, H_RWKV // N_GRP, RWKV_HEAD)
    return jnp.swapaxes(s, 3, 4).reshape(B, 2, H_RWKV, RWKV_HEAD, RWKV_HEAD)


def _even_mixer(x, mod_l, goff, B, L, gamma, p, rope, ctx):
    group_tokens = x.shape[0] if ctx is None else L
    z_r = _inproj_call(x, mod_l, goff, group_tokens, gamma, p['w_r'], RWKV_COLS // 3)
    z_m = _inproj_call(x, mod_l, goff, group_tokens, gamma, p['w_m'], MLA_COLS)
    names = ('r', 'v', 'c', 'w0', 'b0', 'k0', 'w1', 'b1', 'k1', 'bonus', 'g')
    pre = dict(zip(names, _rwkv_prep_call(z_r, L, p)))
    seq = {n: pre[n].reshape(B, L, D_RWKV) for n in names[:9]}
    if ctx is None:
        s0 = jnp.zeros((B, 2, N_GRP, RWKV_HEAD, MXU_DIM), F32)
    else:
        s0 = _state_to_groups(ctx[2].astype(F32))
    y0, y1, s_fin = _scan_call(seq, s0, p['eye'], p['ones_blk'])
    y_r = _rwkv_post_call(y0.reshape(B * L, D_RWKV), y1.reshape(B * L, D_RWKV), pre['bonus'], pre['g'],
                          p['gn_w'], p['gn_b'], p['ones_blk'])

    q, k, v, ckv, krp = _mla_prep_call(z_m, rope[0], rope[1], L, p)
    q = q.reshape(B, L, H_MLA * HEAD_SLOT)
    k = k.reshape(B, L, H_MLA * HEAD_SLOT)
    v = v.reshape(B, L, H_MLA * V_HEAD)
    if ctx is not None:
        P = ctx[0].shape[1]
        k_ctx, v_ctx = _ctx_kv_call(ctx[0].reshape(B * P, KV_RANK),
                                    _pad_cols(ctx[1].reshape(B * P, QK_ROPE), LANE), p['w_kvb'])
        k = jnp.concatenate([k, k_ctx.reshape(B, P, H_MLA * HEAD_SLOT)], axis=1)
        v = jnp.concatenate([v, v_ctx.reshape(B, P, H_MLA * V_HEAD)], axis=1)
    y_m = _attn_call(q, k, v).reshape(B * L, H_MLA * V_HEAD)
    x = _outproj_call(x, mod_l, goff, group_tokens, y_r, y_m, p['w_out'])
    state = (_groups_to_state(s_fin), ckv.reshape(B, L, KV_RANK), krp[:, :QK_ROPE].reshape(B, L, QK_ROPE))
    return x, state


def _odd_mixer(x, mod_l, goff, group_tokens, B, L, gamma, tables, k_planes, p):
    z = _inproj_parity_call(x, mod_l, goff, group_tokens, gamma, p['w_in'], 1536)
    y = _hyena_mixer(z, B, L, tables, k_planes, p)
    return _outproj_parity_call(x, mod_l, goff, group_tokens, y, p['w_out'])


def kernel(x_prompt, x_sample, cache_mla_ckv, cache_mla_krope, state_rwkv, c, c_ctx,
           w_mod, b_mod, norm_g, w_ffn_in, w_ffn_out, final_norm_g,
           w_in_even, mu_prev, mu_next, rwkv_w0, rwkv_w2, rwkv_a0, rwkv_a2, rwkv_g2,
           rwkv_kk, rwkv_ka, rwkv_rk, rwkv_gn_w, rwkv_gn_b,
           mla_q_norm, mla_kv_norm, mla_w_qb, mla_w_kvb, w_out_even,
           w_in_odd, hy_conv_w, hy_conv_b, hy_filt_w1, hy_filt_b1, hy_filt_w2, hy_filt_b2,
           hy_filt_w3, hy_filt_freq, hy_bias, w_out_odd):
    Bp, Lp, D = x_prompt.shape
    Bs, Ls, _ = x_sample.shape
    depth = w_mod.shape[0]
    xp = x_prompt.reshape(Bp * Lp, D)
    xs = x_sample.reshape(Bs * Ls, D)
    Tp = Bp * Lp

    cvec = jnp.concatenate([c_ctx[None, :], c, jnp.zeros((SUBLANE - 1 - Bs, D), F32)], axis=0)
    mod = _mod_call(cvec, w_mod, b_mod)

    rope_p = (_pad_cols(jnp.ones((Lp, QK_ROPE), F32), LANE), jnp.zeros((Lp, LANE), F32))
    rope_s = _rope_tables(Ls)
    tabs_p = tabs_s = None
    w_in = w_ffn_in.astype(BF16)
    w_out = w_ffn_out.astype(BF16)
    new_ckv, new_kr, new_s = [], [], []
    for l in range(depth):
        mod_l = mod[l]
        gam = [norm_g[l, s][None, :] for s in range(3)]
        xp = _ffn_call(xp, mod_l, 0, Tp, gam[0], w_in, w_out, l, 0, 0)
        xs = _ffn_call(xs, mod_l, 1, Ls, gam[0], w_in, w_out, l, 0, 0)
        if l % 2 == 0:
            e = l // 2
            p = _pack_even(e, w_in_even, mu_prev, mu_next, rwkv_w0, rwkv_w2, rwkv_a0, rwkv_a2, rwkv_g2,
                           rwkv_kk, rwkv_ka, rwkv_rk, rwkv_gn_w, rwkv_gn_b, mla_q_norm, mla_kv_norm,
                           mla_w_qb, mla_w_kvb, w_out_even)
            ctx = (cache_mla_ckv[:, e], cache_mla_krope[:, e], state_rwkv[:, e])
            xp, st = _even_mixer(xp, mod_l, 0, Bp, Lp, gam[1], p, rope_p, None)
            xs, _ = _even_mixer(xs, mod_l, 1, Bs, Ls, gam[1], p, rope_s, ctx)
            new_s.append(st[0].astype(x_prompt.dtype))
            new_ckv.append(st[1])
            new_kr.append(st[2])
        else:
            o = l // 2
            p = {'w_in': w_in_odd[o].astype(BF16), 'conv_w': hy_conv_w[o], 'conv_b': hy_conv_b[o][None, :],
                 'filt_w1': jnp.pad(hy_filt_w1[o], ((0, POS_PAD - POS_EMB), (0, 0))),
                 'filt_b1': hy_filt_b1[o][None, :], 'filt_w2': hy_filt_w2[o],
                 'filt_b2': hy_filt_b2[o][None, :], 'filt_w3': hy_filt_w3[o],
                 'filt_freq': hy_filt_freq[o], 'bias': hy_bias[o], 'w_out': w_out_odd[o].astype(BF16)}
            if tabs_p is None:
                tabs_p, tabs_s = _dft_tables(Lp), _dft_tables(Ls)
            ks_p = _hyena_filter_spectrum(Lp, tabs_p, p)
            ks_s = _hyena_filter_spectrum(Ls, tabs_s, p)
            xp = _odd_mixer(xp, mod_l, 0, Tp, Bp, Lp, gam[1], tabs_p, ks_p, p)
            xs = _odd_mixer(xs, mod_l, 1, Ls, Bs, Ls, gam[1], tabs_s, ks_s, p)
        xp = _ffn_call(xp, mod_l, 0, Tp, gam[2], w_in, w_out, l, 1, 2)
        xs = _ffn_call(xs, mod_l, 1, Ls, gam[2], w_in, w_out, l, 1, 2)

    fg = final_norm_g[None, :]
    y_prompt = _final_norm_call(xp, fg).reshape(Bp, Lp, D)
    y_sample = _final_norm_call(xs, fg).reshape(Bs, Ls, D)
    return (y_prompt, y_sample, jnp.stack(new_ckv, axis=1), jnp.stack(new_kr, axis=1),
            jnp.stack(new_s, axis=1))
```

```python
import functools
import math

import jax
import jax.numpy as jnp
from jax import lax
from jax.experimental import pallas as pl
from jax.experimental.pallas import tpu as pltpu

F32 = jnp.float32
BF16 = jnp.bfloat16

D_MODEL = 2048
N_MOD = 9
D_FF = 5632
D_RWKV = 1024
RWKV_HEAD = 64
H_RWKV = 16
W_LORA = 64
A_LORA = 64
G_LORA = 160
RWKV_SMALL = 384
RWKV_COLS = 3 * D_RWKV + RWKV_SMALL
RWKV_GN_EPS = 64e-5
H_MLA = 8
QK_NOPE = 128
QK_ROPE = 64
V_HEAD = 128
Q_RANK = 512
KV_RANK = 256
MLA_COLS = Q_RANK + KV_RANK + 2 * QK_ROPE
EVEN_TN = RWKV_COLS // 3
HEAD_SLOT = 256
ROPE_THETA = 10000.0
ATTN_SCALE = (QK_NOPE + QK_ROPE) ** -0.5
LOG2_E = 1.0 / math.log(2.0)
GRID_W = 64
D_HYENA = 2048
POS_EMB = 33
POS_PAD = 128
FILT_HIDDEN = 64
HYENA_TARGET = 1e-2
FAST_DECAY_PCT = 0.3
SLOW_DECAY_PCT = 1.5
LANE = 128
SUBLANE = 8
MXU_DIM = 256
MIB = 1024 * 1024


def _params(sem, vmem_mib):
    return pltpu.CompilerParams(dimension_semantics=sem, vmem_limit_bytes=vmem_mib * MIB)


def _sigmoid(x):
    return 1.0 / (1.0 + jnp.exp(-x))


def _softplus(x):
    return jnp.maximum(x, 0.0) + jnp.log(1.0 + jnp.exp(-jnp.abs(x)))


def _dot(a, b):
    return jnp.dot(a, b, preferred_element_type=F32)


def _norm_mod(x, gamma, shift, scale):
    xn = x * lax.rsqrt(jnp.mean(x * x, axis=-1, keepdims=True) + 1e-6)
    return (xn * gamma) * (1.0 + scale) + shift


def _mod_kernel(c_ref, w_ref, b_ref, o_ref):
    c = c_ref[...]
    s = c * _sigmoid(c)
    o_ref[0] = _dot(s.astype(BF16), w_ref[0].astype(BF16)) + b_ref[0]


def _mod_call(cvec, w_mod, b_mod):
    L, Dm, N = w_mod.shape
    tn = 1024
    out = pl.pallas_call(
        _mod_kernel,
        grid=(L, N // tn),
        in_specs=[pl.BlockSpec((SUBLANE, Dm), lambda l, j: (0, 0)),
                  pl.BlockSpec((1, Dm, tn), lambda l, j: (l, 0, j)),
                  pl.BlockSpec((1, 1, tn), lambda l, j: (l, 0, j))],
        out_specs=pl.BlockSpec((1, SUBLANE, tn), lambda l, j: (l, 0, j)),
        out_shape=jax.ShapeDtypeStruct((L, SUBLANE, N), F32),
        compiler_params=_params(("arbitrary", "arbitrary"), 40),
        name="adaln_mod",
    )(cvec, w_mod, b_mod.reshape(L, 1, N))
    return out.reshape(L, SUBLANE, N_MOD, Dm)


def _mod_spec(goff, tiles_per_group, nargs):
    if nargs == 1:
        return pl.BlockSpec((None, N_MOD, D_MODEL), lambda i: (goff + i // tiles_per_group, 0, 0))
    return pl.BlockSpec((None, N_MOD, D_MODEL), lambda i, j: (goff + i // tiles_per_group, 0, 0))


def _ffn_kernel(x_ref, mod_ref, g_ref, wg_ref, wu_ref, wo_ref, o_ref, h_sc, acc_sc, *, sub):
    f = pl.program_id(1)

    @pl.when(f == 0)
    def _():
        h = _norm_mod(x_ref[...], g_ref[...], mod_ref[3 * sub:3 * sub + 1, :],
                      mod_ref[3 * sub + 1:3 * sub + 2, :])
        h_sc[...] = h.astype(BF16)
        acc_sc[...] = jnp.zeros_like(acc_sc)

    h = h_sc[...]
    a = _dot(h, wg_ref[...])
    u = _dot(h, wu_ref[...])
    act = (a * _sigmoid(a)) * u
    acc_sc[...] += _dot(act.astype(BF16), wo_ref[...])

    @pl.when(f == pl.num_programs(1) - 1)
    def _():
        o_ref[...] = x_ref[...] + 0.5 * mod_ref[3 * sub + 2:3 * sub + 3, :] * acc_sc[...]


def _ffn_call(x, mod_l, goff, group_tokens, gamma, w_in, w_out, l, s, sub):
    T = x.shape[0]
    tm = min(512, group_tokens)
    tf = 512
    nf = D_FF // tf
    return pl.pallas_call(
        functools.partial(_ffn_kernel, sub=sub),
        grid=(T // tm, nf),
        in_specs=[pl.BlockSpec((tm, D_MODEL), lambda i, f: (i, 0)),
                  _mod_spec(goff, group_tokens // tm, 2),
                  pl.BlockSpec((1, D_MODEL), lambda i, f: (0, 0)),
                  pl.BlockSpec((None, None, D_MODEL, tf), lambda i, f: (l, s, 0, f)),
                  pl.BlockSpec((None, None, D_MODEL, tf), lambda i, f: (l, s, 0, f + nf)),
                  pl.BlockSpec((None, None, tf, D_MODEL), lambda i, f: (l, s, f, 0))],
        out_specs=pl.BlockSpec((tm, D_MODEL), lambda i, f: (i, 0)),
        out_shape=jax.ShapeDtypeStruct((T, D_MODEL), F32),
        scratch_shapes=[pltpu.VMEM((tm, D_MODEL), BF16), pltpu.VMEM((tm, D_MODEL), F32)],
        compiler_params=_params(("parallel", "arbitrary"), 52),
        name="ffn_swiglu",
    )(x, mod_l, gamma, w_in, w_in, w_out)


def _inproj_kernel(x_ref, mod_ref, g_ref, w_ref, o_ref, h_sc, *, col_axis):
    @pl.when(pl.program_id(col_axis) == 0)
    def _():
        h = _norm_mod(x_ref[...], g_ref[...], mod_ref[3:4, :], mod_ref[4:5, :])
        h_sc[...] = h.astype(BF16)

    o_ref[...] = _dot(h_sc[...], w_ref[...])


def _inproj_call(x, mod_l, goff, group_tokens, gamma, w, tn):
    T = x.shape[0]
    N = w.shape[1]
    tm = min(512, group_tokens)
    return pl.pallas_call(
        functools.partial(_inproj_kernel, col_axis=1),
        grid=(T // tm, N // tn),
        in_specs=[pl.BlockSpec((tm, D_MODEL), lambda i, j: (i, 0)),
                  _mod_spec(goff, group_tokens // tm, 2),
                  pl.BlockSpec((1, D_MODEL), lambda i, j: (0, 0)),
                  pl.BlockSpec((D_MODEL, tn), lambda i, j: (0, j))],
        out_specs=pl.BlockSpec((tm, tn), lambda i, j: (i, j)),
        out_shape=jax.ShapeDtypeStruct((T, N), F32),
        scratch_shapes=[pltpu.VMEM((tm, D_MODEL), BF16)],
        compiler_params=_params(("parallel", "arbitrary"), 48),
        name="mixer_inproj",
    )(x, mod_l, gamma, w)


def _rows_of_parity(lane_sc, x, par):
    rows = x.shape[0]
    parts = []
    for c in range(x.shape[1] // LANE):
        lane_sc[c] = x[:, c * LANE:(c + 1) * LANE]
        parts.append(lane_sc[c, pl.ds(par, rows // 2, stride=2), :])
    return jnp.concatenate(parts, axis=1)


def _interleave_rows(lane_sc, even, odd):
    half = even.shape[0]
    parts = []
    for c in range(even.shape[1] // LANE):
        lane_sc[c, pl.ds(0, half, stride=2), :] = even[:, c * LANE:(c + 1) * LANE]
        lane_sc[c, pl.ds(1, half, stride=2), :] = odd[:, c * LANE:(c + 1) * LANE]
        parts.append(lane_sc[c])
    return jnp.concatenate(parts, axis=1)


def _inproj_parity_kernel(x_ref, mod_ref, g_ref, w_ref, o_ref, h_sc, lane_sc, *, half):
    @pl.when(pl.program_id(1) == 0)
    def _():
        h = _norm_mod(x_ref[...], g_ref[...], mod_ref[3:4, :], mod_ref[4:5, :])
        for p in range(2):
            h_sc[p * half:(p + 1) * half, :] = _rows_of_parity(lane_sc, h, p).astype(BF16)

    z = _dot(h_sc[...], w_ref[...])
    o_ref[0] = z[:half]
    o_ref[1] = z[half:]


def _inproj_parity_call(x, mod_l, goff, group_tokens, gamma, w, tn):
    T = x.shape[0]
    N = w.shape[1]
    tm = min(512, group_tokens)
    half = tm // 2
    return pl.pallas_call(
        functools.partial(_inproj_parity_kernel, half=half),
        grid=(T // tm, N // tn),
        in_specs=[pl.BlockSpec((tm, D_MODEL), lambda i, j: (i, 0)),
                  _mod_spec(goff, group_tokens // tm, 2),
                  pl.BlockSpec((1, D_MODEL), lambda i, j: (0, 0)),
                  pl.BlockSpec((D_MODEL, tn), lambda i, j: (0, j))],
        out_specs=pl.BlockSpec((2, half, tn), lambda i, j: (0, i, j)),
        out_shape=jax.ShapeDtypeStruct((2, T // 2, N), F32),
        scratch_shapes=[pltpu.VMEM((tm, D_MODEL), BF16), pltpu.VMEM((D_MODEL // LANE, tm, LANE), F32)],
        compiler_params=_params(("parallel", "arbitrary"), 48),
        name="mixer_inproj_parity",
    )(x, mod_l, gamma, w)


def _outproj_kernel(x_ref, mod_ref, a1_ref, a2_ref, w1_ref, w2_ref, o_ref):
    y = _dot(a1_ref[...], w1_ref[...]) + _dot(a2_ref[...], w2_ref[...])
    o_ref[...] = x_ref[...] + mod_ref[5:6, :] * y


def _outproj_call(x, mod_l, goff, group_tokens, a1, a2, w):
    T = x.shape[0]
    tm = min(512, group_tokens)
    half = D_MODEL // 2
    return pl.pallas_call(
        _outproj_kernel,
        grid=(T // tm,),
        in_specs=[pl.BlockSpec((tm, D_MODEL), lambda i: (i, 0)),
                  _mod_spec(goff, group_tokens // tm, 1),
                  pl.BlockSpec((tm, half), lambda i: (i, 0)),
                  pl.BlockSpec((tm, half), lambda i: (i, 0)),
                  pl.BlockSpec((half, D_MODEL), lambda i: (0, 0)),
                  pl.BlockSpec((half, D_MODEL), lambda i: (1, 0))],
        out_specs=pl.BlockSpec((tm, D_MODEL), lambda i: (i, 0)),
        out_shape=jax.ShapeDtypeStruct((T, D_MODEL), F32),
        compiler_params=_params(("parallel",), 48),
        name="mixer_outproj",
    )(x, mod_l, a1, a2, w, w)


def _outproj_parity_kernel(x_ref, mod_ref, a1_ref, a2_ref, w1_ref, w2_ref, o_ref, lane_sc):
    ys = [_dot(a1_ref[p], w1_ref[...]) + _dot(a2_ref[p], w2_ref[...]) for p in range(2)]
    o_ref[...] = x_ref[...] + mod_ref[5:6, :] * _interleave_rows(lane_sc, ys[0], ys[1])


def _outproj_parity_call(x, mod_l, goff, group_tokens, a, w):
    T = x.shape[0]
    tm = min(512, group_tokens)
    half = tm // 2
    hd = D_MODEL // 2
    return pl.pallas_call(
        _outproj_parity_kernel,
        grid=(T // tm,),
        in_specs=[pl.BlockSpec((tm, D_MODEL), lambda i: (i, 0)),
                  _mod_spec(goff, group_tokens // tm, 1),
                  pl.BlockSpec((2, half, hd), lambda i: (0, i, 0)),
                  pl.BlockSpec((2, half, hd), lambda i: (0, i, 1)),
                  pl.BlockSpec((hd, D_MODEL), lambda i: (0, 0)),
                  pl.BlockSpec((hd, D_MODEL), lambda i: (1, 0))],
        out_specs=pl.BlockSpec((tm, D_MODEL), lambda i: (i, 0)),
        out_shape=jax.ShapeDtypeStruct((T, D_MODEL), F32),
        scratch_shapes=[pltpu.VMEM((D_MODEL // LANE, tm, LANE), F32)],
        compiler_params=_params(("parallel",), 48),
        name="mixer_outproj_parity",
    )(x, mod_l, a, a, w, w)


def _final_norm_kernel(x_ref, g_ref, o_ref):
    x = x_ref[...]
    o_ref[...] = (x * lax.rsqrt(jnp.mean(x * x, axis=-1, keepdims=True) + 1e-6)) * g_ref[...]


def _final_norm_call(x, gamma):
    T = x.shape[0]
    tm = 512
    return pl.pallas_call(
        _final_norm_kernel,
        grid=(T // tm,),
        in_specs=[pl.BlockSpec((tm, D_MODEL), lambda i: (i, 0)),
                  pl.BlockSpec((1, D_MODEL), lambda i: (0, 0))],
        out_specs=pl.BlockSpec((tm, D_MODEL), lambda i: (i, 0)),
        out_shape=jax.ShapeDtypeStruct((T, D_MODEL), F32),
        compiler_params=_params(("parallel",), 32),
        name="final_norm",
    )(x, gamma)


def _shift_prev(cur, halo_prev, row0, seq_len):
    tt = cur.shape[0]
    rid = lax.broadcasted_iota(jnp.int32, (tt, 1), 0)
    pos = jnp.bitwise_and(rid + row0, seq_len - 1)
    prev = pltpu.roll(cur, 1, 0)
    prev = jnp.where(rid == 0, halo_prev[SUBLANE - 1:SUBLANE, :], prev)
    return jnp.where(pos == 0, 0.0, prev)


def _shift_next(cur, halo_next, row0, seq_len):
    tt = cur.shape[0]
    rid = lax.broadcasted_iota(jnp.int32, (tt, 1), 0)
    pos = jnp.bitwise_and(rid + row0, seq_len - 1)
    nxt = pltpu.roll(cur, tt - 1, 0)
    nxt = jnp.where(rid == tt - 1, halo_next[0:1, :], nxt)
    return jnp.where(pos == seq_len - 1, 0.0, nxt)


def _shift_prev_next(cur, halo_prev, halo_next, row0, seq_len):
    return _shift_prev(cur, halo_prev, row0, seq_len), _shift_next(cur, halo_next, row0, seq_len)


def _halo_specs(tt, width, col, total_rows):
    per = tt // SUBLANE
    last = total_rows // SUBLANE - 1
    return [pl.BlockSpec((tt, width), lambda i: (i, col)),
            pl.BlockSpec((SUBLANE, width), lambda i: (jnp.maximum(i * per - 1, 0), col)),
            pl.BlockSpec((SUBLANE, width), lambda i: (jnp.minimum((i + 1) * per, last), col))]


def _segsum(x, ones_blk):
    hi = x.astype(BF16)
    lo = (x - hi.astype(F32)).astype(BF16)
    outs = []
    for g in range(x.shape[1] // MXU_DIM):
        sl = slice(g * MXU_DIM, (g + 1) * MXU_DIM)
        outs.append(_dot(hi[:, sl], ones_blk) + _dot(lo[:, sl], ones_blk))
    return jnp.concatenate(outs, axis=1)


def _rwkv_prep_kernel(z_ref, zp_ref, zn_ref, mup_ref, mun_ref, kk_ref, ka_ref, rk_ref,
                      w0_ref, w2_ref, a0_ref, a2_ref, g2_ref, ones_ref,
                      r_o, v_o, c_o, w0_o, b0_o, k0_o, w1_o, b1_o, k1_o, bonus_o, g_o,
                      *, tt, seq_len):
    row0 = pl.program_id(0) * tt
    cur = z_ref[...]
    prev, nxt = _shift_prev_next(cur, zp_ref[...], zn_ref[...], row0, seq_len)
    zs = cur + mup_ref[...] * (prev - cur) + mun_ref[...] * (nxt - cur)
    r = zs[:, 0:D_RWKV]
    k = zs[:, D_RWKV:2 * D_RWKV]
    v = zs[:, 2 * D_RWKV:3 * D_RWKV]
    small = zs[:, 3 * D_RWKV:RWKV_COLS]
    ones_blk = ones_ref[...]

    kk = k * kk_ref[...]
    kk = kk / jnp.maximum(jnp.sqrt(_segsum(kk * kk, ones_blk)), 1e-12)
    tw = jnp.tanh(small).astype(BF16)
    sg = _sigmoid(small).astype(BF16)
    xs = small.astype(BF16)
    r_o[...] = r
    v_o[...] = v
    c_o[...] = -kk
    g_o[...] = _dot(sg, g2_ref[...])

    bonus = jnp.zeros_like(r)
    outs = ((w0_o, b0_o, k0_o), (w1_o, b1_o, k1_o))
    for d in range(2):
        wl = -_softplus(-(w0_ref[d:d + 1, :] + _dot(tw, w2_ref[d]))) - 0.5
        a = _sigmoid(a0_ref[d:d + 1, :] + _dot(xs, a2_ref[d]))
        kd = k * (1.0 + (a - 1.0) * ka_ref[...])
        w_o, b_o, k_o = outs[d]
        w_o[...] = jnp.exp(-jnp.exp(wl))
        b_o[...] = kk * a
        k_o[...] = kd
        bonus = bonus + _segsum(r * kd * rk_ref[...], ones_blk) * v
    bonus_o[...] = bonus


def _rwkv_prep_call(z_r, seq_len, p):
    T = z_r.shape[0]
    tt = min(256, seq_len)
    row = lambda n: pl.BlockSpec((1, n), lambda i: (0, 0))
    full2 = lambda a, b: pl.BlockSpec((a, b), lambda i: (0, 0))
    full3 = lambda a, b, c: pl.BlockSpec((a, b, c), lambda i: (0, 0, 0))
    in_specs = _halo_specs(tt, RWKV_COLS, 0, T) + [
        row(RWKV_COLS), row(RWKV_COLS), row(D_RWKV), row(D_RWKV), row(D_RWKV),
        full2(2, D_RWKV), full3(2, RWKV_SMALL, D_RWKV), full2(2, D_RWKV), full3(2, RWKV_SMALL, D_RWKV),
        full2(RWKV_SMALL, D_RWKV), full2(MXU_DIM, MXU_DIM)]
    out_spec = pl.BlockSpec((tt, D_RWKV), lambda i: (i, 0))
    out_shape = jax.ShapeDtypeStruct((T, D_RWKV), F32)
    return pl.pallas_call(
        functools.partial(_rwkv_prep_kernel, tt=tt, seq_len=seq_len),
        grid=(T // tt,),
        in_specs=in_specs,
        out_specs=[out_spec] * 11,
        out_shape=[out_shape] * 11,
        compiler_params=_params(("parallel",), 52),
        name="rwkv_prep",
    )(z_r, z_r, z_r, p['mu_prev'], p['mu_next'], p['k_k'], p['k_a'], p['r_k'],
      p['w0'], p['w2'], p['a0'], p['a2'], p['g2'], p['ones_blk'])


N_GRP = D_RWKV // MXU_DIM
HEADS_PER_GRP = H_RWKV // N_GRP


SCAN_ROWS = 16
SCAN_UNROLL = 8


def _scan_head_mask():
    return (jnp.arange(H_RWKV)[:, None] == (jnp.arange(D_RWKV) // RWKV_HEAD)[None, :]).astype(F32)


def _look_kernel(c_ref, cp_ref, cn_ref, v_ref, w0_ref, b0_ref, k0_ref, w1_ref, b1_ref, k1_ref,
                 ones_ref, sel_ref, wc0_o, vk0_o, be0_o, wc1_o, vk1_o, be1_o, *, tt, seq_len):
    row0 = pl.program_id(0) * tt
    c = c_ref[...]
    c_prev, c_next = _shift_prev_next(c, cp_ref[...], cn_ref[...], row0, seq_len)
    ones_blk = ones_ref[...]
    sel = sel_ref[...]
    v = v_ref[...]
    for cn, (w_ref, b_ref, k_ref, wc_o, vk_o, be_o) in (
            (c_next, (w0_ref, b0_ref, k0_ref, wc0_o, vk0_o, be0_o)),
            (c_prev, (w1_ref, b1_ref, k1_ref, wc1_o, vk1_o, be1_o))):
        wc_o[...] = w_ref[...] * cn
        vk_o[...] = v * _segsum(k_ref[...] * cn, ones_blk)
        bc = b_ref[...] * cn
        hi = bc.astype(BF16)
        lo = (bc - hi.astype(F32)).astype(BF16)
        be_o[...] = _dot(hi, sel) + _dot(lo, sel)


def _look_call(pre, seq_len, ones_blk):
    T = pre['c'].shape[0]
    tt = min(256, seq_len)
    blk = pl.BlockSpec((tt, D_RWKV), lambda i: (i, 0))
    sel = (jnp.arange(D_RWKV)[:, None] // RWKV_HEAD == jnp.arange(LANE)[None, :]).astype(BF16)
    big = jax.ShapeDtypeStruct((T, D_RWKV), F32)
    small = jax.ShapeDtypeStruct((T, LANE), F32)
    sblk = pl.BlockSpec((tt, LANE), lambda i: (i, 0))
    outs = pl.pallas_call(
        functools.partial(_look_kernel, tt=tt, seq_len=seq_len),
        grid=(T // tt,),
        in_specs=_halo_specs(tt, D_RWKV, 0, T) + [blk] * 7 + [
            pl.BlockSpec((MXU_DIM, MXU_DIM), lambda i: (0, 0)),
            pl.BlockSpec((D_RWKV, LANE), lambda i: (0, 0))],
        out_specs=[blk, blk, sblk, blk, blk, sblk],
        out_shape=[big, big, small, big, big, small],
        compiler_params=_params(("parallel",), 48),
        name="rwkv_lookahead",
    )(pre['c'], pre['c'], pre['c'], pre['v'], pre['w0'], pre['b0'], pre['k0'],
      pre['w1'], pre['b1'], pre['k1'], ones_blk, sel)
    return outs[0:3], outs[3:6]


def _scan_kernel(rf, wcf, wf, bf, kf, vf, vhf, vkf, bef, rb, wcb, wb, bb, kb, vb, vhb, vkb, beb,
                 c0_ref, v0_ref, s0_ref, eye_ref, hm_ref, y0_ref, y1_ref, sfin_ref, st, uv, *, tc):
    j = pl.program_id(1)
    eye = eye_ref[...]
    hmask = hm_ref[...]
    dirs = ((rf, wcf, wf, bf, kf, (vf, vhf), vkf, bef, y0_ref),
            (rb, wcb, wb, bb, kb, (vb, vhb), vkb, beb, y1_ref))
    nt = (((1,), (1,)), ((), ()))
    zrows = jnp.zeros((SCAN_ROWS, D_RWKV), BF16)
    zeye = jnp.zeros((SCAN_ROWS, MXU_DIM), BF16)
    seqs = s0_ref.shape[0]
    chains = [(b, d) for b in range(seqs) for d in range(2)]

    def head_rows(row):
        return (row * hmask).astype(BF16)

    def value_rows(row):
        x = row * hmask
        folded = x[:, 0:MXU_DIM]
        for g in range(1, N_GRP):
            folded = folded + x[:, g * MXU_DIM:(g + 1) * MXU_DIM]
        return folded.astype(BF16)

    def next_value_row(v_refs, b, d, t):
        v_, vh_ = v_refs
        if d == 0:
            inside = v_[b, pl.ds(jnp.minimum(t + 1, tc - 1), 1), :]
            return jnp.where(t == tc - 1, vh_[b, 0:1, :], inside)
        inside = v_[b, pl.ds(jnp.maximum(t - 1, 0), 1), :]
        return jnp.where(t == 0, vh_[b, SUBLANE - 1:SUBLANE, :], inside)

    def state_products(b, d, rows0, eye0, eye1, rows2):
        lhs = jnp.concatenate([st[b, d, g].astype(BF16) for g in range(N_GRP)] + [eye], axis=1)
        wr = jnp.concatenate([
            jnp.concatenate([rows0, eye0], axis=1),
            jnp.concatenate([zrows, eye1], axis=1),
            jnp.concatenate([rows2, zeye], axis=1),
            jnp.concatenate([zrows, zeye], axis=1)], axis=0)
        return lax.dot_general(lhs, wr, nt, preferred_element_type=F32)

    def store_y(y_, b, t, prod):
        tr = prod.T
        y_[b, pl.ds(t, 1), :, :] = tr[2 * SCAN_ROWS:3 * SCAN_ROWS, :].reshape(1, H_RWKV, RWKV_HEAD)

    def time_of(d, s):
        s = jnp.clip(s, 0, tc - 1)
        return s if d == 0 else tc - 1 - s

    @pl.when(j == 0)
    def _():
        st[...] = s0_ref[...]
        for (b, d) in chains:
            uv[b, d] = state_products(b, d, head_rows(c0_ref[b, d:d + 1, :]), zeye,
                                      value_rows(v0_ref[b, d:d + 1, :]), zrows)

    def issue(group, s):
        out = []
        for (b, d) in group:
            r_, wc_, w_, b_, k_, vn_, vk_, be_, y_ = dirs[d]
            t, tp = time_of(d, s), time_of(d, s - 1)
            prod = state_products(b, d, head_rows(wc_[b, pl.ds(t, 1), :]),
                                  value_rows(vk_[b, pl.ds(t, 1), :]),
                                  value_rows(next_value_row(vn_, b, d, t)),
                                  head_rows(r_[b, pl.ds(tp, 1), :]))
            w2 = jnp.concatenate([head_rows(b_[b, pl.ds(t, 1), :]), head_rows(k_[b, pl.ds(t, 1), :]),
                                  zrows, zrows], axis=0)
            out.append((prod, _dot(uv[b, d].astype(BF16), w2)))
        return out

    def finish(group, s, results):
        for (b, d), (prod, upd) in zip(group, results):
            r_, wc_, w_, b_, k_, vn_, vk_, be_, y_ = dirs[d]
            t, tp = time_of(d, s), time_of(d, s - 1)
            wrow = w_[b, pl.ds(t, 1), :]
            for g in range(N_GRP):
                sl = slice(g * MXU_DIM, (g + 1) * MXU_DIM)
                st[b, d, g] = st[b, d, g] * wrow[:, sl] + upd[:, sl]
            store_y(y_, b, tp, prod)
            uv[b, d] = prod + uv[b, d] * be_[b, pl.ds(t, 1), 0:4 * SCAN_ROWS]

    lead, lag = chains[:seqs], chains[seqs:]

    def body(i, carry):
        for u in range(SCAN_UNROLL):
            s = i * SCAN_UNROLL + u
            lead_results = issue(lead, s)
            lag_results = issue(lag, s)
            finish(lead, s, lead_results)
            finish(lag, s, lag_results)
        return carry

    lax.fori_loop(0, tc // SCAN_UNROLL, body, 0)

    for (b, d) in chains:
        t = time_of(d, tc - 1)
        prod = state_products(b, d, zrows, zeye, zeye, head_rows(dirs[d][0][b, pl.ds(t, 1), :]))
        store_y(dirs[d][8], b, t, prod)

    @pl.when(j == pl.num_programs(1) - 1)
    def _():
        sfin_ref[...] = st[...]


def _scan_call(pre, s0, eye, ones_blk):
    B, T, _ = pre['r'].shape
    nb = 4 if B % 4 == 0 else 2
    tc = min(128 if nb == 2 else 32, T)
    nj = T // tc
    fwd = pl.BlockSpec((nb, tc, D_RWKV), lambda bi, j: (bi, j, 0))
    bwd = pl.BlockSpec((nb, tc, D_RWKV), lambda bi, j: (bi, nj - 1 - j, 0))
    yfwd = pl.BlockSpec((nb, tc, H_RWKV, RWKV_HEAD), lambda bi, j: (bi, j, 0, 0))
    ybwd = pl.BlockSpec((nb, tc, H_RWKV, RWKV_HEAD), lambda bi, j: (bi, nj - 1 - j, 0, 0))
    st_spec = pl.BlockSpec((nb, 2, N_GRP, RWKV_HEAD, MXU_DIM), lambda bi, j: (bi, 0, 0, 0, 0))
    y_shape = jax.ShapeDtypeStruct((B, T, H_RWKV, RWKV_HEAD), F32)
    befwd = pl.BlockSpec((nb, tc, LANE), lambda bi, j: (bi, j, 0))
    bebwd = pl.BlockSpec((nb, tc, LANE), lambda bi, j: (bi, nj - 1 - j, 0))
    per = tc // SUBLANE
    last8 = T // SUBLANE - 1
    hfwd = pl.BlockSpec((nb, SUBLANE, D_RWKV), lambda bi, j: (bi, jnp.minimum((j + 1) * per, last8), 0))
    hbwd = pl.BlockSpec((nb, SUBLANE, D_RWKV), lambda bi, j: (bi, jnp.maximum((nj - 1 - j) * per - 1, 0), 0))
    first = pl.BlockSpec((nb, SUBLANE, D_RWKV), lambda bi, j: (bi, 0, 0))

    flat = {n: a.reshape(B * T, D_RWKV) for n, a in pre.items()}
    (wc0, vk0, be0), (wc1, vk1, be1) = [
        (wc.reshape(B, T, D_RWKV), vk.reshape(B, T, D_RWKV), be.reshape(B, T, LANE))
        for (wc, vk, be) in _look_call(flat, T, ones_blk)]
    v = pre['v']
    pad = jnp.zeros((B, SUBLANE - 2, D_RWKV), F32)
    c_first = jnp.concatenate([pre['c'][:, 0:1], pre['c'][:, T - 1:T], pad], axis=1)
    v_first = jnp.concatenate([v[:, 0:1], v[:, T - 1:T], pad], axis=1)
    return pl.pallas_call(
        functools.partial(_scan_kernel, tc=tc),
        grid=(B // nb, nj),
        in_specs=[fwd] * 6 + [hfwd, fwd, befwd] + [bwd] * 6 + [hbwd, bwd, bebwd] + [
            first, first,
            st_spec,
            pl.BlockSpec((RWKV_HEAD, MXU_DIM), lambda bi, j: (0, 0)),
            pl.BlockSpec((H_RWKV, D_RWKV), lambda bi, j: (0, 0))],
        out_specs=[yfwd, ybwd, st_spec],
        out_shape=[y_shape, y_shape, jax.ShapeDtypeStruct((B, 2, N_GRP, RWKV_HEAD, MXU_DIM), F32)],
        scratch_shapes=[pltpu.VMEM((nb, 2, N_GRP, RWKV_HEAD, MXU_DIM), F32),
                        pltpu.VMEM((nb, 2, RWKV_HEAD, 4 * SCAN_ROWS), F32)],
        compiler_params=_params(("arbitrary", "arbitrary"), 52),
        name="rwkv_scan",
    )(pre['r'], wc0, pre['w0'], pre['b0'], pre['k0'], v, v, vk0, be0,
      pre['r'], wc1, pre['w1'], pre['b1'], pre['k1'], v, v, vk1, be1,
      c_first, v_first, s0, eye.astype(BF16), _scan_head_mask())


def _rwkv_post_kernel(y0_ref, y1_ref, bonus_ref, g_ref, gw_ref, gb_ref, ones_ref, o_ref):
    ones_blk = ones_ref[...]
    y = y0_ref[...] + y1_ref[...]
    mu = _segsum(y, ones_blk) * (1.0 / RWKV_HEAD)
    yc = y - mu
    var = _segsum(yc * yc, ones_blk) * (1.0 / RWKV_HEAD)
    yn = yc * lax.rsqrt(var + RWKV_GN_EPS)
    out = (yn * gw_ref[...] + gb_ref[...] + bonus_ref[...]) * g_ref[...]
    o_ref[...] = out.astype(BF16)


def _rwkv_post_call(y0, y1, bonus, g, gn_w, gn_b, ones_blk):
    T = y0.shape[0]
    tt = 512
    blk = pl.BlockSpec((tt, D_RWKV), lambda i: (i, 0))
    row = pl.BlockSpec((1, D_RWKV), lambda i: (0, 0))
    return pl.pallas_call(
        _rwkv_post_kernel,
        grid=(T // tt,),
        in_specs=[blk, blk, blk, blk, row, row, pl.BlockSpec((MXU_DIM, MXU_DIM), lambda i: (0, 0))],
        out_specs=blk,
        out_shape=jax.ShapeDtypeStruct((T, D_RWKV), BF16),
        compiler_params=_params(("parallel",), 40),
        name="rwkv_post",
    )(y0, y1, bonus, g, gn_w, gn_b, ones_blk)


def _rope128(x, cos_t, sin_t):
    return x * cos_t + pltpu.roll(x, QK_ROPE, 1) * sin_t


def _pack_kv(kv, kr_rot, k_o, v_o):
    for h in range(H_MLA):
        k_o[:, h * HEAD_SLOT:h * HEAD_SLOT + QK_NOPE] = kv[:, h * HEAD_SLOT:h * HEAD_SLOT + QK_NOPE].astype(BF16)
        k_o[:, h * HEAD_SLOT + QK_NOPE:(h + 1) * HEAD_SLOT] = kr_rot.astype(BF16)
        v_o[:, h * V_HEAD:(h + 1) * V_HEAD] = kv[:, h * HEAD_SLOT + QK_NOPE:(h + 1) * HEAD_SLOT].astype(BF16)


def _mla_prep_kernel(z_ref, cos_ref, sin_ref, qn_ref, kvn_ref, wq_ref, wkv_ref,
                     q_o, k_o, v_o, ckv_o, kr_o):
    z = z_ref[...]
    cq = z[:, 0:Q_RANK]
    ckv = z[:, Q_RANK:Q_RANK + KV_RANK]
    krp = z[:, Q_RANK + KV_RANK:MLA_COLS]
    cos_t = cos_ref[...]
    sin_t = sin_ref[...]
    cq = (cq * lax.rsqrt(jnp.mean(cq * cq, axis=-1, keepdims=True) + 1e-6)) * qn_ref[...]
    ckv = (ckv * lax.rsqrt(jnp.mean(ckv * ckv, axis=-1, keepdims=True) + 1e-6)) * kvn_ref[...]
    ckv_o[...] = ckv
    kr_o[...] = krp
    q = _dot(cq.astype(BF16), wq_ref[...]) * (ATTN_SCALE * LOG2_E)
    for h in range(H_MLA):
        q_o[:, h * HEAD_SLOT:h * HEAD_SLOT + QK_NOPE] = q[:, h * HEAD_SLOT:h * HEAD_SLOT + QK_NOPE].astype(BF16)
        q_o[:, h * HEAD_SLOT + QK_NOPE:(h + 1) * HEAD_SLOT] = _rope128(
            q[:, h * HEAD_SLOT + QK_NOPE:(h + 1) * HEAD_SLOT], cos_t, sin_t).astype(BF16)
    kv = _dot(ckv.astype(BF16), wkv_ref[...])
    _pack_kv(kv, _rope128(krp, cos_t, sin_t), k_o, v_o)


def _mla_prep_call(z_m, cos_t, sin_t, seq_len, p):
    T = z_m.shape[0]
    tm = min(512, seq_len)
    per_seq = seq_len // tm
    row = lambda n: pl.BlockSpec((1, n), lambda i: (0, 0))
    blk = lambda n: pl.BlockSpec((tm, n), lambda i: (i, 0))
    tab = pl.BlockSpec((tm, LANE), lambda i: (i % per_seq, 0))
    return pl.pallas_call(
        _mla_prep_kernel,
        grid=(T // tm,),
        in_specs=[pl.BlockSpec((tm, EVEN_TN), lambda i: (i, RWKV_COLS // EVEN_TN)), tab, tab,
                  row(Q_RANK), row(KV_RANK),
                  pl.BlockSpec((Q_RANK, H_MLA * HEAD_SLOT), lambda i: (0, 0)),
                  pl.BlockSpec((KV_RANK, H_MLA * HEAD_SLOT), lambda i: (0, 0))],
        out_specs=[blk(H_MLA * HEAD_SLOT), blk(H_MLA * HEAD_SLOT), blk(H_MLA * V_HEAD),
                   blk(KV_RANK), blk(LANE)],
        out_shape=[jax.ShapeDtypeStruct((T, H_MLA * HEAD_SLOT), BF16),
                   jax.ShapeDtypeStruct((T, H_MLA * HEAD_SLOT), BF16),
                   jax.ShapeDtypeStruct((T, H_MLA * V_HEAD), BF16),
                   jax.ShapeDtypeStruct((T, KV_RANK), F32),
                   jax.ShapeDtypeStruct((T, LANE), F32)],
        compiler_params=_params(("parallel",), 48),
        name="mla_prep",
    )(z_m, cos_t, sin_t, p['q_norm'], p['kv_norm'], p['w_qb'], p['w_kvb'])


def _ctx_kv_kernel(ckv_ref, kr_ref, wkv_ref, k_o, v_o):
    kv = _dot(ckv_ref[...].astype(BF16), wkv_ref[...])
    _pack_kv(kv, kr_ref[...], k_o, v_o)


def _ctx_kv_call(ckv_ctx, kr_ctx_pad, w_kvb):
    T = ckv_ctx.shape[0]
    tm = min(512, T)
    blk = lambda n: pl.BlockSpec((tm, n), lambda i: (i, 0))
    return pl.pallas_call(
        _ctx_kv_kernel,
        grid=(T // tm,),
        in_specs=[blk(KV_RANK), blk(LANE), pl.BlockSpec((KV_RANK, H_MLA * HEAD_SLOT), lambda i: (0, 0))],
        out_specs=[blk(H_MLA * HEAD_SLOT), blk(H_MLA * V_HEAD)],
        out_shape=[jax.ShapeDtypeStruct((T, H_MLA * HEAD_SLOT), BF16),
                   jax.ShapeDtypeStruct((T, H_MLA * V_HEAD), BF16)],
        compiler_params=_params(("parallel",), 32),
        name="mla_ctx_kv",
    )(ckv_ctx, kr_ctx_pad, w_kvb)


ATTN_ROWS = 64


def _attn_kernel(q_ref, k_ref, v_ref, o_ref, s_sc, p_sc, *, kc):
    tq, tk = s_sc.shape
    hq = tq // 2
    nt = (((1,), (1,)), ((), ()))

    def scores(h):
        q = q_ref[h * hq:(h + 1) * hq, :]
        for c in range(tk // kc):
            s_sc[h * hq:(h + 1) * hq, c * kc:(c + 1) * kc] = lax.dot_general(
                q, k_ref[c * kc:(c + 1) * kc, :], nt, preferred_element_type=F32)

    def softmax(h):
        sums = []
        for r in range(hq // ATTN_ROWS):
            rows = slice(h * hq + r * ATTN_ROWS, h * hq + (r + 1) * ATTN_ROWS)
            mpart = s_sc[rows, 0:LANE]
            for t in range(1, tk // LANE):
                mpart = jnp.maximum(mpart, s_sc[rows, t * LANE:(t + 1) * LANE])
            m = jnp.max(mpart, axis=-1, keepdims=True)
            lpart = jnp.zeros((ATTN_ROWS, LANE), F32)
            for t in range(tk // LANE):
                p = jnp.exp2(s_sc[rows, t * LANE:(t + 1) * LANE] - m)
                lpart = lpart + p
                p_sc[rows, t * LANE:(t + 1) * LANE] = p.astype(BF16)
            sums.append(jnp.sum(lpart, axis=-1, keepdims=True))
        return jnp.concatenate(sums, axis=0)

    def weighted_values(h, l):
        acc = jnp.zeros((hq, V_HEAD), F32)
        for c in range(tk // kc):
            acc = acc + _dot(p_sc[h * hq:(h + 1) * hq, c * kc:(c + 1) * kc], v_ref[c * kc:(c + 1) * kc, :])
        o_ref[h * hq:(h + 1) * hq, :] = (acc / l).astype(BF16)

    scores(0)
    scores(1)
    l0 = softmax(0)
    weighted_values(0, l0)
    l1 = softmax(1)
    weighted_values(1, l1)


def _attn_call(q, k, v):
    B, Tq, _ = q.shape
    Tk = k.shape[1]
    tq = min(512, Tq)
    return pl.pallas_call(
        functools.partial(_attn_kernel, kc=MXU_DIM),
        grid=(B, H_MLA, Tq // tq),
        in_specs=[pl.BlockSpec((None, tq, HEAD_SLOT), lambda b, h, i: (b, i, h)),
                  pl.BlockSpec((None, Tk, HEAD_SLOT), lambda b, h, i: (b, 0, h)),
                  pl.BlockSpec((None, Tk, V_HEAD), lambda b, h, i: (b, 0, h))],
        out_specs=pl.BlockSpec((None, tq, V_HEAD), lambda b, h, i: (b, i, h)),
        out_shape=jax.ShapeDtypeStruct((B, Tq, H_MLA * V_HEAD), BF16),
        scratch_shapes=[pltpu.VMEM((tq, Tk), F32), pltpu.VMEM((tq, Tk), BF16)],
        compiler_params=_params(("parallel", "parallel", "arbitrary"), 48),
        name="mla_attention",
    )(q, k, v)


def _conv3_kernel(*refs, tt, half_len):
    ins, (cw_refs, cb_refs), outs = refs[0:12], (refs[12:15], refs[15:18]), refs[18:]
    row0 = pl.program_id(0) * tt
    for s in range(3):
        even, odd = ins[4 * s][...], ins[4 * s + 1][...]
        odd_prev = _shift_prev(odd, ins[4 * s + 2][...], row0, half_len)
        even_next = _shift_next(even, ins[4 * s + 3][...], row0, half_len)
        cw = cw_refs[s][...]
        bias = cb_refs[s][...]
        y_even = cw[0:1, :] * odd_prev + cw[1:2, :] * even + cw[2:3, :] * odd + bias
        y_odd = cw[0:1, :] * even + cw[1:2, :] * odd + cw[2:3, :] * even_next + bias
        outs[s][0] = y_even
        outs[s][1] = y_odd
        if s == 2:
            outs[3][0] = y_even.astype(BF16)
            outs[3][1] = y_odd.astype(BF16)


def _conv3_call(z, half_len, conv_w, conv_b):
    T2 = z.shape[1]
    tt = min(128, half_len)
    per = tt // SUBLANE
    last = T2 // SUBLANE - 1
    in_specs = []
    for s in range(3):
        in_specs += [
            pl.BlockSpec((None, tt, D_HYENA), lambda i, s=s: (0, i, s)),
            pl.BlockSpec((None, tt, D_HYENA), lambda i, s=s: (1, i, s)),
            pl.BlockSpec((None, SUBLANE, D_HYENA), lambda i, s=s: (1, jnp.maximum(i * per - 1, 0), s)),
            pl.BlockSpec((None, SUBLANE, D_HYENA), lambda i, s=s: (0, jnp.minimum((i + 1) * per, last), s))]
    in_specs += [pl.BlockSpec((3, D_HYENA), lambda i, s=s: (0, s)) for s in range(3)]
    in_specs += [pl.BlockSpec((1, D_HYENA), lambda i, s=s: (0, s)) for s in range(3)]
    blk = pl.BlockSpec((2, tt, D_HYENA), lambda i: (0, i, 0))
    f32s = jax.ShapeDtypeStruct((2, T2, D_HYENA), F32)
    return pl.pallas_call(
        functools.partial(_conv3_kernel, tt=tt, half_len=half_len),
        grid=(T2 // tt,),
        in_specs=in_specs,
        out_specs=[blk] * 4,
        out_shape=[f32s, f32s, f32s, jax.ShapeDtypeStruct((2, T2, D_HYENA), BF16)],
        compiler_params=_params(("parallel",), 48),
        name="hyena_conv3",
    )(*([z] * 12), conv_w, conv_w, conv_w, conv_b, conv_b, conv_b)


def _filt_mlp_kernel(z_ref, w1_ref, b1_ref, w2_ref, b2_ref, fr_ref, o_ref):
    h = jnp.sin(fr_ref[0:1, :] * (_dot(z_ref[...].astype(BF16), w1_ref[...].astype(BF16)) + b1_ref[...]))
    h = jnp.sin(fr_ref[1:2, :] * (_dot(h.astype(BF16), w2_ref[...].astype(BF16)) + b2_ref[...]))
    o_ref[...] = h.astype(BF16)


def _filt_mlp_call(zpos, w1p, b1, w2, b2, freq):
    L = zpos.shape[0]
    return pl.pallas_call(
        _filt_mlp_kernel,
        out_shape=jax.ShapeDtypeStruct((L, FILT_HIDDEN), BF16),
        compiler_params=pltpu.CompilerParams(vmem_limit_bytes=32 * MIB),
        name="hyena_filter_mlp",
    )(zpos, w1p, b1, w2, b2, freq)


def _filt_gen_kernel(h_ref, tn_ref, dl_ref, w00, w01, w10, w11, o_ref, taps_sc):
    h = h_ref[...]
    L = h.shape[0]
    win = jnp.exp(-tn_ref[...] * dl_ref[...])
    not_first = lax.broadcasted_iota(jnp.int32, (L, 1), 0) > 0
    ws = ((w00, w01), (w10, w11))

    def emit(k, taps):
        taps_sc[...] = taps
        for par in range(2):
            o_ref[k, par] = taps_sc[pl.ds(par, L // 2, stride=2), :].astype(BF16)

    for n in range(2):
        causal = _dot(h, ws[n][0][...].astype(BF16)) * win
        anti = jnp.where(not_first, _dot(h, ws[n][1][...].astype(BF16)) * win, 0.0)
        norm = (jnp.sum(jnp.abs(causal), axis=0, keepdims=True)
                + jnp.sum(jnp.abs(anti), axis=0, keepdims=True))
        emit(2 * n, causal / norm)
        emit(2 * n + 1, anti / norm)


def _filt_gen_call(h2, tnorm, deltas, w3):
    L = h2.shape[0]
    tc = 128
    nc = D_HYENA // tc
    wspec = lambda k: pl.BlockSpec((FILT_HIDDEN, tc), lambda j, k=k: (0, k * nc + j))
    return pl.pallas_call(
        _filt_gen_kernel,
        grid=(nc,),
        in_specs=[pl.BlockSpec((L, FILT_HIDDEN), lambda j: (0, 0)),
                  pl.BlockSpec((L, 1), lambda j: (0, 0)),
                  pl.BlockSpec((1, tc), lambda j: (0, j)),
                  wspec(0), wspec(1), wspec(2), wspec(3)],
        out_specs=pl.BlockSpec((4, 2, L // 2, tc), lambda j: (0, 0, 0, j)),
        out_shape=jax.ShapeDtypeStruct((4, 2, L // 2, D_HYENA), BF16),
        scratch_shapes=[pltpu.VMEM((L, tc), F32)],
        compiler_params=_params(("parallel",), 48),
        name="hyena_filter_gen",
    )(h2, tnorm, deltas, w3, w3, w3, w3)


def _dft_fwd_kernel(f_ref, u_ref, o_ref):
    o_ref[...] = _dot(f_ref[...], u_ref[...])


def _dft_fwd_call(fmat, u):
    B, K, C = u.shape
    M = fmat.shape[0]
    tm = min(512, M)
    tn = min(C, 2048)
    return pl.pallas_call(
        _dft_fwd_kernel,
        grid=(B, C // tn, M // tm),
        in_specs=[pl.BlockSpec((tm, K), lambda b, j, i: (i, 0)),
                  pl.BlockSpec((None, K, tn), lambda b, j, i: (b, 0, j))],
        out_specs=pl.BlockSpec((None, tm, tn), lambda b, j, i: (b, i, j)),
        out_shape=jax.ShapeDtypeStruct((B, M, C), F32),
        compiler_params=_params(("parallel", "parallel", "arbitrary"), 40),
        name="hyena_dft_fwd",
    )(fmat, u)


def _butterfly(gc, gs, hc, hs, tw_c, tw_s, first):
    tc = hc * tw_c - hs * tw_s
    ts = hs * tw_c + hc * tw_s
    p0 = gc + tc
    p2 = gc - tc
    p1 = jnp.where(first, gs, gs + ts)
    p3 = jnp.where(first, hs, ts - gs)
    return p0, p1, p2, p3


def _cmul(ac, a_s, bc, bs):
    return ac * bc - a_s * bs, ac * bs + a_s * bc


def _filt_planes_kernel(ge_ref, ho_ref, twc_ref, tws_ref, o_ref, *, tr):
    first = (lax.broadcasted_iota(jnp.int32, (tr, 1), 0) + pl.program_id(1) * tr) == 0
    tw_c, tw_s = twc_ref[...], tws_ref[...]
    a = _butterfly(ge_ref[0, 0], ge_ref[0, 1], ho_ref[0, 0], ho_ref[0, 1], tw_c, tw_s, first)
    b = _butterfly(ge_ref[1, 0], ge_ref[1, 1], ho_ref[1, 0], ho_ref[1, 1], tw_c, tw_s, first)
    o_ref[0] = a[0] + b[0]
    o_ref[1] = jnp.where(first, a[1] + b[1], a[1] - b[1])
    o_ref[2] = a[2] + b[2]
    o_ref[3] = a[3] - b[3]


def _filt_planes_call(raw, twc, tws):
    _, L, C = raw.shape
    H = L // 2
    tr = min(256, H)
    tc = 1024 if H >= 256 else 2048
    blk = lambda par: pl.BlockSpec((None, 2, None, 2, tr, tc), lambda n, i, j: (n, 0, par, 0, i, j))
    tw = pl.BlockSpec((tr, 1), lambda n, i, j: (i, 0))
    raw6 = raw.reshape(2, 2, 2, 2, H, C)
    return pl.pallas_call(
        functools.partial(_filt_planes_kernel, tr=tr),
        grid=(2, H // tr, C // tc),
        in_specs=[blk(0), blk(1), tw, tw],
        out_specs=pl.BlockSpec((None, 4, tr, tc), lambda n, i, j: (n, 0, i, j)),
        out_shape=jax.ShapeDtypeStruct((2, 4, H, C), F32),
        compiler_params=_params(("parallel", "parallel", "parallel"), 40),
        name="hyena_filter_planes",
    )(raw6, raw6, twc, tws)


def _spec_mul_kernel(raw_ref, k_ref, twc_ref, tws_ref, o_ref, *, tr):
    first = (lax.broadcasted_iota(jnp.int32, (tr, 1), 0) + pl.program_id(1) * tr) == 0
    tw_c, tw_s = twc_ref[...], tws_ref[...]
    p0, p1, p2, p3 = _butterfly(raw_ref[0, 0], raw_ref[0, 1], raw_ref[1, 0], raw_ref[1, 1], tw_c, tw_s, first)
    k0, k1, k2, k3 = k_ref[0], k_ref[1], k_ref[2], k_ref[3]
    yac, yas = _cmul(p0, p1, k0, k1)
    ybc, ybs = _cmul(p2, p3, k2, k3)
    ymc, yms = _cmul(p1, p3, k1, k3)
    yac = jnp.where(first, p0 * k0, yac)
    ybc = jnp.where(first, p2 * k2, ybc)
    dc = yac - ybc
    ds = yas + ybs
    o_ref[0, 0] = (yac + ybc).astype(BF16)
    o_ref[0, 1] = jnp.where(first, ymc, yas - ybs).astype(BF16)
    o_ref[1, 0] = jnp.where(first, dc, dc * tw_c + ds * tw_s).astype(BF16)
    o_ref[1, 1] = jnp.where(first, yms, ds * tw_c - dc * tw_s).astype(BF16)


def _spec_mul_call(raw, k_planes, order, twc, tws):
    B2, L, C = raw.shape
    B, H = B2 // 2, L // 2
    tr = min(256, H)
    tc = 1024 if H >= 256 else 2048
    tw = pl.BlockSpec((tr, 1), lambda b, i, j: (i, 0))
    out = pl.pallas_call(
        functools.partial(_spec_mul_kernel, tr=tr),
        grid=(B, H // tr, C // tc),
        in_specs=[pl.BlockSpec((2, None, 2, tr, tc), lambda b, i, j: (0, b, 0, i, j)),
                  pl.BlockSpec((None, 4, tr, tc), lambda b, i, j: (order, 0, i, j)),
                  tw, tw],
        out_specs=pl.BlockSpec((2, None, 2, tr, tc), lambda b, i, j: (0, b, 0, i, j)),
        out_shape=jax.ShapeDtypeStruct((2, B, 2, H, C), BF16),
        compiler_params=_params(("parallel", "parallel", "parallel"), 40),
        name="hyena_spectral_mul",
    )(raw.reshape(2, B, 2, H, C), k_planes, twc, tws)
    return out.reshape(B2, L, C)


def _dft_inv_kernel(f_ref, y_ref, gate_ref, u_ref, bias_ref, o_ref, ob_ref):
    conv = _dot(f_ref[...], y_ref[...])
    out = gate_ref[...] * (conv + u_ref[...] * bias_ref[...])
    o_ref[...] = out
    ob_ref[...] = out.astype(BF16)


def _dft_inv_call(imat, y_spec, gate, u, bias):
    B, M, C = u.shape
    K = imat.shape[1]
    tm = min(512, M)
    tn = 1024 if K > 1024 else min(C, 2048)
    blk = pl.BlockSpec((None, tm, tn), lambda b, j, i: (b, i, j))
    return pl.pallas_call(
        _dft_inv_kernel,
        grid=(B, C // tn, M // tm),
        in_specs=[pl.BlockSpec((tm, K), lambda b, j, i: (i, 0)),
                  pl.BlockSpec((None, K, tn), lambda b, j, i: (b, 0, j)),
                  blk, blk, pl.BlockSpec((1, tn), lambda b, j, i: (0, j))],
        out_specs=[blk, blk],
        out_shape=[jax.ShapeDtypeStruct((B, M, C), F32), jax.ShapeDtypeStruct((B, M, C), BF16)],
        compiler_params=_params(("parallel", "parallel", "arbitrary"), 48),
        name="hyena_dft_inv",
    )(imat, y_spec, gate, u, bias)


def _dft_tables(L):
    H = L // 2
    lo = min(64, H)
    hi = H // lo
    g = jnp.arange(H, dtype=jnp.int32)
    theta = 2.0 * math.pi / L
    ang_hi = ((g[:, None] * (jnp.arange(hi, dtype=jnp.int32) * lo)[None, :]) % L).astype(F32) * theta
    ang_lo = ((g[:, None] * jnp.arange(lo, dtype=jnp.int32)[None, :]) % L).astype(F32) * theta
    ch, sh, cl, sl = jnp.cos(ang_hi), jnp.sin(ang_hi), jnp.cos(ang_lo), jnp.sin(ang_lo)
    cos_m = (ch[:, :, None] * cl[:, None, :] - sh[:, :, None] * sl[:, None, :]).reshape(H, H)
    sin_m = (sh[:, :, None] * cl[:, None, :] + ch[:, :, None] * sl[:, None, :]).reshape(H, H)
    alt = jnp.where(jnp.arange(H) % 2 == 0, 1.0, -1.0).astype(F32)
    sin_m = jnp.where((g == 0)[:, None], alt[None, :], sin_m)
    fwd = jnp.concatenate([cos_m, sin_m], axis=0)
    n = 2.0 * L
    w_cos = jnp.where(g == 0, 1.0 / n, 2.0 / n).astype(F32)
    inv = jnp.concatenate([cos_m * w_cos[:, None], sin_m * (2.0 / n)], axis=0).T
    ang_tw = g.astype(F32) * (math.pi / L)
    return fwd.astype(BF16), inv.astype(BF16), jnp.cos(ang_tw)[:, None], jnp.sin(ang_tw)[:, None]


def _filter_positions(L):
    t = jnp.arange(L, dtype=F32)
    t_norm = t / max(L - 1, 1)
    bands = (POS_EMB - 1) // 2
    freqs = jnp.linspace(1e-4, bands - 1, bands, dtype=F32)
    ang = (2.0 * math.pi / L) * t[:, None] * freqs[None, :]
    z = jnp.concatenate([t_norm[:, None], jnp.cos(ang), -jnp.sin(ang)], axis=-1)
    return jnp.pad(z, ((0, 0), (0, POS_PAD - POS_EMB))), t_norm[:, None]


def _hyena_deltas():
    return jnp.linspace(abs(math.log(HYENA_TARGET)) / SLOW_DECAY_PCT,
                        abs(math.log(HYENA_TARGET)) / FAST_DECAY_PCT, D_HYENA, dtype=F32)[None, :]


def _hyena_filter_spectrum(L, tables, p):
    fwd, _, twc, tws = tables
    zpos, tnorm = _filter_positions(L)
    h2 = _filt_mlp_call(zpos, p['filt_w1'], p['filt_b1'], p['filt_w2'], p['filt_b2'], p['filt_freq'])
    filt = _filt_gen_call(h2, tnorm, _hyena_deltas(), p['filt_w3'])
    raw = _dft_fwd_call(fwd, filt.reshape(8, L // 2, D_HYENA))
    return _filt_planes_call(raw, twc, tws)


def _hyena_mixer(z, B, L, tables, k_planes, p):
    fwd, inv, twc, tws = tables
    H = L // 2
    x1, x2, v, vb = _conv3_call(z, H, p['conv_w'], p['conv_b'])
    shp = (2 * B, H, D_HYENA)
    u, ub = v.reshape(shp), vb.reshape(shp)
    for n, gate in enumerate((x1, x2)):
        raw = _dft_fwd_call(fwd, ub)
        yspec = _spec_mul_call(raw, k_planes, n, twc, tws)
        u, ub = _dft_inv_call(inv, yspec, gate.reshape(shp), u, p['bias'][n:n + 1])
    return ub.reshape(2, B * H, D_HYENA)


def _rope_swap(w):
    q = QK_ROPE // 4
    return jnp.concatenate([w[..., q:2 * q], w[..., 0:q], w[..., 3 * q:4 * q], w[..., 2 * q:3 * q]], axis=-1)


def _pad_cols(w, n):
    return jnp.pad(w, [(0, 0)] * (w.ndim - 1) + [(0, n - w.shape[-1])])


def _pack_even(e, w_in_even, mu_prev, mu_next, rwkv_w0, rwkv_w2, rwkv_a0, rwkv_a2, rwkv_g2,
               rwkv_kk, rwkv_ka, rwkv_rk, rwkv_gn_w, rwkv_gn_b, mla_q_norm, mla_kv_norm,
               mla_w_qb, mla_w_kvb, w_out_even):
    n_r = 3 * D_RWKV + W_LORA + A_LORA + G_LORA
    w_in = w_in_even[e]
    w_r = _pad_cols(w_in[:, :n_r], RWKV_COLS).astype(BF16)
    w_m = w_in[:, n_r:]
    kr_cols = w_m[:, Q_RANK + KV_RANK:]
    w_m = jnp.concatenate([w_m, _rope_swap(kr_cols)], axis=-1).astype(BF16)
    small_rows = lambda w, off: jnp.pad(w, [(0, 0)] * (w.ndim - 2)
                                        + [(off, RWKV_SMALL - off - w.shape[-2]), (0, 0)]).astype(BF16)
    wq = mla_w_qb[e].reshape(Q_RANK, H_MLA, QK_NOPE + QK_ROPE)
    wq = jnp.concatenate([wq, _rope_swap(wq[..., QK_NOPE:])], axis=-1).reshape(Q_RANK, H_MLA * HEAD_SLOT)
    blk = jnp.arange(MXU_DIM) // RWKV_HEAD
    return {
        'w_in': jnp.concatenate([w_r, _pad_cols(w_m, EVEN_TN)], axis=-1),
        'mu_prev': _pad_cols(mu_prev[e][None, :], RWKV_COLS),
        'mu_next': _pad_cols(mu_next[e][None, :], RWKV_COLS),
        'k_k': rwkv_kk[e][None, :], 'k_a': rwkv_ka[e][None, :],
        'r_k': rwkv_rk[e].reshape(1, D_RWKV),
        'w0': rwkv_w0[e], 'w2': small_rows(rwkv_w2[e], 0),
        'a0': rwkv_a0[e], 'a2': small_rows(rwkv_a2[e], W_LORA),
        'g2': small_rows(rwkv_g2[e], W_LORA + A_LORA),
        'gn_w': rwkv_gn_w[e][None, :], 'gn_b': rwkv_gn_b[e][None, :],
        'q_norm': mla_q_norm[e][None, :], 'kv_norm': mla_kv_norm[e][None, :],
        'w_qb': wq.astype(BF16), 'w_kvb': mla_w_kvb[e].astype(BF16),
        'w_out': w_out_even[e].astype(BF16),
        'ones_blk': (blk[:, None] == blk[None, :]).astype(BF16),
        'eye': (jnp.arange(RWKV_HEAD)[:, None] == (jnp.arange(MXU_DIM) % RWKV_HEAD)[None, :]).astype(F32),
    }


def _rope_tables(L):
    rows = L // GRID_W
    row = jnp.repeat(jnp.arange(rows, dtype=F32), GRID_W)
    col = jnp.tile(jnp.arange(GRID_W, dtype=F32), rows)
    half = QK_ROPE // 2
    inv = 1.0 / (ROPE_THETA ** (jnp.arange(0, half, 2, dtype=F32) / half))
    ar, ac = row[:, None] * inv[None, :], col[:, None] * inv[None, :]
    cos_t = jnp.concatenate([jnp.cos(ar), jnp.cos(ar), jnp.cos(ac), jnp.cos(ac)], axis=-1)
    sin_t = jnp.concatenate([-jnp.sin(ar), jnp.sin(ar), -jnp.sin(ac), jnp.sin(ac)], axis=-1)
    return _pad_cols(cos_t, LANE), _pad_cols(sin_t, LANE)


def _state_to_groups(s):
    B = s.shape[0]
    s = s.reshape(B, 2, N_GRP, H_RWKV // N_GRP, RWKV_HEAD, RWKV_HEAD)
    return jnp.swapaxes(s, 3, 4).reshape(B, 2, N_GRP, RWKV_HEAD, MXU_DIM)


def _groups_to_state(s):
    B = s.shape[0]
    s = s.reshape(B, 2, N_GRP, RWKV_HEAD, H_RWKV // N_GRP, RWKV_HEAD)
    return jnp.swapaxes(s, 3, 4).reshape(B, 2, H_RWKV, RWKV_HEAD, RWKV_HEAD)


def _even_mixer(x, mod_l, goff, B, L, gamma, p, rope, ctx):
    group_tokens = x.shape[0] if ctx is None else L
    z_r = z_m = _inproj_call(x, mod_l, goff, group_tokens, gamma, p['w_in'], EVEN_TN)
    names = ('r', 'v', 'c', 'w0', 'b0', 'k0', 'w1', 'b1', 'k1', 'bonus', 'g')
    pre = dict(zip(names, _rwkv_prep_call(z_r, L, p)))
    seq = {n: pre[n].reshape(B, L, D_RWKV) for n in names[:9]}
    if ctx is None:
        s0 = jnp.zeros((B, 2, N_GRP, RWKV_HEAD, MXU_DIM), F32)
    else:
        s0 = _state_to_groups(ctx[2].astype(F32))
    y0, y1, s_fin = _scan_call(seq, s0, p['eye'], p['ones_blk'])
    y_r = _rwkv_post_call(y0.reshape(B * L, D_RWKV), y1.reshape(B * L, D_RWKV), pre['bonus'], pre['g'],
                          p['gn_w'], p['gn_b'], p['ones_blk'])

    q, k, v, ckv, krp = _mla_prep_call(z_m, rope[0], rope[1], L, p)
    q = q.reshape(B, L, H_MLA * HEAD_SLOT)
    k = k.reshape(B, L, H_MLA * HEAD_SLOT)
    v = v.reshape(B, L, H_MLA * V_HEAD)
    if ctx is not None:
        P = ctx[0].shape[1]
        k_ctx, v_ctx = _ctx_kv_call(ctx[0].reshape(B * P, KV_RANK),
                                    _pad_cols(ctx[1].reshape(B * P, QK_ROPE), LANE), p['w_kvb'])
        k = jnp.concatenate([k, k_ctx.reshape(B, P, H_MLA * HEAD_SLOT)], axis=1)
        v = jnp.concatenate([v, v_ctx.reshape(B, P, H_MLA * V_HEAD)], axis=1)
    y_m = _attn_call(q, k, v).reshape(B * L, H_MLA * V_HEAD)
    x = _outproj_call(x, mod_l, goff, group_tokens, y_r, y_m, p['w_out'])
    state = (_groups_to_state(s_fin), ckv.reshape(B, L, KV_RANK), krp[:, :QK_ROPE].reshape(B, L, QK_ROPE))
    return x, state


def _odd_mixer(x, mod_l, goff, group_tokens, B, L, gamma, tables, k_planes, p):
    z = _inproj_parity_call(x, mod_l, goff, group_tokens, gamma, p['w_in'], 1536)
    y = _hyena_mixer(z, B, L, tables, k_planes, p)
    return _outproj_parity_call(x, mod_l, goff, group_tokens, y, p['w_out'])


def kernel(x_prompt, x_sample, cache_mla_ckv, cache_mla_krope, state_rwkv, c, c_ctx,
           w_mod, b_mod, norm_g, w_ffn_in, w_ffn_out, final_norm_g,
           w_in_even, mu_prev, mu_next, rwkv_w0, rwkv_w2, rwkv_a0, rwkv_a2, rwkv_g2,
           rwkv_kk, rwkv_ka, rwkv_rk, rwkv_gn_w, rwkv_gn_b,
           mla_q_norm, mla_kv_norm, mla_w_qb, mla_w_kvb, w_out_even,
           w_in_odd, hy_conv_w, hy_conv_b, hy_filt_w1, hy_filt_b1, hy_filt_w2, hy_filt_b2,
           hy_filt_w3, hy_filt_freq, hy_bias, w_out_odd):
    Bp, Lp, D = x_prompt.shape
    Bs, Ls, _ = x_sample.shape
    depth = w_mod.shape[0]
    xp = x_prompt.reshape(Bp * Lp, D)
    xs = x_sample.reshape(Bs * Ls, D)
    Tp = Bp * Lp

    cvec = jnp.concatenate([c_ctx[None, :], c, jnp.zeros((SUBLANE - 1 - Bs, D), F32)], axis=0)
    mod = _mod_call(cvec, w_mod, b_mod)

    rope_p = (_pad_cols(jnp.ones((Lp, QK_ROPE), F32), LANE), jnp.zeros((Lp, LANE), F32))
    rope_s = _rope_tables(Ls)
    tabs_p = tabs_s = None
    w_in = w_ffn_in.astype(BF16)
    w_out = w_ffn_out.astype(BF16)
    new_ckv, new_kr, new_s = [], [], []
    for l in range(depth):
        mod_l = mod[l]
        gam = [norm_g[l, s][None, :] for s in range(3)]
        xp = _ffn_call(xp, mod_l, 0, Tp, gam[0], w_in, w_out, l, 0, 0)
        xs = _ffn_call(xs, mod_l, 1, Ls, gam[0], w_in, w_out, l, 0, 0)
        if l % 2 == 0:
            e = l // 2
            p = _pack_even(e, w_in_even, mu_prev, mu_next, rwkv_w0, rwkv_w2, rwkv_a0, rwkv_a2, rwkv_g2,
                           rwkv_kk, rwkv_ka, rwkv_rk, rwkv_gn_w, rwkv_gn_b, mla_q_norm, mla_kv_norm,
                           mla_w_qb, mla_w_kvb, w_out_even)
            ctx = (cache_mla_ckv[:, e], cache_mla_krope[:, e], state_rwkv[:, e])
            xp, st = _even_mixer(xp, mod_l, 0, Bp, Lp, gam[1], p, rope_p, None)
            xs, _ = _even_mixer(xs, mod_l, 1, Bs, Ls, gam[1], p, rope_s, ctx)
            new_s.append(st[0].astype(x_prompt.dtype))
            new_ckv.append(st[1])
            new_kr.append(st[2])
        else:
            o = l // 2
            p = {'w_in': w_in_odd[o].astype(BF16), 'conv_w': hy_conv_w[o], 'conv_b': hy_conv_b[o][None, :],
                 'filt_w1': jnp.pad(hy_filt_w1[o], ((0, POS_PAD - POS_EMB), (0, 0))),
                 'filt_b1': hy_filt_b1[o][None, :], 'filt_w2': hy_filt_w2[o],
                 'filt_b2': hy_filt_b2[o][None, :], 'filt_w3': hy_filt_w3[o],
                 'filt_freq': hy_filt_freq[o], 'bias': hy_bias[o], 'w_out': w_out_odd[o].astype(BF16)}
            if tabs_p is None:
                tabs_p, tabs_s = _dft_tables(Lp), _dft_tables(Ls)
            ks_p = _hyena_filter_spectrum(Lp, tabs_p, p)
            ks_s = _hyena_filter_spectrum(Ls, tabs_s, p)
            xp = _odd_mixer(xp, mod_l, 0, Tp, Bp, Lp, gam[1], tabs_p, ks_p, p)
            xs = _odd_mixer(xs, mod_l, 1, Ls, Bs, Ls, gam[1], tabs_s, ks_s, p)
        xp = _ffn_call(xp, mod_l, 0, Tp, gam[2], w_in, w_out, l, 1, 2)
        xs = _ffn_call(xs, mod_l, 1, Ls, gam[2], w_in, w_out, l, 1, 2)

    fg = final_norm_g[None, :]
    y_prompt = _final_norm_call(xp, fg).reshape(Bp, Lp, D)
    y_sample = _final_norm_call(xs, fg).reshape(Bs, Ls, D)
    return (y_prompt, y_sample, jnp.stack(new_ckv, axis=1), jnp.stack(new_kr, axis=1),
            jnp.stack(new_s, axis=1))
```

```python
import functools
import math

import jax
import jax.numpy as jnp
from jax import lax
from jax.experimental import pallas as pl
from jax.experimental.pallas import tpu as pltpu

F32 = jnp.float32
BF16 = jnp.bfloat16

D_MODEL = 2048
N_MOD = 9
D_FF = 5632
D_RWKV = 1024
RWKV_HEAD = 64
H_RWKV = 16
W_LORA = 64
A_LORA = 64
G_LORA = 160
RWKV_SMALL = 384
RWKV_COLS = 3 * D_RWKV + RWKV_SMALL
RWKV_GN_EPS = 64e-5
H_MLA = 8
QK_NOPE = 128
QK_ROPE = 64
V_HEAD = 128
Q_RANK = 512
KV_RANK = 256
MLA_COLS = Q_RANK + KV_RANK + 2 * QK_ROPE
HEAD_SLOT = 256
ROPE_THETA = 10000.0
ATTN_SCALE = (QK_NOPE + QK_ROPE) ** -0.5
LOG2_E = 1.0 / math.log(2.0)
GRID_W = 64
D_HYENA = 2048
POS_EMB = 33
POS_PAD = 128
FILT_HIDDEN = 64
HYENA_TARGET = 1e-2
FAST_DECAY_PCT = 0.3
SLOW_DECAY_PCT = 1.5
LANE = 128
SUBLANE = 8
MXU_DIM = 256
MIB = 1024 * 1024


def _params(sem, vmem_mib):
    return pltpu.CompilerParams(dimension_semantics=sem, vmem_limit_bytes=vmem_mib * MIB)


def _sigmoid(x):
    return 1.0 / (1.0 + jnp.exp(-x))


def _softplus(x):
    return jnp.maximum(x, 0.0) + jnp.log(1.0 + jnp.exp(-jnp.abs(x)))


def _dot(a, b):
    return jnp.dot(a, b, preferred_element_type=F32)


def _norm_mod(x, gamma, shift, scale):
    xn = x * lax.rsqrt(jnp.mean(x * x, axis=-1, keepdims=True) + 1e-6)
    return (xn * gamma) * (1.0 + scale) + shift


def _mod_kernel(c_ref, w_ref, b_ref, o_ref):
    c = c_ref[...]
    s = c * _sigmoid(c)
    o_ref[0] = _dot(s.astype(BF16), w_ref[0].astype(BF16)) + b_ref[0]


def _mod_call(cvec, w_mod, b_mod):
    L, Dm, N = w_mod.shape
    tn = 1024
    out = pl.pallas_call(
        _mod_kernel,
        grid=(L, N // tn),
        in_specs=[pl.BlockSpec((SUBLANE, Dm), lambda l, j: (0, 0)),
                  pl.BlockSpec((1, Dm, tn), lambda l, j: (l, 0, j)),
                  pl.BlockSpec((1, 1, tn), lambda l, j: (l, 0, j))],
        out_specs=pl.BlockSpec((1, SUBLANE, tn), lambda l, j: (l, 0, j)),
        out_shape=jax.ShapeDtypeStruct((L, SUBLANE, N), F32),
        compiler_params=_params(("arbitrary", "arbitrary"), 40),
        name="adaln_mod",
    )(cvec, w_mod, b_mod.reshape(L, 1, N))
    return out.reshape(L, SUBLANE, N_MOD, Dm)


def _mod_spec(goff, tiles_per_group, nargs):
    if nargs == 1:
        return pl.BlockSpec((None, N_MOD, D_MODEL), lambda i: (goff + i // tiles_per_group, 0, 0))
    return pl.BlockSpec((None, N_MOD, D_MODEL), lambda i, j: (goff + i // tiles_per_group, 0, 0))


def _ffn_kernel(x_ref, mod_ref, g_ref, wg_ref, wu_ref, wo_ref, o_ref, h_sc, acc_sc, *, sub):
    f = pl.program_id(1)

    @pl.when(f == 0)
    def _():
        h = _norm_mod(x_ref[...], g_ref[...], mod_ref[3 * sub:3 * sub + 1, :],
                      mod_ref[3 * sub + 1:3 * sub + 2, :])
        h_sc[...] = h.astype(BF16)
        acc_sc[...] = jnp.zeros_like(acc_sc)

    h = h_sc[...]
    a = _dot(h, wg_ref[...])
    u = _dot(h, wu_ref[...])
    act = (a * _sigmoid(a)) * u
    acc_sc[...] += _dot(act.astype(BF16), wo_ref[...])

    @pl.when(f == pl.num_programs(1) - 1)
    def _():
        o_ref[...] = x_ref[...] + 0.5 * mod_ref[3 * sub + 2:3 * sub + 3, :] * acc_sc[...]


def _ffn_call(x, mod_l, goff, group_tokens, gamma, w_in, w_out, l, s, sub):
    T = x.shape[0]
    tm = min(512, group_tokens)
    tf = 512
    nf = D_FF // tf
    return pl.pallas_call(
        functools.partial(_ffn_kernel, sub=sub),
        grid=(T // tm, nf),
        in_specs=[pl.BlockSpec((tm, D_MODEL), lambda i, f: (i, 0)),
                  _mod_spec(goff, group_tokens // tm, 2),
                  pl.BlockSpec((1, D_MODEL), lambda i, f: (0, 0)),
                  pl.BlockSpec((None, None, D_MODEL, tf), lambda i, f: (l, s, 0, f)),
                  pl.BlockSpec((None, None, D_MODEL, tf), lambda i, f: (l, s, 0, f + nf)),
                  pl.BlockSpec((None, None, tf, D_MODEL), lambda i, f: (l, s, f, 0))],
        out_specs=pl.BlockSpec((tm, D_MODEL), lambda i, f: (i, 0)),
        out_shape=jax.ShapeDtypeStruct((T, D_MODEL), F32),
        scratch_shapes=[pltpu.VMEM((tm, D_MODEL), BF16), pltpu.VMEM((tm, D_MODEL), F32)],
        compiler_params=_params(("parallel", "arbitrary"), 52),
        name="ffn_swiglu",
    )(x, mod_l, gamma, w_in, w_in, w_out)


def _inproj_kernel(x_ref, mod_ref, g_ref, w_ref, o_ref, h_sc, *, col_axis):
    @pl.when(pl.program_id(col_axis) == 0)
    def _():
        h = _norm_mod(x_ref[...], g_ref[...], mod_ref[3:4, :], mod_ref[4:5, :])
        h_sc[...] = h.astype(BF16)

    o_ref[...] = _dot(h_sc[...], w_ref[...])


def _inproj_call(x, mod_l, goff, group_tokens, gamma, w, tn):
    T = x.shape[0]
    N = w.shape[1]
    tm = min(512, group_tokens)
    return pl.pallas_call(
        functools.partial(_inproj_kernel, col_axis=1),
        grid=(T // tm, N // tn),
        in_specs=[pl.BlockSpec((tm, D_MODEL), lambda i, j: (i, 0)),
                  _mod_spec(goff, group_tokens // tm, 2),
                  pl.BlockSpec((1, D_MODEL), lambda i, j: (0, 0)),
                  pl.BlockSpec((D_MODEL, tn), lambda i, j: (0, j))],
        out_specs=pl.BlockSpec((tm, tn), lambda i, j: (i, j)),
        out_shape=jax.ShapeDtypeStruct((T, N), F32),
        scratch_shapes=[pltpu.VMEM((tm, D_MODEL), BF16)],
        compiler_params=_params(("parallel", "arbitrary"), 48),
        name="mixer_inproj",
    )(x, mod_l, gamma, w)


def _rows_of_parity(lane_sc, x, par):
    rows = x.shape[0]
    parts = []
    for c in range(x.shape[1] // LANE):
        lane_sc[c] = x[:, c * LANE:(c + 1) * LANE]
        parts.append(lane_sc[c, pl.ds(par, rows // 2, stride=2), :])
    return jnp.concatenate(parts, axis=1)


def _interleave_rows(lane_sc, even, odd):
    half = even.shape[0]
    parts = []
    for c in range(even.shape[1] // LANE):
        lane_sc[c, pl.ds(0, half, stride=2), :] = even[:, c * LANE:(c + 1) * LANE]
        lane_sc[c, pl.ds(1, half, stride=2), :] = odd[:, c * LANE:(c + 1) * LANE]
        parts.append(lane_sc[c])
    return jnp.concatenate(parts, axis=1)


def _inproj_parity_kernel(x_ref, mod_ref, g_ref, w_ref, o_ref, h_sc, lane_sc, *, half):
    @pl.when(pl.program_id(1) == 0)
    def _():
        h = _norm_mod(x_ref[...], g_ref[...], mod_ref[3:4, :], mod_ref[4:5, :])
        for p in range(2):
            h_sc[p * half:(p + 1) * half, :] = _rows_of_parity(lane_sc, h, p).astype(BF16)

    z = _dot(h_sc[...], w_ref[...])
    o_ref[0] = z[:half]
    o_ref[1] = z[half:]


def _inproj_parity_call(x, mod_l, goff, group_tokens, gamma, w, tn):
    T = x.shape[0]
    N = w.shape[1]
    tm = min(512, group_tokens)
    half = tm // 2
    return pl.pallas_call(
        functools.partial(_inproj_parity_kernel, half=half),
        grid=(T // tm, N // tn),
        in_specs=[pl.BlockSpec((tm, D_MODEL), lambda i, j: (i, 0)),
                  _mod_spec(goff, group_tokens // tm, 2),
                  pl.BlockSpec((1, D_MODEL), lambda i, j: (0, 0)),
                  pl.BlockSpec((D_MODEL, tn), lambda i, j: (0, j))],
        out_specs=pl.BlockSpec((2, half, tn), lambda i, j: (0, i, j)),
        out_shape=jax.ShapeDtypeStruct((2, T // 2, N), F32),
        scratch_shapes=[pltpu.VMEM((tm, D_MODEL), BF16), pltpu.VMEM((D_MODEL // LANE, tm, LANE), F32)],
        compiler_params=_params(("parallel", "arbitrary"), 48),
        name="mixer_inproj_parity",
    )(x, mod_l, gamma, w)


def _outproj_kernel(x_ref, mod_ref, a1_ref, a2_ref, w1_ref, w2_ref, o_ref):
    y = _dot(a1_ref[...], w1_ref[...]) + _dot(a2_ref[...], w2_ref[...])
    o_ref[...] = x_ref[...] + mod_ref[5:6, :] * y


def _outproj_call(x, mod_l, goff, group_tokens, a1, a2, w):
    T = x.shape[0]
    tm = min(512, group_tokens)
    half = D_MODEL // 2
    return pl.pallas_call(
        _outproj_kernel,
        grid=(T // tm,),
        in_specs=[pl.BlockSpec((tm, D_MODEL), lambda i: (i, 0)),
                  _mod_spec(goff, group_tokens // tm, 1),
                  pl.BlockSpec((tm, half), lambda i: (i, 0)),
                  pl.BlockSpec((tm, half), lambda i: (i, 0)),
                  pl.BlockSpec((half, D_MODEL), lambda i: (0, 0)),
                  pl.BlockSpec((half, D_MODEL), lambda i: (1, 0))],
        out_specs=pl.BlockSpec((tm, D_MODEL), lambda i: (i, 0)),
        out_shape=jax.ShapeDtypeStruct((T, D_MODEL), F32),
        compiler_params=_params(("parallel",), 48),
        name="mixer_outproj",
    )(x, mod_l, a1, a2, w, w)


def _outproj_parity_kernel(x_ref, mod_ref, a1_ref, a2_ref, w1_ref, w2_ref, o_ref, lane_sc):
    ys = [_dot(a1_ref[p], w1_ref[...]) + _dot(a2_ref[p], w2_ref[...]) for p in range(2)]
    o_ref[...] = x_ref[...] + mod_ref[5:6, :] * _interleave_rows(lane_sc, ys[0], ys[1])


def _outproj_parity_call(x, mod_l, goff, group_tokens, a, w):
    T = x.shape[0]
    tm = min(512, group_tokens)
    half = tm // 2
    hd = D_MODEL // 2
    return pl.pallas_call(
        _outproj_parity_kernel,
        grid=(T // tm,),
        in_specs=[pl.BlockSpec((tm, D_MODEL), lambda i: (i, 0)),
                  _mod_spec(goff, group_tokens // tm, 1),
                  pl.BlockSpec((2, half, hd), lambda i: (0, i, 0)),
                  pl.BlockSpec((2, half, hd), lambda i: (0, i, 1)),
                  pl.BlockSpec((hd, D_MODEL), lambda i: (0, 0)),
                  pl.BlockSpec((hd, D_MODEL), lambda i: (1, 0))],
        out_specs=pl.BlockSpec((tm, D_MODEL), lambda i: (i, 0)),
        out_shape=jax.ShapeDtypeStruct((T, D_MODEL), F32),
        scratch_shapes=[pltpu.VMEM((D_MODEL // LANE, tm, LANE), F32)],
        compiler_params=_params(("parallel",), 48),
        name="mixer_outproj_parity",
    )(x, mod_l, a, a, w, w)


def _final_norm_kernel(x_ref, g_ref, o_ref):
    x = x_ref[...]
    o_ref[...] = (x * lax.rsqrt(jnp.mean(x * x, axis=-1, keepdims=True) + 1e-6)) * g_ref[...]


def _final_norm_call(x, gamma):
    T = x.shape[0]
    tm = 512
    return pl.pallas_call(
        _final_norm_kernel,
        grid=(T // tm,),
        in_specs=[pl.BlockSpec((tm, D_MODEL), lambda i: (i, 0)),
                  pl.BlockSpec((1, D_MODEL), lambda i: (0, 0))],
        out_specs=pl.BlockSpec((tm, D_MODEL), lambda i: (i, 0)),
        out_shape=jax.ShapeDtypeStruct((T, D_MODEL), F32),
        compiler_params=_params(("parallel",), 32),
        name="final_norm",
    )(x, gamma)


def _shift_prev(cur, halo_prev, row0, seq_len):
    tt = cur.shape[0]
    rid = lax.broadcasted_iota(jnp.int32, (tt, 1), 0)
    pos = jnp.bitwise_and(rid + row0, seq_len - 1)
    prev = pltpu.roll(cur, 1, 0)
    prev = jnp.where(rid == 0, halo_prev[SUBLANE - 1:SUBLANE, :], prev)
    return jnp.where(pos == 0, 0.0, prev)


def _shift_next(cur, halo_next, row0, seq_len):
    tt = cur.shape[0]
    rid = lax.broadcasted_iota(jnp.int32, (tt, 1), 0)
    pos = jnp.bitwise_and(rid + row0, seq_len - 1)
    nxt = pltpu.roll(cur, tt - 1, 0)
    nxt = jnp.where(rid == tt - 1, halo_next[0:1, :], nxt)
    return jnp.where(pos == seq_len - 1, 0.0, nxt)


def _shift_prev_next(cur, halo_prev, halo_next, row0, seq_len):
    return _shift_prev(cur, halo_prev, row0, seq_len), _shift_next(cur, halo_next, row0, seq_len)


def _halo_specs(tt, width, col, total_rows):
    per = tt // SUBLANE
    last = total_rows // SUBLANE - 1
    return [pl.BlockSpec((tt, width), lambda i: (i, col)),
            pl.BlockSpec((SUBLANE, width), lambda i: (jnp.maximum(i * per - 1, 0), col)),
            pl.BlockSpec((SUBLANE, width), lambda i: (jnp.minimum((i + 1) * per, last), col))]


def _segsum(x, ones_blk):
    hi = x.astype(BF16)
    lo = (x - hi.astype(F32)).astype(BF16)
    outs = []
    for g in range(x.shape[1] // MXU_DIM):
        sl = slice(g * MXU_DIM, (g + 1) * MXU_DIM)
        outs.append(_dot(hi[:, sl], ones_blk) + _dot(lo[:, sl], ones_blk))
    return jnp.concatenate(outs, axis=1)


def _rwkv_prep_kernel(z_ref, zp_ref, zn_ref, mup_ref, mun_ref, kk_ref, ka_ref, rk_ref,
                      w0_ref, w2_ref, a0_ref, a2_ref, g2_ref, ones_ref, sel_ref,
                      r_o, v_o, c_o, w0_o, b0_o, k0_o, wc0_o, vk0_o, be0_o,
                      w1_o, b1_o, k1_o, wc1_o, vk1_o, be1_o, bonus_o, g_o, *, tt, seq_len):
    row0 = pl.program_id(0) * tt
    cur = z_ref[...]
    halo_p, halo_n = zp_ref[...], zn_ref[...]
    prev, nxt = _shift_prev_next(cur, halo_p, halo_n, row0, seq_len)
    mup, mun = mup_ref[...], mun_ref[...]
    zs = cur + mup * (prev - cur) + mun * (nxt - cur)
    r = zs[:, 0:D_RWKV]
    k = zs[:, D_RWKV:2 * D_RWKV]
    v = zs[:, 2 * D_RWKV:3 * D_RWKV]
    small = zs[:, 3 * D_RWKV:RWKV_COLS]
    ones_blk = ones_ref[...]

    def unit_keys(keys):
        kk = keys * kk_ref[...]
        return kk / jnp.maximum(jnp.sqrt(_segsum(kk * kk, ones_blk)), 1e-12)

    kk = unit_keys(k)
    c = -kk
    ks = slice(D_RWKV, 2 * D_RWKV)
    hp, hn, mp, mn = halo_p[:, ks], halo_n[:, ks], mup[:, ks], mun[:, ks]
    zc = cur[:, ks]
    k_before = hp[7:8] + mp * (hp[6:7] - hp[7:8]) + mn * (zc[0:1] - hp[7:8])
    k_after = hn[0:1] + mp * (zc[tt - 1:tt] - hn[0:1]) + mn * (hn[1:2] - hn[0:1])
    edge = -unit_keys(jnp.concatenate([k_before, k_after, jnp.zeros((SUBLANE - 2, D_RWKV), F32)], axis=0))
    c_prev = _shift_prev(c, jnp.broadcast_to(edge[0:1], (SUBLANE, D_RWKV)), row0, seq_len)
    c_next = _shift_next(c, jnp.broadcast_to(edge[1:2], (SUBLANE, D_RWKV)), row0, seq_len)

    tw = jnp.tanh(small).astype(BF16)
    sg = _sigmoid(small).astype(BF16)
    xs = small.astype(BF16)
    r_o[...] = r
    v_o[...] = v
    c_o[...] = c
    g_o[...] = _dot(sg, g2_ref[...])

    sel = sel_ref[...]
    bonus = jnp.zeros_like(r)
    outs = ((w0_o, b0_o, k0_o, wc0_o, vk0_o, be0_o, c_next), (w1_o, b1_o, k1_o, wc1_o, vk1_o, be1_o, c_prev))
    for d in range(2):
        wl = -_softplus(-(w0_ref[d:d + 1, :] + _dot(tw, w2_ref[d]))) - 0.5
        a = _sigmoid(a0_ref[d:d + 1, :] + _dot(xs, a2_ref[d]))
        kd = k * (1.0 + (a - 1.0) * ka_ref[...])
        w_o, b_o, k_o, wc_o, vk_o, be_o, c_after = outs[d]
        decay = jnp.exp(-jnp.exp(wl))
        b = kk * a
        w_o[...] = decay
        b_o[...] = b
        k_o[...] = kd
        wc_o[...] = decay * c_after
        vk_o[...] = v * _segsum(kd * c_after, ones_blk)
        bc = b * c_after
        hi = bc.astype(BF16)
        lo = (bc - hi.astype(F32)).astype(BF16)
        be_o[...] = _dot(hi, sel) + _dot(lo, sel)
        bonus = bonus + _segsum(r * kd * rk_ref[...], ones_blk) * v
    bonus_o[...] = bonus


def _rwkv_prep_call(z_r, seq_len, p):
    T = z_r.shape[0]
    tt = min(128, seq_len)
    row = lambda n: pl.BlockSpec((1, n), lambda i: (0, 0))
    full2 = lambda a, b: pl.BlockSpec((a, b), lambda i: (0, 0))
    full3 = lambda a, b, c: pl.BlockSpec((a, b, c), lambda i: (0, 0, 0))
    in_specs = _halo_specs(tt, RWKV_COLS, 0, T) + [
        row(RWKV_COLS), row(RWKV_COLS), row(D_RWKV), row(D_RWKV), row(D_RWKV),
        full2(2, D_RWKV), full3(2, RWKV_SMALL, D_RWKV), full2(2, D_RWKV), full3(2, RWKV_SMALL, D_RWKV),
        full2(RWKV_SMALL, D_RWKV), full2(MXU_DIM, MXU_DIM), full2(D_RWKV, LANE)]
    wide = (pl.BlockSpec((tt, D_RWKV), lambda i: (i, 0)), jax.ShapeDtypeStruct((T, D_RWKV), F32))
    slim = (pl.BlockSpec((tt, LANE), lambda i: (i, 0)), jax.ShapeDtypeStruct((T, LANE), F32))
    outs = [wide] * 3 + ([wide] * 5 + [slim]) * 2 + [wide] * 2
    sel = (jnp.arange(D_RWKV)[:, None] // RWKV_HEAD == jnp.arange(LANE)[None, :]).astype(BF16)
    return pl.pallas_call(
        functools.partial(_rwkv_prep_kernel, tt=tt, seq_len=seq_len),
        grid=(T // tt,),
        in_specs=in_specs,
        out_specs=[o[0] for o in outs],
        out_shape=[o[1] for o in outs],
        compiler_params=_params(("parallel",), 56),
        name="rwkv_prep",
    )(z_r, z_r, z_r, p['mu_prev'], p['mu_next'], p['k_k'], p['k_a'], p['r_k'],
      p['w0'], p['w2'], p['a0'], p['a2'], p['g2'], p['ones_blk'], sel)


N_GRP = D_RWKV // MXU_DIM
HEADS_PER_GRP = H_RWKV // N_GRP


SCAN_ROWS = 16
SCAN_UNROLL = 8


def _scan_head_mask():
    return (jnp.arange(H_RWKV)[:, None] == (jnp.arange(D_RWKV) // RWKV_HEAD)[None, :]).astype(F32)


def _scan_kernel(rf, wcf, wf, bf, kf, vf, vhf, vkf, bef, rb, wcb, wb, bb, kb, vb, vhb, vkb, beb,
                 c0_ref, v0_ref, s0_ref, eye_ref, hm_ref, y0_ref, y1_ref, sfin_ref, st, uv, *, tc):
    j = pl.program_id(1)
    eye = eye_ref[...]
    hmask = hm_ref[...]
    dirs = ((rf, wcf, wf, bf, kf, (vf, vhf), vkf, bef, y0_ref),
            (rb, wcb, wb, bb, kb, (vb, vhb), vkb, beb, y1_ref))
    nt = (((1,), (1,)), ((), ()))
    zrows = jnp.zeros((SCAN_ROWS, D_RWKV), BF16)
    zeye = jnp.zeros((SCAN_ROWS, MXU_DIM), BF16)
    seqs = s0_ref.shape[0]
    chains = [(b, d) for b in range(seqs) for d in range(2)]

    def head_rows(row):
        return (row * hmask).astype(BF16)

    def value_rows(row):
        x = row * hmask
        folded = x[:, 0:MXU_DIM]
        for g in range(1, N_GRP):
            folded = folded + x[:, g * MXU_DIM:(g + 1) * MXU_DIM]
        return folded.astype(BF16)

    def next_value_row(v_refs, b, d, t):
        v_, vh_ = v_refs
        if d == 0:
            inside = v_[b, pl.ds(jnp.minimum(t + 1, tc - 1), 1), :]
            return jnp.where(t == tc - 1, vh_[b, 0:1, :], inside)
        inside = v_[b, pl.ds(jnp.maximum(t - 1, 0), 1), :]
        return jnp.where(t == 0, vh_[b, SUBLANE - 1:SUBLANE, :], inside)

    def state_products(b, d, rows0, eye0, eye1, rows2):
        lhs = jnp.concatenate([st[b, d, g].astype(BF16) for g in range(N_GRP)] + [eye], axis=1)
        wr = jnp.concatenate([
            jnp.concatenate([rows0, eye0], axis=1),
            jnp.concatenate([zrows, eye1], axis=1),
            jnp.concatenate([rows2, zeye], axis=1),
            jnp.concatenate([zrows, zeye], axis=1)], axis=0)
        return lax.dot_general(lhs, wr, nt, preferred_element_type=F32)

    def store_y(y_, b, t, prod):
        tr = prod.T
        y_[b, pl.ds(t, 1), :, :] = tr[2 * SCAN_ROWS:3 * SCAN_ROWS, :].reshape(1, H_RWKV, RWKV_HEAD)

    def time_of(d, s):
        s = jnp.clip(s, 0, tc - 1)
        return s if d == 0 else tc - 1 - s

    @pl.when(j == 0)
    def _():
        st[...] = s0_ref[...]
        for (b, d) in chains:
            uv[b, d] = state_products(b, d, head_rows(c0_ref[b, d:d + 1, :]), zeye,
                                      value_rows(v0_ref[b, d:d + 1, :]), zrows)

    def issue(group, s):
        out = []
        for (b, d) in group:
            r_, wc_, w_, b_, k_, vn_, vk_, be_, y_ = dirs[d]
            t, tp = time_of(d, s), time_of(d, s - 1)
            prod = state_products(b, d, head_rows(wc_[b, pl.ds(t, 1), :]),
                                  value_rows(vk_[b, pl.ds(t, 1), :]),
                                  value_rows(next_value_row(vn_, b, d, t)),
                                  head_rows(r_[b, pl.ds(tp, 1), :]))
            w2 = jnp.concatenate([head_rows(b_[b, pl.ds(t, 1), :]), head_rows(k_[b, pl.ds(t, 1), :]),
                                  zrows, zrows], axis=0)
            out.append((prod, _dot(uv[b, d].astype(BF16), w2)))
        return out

    def finish(group, s, results):
        for (b, d), (prod, upd) in zip(group, results):
            r_, wc_, w_, b_, k_, vn_, vk_, be_, y_ = dirs[d]
            t, tp = time_of(d, s), time_of(d, s - 1)
            wrow = w_[b, pl.ds(t, 1), :]
            for g in range(N_GRP):
                sl = slice(g * MXU_DIM, (g + 1) * MXU_DIM)
                st[b, d, g] = st[b, d, g] * wrow[:, sl] + upd[:, sl]
            store_y(y_, b, tp, prod)
            uv[b, d] = prod + uv[b, d] * be_[b, pl.ds(t, 1), 0:4 * SCAN_ROWS]

    lead, lag = chains[:seqs], chains[seqs:]

    def body(i, carry):
        for u in range(SCAN_UNROLL):
            s = i * SCAN_UNROLL + u
            lead_results = issue(lead, s)
            lag_results = issue(lag, s)
            finish(lead, s, lead_results)
            finish(lag, s, lag_results)
        return carry

    lax.fori_loop(0, tc // SCAN_UNROLL, body, 0)

    for (b, d) in chains:
        t = time_of(d, tc - 1)
        prod = state_products(b, d, zrows, zeye, zeye, head_rows(dirs[d][0][b, pl.ds(t, 1), :]))
        store_y(dirs[d][8], b, t, prod)

    @pl.when(j == pl.num_programs(1) - 1)
    def _():
        sfin_ref[...] = st[...]


def _scan_call(pre, s0, eye):
    B, T, _ = pre['r'].shape
    nb = 4 if B % 4 == 0 else 2
    tc = min(128 if nb == 2 else 32, T)
    nj = T // tc
    fwd = pl.BlockSpec((nb, tc, D_RWKV), lambda bi, j: (bi, j, 0))
    bwd = pl.BlockSpec((nb, tc, D_RWKV), lambda bi, j: (bi, nj - 1 - j, 0))
    yfwd = pl.BlockSpec((nb, tc, H_RWKV, RWKV_HEAD), lambda bi, j: (bi, j, 0, 0))
    ybwd = pl.BlockSpec((nb, tc, H_RWKV, RWKV_HEAD), lambda bi, j: (bi, nj - 1 - j, 0, 0))
    st_spec = pl.BlockSpec((nb, 2, N_GRP, RWKV_HEAD, MXU_DIM), lambda bi, j: (bi, 0, 0, 0, 0))
    y_shape = jax.ShapeDtypeStruct((B, T, H_RWKV, RWKV_HEAD), F32)
    befwd = pl.BlockSpec((nb, tc, LANE), lambda bi, j: (bi, j, 0))
    bebwd = pl.BlockSpec((nb, tc, LANE), lambda bi, j: (bi, nj - 1 - j, 0))
    per = tc // SUBLANE
    last8 = T // SUBLANE - 1
    hfwd = pl.BlockSpec((nb, SUBLANE, D_RWKV), lambda bi, j: (bi, jnp.minimum((j + 1) * per, last8), 0))
    hbwd = pl.BlockSpec((nb, SUBLANE, D_RWKV), lambda bi, j: (bi, jnp.maximum((nj - 1 - j) * per - 1, 0), 0))
    first = pl.BlockSpec((nb, SUBLANE, D_RWKV), lambda bi, j: (bi, 0, 0))

    wc0, vk0, be0, wc1, vk1, be1 = (pre[n] for n in ('wc0', 'vk0', 'be0', 'wc1', 'vk1', 'be1'))
    v = pre['v']
    pad = jnp.zeros((B, SUBLANE - 2, D_RWKV), F32)
    c_first = jnp.concatenate([pre['c'][:, 0:1], pre['c'][:, T - 1:T], pad], axis=1)
    v_first = jnp.concatenate([v[:, 0:1], v[:, T - 1:T], pad], axis=1)
    return pl.pallas_call(
        functools.partial(_scan_kernel, tc=tc),
        grid=(B // nb, nj),
        in_specs=[fwd] * 6 + [hfwd, fwd, befwd] + [bwd] * 6 + [hbwd, bwd, bebwd] + [
            first, first,
            st_spec,
            pl.BlockSpec((RWKV_HEAD, MXU_DIM), lambda bi, j: (0, 0)),
            pl.BlockSpec((H_RWKV, D_RWKV), lambda bi, j: (0, 0))],
        out_specs=[yfwd, ybwd, st_spec],
        out_shape=[y_shape, y_shape, jax.ShapeDtypeStruct((B, 2, N_GRP, RWKV_HEAD, MXU_DIM), F32)],
        scratch_shapes=[pltpu.VMEM((nb, 2, N_GRP, RWKV_HEAD, MXU_DIM), F32),
                        pltpu.VMEM((nb, 2, RWKV_HEAD, 4 * SCAN_ROWS), F32)],
        compiler_params=_params(("arbitrary", "arbitrary"), 52),
        name="rwkv_scan",
    )(pre['r'], wc0, pre['w0'], pre['b0'], pre['k0'], v, v, vk0, be0,
      pre['r'], wc1, pre['w1'], pre['b1'], pre['k1'], v, v, vk1, be1,
      c_first, v_first, s0, eye.astype(BF16), _scan_head_mask())


def _rwkv_post_kernel(y0_ref, y1_ref, bonus_ref, g_ref, gw_ref, gb_ref, ones_ref, o_ref):
    ones_blk = ones_ref[...]
    y = y0_ref[...] + y1_ref[...]
    mu = _segsum(y, ones_blk) * (1.0 / RWKV_HEAD)
    yc = y - mu
    var = _segsum(yc * yc, ones_blk) * (1.0 / RWKV_HEAD)
    yn = yc * lax.rsqrt(var + RWKV_GN_EPS)
    out = (yn * gw_ref[...] + gb_ref[...] + bonus_ref[...]) * g_ref[...]
    o_ref[...] = out.astype(BF16)


def _rwkv_post_call(y0, y1, bonus, g, gn_w, gn_b, ones_blk):
    T = y0.shape[0]
    tt = 512
    blk = pl.BlockSpec((tt, D_RWKV), lambda i: (i, 0))
    row = pl.BlockSpec((1, D_RWKV), lambda i: (0, 0))
    return pl.pallas_call(
        _rwkv_post_kernel,
        grid=(T // tt,),
        in_specs=[blk, blk, blk, blk, row, row, pl.BlockSpec((MXU_DIM, MXU_DIM), lambda i: (0, 0))],
        out_specs=blk,
        out_shape=jax.ShapeDtypeStruct((T, D_RWKV), BF16),
        compiler_params=_params(("parallel",), 40),
        name="rwkv_post",
    )(y0, y1, bonus, g, gn_w, gn_b, ones_blk)


def _rope128(x, cos_t, sin_t):
    return x * cos_t + pltpu.roll(x, QK_ROPE, 1) * sin_t


def _pack_kv(kv, kr_rot, k_o, v_o):
    for h in range(H_MLA):
        k_o[:, h * HEAD_SLOT:h * HEAD_SLOT + QK_NOPE] = kv[:, h * HEAD_SLOT:h * HEAD_SLOT + QK_NOPE].astype(BF16)
        k_o[:, h * HEAD_SLOT + QK_NOPE:(h + 1) * HEAD_SLOT] = kr_rot.astype(BF16)
        v_o[:, h * V_HEAD:(h + 1) * V_HEAD] = kv[:, h * HEAD_SLOT + QK_NOPE:(h + 1) * HEAD_SLOT].astype(BF16)


def _mla_prep_kernel(z_ref, cos_ref, sin_ref, qn_ref, kvn_ref, wq_ref, wkv_ref,
                     q_o, k_o, v_o, ckv_o, kr_o):
    z = z_ref[...]
    cq = z[:, 0:Q_RANK]
    ckv = z[:, Q_RANK:Q_RANK + KV_RANK]
    krp = z[:, Q_RANK + KV_RANK:MLA_COLS]
    cos_t = cos_ref[...]
    sin_t = sin_ref[...]
    cq = (cq * lax.rsqrt(jnp.mean(cq * cq, axis=-1, keepdims=True) + 1e-6)) * qn_ref[...]
    ckv = (ckv * lax.rsqrt(jnp.mean(ckv * ckv, axis=-1, keepdims=True) + 1e-6)) * kvn_ref[...]
    ckv_o[...] = ckv
    kr_o[...] = krp
    q = _dot(cq.astype(BF16), wq_ref[...]) * (ATTN_SCALE * LOG2_E)
    for h in range(H_MLA):
        q_o[:, h * HEAD_SLOT:h * HEAD_SLOT + QK_NOPE] = q[:, h * HEAD_SLOT:h * HEAD_SLOT + QK_NOPE].astype(BF16)
        q_o[:, h * HEAD_SLOT + QK_NOPE:(h + 1) * HEAD_SLOT] = _rope128(
            q[:, h * HEAD_SLOT + QK_NOPE:(h + 1) * HEAD_SLOT], cos_t, sin_t).astype(BF16)
    kv = _dot(ckv.astype(BF16), wkv_ref[...])
    _pack_kv(kv, _rope128(krp, cos_t, sin_t), k_o, v_o)


def _mla_prep_call(z_m, cos_t, sin_t, seq_len, p):
    T = z_m.shape[0]
    tm = min(512, seq_len)
    per_seq = seq_len // tm
    row = lambda n: pl.BlockSpec((1, n), lambda i: (0, 0))
    blk = lambda n: pl.BlockSpec((tm, n), lambda i: (i, 0))
    tab = pl.BlockSpec((tm, LANE), lambda i: (i % per_seq, 0))
    return pl.pallas_call(
        _mla_prep_kernel,
        grid=(T // tm,),
        in_specs=[blk(MLA_COLS), tab, tab, row(Q_RANK), row(KV_RANK),
                  pl.BlockSpec((Q_RANK, H_MLA * HEAD_SLOT), lambda i: (0, 0)),
                  pl.BlockSpec((KV_RANK, H_MLA * HEAD_SLOT), lambda i: (0, 0))],
        out_specs=[blk(H_MLA * HEAD_SLOT), blk(H_MLA * HEAD_SLOT), blk(H_MLA * V_HEAD),
                   blk(KV_RANK), blk(LANE)],
        out_shape=[jax.ShapeDtypeStruct((T, H_MLA * HEAD_SLOT), BF16),
                   jax.ShapeDtypeStruct((T, H_MLA * HEAD_SLOT), BF16),
                   jax.ShapeDtypeStruct((T, H_MLA * V_HEAD), BF16),
                   jax.ShapeDtypeStruct((T, KV_RANK), F32),
                   jax.ShapeDtypeStruct((T, LANE), F32)],
        compiler_params=_params(("parallel",), 48),
        name="mla_prep",
    )(z_m, cos_t, sin_t, p['q_norm'], p['kv_norm'], p['w_qb'], p['w_kvb'])


def _ctx_kv_kernel(ckv_ref, kr_ref, wkv_ref, k_o, v_o):
    kv = _dot(ckv_ref[...].astype(BF16), wkv_ref[...])
    _pack_kv(kv, kr_ref[...], k_o, v_o)


def _ctx_kv_call(ckv_ctx, kr_ctx_pad, w_kvb):
    T = ckv_ctx.shape[0]
    tm = min(512, T)
    blk = lambda n: pl.BlockSpec((tm, n), lambda i: (i, 0))
    return pl.pallas_call(
        _ctx_kv_kernel,
        grid=(T // tm,),
        in_specs=[blk(KV_RANK), blk(LANE), pl.BlockSpec((KV_RANK, H_MLA * HEAD_SLOT), lambda i: (0, 0))],
        out_specs=[blk(H_MLA * HEAD_SLOT), blk(H_MLA * V_HEAD)],
        out_shape=[jax.ShapeDtypeStruct((T, H_MLA * HEAD_SLOT), BF16),
                   jax.ShapeDtypeStruct((T, H_MLA * V_HEAD), BF16)],
        compiler_params=_params(("parallel",), 32),
        name="mla_ctx_kv",
    )(ckv_ctx, kr_ctx_pad, w_kvb)


ATTN_ROWS = 64


def _attn_kernel(q_ref, k_ref, v_ref, o_ref, s_sc, p_sc, *, kc):
    tq, tk = s_sc.shape
    hq = tq // 2
    nt = (((1,), (1,)), ((), ()))

    def scores(h):
        q = q_ref[h * hq:(h + 1) * hq, :]
        for c in range(tk // kc):
            s_sc[h * hq:(h + 1) * hq, c * kc:(c + 1) * kc] = lax.dot_general(
                q, k_ref[c * kc:(c + 1) * kc, :], nt, preferred_element_type=F32)

    def softmax(h):
        sums = []
        for r in range(hq // ATTN_ROWS):
            rows = slice(h * hq + r * ATTN_ROWS, h * hq + (r + 1) * ATTN_ROWS)
            mpart = s_sc[rows, 0:LANE]
            for t in range(1, tk // LANE):
                mpart = jnp.maximum(mpart, s_sc[rows, t * LANE:(t + 1) * LANE])
            m = jnp.max(mpart, axis=-1, keepdims=True)
            lpart = jnp.zeros((ATTN_ROWS, LANE), F32)
            for t in range(tk // LANE):
                p = jnp.exp2(s_sc[rows, t * LANE:(t + 1) * LANE] - m)
                lpart = lpart + p
                p_sc[rows, t * LANE:(t + 1) * LANE] = p.astype(BF16)
            sums.append(jnp.sum(lpart, axis=-1, keepdims=True))
        return jnp.concatenate(sums, axis=0)

    def weighted_values(h, l):
        acc = jnp.zeros((hq, V_HEAD), F32)
        for c in range(tk // kc):
            acc = acc + _dot(p_sc[h * hq:(h + 1) * hq, c * kc:(c + 1) * kc], v_ref[c * kc:(c + 1) * kc, :])
        o_ref[h * hq:(h + 1) * hq, :] = (acc / l).astype(BF16)

    scores(0)
    scores(1)
    l0 = softmax(0)
    weighted_values(0, l0)
    l1 = softmax(1)
    weighted_values(1, l1)


def _attn_call(q, k, v):
    B, Tq, _ = q.shape
    Tk = k.shape[1]
    tq = min(512, Tq)
    return pl.pallas_call(
        functools.partial(_attn_kernel, kc=MXU_DIM),
        grid=(B, H_MLA, Tq // tq),
        in_specs=[pl.BlockSpec((None, tq, HEAD_SLOT), lambda b, h, i: (b, i, h)),
                  pl.BlockSpec((None, Tk, HEAD_SLOT), lambda b, h, i: (b, 0, h)),
                  pl.BlockSpec((None, Tk, V_HEAD), lambda b, h, i: (b, 0, h))],
        out_specs=pl.BlockSpec((None, tq, V_HEAD), lambda b, h, i: (b, i, h)),
        out_shape=jax.ShapeDtypeStruct((B, Tq, H_MLA * V_HEAD), BF16),
        scratch_shapes=[pltpu.VMEM((tq, Tk), F32), pltpu.VMEM((tq, Tk), BF16)],
        compiler_params=_params(("parallel", "parallel", "arbitrary"), 48),
        name="mla_attention",
    )(q, k, v)


def _conv3_kernel(*refs, tt, half_len):
    ins, (cw_refs, cb_refs), outs = refs[0:12], (refs[12:15], refs[15:18]), refs[18:]
    row0 = pl.program_id(0) * tt
    for s in range(3):
        even, odd = ins[4 * s][...], ins[4 * s + 1][...]
        odd_prev = _shift_prev(odd, ins[4 * s + 2][...], row0, half_len)
        even_next = _shift_next(even, ins[4 * s + 3][...], row0, half_len)
        cw = cw_refs[s][...]
        bias = cb_refs[s][...]
        y_even = cw[0:1, :] * odd_prev + cw[1:2, :] * even + cw[2:3, :] * odd + bias
        y_odd = cw[0:1, :] * even + cw[1:2, :] * odd + cw[2:3, :] * even_next + bias
        outs[s][0] = y_even
        outs[s][1] = y_odd
        if s == 2:
            outs[3][0] = y_even.astype(BF16)
            outs[3][1] = y_odd.astype(BF16)


def _conv3_call(z, half_len, conv_w, conv_b):
    T2 = z.shape[1]
    tt = min(128, half_len)
    per = tt // SUBLANE
    last = T2 // SUBLANE - 1
    in_specs = []
    for s in range(3):
        in_specs += [
            pl.BlockSpec((None, tt, D_HYENA), lambda i, s=s: (0, i, s)),
            pl.BlockSpec((None, tt, D_HYENA), lambda i, s=s: (1, i, s)),
            pl.BlockSpec((None, SUBLANE, D_HYENA), lambda i, s=s: (1, jnp.maximum(i * per - 1, 0), s)),
            pl.BlockSpec((None, SUBLANE, D_HYENA), lambda i, s=s: (0, jnp.minimum((i + 1) * per, last), s))]
    in_specs += [pl.BlockSpec((3, D_HYENA), lambda i, s=s: (0, s)) for s in range(3)]
    in_specs += [pl.BlockSpec((1, D_HYENA), lambda i, s=s: (0, s)) for s in range(3)]
    blk = pl.BlockSpec((2, tt, D_HYENA), lambda i: (0, i, 0))
    f32s = jax.ShapeDtypeStruct((2, T2, D_HYENA), F32)
    return pl.pallas_call(
        functools.partial(_conv3_kernel, tt=tt, half_len=half_len),
        grid=(T2 // tt,),
        in_specs=in_specs,
        out_specs=[blk] * 4,
        out_shape=[f32s, f32s, f32s, jax.ShapeDtypeStruct((2, T2, D_HYENA), BF16)],
        compiler_params=_params(("parallel",), 48),
        name="hyena_conv3",
    )(*([z] * 12), conv_w, conv_w, conv_w, conv_b, conv_b, conv_b)


def _filt_mlp_kernel(z_ref, w1_ref, b1_ref, w2_ref, b2_ref, fr_ref, o_ref):
    h = jnp.sin(fr_ref[0:1, :] * (_dot(z_ref[...].astype(BF16), w1_ref[...].astype(BF16)) + b1_ref[...]))
    h = jnp.sin(fr_ref[1:2, :] * (_dot(h.astype(BF16), w2_ref[...].astype(BF16)) + b2_ref[...]))
    o_ref[...] = h.astype(BF16)


def _filt_mlp_call(zpos, w1p, b1, w2, b2, freq):
    L = zpos.shape[0]
    return pl.pallas_call(
        _filt_mlp_kernel,
        out_shape=jax.ShapeDtypeStruct((L, FILT_HIDDEN), BF16),
        compiler_params=pltpu.CompilerParams(vmem_limit_bytes=32 * MIB),
        name="hyena_filter_mlp",
    )(zpos, w1p, b1, w2, b2, freq)


def _filt_gen_kernel(h_ref, tn_ref, dl_ref, w00, w01, w10, w11, o_ref, taps_sc):
    h = h_ref[...]
    L = h.shape[0]
    win = jnp.exp(-tn_ref[...] * dl_ref[...])
    not_first = lax.broadcasted_iota(jnp.int32, (L, 1), 0) > 0
    ws = ((w00, w01), (w10, w11))

    def emit(k, taps):
        taps_sc[...] = taps
        for par in range(2):
            o_ref[k, par] = taps_sc[pl.ds(par, L // 2, stride=2), :].astype(BF16)

    for n in range(2):
        causal = _dot(h, ws[n][0][...].astype(BF16)) * win
        anti = jnp.where(not_first, _dot(h, ws[n][1][...].astype(BF16)) * win, 0.0)
        norm = (jnp.sum(jnp.abs(causal), axis=0, keepdims=True)
                + jnp.sum(jnp.abs(anti), axis=0, keepdims=True))
        emit(2 * n, causal / norm)
        emit(2 * n + 1, anti / norm)


def _filt_gen_call(h2, tnorm, deltas, w3):
    L = h2.shape[0]
    tc = 128
    nc = D_HYENA // tc
    wspec = lambda k: pl.BlockSpec((FILT_HIDDEN, tc), lambda j, k=k: (0, k * nc + j))
    return pl.pallas_call(
        _filt_gen_kernel,
        grid=(nc,),
        in_specs=[pl.BlockSpec((L, FILT_HIDDEN), lambda j: (0, 0)),
                  pl.BlockSpec((L, 1), lambda j: (0, 0)),
                  pl.BlockSpec((1, tc), lambda j: (0, j)),
                  wspec(0), wspec(1), wspec(2), wspec(3)],
        out_specs=pl.BlockSpec((4, 2, L // 2, tc), lambda j: (0, 0, 0, j)),
        out_shape=jax.ShapeDtypeStruct((4, 2, L // 2, D_HYENA), BF16),
        scratch_shapes=[pltpu.VMEM((L, tc), F32)],
        compiler_params=_params(("parallel",), 48),
        name="hyena_filter_gen",
    )(h2, tnorm, deltas, w3, w3, w3, w3)


def _dft_fwd_kernel(f_ref, u_ref, o_ref):
    o_ref[...] = _dot(f_ref[...], u_ref[...])


def _dft_fwd_call(fmat, u):
    B, K, C = u.shape
    M = fmat.shape[0]
    tm = min(512, M)
    tn = min(C, 2048)
    return pl.pallas_call(
        _dft_fwd_kernel,
        grid=(B, C // tn, M // tm),
        in_specs=[pl.BlockSpec((tm, K), lambda b, j, i: (i, 0)),
                  pl.BlockSpec((None, K, tn), lambda b, j, i: (b, 0, j))],
        out_specs=pl.BlockSpec((None, tm, tn), lambda b, j, i: (b, i, j)),
        out_shape=jax.ShapeDtypeStruct((B, M, C), F32),
        compiler_params=_params(("parallel", "parallel", "arbitrary"), 40),
        name="hyena_dft_fwd",
    )(fmat, u)


def _butterfly(gc, gs, hc, hs, tw_c, tw_s, first):
    tc = hc * tw_c - hs * tw_s
    ts = hs * tw_c + hc * tw_s
    p0 = gc + tc
    p2 = gc - tc
    p1 = jnp.where(first, gs, gs + ts)
    p3 = jnp.where(first, hs, ts - gs)
    return p0, p1, p2, p3


def _cmul(ac, a_s, bc, bs):
    return ac * bc - a_s * bs, ac * bs + a_s * bc


def _filt_planes_kernel(ge_ref, ho_ref, twc_ref, tws_ref, o_ref, *, tr):
    first = (lax.broadcasted_iota(jnp.int32, (tr, 1), 0) + pl.program_id(1) * tr) == 0
    tw_c, tw_s = twc_ref[...], tws_ref[...]
    a = _butterfly(ge_ref[0, 0], ge_ref[0, 1], ho_ref[0, 0], ho_ref[0, 1], tw_c, tw_s, first)
    b = _butterfly(ge_ref[1, 0], ge_ref[1, 1], ho_ref[1, 0], ho_ref[1, 1], tw_c, tw_s, first)
    o_ref[0] = a[0] + b[0]
    o_ref[1] = jnp.where(first, a[1] + b[1], a[1] - b[1])
    o_ref[2] = a[2] + b[2]
    o_ref[3] = a[3] - b[3]


def _filt_planes_call(raw, twc, tws):
    _, L, C = raw.shape
    H = L // 2
    tr = min(256, H)
    tc = 512
    blk = lambda par: pl.BlockSpec((None, 2, None, 2, tr, tc), lambda n, i, j: (n, 0, par, 0, i, j))
    tw = pl.BlockSpec((tr, 1), lambda n, i, j: (i, 0))
    raw6 = raw.reshape(2, 2, 2, 2, H, C)
    return pl.pallas_call(
        functools.partial(_filt_planes_kernel, tr=tr),
        grid=(2, H // tr, C // tc),
        in_specs=[blk(0), blk(1), tw, tw],
        out_specs=pl.BlockSpec((None, 4, tr, tc), lambda n, i, j: (n, 0, i, j)),
        out_shape=jax.ShapeDtypeStruct((2, 4, H, C), F32),
        compiler_params=_params(("parallel", "parallel", "parallel"), 40),
        name="hyena_filter_planes",
    )(raw6, raw6, twc, tws)


def _spec_mul_kernel(raw_ref, k_ref, twc_ref, tws_ref, o_ref, *, tr):
    first = (lax.broadcasted_iota(jnp.int32, (tr, 1), 0) + pl.program_id(1) * tr) == 0
    tw_c, tw_s = twc_ref[...], tws_ref[...]
    p0, p1, p2, p3 = _butterfly(raw_ref[0, 0], raw_ref[0, 1], raw_ref[1, 0], raw_ref[1, 1], tw_c, tw_s, first)
    k0, k1, k2, k3 = k_ref[0], k_ref[1], k_ref[2], k_ref[3]
    yac, yas = _cmul(p0, p1, k0, k1)
    ybc, ybs = _cmul(p2, p3, k2, k3)
    ymc, yms = _cmul(p1, p3, k1, k3)
    yac = jnp.where(first, p0 * k0, yac)
    ybc = jnp.where(first, p2 * k2, ybc)
    dc = yac - ybc
    ds = yas + ybs
    o_ref[0, 0] = (yac + ybc).astype(BF16)
    o_ref[0, 1] = jnp.where(first, ymc, yas - ybs).astype(BF16)
    o_ref[1, 0] = jnp.where(first, dc, dc * tw_c + ds * tw_s).astype(BF16)
    o_ref[1, 1] = jnp.where(first, yms, ds * tw_c - dc * tw_s).astype(BF16)


def _spec_mul_call(raw, k_planes, order, twc, tws):
    B2, L, C = raw.shape
    B, H = B2 // 2, L // 2
    tr = min(256, H)
    tc = 512
    tw = pl.BlockSpec((tr, 1), lambda b, i, j: (i, 0))
    out = pl.pallas_call(
        functools.partial(_spec_mul_kernel, tr=tr),
        grid=(B, H // tr, C // tc),
        in_specs=[pl.BlockSpec((2, None, 2, tr, tc), lambda b, i, j: (0, b, 0, i, j)),
                  pl.BlockSpec((None, 4, tr, tc), lambda b, i, j: (order, 0, i, j)),
                  tw, tw],
        out_specs=pl.BlockSpec((2, None, 2, tr, tc), lambda b, i, j: (0, b, 0, i, j)),
        out_shape=jax.ShapeDtypeStruct((2, B, 2, H, C), BF16),
        compiler_params=_params(("parallel", "parallel", "parallel"), 40),
        name="hyena_spectral_mul",
    )(raw.reshape(2, B, 2, H, C), k_planes, twc, tws)
    return out.reshape(B2, L, C)


def _dft_inv_kernel(f_ref, y_ref, gate_ref, u_ref, bias_ref, o_ref, ob_ref):
    conv = _dot(f_ref[...], y_ref[...])
    out = gate_ref[...] * (conv + u_ref[...] * bias_ref[...])
    o_ref[...] = out
    ob_ref[...] = out.astype(BF16)


def _dft_inv_call(imat, y_spec, gate, u, bias):
    B, M, C = u.shape
    K = imat.shape[1]
    tm = min(512, M)
    tn = 1024 if K > 1024 else min(C, 2048)
    blk = pl.BlockSpec((None, tm, tn), lambda b, j, i: (b, i, j))
    return pl.pallas_call(
        _dft_inv_kernel,
        grid=(B, C // tn, M // tm),
        in_specs=[pl.BlockSpec((tm, K), lambda b, j, i: (i, 0)),
                  pl.BlockSpec((None, K, tn), lambda b, j, i: (b, 0, j)),
                  blk, blk, pl.BlockSpec((1, tn), lambda b, j, i: (0, j))],
        out_specs=[blk, blk],
        out_shape=[jax.ShapeDtypeStruct((B, M, C), F32), jax.ShapeDtypeStruct((B, M, C), BF16)],
        compiler_params=_params(("parallel", "parallel", "arbitrary"), 48),
        name="hyena_dft_inv",
    )(imat, y_spec, gate, u, bias)


def _dft_tables(L):
    H = L // 2
    lo = min(64, H)
    hi = H // lo
    g = jnp.arange(H, dtype=jnp.int32)
    theta = 2.0 * math.pi / L
    ang_hi = ((g[:, None] * (jnp.arange(hi, dtype=jnp.int32) * lo)[None, :]) % L).astype(F32) * theta
    ang_lo = ((g[:, None] * jnp.arange(lo, dtype=jnp.int32)[None, :]) % L).astype(F32) * theta
    ch, sh, cl, sl = jnp.cos(ang_hi), jnp.sin(ang_hi), jnp.cos(ang_lo), jnp.sin(ang_lo)
    cos_m = (ch[:, :, None] * cl[:, None, :] - sh[:, :, None] * sl[:, None, :]).reshape(H, H)
    sin_m = (sh[:, :, None] * cl[:, None, :] + ch[:, :, None] * sl[:, None, :]).reshape(H, H)
    alt = jnp.where(jnp.arange(H) % 2 == 0, 1.0, -1.0).astype(F32)
    sin_m = jnp.where((g == 0)[:, None], alt[None, :], sin_m)
    fwd = jnp.concatenate([cos_m, sin_m], axis=0)
    n = 2.0 * L
    w_cos = jnp.where(g == 0, 1.0 / n, 2.0 / n).astype(F32)
    inv = jnp.concatenate([cos_m * w_cos[:, None], sin_m * (2.0 / n)], axis=0).T
    ang_tw = g.astype(F32) * (math.pi / L)
    return fwd.astype(BF16), inv.astype(BF16), jnp.cos(ang_tw)[:, None], jnp.sin(ang_tw)[:, None]


def _filter_positions(L):
    t = jnp.arange(L, dtype=F32)
    t_norm = t / max(L - 1, 1)
    bands = (POS_EMB - 1) // 2
    freqs = jnp.linspace(1e-4, bands - 1, bands, dtype=F32)
    ang = (2.0 * math.pi / L) * t[:, None] * freqs[None, :]
    z = jnp.concatenate([t_norm[:, None], jnp.cos(ang), -jnp.sin(ang)], axis=-1)
    return jnp.pad(z, ((0, 0), (0, POS_PAD - POS_EMB))), t_norm[:, None]


def _hyena_deltas():
    return jnp.linspace(abs(math.log(HYENA_TARGET)) / SLOW_DECAY_PCT,
                        abs(math.log(HYENA_TARGET)) / FAST_DECAY_PCT, D_HYENA, dtype=F32)[None, :]


def _hyena_filter_spectrum(L, tables, p):
    fwd, _, twc, tws = tables
    zpos, tnorm = _filter_positions(L)
    h2 = _filt_mlp_call(zpos, p['filt_w1'], p['filt_b1'], p['filt_w2'], p['filt_b2'], p['filt_freq'])
    filt = _filt_gen_call(h2, tnorm, _hyena_deltas(), p['filt_w3'])
    raw = _dft_fwd_call(fwd, filt.reshape(8, L // 2, D_HYENA))
    return _filt_planes_call(raw, twc, tws)


def _hyena_mixer(z, B, L, tables, k_planes, p):
    fwd, inv, twc, tws = tables
    H = L // 2
    x1, x2, v, vb = _conv3_call(z, H, p['conv_w'], p['conv_b'])
    shp = (2 * B, H, D_HYENA)
    u, ub = v.reshape(shp), vb.reshape(shp)
    for n, gate in enumerate((x1, x2)):
        raw = _dft_fwd_call(fwd, ub)
        yspec = _spec_mul_call(raw, k_planes, n, twc, tws)
        u, ub = _dft_inv_call(inv, yspec, gate.reshape(shp), u, p['bias'][n:n + 1])
    return ub.reshape(2, B * H, D_HYENA)


def _rope_swap(w):
    q = QK_ROPE // 4
    return jnp.concatenate([w[..., q:2 * q], w[..., 0:q], w[..., 3 * q:4 * q], w[..., 2 * q:3 * q]], axis=-1)


def _pad_cols(w, n):
    return jnp.pad(w, [(0, 0)] * (w.ndim - 1) + [(0, n - w.shape[-1])])


def _pack_even(e, w_in_even, mu_prev, mu_next, rwkv_w0, rwkv_w2, rwkv_a0, rwkv_a2, rwkv_g2,
               rwkv_kk, rwkv_ka, rwkv_rk, rwkv_gn_w, rwkv_gn_b, mla_q_norm, mla_kv_norm,
               mla_w_qb, mla_w_kvb, w_out_even):
    n_r = 3 * D_RWKV + W_LORA + A_LORA + G_LORA
    w_in = w_in_even[e]
    w_r = _pad_cols(w_in[:, :n_r], RWKV_COLS).astype(BF16)
    w_m = w_in[:, n_r:]
    kr_cols = w_m[:, Q_RANK + KV_RANK:]
    w_m = jnp.concatenate([w_m, _rope_swap(kr_cols)], axis=-1).astype(BF16)
    small_rows = lambda w, off: jnp.pad(w, [(0, 0)] * (w.ndim - 2)
                                        + [(off, RWKV_SMALL - off - w.shape[-2]), (0, 0)]).astype(BF16)
    wq = mla_w_qb[e].reshape(Q_RANK, H_MLA, QK_NOPE + QK_ROPE)
    wq = jnp.concatenate([wq, _rope_swap(wq[..., QK_NOPE:])], axis=-1).reshape(Q_RANK, H_MLA * HEAD_SLOT)
    blk = jnp.arange(MXU_DIM) // RWKV_HEAD
    return {
        'w_r': w_r, 'w_m': w_m,
        'mu_prev': _pad_cols(mu_prev[e][None, :], RWKV_COLS),
        'mu_next': _pad_cols(mu_next[e][None, :], RWKV_COLS),
        'k_k': rwkv_kk[e][None, :], 'k_a': rwkv_ka[e][None, :],
        'r_k': rwkv_rk[e].reshape(1, D_RWKV),
        'w0': rwkv_w0[e], 'w2': small_rows(rwkv_w2[e], 0),
        'a0': rwkv_a0[e], 'a2': small_rows(rwkv_a2[e], W_LORA),
        'g2': small_rows(rwkv_g2[e], W_LORA + A_LORA),
        'gn_w': rwkv_gn_w[e][None, :], 'gn_b': rwkv_gn_b[e][None, :],
        'q_norm': mla_q_norm[e][None, :], 'kv_norm': mla_kv_norm[e][None, :],
        'w_qb': wq.astype(BF16), 'w_kvb': mla_w_kvb[e].astype(BF16),
        'w_out': w_out_even[e].astype(BF16),
        'ones_blk': (blk[:, None] == blk[None, :]).astype(BF16),
        'eye': (jnp.arange(RWKV_HEAD)[:, None] == (jnp.arange(MXU_DIM) % RWKV_HEAD)[None, :]).astype(F32),
    }


def _rope_tables(L):
    rows = L // GRID_W
    row = jnp.repeat(jnp.arange(rows, dtype=F32), GRID_W)
    col = jnp.tile(jnp.arange(GRID_W, dtype=F32), rows)
    half = QK_ROPE // 2
    inv = 1.0 / (ROPE_THETA ** (jnp.arange(0, half, 2, dtype=F32) / half))
    ar, ac = row[:, None] * inv[None, :], col[:, None] * inv[None, :]
    cos_t = jnp.concatenate([jnp.cos(ar), jnp.cos(ar), jnp.cos(ac), jnp.cos(ac)], axis=-1)
    sin_t = jnp.concatenate([-jnp.sin(ar), jnp.sin(ar), -jnp.sin(ac), jnp.sin(ac)], axis=-1)
    return _pad_cols(cos_t, LANE), _pad_cols(sin_t, LANE)


def _state_to_groups(s):
    B = s.shape[0]
    s = s.reshape(B, 2, N_GRP, H_RWKV // N_GRP, RWKV_HEAD, RWKV_HEAD)
    return jnp.swapaxes(s, 3, 4).reshape(B, 2, N_GRP, RWKV_HEAD, MXU_DIM)


def _groups_to_state(s):
    B = s.shape[0]
    s = s.reshape(B, 2, N_GRP, RWKV_HEAD, H_RWKV // N_GRP, RWKV_HEAD)
    return jnp.swapaxes(s, 3, 4).reshape(B, 2, H_RWKV, RWKV_HEAD, RWKV_HEAD)


def _even_mixer(x, mod_l, goff, B, L, gamma, p, rope, ctx):
    group_tokens = x.shape[0] if ctx is None else L
    z_r = _inproj_call(x, mod_l, goff, group_tokens, gamma, p['w_r'], RWKV_COLS // 3)
    z_m = _inproj_call(x, mod_l, goff, group_tokens, gamma, p['w_m'], MLA_COLS)
    names = ('r', 'v', 'c', 'w0', 'b0', 'k0', 'wc0', 'vk0', 'be0', 'w1', 'b1', 'k1', 'wc1', 'vk1', 'be1',
             'bonus', 'g')
    pre = dict(zip(names, _rwkv_prep_call(z_r, L, p)))
    seq = {n: pre[n].reshape(B, L, pre[n].shape[-1]) for n in names[:15]}
    if ctx is None:
        s0 = jnp.zeros((B, 2, N_GRP, RWKV_HEAD, MXU_DIM), F32)
    else:
        s0 = _state_to_groups(ctx[2].astype(F32))
    y0, y1, s_fin = _scan_call(seq, s0, p['eye'])
    y_r = _rwkv_post_call(y0.reshape(B * L, D_RWKV), y1.reshape(B * L, D_RWKV), pre['bonus'], pre['g'],
                          p['gn_w'], p['gn_b'], p['ones_blk'])

    q, k, v, ckv, krp = _mla_prep_call(z_m, rope[0], rope[1], L, p)
    q = q.reshape(B, L, H_MLA * HEAD_SLOT)
    k = k.reshape(B, L, H_MLA * HEAD_SLOT)
    v = v.reshape(B, L, H_MLA * V_HEAD)
    if ctx is not None:
        P = ctx[0].shape[1]
        k_ctx, v_ctx = _ctx_kv_call(ctx[0].reshape(B * P, KV_RANK),
                                    _pad_cols(ctx[1].reshape(B * P, QK_ROPE), LANE), p['w_kvb'])
        k = jnp.concatenate([k, k_ctx.reshape(B, P, H_MLA * HEAD_SLOT)], axis=1)
        v = jnp.concatenate([v, v_ctx.reshape(B, P, H_MLA * V_HEAD)], axis=1)
    y_m = _attn_call(q, k, v).reshape(B * L, H_MLA * V_HEAD)
    x = _outproj_call(x, mod_l, goff, group_tokens, y_r, y_m, p['w_out'])
    state = (_groups_to_state(s_fin), ckv.reshape(B, L, KV_RANK), krp[:, :QK_ROPE].reshape(B, L, QK_ROPE))
    return x, state


def _odd_mixer(x, mod_l, goff, group_tokens, B, L, gamma, tables, k_planes, p):
    z = _inproj_parity_call(x, mod_l, goff, group_tokens, gamma, p['w_in'], 1536)
    y = _hyena_mixer(z, B, L, tables, k_planes, p)
    return _outproj_parity_call(x, mod_l, goff, group_tokens, y, p['w_out'])


def kernel(x_prompt, x_sample, cache_mla_ckv, cache_mla_krope, state_rwkv, c, c_ctx,
           w_mod, b_mod, norm_g, w_ffn_in, w_ffn_out, final_norm_g,
           w_in_even, mu_prev, mu_next, rwkv_w0, rwkv_w2, rwkv_a0, rwkv_a2, rwkv_g2,
           rwkv_kk, rwkv_ka, rwkv_rk, rwkv_gn_w, rwkv_gn_b,
           mla_q_norm, mla_kv_norm, mla_w_qb, mla_w_kvb, w_out_even,
           w_in_odd, hy_conv_w, hy_conv_b, hy_filt_w1, hy_filt_b1, hy_filt_w2, hy_filt_b2,
           hy_filt_w3, hy_filt_freq, hy_bias, w_out_odd):
    Bp, Lp, D = x_prompt.shape
    Bs, Ls, _ = x_sample.shape
    depth = w_mod.shape[0]
    xp = x_prompt.reshape(Bp * Lp, D)
    xs = x_sample.reshape(Bs * Ls, D)
    Tp = Bp * Lp

    cvec = jnp.concatenate([c_ctx[None, :], c, jnp.zeros((SUBLANE - 1 - Bs, D), F32)], axis=0)
    mod = _mod_call(cvec, w_mod, b_mod)

    rope_p = (_pad_cols(jnp.ones((Lp, QK_ROPE), F32), LANE), jnp.zeros((Lp, LANE), F32))
    rope_s = _rope_tables(Ls)
    tabs_p = tabs_s = None
    w_in = w_ffn_in.astype(BF16)
    w_out = w_ffn_out.astype(BF16)
    new_ckv, new_kr, new_s = [], [], []
    for l in range(depth):
        mod_l = mod[l]
        gam = [norm_g[l, s][None, :] for s in range(3)]
        xp = _ffn_call(xp, mod_l, 0, Tp, gam[0], w_in, w_out, l, 0, 0)
        xs = _ffn_call(xs, mod_l, 1, Ls, gam[0], w_in, w_out, l, 0, 0)
        if l % 2 == 0:
            e = l // 2
            p = _pack_even(e, w_in_even, mu_prev, mu_next, rwkv_w0, rwkv_w2, rwkv_a0, rwkv_a2, rwkv_g2,
                           rwkv_kk, rwkv_ka, rwkv_rk, rwkv_gn_w, rwkv_gn_b, mla_q_norm, mla_kv_norm,
                           mla_w_qb, mla_w_kvb, w_out_even)
            ctx = (cache_mla_ckv[:, e], cache_mla_krope[:, e], state_rwkv[:, e])
            xp, st = _even_mixer(xp, mod_l, 0, Bp, Lp, gam[1], p, rope_p, None)
            xs, _ = _even_mixer(xs, mod_l, 1, Bs, Ls, gam[1], p, rope_s, ctx)
            new_s.append(st[0].astype(x_prompt.dtype))
            new_ckv.append(st[1])
            new_kr.append(st[2])
        else:
            o = l // 2
            p = {'w_in': w_in_odd[o].astype(BF16), 'conv_w': hy_conv_w[o], 'conv_b': hy_conv_b[o][None, :],
                 'filt_w1': jnp.pad(hy_filt_w1[o], ((0, POS_PAD - POS_EMB), (0, 0))),
                 'filt_b1': hy_filt_b1[o][None, :], 'filt_w2': hy_filt_w2[o],
                 'filt_b2': hy_filt_b2[o][None, :], 'filt_w3': hy_filt_w3[o],
                 'filt_freq': hy_filt_freq[o], 'bias': hy_bias[o], 'w_out': w_out_odd[o].astype(BF16)}
            if tabs_p is None:
                tabs_p, tabs_s = _dft_tables(Lp), _dft_tables(Ls)
            ks_p = _hyena_filter_spectrum(Lp, tabs_p, p)
            ks_s = _hyena_filter_spectrum(Ls, tabs_s, p)
            xp = _odd_mixer(xp, mod_l, 0, Tp, Bp, Lp, gam[1], tabs_p, ks_p, p)
            xs = _odd_mixer(xs, mod_l, 1, Ls, Bs, Ls, gam[1], tabs_s, ks_s, p)
        xp = _ffn_call(xp, mod_l, 0, Tp, gam[2], w_in, w_out, l, 1, 2)
        xs = _ffn_call(xs, mod_l, 1, Ls, gam[2], w_in, w_out, l, 1, 2)

    fg = final_norm_g[None, :]
    y_prompt = _final_norm_call(xp, fg).reshape(Bp, Lp, D)
    y_sample = _final_norm_call(xs, fg).reshape(Bs, Ls, D)
    return (y_prompt, y_sample, jnp.stack(new_ckv, axis=1), jnp.stack(new_kr, axis=1),
            jnp.stack(new_s, axis=1))
```

```python
import functools
import math

import jax
import jax.numpy as jnp
from jax import lax
from jax.experimental import pallas as pl
from jax.experimental.pallas import tpu as pltpu

F32 = jnp.float32
BF16 = jnp.bfloat16

D_MODEL = 2048
N_MOD = 9
D_FF = 5632
D_RWKV = 1024
RWKV_HEAD = 64
H_RWKV = 16
W_LORA = 64
A_LORA = 64
G_LORA = 160
RWKV_SMALL = 384
RWKV_COLS = 3 * D_RWKV + RWKV_SMALL
RWKV_GN_EPS = 64e-5
H_MLA = 8
QK_NOPE = 128
QK_ROPE = 64
V_HEAD = 128
Q_RANK = 512
KV_RANK = 256
MLA_COLS = Q_RANK + KV_RANK + 2 * QK_ROPE
HEAD_SLOT = 256
ROPE_THETA = 10000.0
ATTN_SCALE = (QK_NOPE + QK_ROPE) ** -0.5
LOG2_E = 1.0 / math.log(2.0)
GRID_W = 64
D_HYENA = 2048
POS_EMB = 33
POS_PAD = 128
FILT_HIDDEN = 64
HYENA_TARGET = 1e-2
FAST_DECAY_PCT = 0.3
SLOW_DECAY_PCT = 1.5
LANE = 128
SUBLANE = 8
MXU_DIM = 256
MIB = 1024 * 1024


def _params(sem, vmem_mib):
    return pltpu.CompilerParams(dimension_semantics=sem, vmem_limit_bytes=vmem_mib * MIB)


def _sigmoid(x):
    return 1.0 / (1.0 + jnp.exp(-x))


def _softplus(x):
    return jnp.maximum(x, 0.0) + jnp.log(1.0 + jnp.exp(-jnp.abs(x)))


def _dot(a, b):
    return jnp.dot(a, b, preferred_element_type=F32)


def _norm_mod(x, gamma, shift, scale):
    xn = x * lax.rsqrt(jnp.mean(x * x, axis=-1, keepdims=True) + 1e-6)
    return xn * (gamma * (1.0 + scale)) + shift


def _mod_kernel(c_ref, w_ref, b_ref, o_ref):
    c = c_ref[...]
    s = c * _sigmoid(c)
    o_ref[0] = _dot(s.astype(BF16), w_ref[0].astype(BF16)) + b_ref[0]


def _mod_call(cvec, w_mod, b_mod):
    L, Dm, N = w_mod.shape
    tn = 1024
    out = pl.pallas_call(
        _mod_kernel,
        grid=(L, N // tn),
        in_specs=[pl.BlockSpec((SUBLANE, Dm), lambda l, j: (0, 0)),
                  pl.BlockSpec((1, Dm, tn), lambda l, j: (l, 0, j)),
                  pl.BlockSpec((1, 1, tn), lambda l, j: (l, 0, j))],
        out_specs=pl.BlockSpec((1, SUBLANE, tn), lambda l, j: (l, 0, j)),
        out_shape=jax.ShapeDtypeStruct((L, SUBLANE, N), F32),
        compiler_params=_params(("arbitrary", "arbitrary"), 40),
        name="adaln_mod",
    )(cvec, w_mod, b_mod.reshape(L, 1, N))
    return out.reshape(L, SUBLANE, N_MOD, Dm)


def _mod_spec(goff, tiles_per_group, nargs):
    if nargs == 1:
        return pl.BlockSpec((None, N_MOD, D_MODEL), lambda i: (goff + i // tiles_per_group, 0, 0))
    return pl.BlockSpec((None, N_MOD, D_MODEL), lambda i, j: (goff + i // tiles_per_group, 0, 0))


def _ffn_kernel(x_ref, mod_ref, g_ref, wg_ref, wu_ref, wo_ref, o_ref, h_sc, acc_sc, *, sub):
    f = pl.program_id(1)

    @pl.when(f == 0)
    def _():
        h = _norm_mod(x_ref[...], g_ref[...], mod_ref[3 * sub:3 * sub + 1, :],
                      mod_ref[3 * sub + 1:3 * sub + 2, :])
        h_sc[...] = h.astype(BF16)
        acc_sc[...] = jnp.zeros_like(acc_sc)

    h = h_sc[...]
    a = _dot(h, wg_ref[...])
    u = _dot(h, wu_ref[...])
    act = (a * _sigmoid(a)) * u
    acc_sc[...] += _dot(act.astype(BF16), wo_ref[...])

    @pl.when(f == pl.num_programs(1) - 1)
    def _():
        o_ref[...] = x_ref[...] + 0.5 * mod_ref[3 * sub + 2:3 * sub + 3, :] * acc_sc[...]


def _ffn_call(x, mod_l, goff, group_tokens, gamma, w_in, w_out, l, s, sub):
    T = x.shape[0]
    tm = min(512, group_tokens)
    tf = 512
    nf = D_FF // tf
    return pl.pallas_call(
        functools.partial(_ffn_kernel, sub=sub),
        grid=(T // tm, nf),
        in_specs=[pl.BlockSpec((tm, D_MODEL), lambda i, f: (i, 0)),
                  _mod_spec(goff, group_tokens // tm, 2),
                  pl.BlockSpec((1, D_MODEL), lambda i, f: (0, 0)),
                  pl.BlockSpec((None, None, D_MODEL, tf), lambda i, f: (l, s, 0, f)),
                  pl.BlockSpec((None, None, D_MODEL, tf), lambda i, f: (l, s, 0, f + nf)),
                  pl.BlockSpec((None, None, tf, D_MODEL), lambda i, f: (l, s, f, 0))],
        out_specs=pl.BlockSpec((tm, D_MODEL), lambda i, f: (i, 0)),
        out_shape=jax.ShapeDtypeStruct((T, D_MODEL), F32),
        scratch_shapes=[pltpu.VMEM((tm, D_MODEL), BF16), pltpu.VMEM((tm, D_MODEL), F32)],
        compiler_params=_params(("parallel", "arbitrary"), 52),
        name="ffn_swiglu",
    )(x, mod_l, gamma, w_in, w_in, w_out)


def _inproj_kernel(x_ref, mod_ref, g_ref, w_ref, o_ref, h_sc, *, col_axis):
    @pl.when(pl.program_id(col_axis) == 0)
    def _():
        h = _norm_mod(x_ref[...], g_ref[...], mod_ref[3:4, :], mod_ref[4:5, :])
        h_sc[...] = h.astype(BF16)

    o_ref[...] = _dot(h_sc[...], w_ref[...])


def _inproj_call(x, mod_l, goff, group_tokens, gamma, w, tn):
    T = x.shape[0]
    N = w.shape[1]
    tm = min(512, group_tokens)
    return pl.pallas_call(
        functools.partial(_inproj_kernel, col_axis=1),
        grid=(T // tm, N // tn),
        in_specs=[pl.BlockSpec((tm, D_MODEL), lambda i, j: (i, 0)),
                  _mod_spec(goff, group_tokens // tm, 2),
                  pl.BlockSpec((1, D_MODEL), lambda i, j: (0, 0)),
                  pl.BlockSpec((D_MODEL, tn), lambda i, j: (0, j))],
        out_specs=pl.BlockSpec((tm, tn), lambda i, j: (i, j)),
        out_shape=jax.ShapeDtypeStruct((T, N), F32),
        scratch_shapes=[pltpu.VMEM((tm, D_MODEL), BF16)],
        compiler_params=_params(("parallel", "arbitrary"), 48),
        name="mixer_inproj",
    )(x, mod_l, gamma, w)


def _rows_of_parity(lane_sc, x, par):
    rows = x.shape[0]
    parts = []
    for c in range(x.shape[1] // LANE):
        lane_sc[c] = x[:, c * LANE:(c + 1) * LANE]
        parts.append(lane_sc[c, pl.ds(par, rows // 2, stride=2), :])
    return jnp.concatenate(parts, axis=1)


def _interleave_rows(lane_sc, even, odd):
    half = even.shape[0]
    parts = []
    for c in range(even.shape[1] // LANE):
        lane_sc[c, pl.ds(0, half, stride=2), :] = even[:, c * LANE:(c + 1) * LANE]
        lane_sc[c, pl.ds(1, half, stride=2), :] = odd[:, c * LANE:(c + 1) * LANE]
        parts.append(lane_sc[c])
    return jnp.concatenate(parts, axis=1)


def _inproj_parity_kernel(x_ref, mod_ref, g_ref, w_ref, o_ref, h_sc, lane_sc, *, half):
    @pl.when(pl.program_id(1) == 0)
    def _():
        h = _norm_mod(x_ref[...], g_ref[...], mod_ref[3:4, :], mod_ref[4:5, :])
        for p in range(2):
            h_sc[p * half:(p + 1) * half, :] = _rows_of_parity(lane_sc, h, p).astype(BF16)

    z = _dot(h_sc[...], w_ref[...])
    o_ref[0] = z[:half]
    o_ref[1] = z[half:]


def _inproj_parity_call(x, mod_l, goff, group_tokens, gamma, w, tn):
    T = x.shape[0]
    N = w.shape[1]
    tm = min(512, group_tokens)
    half = tm // 2
    return pl.pallas_call(
        functools.partial(_inproj_parity_kernel, half=half),
        grid=(T // tm, N // tn),
        in_specs=[pl.BlockSpec((tm, D_MODEL), lambda i, j: (i, 0)),
                  _mod_spec(goff, group_tokens // tm, 2),
                  pl.BlockSpec((1, D_MODEL), lambda i, j: (0, 0)),
                  pl.BlockSpec((D_MODEL, tn), lambda i, j: (0, j))],
        out_specs=pl.BlockSpec((2, half, tn), lambda i, j: (0, i, j)),
        out_shape=jax.ShapeDtypeStruct((2, T // 2, N), F32),
        scratch_shapes=[pltpu.VMEM((tm, D_MODEL), BF16), pltpu.VMEM((D_MODEL // LANE, tm, LANE), F32)],
        compiler_params=_params(("parallel", "arbitrary"), 48),
        name="mixer_inproj_parity",
    )(x, mod_l, gamma, w)


def _outproj_kernel(x_ref, mod_ref, a1_ref, a2_ref, w1_ref, w2_ref, o_ref):
    y = _dot(a1_ref[...], w1_ref[...]) + _dot(a2_ref[...], w2_ref[...])
    o_ref[...] = x_ref[...] + mod_ref[5:6, :] * y


def _outproj_call(x, mod_l, goff, group_tokens, a1, a2, w):
    T = x.shape[0]
    tm = min(512, group_tokens)
    half = D_MODEL // 2
    return pl.pallas_call(
        _outproj_kernel,
        grid=(T // tm,),
        in_specs=[pl.BlockSpec((tm, D_MODEL), lambda i: (i, 0)),
                  _mod_spec(goff, group_tokens // tm, 1),
                  pl.BlockSpec((tm, half), lambda i: (i, 0)),
                  pl.BlockSpec((tm, half), lambda i: (i, 0)),
                  pl.BlockSpec((half, D_MODEL), lambda i: (0, 0)),
                  pl.BlockSpec((half, D_MODEL), lambda i: (1, 0))],
        out_specs=pl.BlockSpec((tm, D_MODEL), lambda i: (i, 0)),
        out_shape=jax.ShapeDtypeStruct((T, D_MODEL), F32),
        compiler_params=_params(("parallel",), 48),
        name="mixer_outproj",
    )(x, mod_l, a1, a2, w, w)


def _outproj_parity_kernel(x_ref, mod_ref, a1_ref, a2_ref, w1_ref, w2_ref, o_ref, lane_sc):
    ys = [_dot(a1_ref[p], w1_ref[...]) + _dot(a2_ref[p], w2_ref[...]) for p in range(2)]
    o_ref[...] = x_ref[...] + mod_ref[5:6, :] * _interleave_rows(lane_sc, ys[0], ys[1])


def _outproj_parity_call(x, mod_l, goff, group_tokens, a, w):
    T = x.shape[0]
    tm = min(512, group_tokens)
    half = tm // 2
    hd = D_MODEL // 2
    return pl.pallas_call(
        _outproj_parity_kernel,
        grid=(T // tm,),
        in_specs=[pl.BlockSpec((tm, D_MODEL), lambda i: (i, 0)),
                  _mod_spec(goff, group_tokens // tm, 1),
                  pl.BlockSpec((2, half, hd), lambda i: (0, i, 0)),
                  pl.BlockSpec((2, half, hd), lambda i: (0, i, 1)),
                  pl.BlockSpec((hd, D_MODEL), lambda i: (0, 0)),
                  pl.BlockSpec((hd, D_MODEL), lambda i: (1, 0))],
        out_specs=pl.BlockSpec((tm, D_MODEL), lambda i: (i, 0)),
        out_shape=jax.ShapeDtypeStruct((T, D_MODEL), F32),
        scratch_shapes=[pltpu.VMEM((D_MODEL // LANE, tm, LANE), F32)],
        compiler_params=_params(("parallel",), 48),
        name="mixer_outproj_parity",
    )(x, mod_l, a, a, w, w)


def _final_norm_kernel(x_ref, g_ref, o_ref):
    x = x_ref[...]
    o_ref[...] = (x * lax.rsqrt(jnp.mean(x * x, axis=-1, keepdims=True) + 1e-6)) * g_ref[...]


def _final_norm_call(x, gamma):
    T = x.shape[0]
    tm = 512
    return pl.pallas_call(
        _final_norm_kernel,
        grid=(T // tm,),
        in_specs=[pl.BlockSpec((tm, D_MODEL), lambda i: (i, 0)),
                  pl.BlockSpec((1, D_MODEL), lambda i: (0, 0))],
        out_specs=pl.BlockSpec((tm, D_MODEL), lambda i: (i, 0)),
        out_shape=jax.ShapeDtypeStruct((T, D_MODEL), F32),
        compiler_params=_params(("parallel",), 32),
        name="final_norm",
    )(x, gamma)


def _shift_prev(cur, halo_prev, row0, seq_len):
    tt = cur.shape[0]
    rid = lax.broadcasted_iota(jnp.int32, (tt, 1), 0)
    pos = jnp.bitwise_and(rid + row0, seq_len - 1)
    prev = pltpu.roll(cur, 1, 0)
    prev = jnp.where(rid == 0, halo_prev[SUBLANE - 1:SUBLANE, :], prev)
    return jnp.where(pos == 0, 0.0, prev)


def _shift_next(cur, halo_next, row0, seq_len):
    tt = cur.shape[0]
    rid = lax.broadcasted_iota(jnp.int32, (tt, 1), 0)
    pos = jnp.bitwise_and(rid + row0, seq_len - 1)
    nxt = pltpu.roll(cur, tt - 1, 0)
    nxt = jnp.where(rid == tt - 1, halo_next[0:1, :], nxt)
    return jnp.where(pos == seq_len - 1, 0.0, nxt)


def _shift_prev_next(cur, halo_prev, halo_next, row0, seq_len):
    return _shift_prev(cur, halo_prev, row0, seq_len), _shift_next(cur, halo_next, row0, seq_len)


def _halo_specs(tt, width, col, total_rows):
    per = tt // SUBLANE
    last = total_rows // SUBLANE - 1
    return [pl.BlockSpec((tt, width), lambda i: (i, col)),
            pl.BlockSpec((SUBLANE, width), lambda i: (jnp.maximum(i * per - 1, 0), col)),
            pl.BlockSpec((SUBLANE, width), lambda i: (jnp.minimum((i + 1) * per, last), col))]


def _segsum(x, ones_blk):
    hi = x.astype(BF16)
    lo = (x - hi.astype(F32)).astype(BF16)
    outs = []
    for g in range(x.shape[1] // MXU_DIM):
        sl = slice(g * MXU_DIM, (g + 1) * MXU_DIM)
        outs.append(_dot(hi[:, sl], ones_blk) + _dot(lo[:, sl], ones_blk))
    return jnp.concatenate(outs, axis=1)


def _rwkv_prep_kernel(z_ref, zp_ref, zn_ref, mup_ref, mun_ref, kk_ref, ka_ref, rk_ref,
                      w0_ref, w2_ref, a0_ref, a2_ref, g2_ref, ones_ref, sel_ref,
                      r_o, v_o, c_o, w0_o, b0_o, k0_o, wc0_o, vk0_o, be0_o,
                      w1_o, b1_o, k1_o, wc1_o, vk1_o, be1_o, bonus_o, g_o, *, tt, seq_len):
    row0 = pl.program_id(0) * tt
    cur = z_ref[...]
    halo_p, halo_n = zp_ref[...], zn_ref[...]
    prev, nxt = _shift_prev_next(cur, halo_p, halo_n, row0, seq_len)
    mup, mun = mup_ref[...], mun_ref[...]
    zs = cur + mup * (prev - cur) + mun * (nxt - cur)
    r = zs[:, 0:D_RWKV]
    k = zs[:, D_RWKV:2 * D_RWKV]
    v = zs[:, 2 * D_RWKV:3 * D_RWKV]
    small = zs[:, 3 * D_RWKV:RWKV_COLS]
    ones_blk = ones_ref[...]

    def unit_keys(keys):
        kk = keys * kk_ref[...]
        return kk / jnp.maximum(jnp.sqrt(_segsum(kk * kk, ones_blk)), 1e-12)

    kk = unit_keys(k)
    c = -kk
    ks = slice(D_RWKV, 2 * D_RWKV)
    hp, hn, mp, mn = halo_p[:, ks], halo_n[:, ks], mup[:, ks], mun[:, ks]
    zc = cur[:, ks]
    k_before = hp[7:8] + mp * (hp[6:7] - hp[7:8]) + mn * (zc[0:1] - hp[7:8])
    k_after = hn[0:1] + mp * (zc[tt - 1:tt] - hn[0:1]) + mn * (hn[1:2] - hn[0:1])
    edge = -unit_keys(jnp.concatenate([k_before, k_after, jnp.zeros((SUBLANE - 2, D_RWKV), F32)], axis=0))
    c_prev = _shift_prev(c, jnp.broadcast_to(edge[0:1], (SUBLANE, D_RWKV)), row0, seq_len)
    c_next = _shift_next(c, jnp.broadcast_to(edge[1:2], (SUBLANE, D_RWKV)), row0, seq_len)

    tw = jnp.tanh(small).astype(BF16)
    sg = _sigmoid(small).astype(BF16)
    xs = small.astype(BF16)
    r_o[...] = r
    v_o[...] = v
    c_o[...] = c
    g_o[...] = _dot(sg, g2_ref[...])

    sel = sel_ref[...]
    bonus = jnp.zeros_like(r)
    outs = ((w0_o, b0_o, k0_o, wc0_o, vk0_o, be0_o, c_next), (w1_o, b1_o, k1_o, wc1_o, vk1_o, be1_o, c_prev))
    for d in range(2):
        wl = -_softplus(-(w0_ref[d:d + 1, :] + _dot(tw, w2_ref[d]))) - 0.5
        a = _sigmoid(a0_ref[d:d + 1, :] + _dot(xs, a2_ref[d]))
        kd = k * (1.0 + (a - 1.0) * ka_ref[...])
        w_o, b_o, k_o, wc_o, vk_o, be_o, c_after = outs[d]
        decay = jnp.exp(-jnp.exp(wl))
        b = kk * a
        w_o[...] = decay
        b_o[...] = b
        k_o[...] = kd
        wc_o[...] = decay * c_after
        vk_o[...] = v * _segsum(kd * c_after, ones_blk)
        bc = b * c_after
        hi = bc.astype(BF16)
        lo = (bc - hi.astype(F32)).astype(BF16)
        be_o[...] = _dot(hi, sel) + _dot(lo, sel)
        bonus = bonus + _segsum(r * kd * rk_ref[...], ones_blk) * v
    bonus_o[...] = bonus


def _rwkv_prep_call(z_r, seq_len, p):
    T = z_r.shape[0]
    tt = min(128, seq_len)
    row = lambda n: pl.BlockSpec((1, n), lambda i: (0, 0))
    full2 = lambda a, b: pl.BlockSpec((a, b), lambda i: (0, 0))
    full3 = lambda a, b, c: pl.BlockSpec((a, b, c), lambda i: (0, 0, 0))
    in_specs = _halo_specs(tt, RWKV_COLS, 0, T) + [
        row(RWKV_COLS), row(RWKV_COLS), row(D_RWKV), row(D_RWKV), row(D_RWKV),
        full2(2, D_RWKV), full3(2, RWKV_SMALL, D_RWKV), full2(2, D_RWKV), full3(2, RWKV_SMALL, D_RWKV),
        full2(RWKV_SMALL, D_RWKV), full2(MXU_DIM, MXU_DIM), full2(D_RWKV, LANE)]
    wide = (pl.BlockSpec((tt, D_RWKV), lambda i: (i, 0)), jax.ShapeDtypeStruct((T, D_RWKV), F32))
    slim = (pl.BlockSpec((tt, LANE), lambda i: (i, 0)), jax.ShapeDtypeStruct((T, LANE), F32))
    outs = [wide] * 3 + ([wide] * 5 + [slim]) * 2 + [wide] * 2
    sel = (jnp.arange(D_RWKV)[:, None] // RWKV_HEAD == jnp.arange(LANE)[None, :]).astype(BF16)
    return pl.pallas_call(
        functools.partial(_rwkv_prep_kernel, tt=tt, seq_len=seq_len),
        grid=(T // tt,),
        in_specs=in_specs,
        out_specs=[o[0] for o in outs],
        out_shape=[o[1] for o in outs],
        compiler_params=_params(("parallel",), 56),
        name="rwkv_prep",
    )(z_r, z_r, z_r, p['mu_prev'], p['mu_next'], p['k_k'], p['k_a'], p['r_k'],
      p['w0'], p['w2'], p['a0'], p['a2'], p['g2'], p['ones_blk'], sel)


N_GRP = D_RWKV // MXU_DIM
HEADS_PER_GRP = H_RWKV // N_GRP


SCAN_ROWS = 16
SCAN_UNROLL = 8


def _scan_head_mask():
    return (jnp.arange(H_RWKV)[:, None] == (jnp.arange(D_RWKV) // RWKV_HEAD)[None, :]).astype(F32)


def _scan_kernel(rf, wcf, wf, bf, kf, vf, vhf, vkf, bef, rb, wcb, wb, bb, kb, vb, vhb, vkb, beb,
                 c0_ref, v0_ref, s0_ref, eye_ref, hm_ref, y0_ref, y1_ref, sfin_ref, st, uv, *, tc):
    j = pl.program_id(1)
    eye = eye_ref[...]
    hmask = hm_ref[...]
    dirs = ((rf, wcf, wf, bf, kf, (vf, vhf), vkf, bef, y0_ref),
            (rb, wcb, wb, bb, kb, (vb, vhb), vkb, beb, y1_ref))
    nt = (((1,), (1,)), ((), ()))
    zrows = jnp.zeros((SCAN_ROWS, D_RWKV), BF16)
    zeye = jnp.zeros((SCAN_ROWS, MXU_DIM), BF16)
    seqs = s0_ref.shape[0]
    chains = [(b, d) for b in range(seqs) for d in range(2)]

    def head_rows(row):
        return (row * hmask).astype(BF16)

    def value_rows(row):
        x = row * hmask
        folded = x[:, 0:MXU_DIM]
        for g in range(1, N_GRP):
            folded = folded + x[:, g * MXU_DIM:(g + 1) * MXU_DIM]
        return folded.astype(BF16)

    def next_value_row(v_refs, b, d, t):
        v_, vh_ = v_refs
        if d == 0:
            inside = v_[b, pl.ds(jnp.minimum(t + 1, tc - 1), 1), :]
            return jnp.where(t == tc - 1, vh_[b, 0:1, :], inside)
        inside = v_[b, pl.ds(jnp.maximum(t - 1, 0), 1), :]
        return jnp.where(t == 0, vh_[b, SUBLANE - 1:SUBLANE, :], inside)

    def state_products(b, d, rows0, eye0, eye1, rows2):
        lhs = jnp.concatenate([st[b, d, g].astype(BF16) for g in range(N_GRP)] + [eye], axis=1)
        wr = jnp.concatenate([
            jnp.concatenate([rows0, eye0], axis=1),
            jnp.concatenate([zrows, eye1], axis=1),
            jnp.concatenate([rows2, zeye], axis=1),
            jnp.concatenate([zrows, zeye], axis=1)], axis=0)
        return lax.dot_general(lhs, wr, nt, preferred_element_type=F32)

    def store_y(y_, b, t, prod):
        tr = prod.T
        y_[b, pl.ds(t, 1), :, :] = tr[2 * SCAN_ROWS:3 * SCAN_ROWS, :].reshape(1, H_RWKV, RWKV_HEAD)

    def time_of(d, s):
        s = jnp.clip(s, 0, tc - 1)
        return s if d == 0 else tc - 1 - s

    @pl.when(j == 0)
    def _():
        st[...] = s0_ref[...]
        for (b, d) in chains:
            uv[b, d] = state_products(b, d, head_rows(c0_ref[b, d:d + 1, :]), zeye,
                                      value_rows(v0_ref[b, d:d + 1, :]), zrows)

    def issue(group, s):
        out = []
        for (b, d) in group:
            r_, wc_, w_, b_, k_, vn_, vk_, be_, y_ = dirs[d]
            t, tp = time_of(d, s), time_of(d, s - 1)
            prod = state_products(b, d, head_rows(wc_[b, pl.ds(t, 1), :]),
                                  value_rows(vk_[b, pl.ds(t, 1), :]),
                                  value_rows(next_value_row(vn_, b, d, t)),
                                  head_rows(r_[b, pl.ds(tp, 1), :]))
            w2 = jnp.concatenate([head_rows(b_[b, pl.ds(t, 1), :]), head_rows(k_[b, pl.ds(t, 1), :]),
                                  zrows, zrows], axis=0)
            out.append((prod, _dot(uv[b, d].astype(BF16), w2)))
        return out

    def finish(group, s, results):
        for (b, d), (prod, upd) in zip(group, results):
            r_, wc_, w_, b_, k_, vn_, vk_, be_, y_ = dirs[d]
            t, tp = time_of(d, s), time_of(d, s - 1)
            wrow = w_[b, pl.ds(t, 1), :]
            for g in range(N_GRP):
                sl = slice(g * MXU_DIM, (g + 1) * MXU_DIM)
                st[b, d, g] = st[b, d, g] * wrow[:, sl] + upd[:, sl]
            store_y(y_, b, tp, prod)
            uv[b, d] = prod + uv[b, d] * be_[b, pl.ds(t, 1), 0:4 * SCAN_ROWS]

    lead, lag = chains[:seqs], chains[seqs:]

    def body(i, carry):
        for u in range(SCAN_UNROLL):
            s = i * SCAN_UNROLL + u
            lead_results = issue(lead, s)
            lag_results = issue(lag, s)
            finish(lead, s, lead_results)
            finish(lag, s, lag_results)
        return carry

    lax.fori_loop(0, tc // SCAN_UNROLL, body, 0)

    for (b, d) in chains:
        t = time_of(d, tc - 1)
        prod = state_products(b, d, zrows, zeye, zeye, head_rows(dirs[d][0][b, pl.ds(t, 1), :]))
        store_y(dirs[d][8], b, t, prod)

    @pl.when(j == pl.num_programs(1) - 1)
    def _():
        sfin_ref[...] = st[...]


def _scan_call(pre, s0, eye):
    B, T, _ = pre['r'].shape
    nb = 4 if B % 4 == 0 else 2
    tc = min(128 if nb == 2 else 32, T)
    nj = T // tc
    fwd = pl.BlockSpec((nb, tc, D_RWKV), lambda bi, j: (bi, j, 0))
    bwd = pl.BlockSpec((nb, tc, D_RWKV), lambda bi, j: (bi, nj - 1 - j, 0))
    yfwd = pl.BlockSpec((nb, tc, H_RWKV, RWKV_HEAD), lambda bi, j: (bi, j, 0, 0))
    ybwd = pl.BlockSpec((nb, tc, H_RWKV, RWKV_HEAD), lambda bi, j: (bi, nj - 1 - j, 0, 0))
    st_spec = pl.BlockSpec((nb, 2, N_GRP, RWKV_HEAD, MXU_DIM), lambda bi, j: (bi, 0, 0, 0, 0))
    y_shape = jax.ShapeDtypeStruct((B, T, H_RWKV, RWKV_HEAD), F32)
    befwd = pl.BlockSpec((nb, tc, LANE), lambda bi, j: (bi, j, 0))
    bebwd = pl.BlockSpec((nb, tc, LANE), lambda bi, j: (bi, nj - 1 - j, 0))
    per = tc // SUBLANE
    last8 = T // SUBLANE - 1
    hfwd = pl.BlockSpec((nb, SUBLANE, D_RWKV), lambda bi, j: (bi, jnp.minimum((j + 1) * per, last8), 0))
    hbwd = pl.BlockSpec((nb, SUBLANE, D_RWKV), lambda bi, j: (bi, jnp.maximum((nj - 1 - j) * per - 1, 0), 0))
    first = pl.BlockSpec((nb, SUBLANE, D_RWKV), lambda bi, j: (bi, 0, 0))

    wc0, vk0, be0, wc1, vk1, be1 = (pre[n] for n in ('wc0', 'vk0', 'be0', 'wc1', 'vk1', 'be1'))
    v = pre['v']
    pad = jnp.zeros((B, SUBLANE - 2, D_RWKV), F32)
    c_first = jnp.concatenate([pre['c'][:, 0:1], pre['c'][:, T - 1:T], pad], axis=1)
    v_first = jnp.concatenate([v[:, 0:1], v[:, T - 1:T], pad], axis=1)
    return pl.pallas_call(
        functools.partial(_scan_kernel, tc=tc),
        grid=(B // nb, nj),
        in_specs=[fwd] * 6 + [hfwd, fwd, befwd] + [bwd] * 6 + [hbwd, bwd, bebwd] + [
            first, first,
            st_spec,
            pl.BlockSpec((RWKV_HEAD, MXU_DIM), lambda bi, j: (0, 0)),
            pl.BlockSpec((H_RWKV, D_RWKV), lambda bi, j: (0, 0))],
        out_specs=[yfwd, ybwd, st_spec],
        out_shape=[y_shape, y_shape, jax.ShapeDtypeStruct((B, 2, N_GRP, RWKV_HEAD, MXU_DIM), F32)],
        scratch_shapes=[pltpu.VMEM((nb, 2, N_GRP, RWKV_HEAD, MXU_DIM), F32),
                        pltpu.VMEM((nb, 2, RWKV_HEAD, 4 * SCAN_ROWS), F32)],
        compiler_params=_params(("arbitrary", "arbitrary"), 52),
        name="rwkv_scan",
    )(pre['r'], wc0, pre['w0'], pre['b0'], pre['k0'], v, v, vk0, be0,
      pre['r'], wc1, pre['w1'], pre['b1'], pre['k1'], v, v, vk1, be1,
      c_first, v_first, s0, eye.astype(BF16), _scan_head_mask())


def _rwkv_post_kernel(y0_ref, y1_ref, bonus_ref, g_ref, gw_ref, gb_ref, ones_ref, o_ref):
    ones_blk = ones_ref[...]
    y = y0_ref[...] + y1_ref[...]
    mu = _segsum(y, ones_blk) * (1.0 / RWKV_HEAD)
    yc = y - mu
    var = _segsum(yc * yc, ones_blk) * (1.0 / RWKV_HEAD)
    yn = yc * lax.rsqrt(var + RWKV_GN_EPS)
    out = (yn * gw_ref[...] + gb_ref[...] + bonus_ref[...]) * g_ref[...]
    o_ref[...] = out.astype(BF16)


def _rwkv_post_call(y0, y1, bonus, g, gn_w, gn_b, ones_blk):
    T = y0.shape[0]
    tt = 512
    blk = pl.BlockSpec((tt, D_RWKV), lambda i: (i, 0))
    row = pl.BlockSpec((1, D_RWKV), lambda i: (0, 0))
    return pl.pallas_call(
        _rwkv_post_kernel,
        grid=(T // tt,),
        in_specs=[blk, blk, blk, blk, row, row, pl.BlockSpec((MXU_DIM, MXU_DIM), lambda i: (0, 0))],
        out_specs=blk,
        out_shape=jax.ShapeDtypeStruct((T, D_RWKV), BF16),
        compiler_params=_params(("parallel",), 40),
        name="rwkv_post",
    )(y0, y1, bonus, g, gn_w, gn_b, ones_blk)


def _rope128(x, cos_t, sin_t):
    return x * cos_t + pltpu.roll(x, QK_ROPE, 1) * sin_t


def _pack_kv(kv, kr_rot, k_o, v_o):
    for h in range(H_MLA):
        k_o[:, h * HEAD_SLOT:h * HEAD_SLOT + QK_NOPE] = kv[:, h * HEAD_SLOT:h * HEAD_SLOT + QK_NOPE].astype(BF16)
        k_o[:, h * HEAD_SLOT + QK_NOPE:(h + 1) * HEAD_SLOT] = kr_rot.astype(BF16)
        v_o[:, h * V_HEAD:(h + 1) * V_HEAD] = kv[:, h * HEAD_SLOT + QK_NOPE:(h + 1) * HEAD_SLOT].astype(BF16)


def _mla_prep_kernel(z_ref, cos_ref, sin_ref, qn_ref, kvn_ref, wq_ref, wkv_ref,
                     q_o, k_o, v_o, ckv_o, kr_o):
    z = z_ref[...]
    cq = z[:, 0:Q_RANK]
    ckv = z[:, Q_RANK:Q_RANK + KV_RANK]
    krp = z[:, Q_RANK + KV_RANK:MLA_COLS]
    cos_t = cos_ref[...]
    sin_t = sin_ref[...]
    cq = (cq * lax.rsqrt(jnp.mean(cq * cq, axis=-1, keepdims=True) + 1e-6)) * qn_ref[...]
    ckv = (ckv * lax.rsqrt(jnp.mean(ckv * ckv, axis=-1, keepdims=True) + 1e-6)) * kvn_ref[...]
    ckv_o[...] = ckv
    kr_o[...] = krp
    q = _dot(cq.astype(BF16), wq_ref[...]) * (ATTN_SCALE * LOG2_E)
    for h in range(H_MLA):
        q_o[:, h * HEAD_SLOT:h * HEAD_SLOT + QK_NOPE] = q[:, h * HEAD_SLOT:h * HEAD_SLOT + QK_NOPE].astype(BF16)
        q_o[:, h * HEAD_SLOT + QK_NOPE:(h + 1) * HEAD_SLOT] = _rope128(
            q[:, h * HEAD_SLOT + QK_NOPE:(h + 1) * HEAD_SLOT], cos_t, sin_t).astype(BF16)
    kv = _dot(ckv.astype(BF16), wkv_ref[...])
    _pack_kv(kv, _rope128(krp, cos_t, sin_t), k_o, v_o)


def _mla_prep_call(z_m, cos_t, sin_t, seq_len, p):
    T = z_m.shape[0]
    tm = min(512, seq_len)
    per_seq = seq_len // tm
    row = lambda n: pl.BlockSpec((1, n), lambda i: (0, 0))
    blk = lambda n: pl.BlockSpec((tm, n), lambda i: (i, 0))
    tab = pl.BlockSpec((tm, LANE), lambda i: (i % per_seq, 0))
    return pl.pallas_call(
        _mla_prep_kernel,
        grid=(T // tm,),
        in_specs=[blk(MLA_COLS), tab, tab, row(Q_RANK), row(KV_RANK),
                  pl.BlockSpec((Q_RANK, H_MLA * HEAD_SLOT), lambda i: (0, 0)),
                  pl.BlockSpec((KV_RANK, H_MLA * HEAD_SLOT), lambda i: (0, 0))],
        out_specs=[blk(H_MLA * HEAD_SLOT), blk(H_MLA * HEAD_SLOT), blk(H_MLA * V_HEAD),
                   blk(KV_RANK), blk(LANE)],
        out_shape=[jax.ShapeDtypeStruct((T, H_MLA * HEAD_SLOT), BF16),
                   jax.ShapeDtypeStruct((T, H_MLA * HEAD_SLOT), BF16),
                   jax.ShapeDtypeStruct((T, H_MLA * V_HEAD), BF16),
                   jax.ShapeDtypeStruct((T, KV_RANK), F32),
                   jax.ShapeDtypeStruct((T, LANE), F32)],
        compiler_params=_params(("parallel",), 48),
        name="mla_prep",
    )(z_m, cos_t, sin_t, p['q_norm'], p['kv_norm'], p['w_qb'], p['w_kvb'])


def _ctx_kv_kernel(ckv_ref, kr_ref, wkv_ref, k_o, v_o):
    kv = _dot(ckv_ref[...].astype(BF16), wkv_ref[...])
    _pack_kv(kv, kr_ref[...], k_o, v_o)


def _ctx_kv_call(ckv_ctx, kr_ctx_pad, w_kvb):
    T = ckv_ctx.shape[0]
    tm = min(512, T)
    blk = lambda n: pl.BlockSpec((tm, n), lambda i: (i, 0))
    return pl.pallas_call(
        _ctx_kv_kernel,
        grid=(T // tm,),
        in_specs=[blk(KV_RANK), blk(LANE), pl.BlockSpec((KV_RANK, H_MLA * HEAD_SLOT), lambda i: (0, 0))],
        out_specs=[blk(H_MLA * HEAD_SLOT), blk(H_MLA * V_HEAD)],
        out_shape=[jax.ShapeDtypeStruct((T, H_MLA * HEAD_SLOT), BF16),
                   jax.ShapeDtypeStruct((T, H_MLA * V_HEAD), BF16)],
        compiler_params=_params(("parallel",), 32),
        name="mla_ctx_kv",
    )(ckv_ctx, kr_ctx_pad, w_kvb)


ATTN_ROWS = 64


def _attn_kernel(q_ref, k_ref, v_ref, o_ref, s_sc, p_sc, *, kc, heads):
    tq, tk = q_ref.shape[0], s_sc.shape[1]
    hq = tq // 2
    nt = (((1,), (1,)), ((), ()))
    units = [(hd, half) for hd in range(heads) for half in range(2)]

    def rows_of(hd, half):
        return hd * tq + half * hq

    def scores(hd, half):
        q = q_ref[half * hq:(half + 1) * hq, hd * HEAD_SLOT:(hd + 1) * HEAD_SLOT]
        r0 = rows_of(hd, half)
        for c in range(tk // kc):
            s_sc[r0:r0 + hq, c * kc:(c + 1) * kc] = lax.dot_general(
                q, k_ref[c * kc:(c + 1) * kc, hd * HEAD_SLOT:(hd + 1) * HEAD_SLOT], nt,
                preferred_element_type=F32)

    def softmax(hd, half):
        sums = []
        for r in range(hq // ATTN_ROWS):
            r0 = rows_of(hd, half) + r * ATTN_ROWS
            rows = slice(r0, r0 + ATTN_ROWS)
            mpart = s_sc[rows, 0:LANE]
            for t in range(1, tk // LANE):
                mpart = jnp.maximum(mpart, s_sc[rows, t * LANE:(t + 1) * LANE])
            m = jnp.max(mpart, axis=-1, keepdims=True)
            lpart = jnp.zeros((ATTN_ROWS, LANE), F32)
            for t in range(tk // LANE):
                p = jnp.exp2(s_sc[rows, t * LANE:(t + 1) * LANE] - m)
                lpart = lpart + p
                p_sc[rows, t * LANE:(t + 1) * LANE] = p.astype(BF16)
            sums.append(jnp.sum(lpart, axis=-1, keepdims=True))
        return jnp.concatenate(sums, axis=0)

    def weighted_values(hd, half, l):
        r0 = rows_of(hd, half)
        acc = jnp.zeros((hq, V_HEAD), F32)
        for c in range(tk // kc):
            acc = acc + _dot(p_sc[r0:r0 + hq, c * kc:(c + 1) * kc],
                             v_ref[c * kc:(c + 1) * kc, hd * V_HEAD:(hd + 1) * V_HEAD])
        o_ref[half * hq:(half + 1) * hq, hd * V_HEAD:(hd + 1) * V_HEAD] = (acc / l).astype(BF16)

    for u in units:
        scores(*u)
    for u in units:
        weighted_values(*u, softmax(*u))


def _attn_call(q, k, v):
    B, Tq, _ = q.shape
    Tk = k.shape[1]
    tq = min(512, Tq)
    heads = H_MLA if Tk <= 512 else 1
    return pl.pallas_call(
        functools.partial(_attn_kernel, kc=MXU_DIM, heads=heads),
        grid=(B, H_MLA // heads, Tq // tq),
        in_specs=[pl.BlockSpec((None, tq, heads * HEAD_SLOT), lambda b, h, i: (b, i, h)),
                  pl.BlockSpec((None, Tk, heads * HEAD_SLOT), lambda b, h, i: (b, 0, h)),
                  pl.BlockSpec((None, Tk, heads * V_HEAD), lambda b, h, i: (b, 0, h))],
        out_specs=pl.BlockSpec((None, tq, heads * V_HEAD), lambda b, h, i: (b, i, h)),
        out_shape=jax.ShapeDtypeStruct((B, Tq, H_MLA * V_HEAD), BF16),
        scratch_shapes=[pltpu.VMEM((heads * tq, Tk), F32), pltpu.VMEM((heads * tq, Tk), BF16)],
        compiler_params=_params(("parallel", "parallel", "arbitrary"), 48),
        name="mla_attention",
    )(q, k, v)


def _conv3_kernel(*refs, tt, half_len):
    ins, (cw_refs, cb_refs), outs = refs[0:12], (refs[12:15], refs[15:18]), refs[18:]
    row0 = pl.program_id(0) * tt
    for s in range(3):
        even, odd = ins[4 * s][...], ins[4 * s + 1][...]
        odd_prev = _shift_prev(odd, ins[4 * s + 2][...], row0, half_len)
        even_next = _shift_next(even, ins[4 * s + 3][...], row0, half_len)
        cw = cw_refs[s][...]
        bias = cb_refs[s][...]
        y_even = cw[0:1, :] * odd_prev + cw[1:2, :] * even + cw[2:3, :] * odd + bias
        y_odd = cw[0:1, :] * even + cw[1:2, :] * odd + cw[2:3, :] * even_next + bias
        outs[s][0] = y_even
        outs[s][1] = y_odd
        if s == 2:
            outs[3][0] = y_even.astype(BF16)
            outs[3][1] = y_odd.astype(BF16)


def _conv3_call(z, half_len, conv_w, conv_b):
    T2 = z.shape[1]
    tt = min(128, half_len)
    per = tt // SUBLANE
    last = T2 // SUBLANE - 1
    in_specs = []
    for s in range(3):
        in_specs += [
            pl.BlockSpec((None, tt, D_HYENA), lambda i, s=s: (0, i, s)),
            pl.BlockSpec((None, tt, D_HYENA), lambda i, s=s: (1, i, s)),
            pl.BlockSpec((None, SUBLANE, D_HYENA), lambda i, s=s: (1, jnp.maximum(i * per - 1, 0), s)),
            pl.BlockSpec((None, SUBLANE, D_HYENA), lambda i, s=s: (0, jnp.minimum((i + 1) * per, last), s))]
    in_specs += [pl.BlockSpec((3, D_HYENA), lambda i, s=s: (0, s)) for s in range(3)]
    in_specs += [pl.BlockSpec((1, D_HYENA), lambda i, s=s: (0, s)) for s in range(3)]
    blk = pl.BlockSpec((2, tt, D_HYENA), lambda i: (0, i, 0))
    f32s = jax.ShapeDtypeStruct((2, T2, D_HYENA), F32)
    return pl.pallas_call(
        functools.partial(_conv3_kernel, tt=tt, half_len=half_len),
        grid=(T2 // tt,),
        in_specs=in_specs,
        out_specs=[blk] * 4,
        out_shape=[f32s, f32s, f32s, jax.ShapeDtypeStruct((2, T2, D_HYENA), BF16)],
        compiler_params=_params(("parallel",), 48),
        name="hyena_conv3",
    )(*([z] * 12), conv_w, conv_w, conv_w, conv_b, conv_b, conv_b)


def _filt_mlp_kernel(z_ref, w1_ref, b1_ref, w2_ref, b2_ref, fr_ref, o_ref):
    h = jnp.sin(fr_ref[0:1, :] * (_dot(z_ref[...].astype(BF16), w1_ref[...].astype(BF16)) + b1_ref[...]))
    h = jnp.sin(fr_ref[1:2, :] * (_dot(h.astype(BF16), w2_ref[...].astype(BF16)) + b2_ref[...]))
    o_ref[...] = h.astype(BF16)


def _filt_mlp_call(zpos, w1p, b1, w2, b2, freq):
    L = zpos.shape[0]
    return pl.pallas_call(
        _filt_mlp_kernel,
        out_shape=jax.ShapeDtypeStruct((L, FILT_HIDDEN), BF16),
        compiler_params=pltpu.CompilerParams(vmem_limit_bytes=32 * MIB),
        name="hyena_filter_mlp",
    )(zpos, w1p, b1, w2, b2, freq)


def _filt_gen_kernel(h_ref, tn_ref, dl_ref, w00, w01, w10, w11, o_ref, taps_sc):
    h = h_ref[...]
    L = h.shape[0]
    win = jnp.exp(-tn_ref[...] * dl_ref[...])
    not_first = lax.broadcasted_iota(jnp.int32, (L, 1), 0) > 0
    ws = ((w00, w01), (w10, w11))

    def emit(k, taps):
        taps_sc[...] = taps
        for par in range(2):
            o_ref[k, par] = taps_sc[pl.ds(par, L // 2, stride=2), :].astype(BF16)

    for n in range(2):
        causal = _dot(h, ws[n][0][...].astype(BF16)) * win
        anti = jnp.where(not_first, _dot(h, ws[n][1][...].astype(BF16)) * win, 0.0)
        norm = (jnp.sum(jnp.abs(causal), axis=0, keepdims=True)
                + jnp.sum(jnp.abs(anti), axis=0, keepdims=True))
        emit(2 * n, causal / norm)
        emit(2 * n + 1, anti / norm)


def _filt_gen_call(h2, tnorm, deltas, w3):
    L = h2.shape[0]
    tc = 128
    nc = D_HYENA // tc
    wspec = lambda k: pl.BlockSpec((FILT_HIDDEN, tc), lambda j, k=k: (0, k * nc + j))
    return pl.pallas_call(
        _filt_gen_kernel,
        grid=(nc,),
        in_specs=[pl.BlockSpec((L, FILT_HIDDEN), lambda j: (0, 0)),
                  pl.BlockSpec((L, 1), lambda j: (0, 0)),
                  pl.BlockSpec((1, tc), lambda j: (0, j)),
                  wspec(0), wspec(1), wspec(2), wspec(3)],
        out_specs=pl.BlockSpec((4, 2, L // 2, tc), lambda j: (0, 0, 0, j)),
        out_shape=jax.ShapeDtypeStruct((4, 2, L // 2, D_HYENA), BF16),
        scratch_shapes=[pltpu.VMEM((L, tc), F32)],
        compiler_params=_params(("parallel",), 48),
        name="hyena_filter_gen",
    )(h2, tnorm, deltas, w3, w3, w3, w3)


def _dft_fwd_kernel(f_ref, u_ref, o_ref):
    o_ref[...] = _dot(f_ref[...], u_ref[...])


def _dft_fwd_call(fmat, u):
    B, K, C = u.shape
    M = fmat.shape[0]
    tm = min(512, M)
    tn = min(C, 2048)
    return pl.pallas_call(
        _dft_fwd_kernel,
        grid=(B, C // tn, M // tm),
        in_specs=[pl.BlockSpec((tm, K), lambda b, j, i: (i, 0)),
                  pl.BlockSpec((None, K, tn), lambda b, j, i: (b, 0, j))],
        out_specs=pl.BlockSpec((None, tm, tn), lambda b, j, i: (b, i, j)),
        out_shape=jax.ShapeDtypeStruct((B, M, C), F32),
        compiler_params=_params(("parallel", "parallel", "arbitrary"), 40),
        name="hyena_dft_fwd",
    )(fmat, u)


def _butterfly(gc, gs, hc, hs, tw_c, tw_s, first):
    tc = hc * tw_c - hs * tw_s
    ts = hs * tw_c + hc * tw_s
    p0 = gc + tc
    p2 = gc - tc
    p1 = jnp.where(first, gs, gs + ts)
    p3 = jnp.where(first, hs, ts - gs)
    return p0, p1, p2, p3


def _cmul(ac, a_s, bc, bs):
    return ac * bc - a_s * bs, ac * bs + a_s * bc


def _filt_planes_kernel(ge_ref, ho_ref, twc_ref, tws_ref, o_ref, *, tr):
    first = (lax.broadcasted_iota(jnp.int32, (tr, 1), 0) + pl.program_id(1) * tr) == 0
    tw_c, tw_s = twc_ref[...], tws_ref[...]
    a = _butterfly(ge_ref[0, 0], ge_ref[0, 1], ho_ref[0, 0], ho_ref[0, 1], tw_c, tw_s, first)
    b = _butterfly(ge_ref[1, 0], ge_ref[1, 1], ho_ref[1, 0], ho_ref[1, 1], tw_c, tw_s, first)
    o_ref[0] = a[0] + b[0]
    o_ref[1] = jnp.where(first, a[1] + b[1], a[1] - b[1])
    o_ref[2] = a[2] + b[2]
    o_ref[3] = a[3] - b[3]


def _filt_planes_call(raw, twc, tws):
    _, L, C = raw.shape
    H = L // 2
    tr = min(256, H)
    tc = 512
    blk = lambda par: pl.BlockSpec((None, 2, None, 2, tr, tc), lambda n, i, j: (n, 0, par, 0, i, j))
    tw = pl.BlockSpec((tr, 1), lambda n, i, j: (i, 0))
    raw6 = raw.reshape(2, 2, 2, 2, H, C)
    return pl.pallas_call(
        functools.partial(_filt_planes_kernel, tr=tr),
        grid=(2, H // tr, C // tc),
        in_specs=[blk(0), blk(1), tw, tw],
        out_specs=pl.BlockSpec((None, 4, tr, tc), lambda n, i, j: (n, 0, i, j)),
        out_shape=jax.ShapeDtypeStruct((2, 4, H, C), F32),
        compiler_params=_params(("parallel", "parallel", "parallel"), 40),
        name="hyena_filter_planes",
    )(raw6, raw6, twc, tws)


def _spec_mul_kernel(raw_ref, k_ref, twc_ref, tws_ref, o_ref, *, tr):
    first = (lax.broadcasted_iota(jnp.int32, (tr, 1), 0) + pl.program_id(1) * tr) == 0
    tw_c, tw_s = twc_ref[...], tws_ref[...]
    p0, p1, p2, p3 = _butterfly(raw_ref[0, 0], raw_ref[0, 1], raw_ref[1, 0], raw_ref[1, 1], tw_c, tw_s, first)
    k0, k1, k2, k3 = k_ref[0], k_ref[1], k_ref[2], k_ref[3]
    yac, yas = _cmul(p0, p1, k0, k1)
    ybc, ybs = _cmul(p2, p3, k2, k3)
    ymc, yms = _cmul(p1, p3, k1, k3)
    yac = jnp.where(first, p0 * k0, yac)
    ybc = jnp.where(first, p2 * k2, ybc)
    dc = yac - ybc
    ds = yas + ybs
    o_ref[0, 0] = (yac + ybc).astype(BF16)
    o_ref[0, 1] = jnp.where(first, ymc, yas - ybs).astype(BF16)
    o_ref[1, 0] = jnp.where(first, dc, dc * tw_c + ds * tw_s).astype(BF16)
    o_ref[1, 1] = jnp.where(first, yms, ds * tw_c - dc * tw_s).astype(BF16)


def _spec_mul_call(raw, k_planes, order, twc, tws):
    B2, L, C = raw.shape
    B, H = B2 // 2, L // 2
    tr = min(256, H)
    tc = 512
    tw = pl.BlockSpec((tr, 1), lambda b, i, j: (i, 0))
    out = pl.pallas_call(
        functools.partial(_spec_mul_kernel, tr=tr),
        grid=(B, H // tr, C // tc),
        in_specs=[pl.BlockSpec((2, None, 2, tr, tc), lambda b, i, j: (0, b, 0, i, j)),
                  pl.BlockSpec((None, 4, tr, tc), lambda b, i, j: (order, 0, i, j)),
                  tw, tw],
        out_specs=pl.BlockSpec((2, None, 2, tr, tc), lambda b, i, j: (0, b, 0, i, j)),
        out_shape=jax.ShapeDtypeStruct((2, B, 2, H, C), BF16),
        compiler_params=_params(("parallel", "parallel", "parallel"), 40),
        name="hyena_spectral_mul",
    )(raw.reshape(2, B, 2, H, C), k_planes, twc, tws)
    return out.reshape(B2, L, C)


def _dft_inv_kernel(f_ref, y_ref, gate_ref, u_ref, bias_ref, o_ref, ob_ref):
    conv = _dot(f_ref[...], y_ref[...])
    out = gate_ref[...] * (conv + u_ref[...] * bias_ref[...])
    o_ref[...] = out
    ob_ref[...] = out.astype(BF16)


def _dft_inv_call(imat, y_spec, gate, u, bias):
    B, M, C = u.shape
    K = imat.shape[1]
    tm = min(512, M)
    tn = 1024 if K > 1024 else min(C, 2048)
    blk = pl.BlockSpec((None, tm, tn), lambda b, j, i: (b, i, j))
    return pl.pallas_call(
        _dft_inv_kernel,
        grid=(B, C // tn, M // tm),
        in_specs=[pl.BlockSpec((tm, K), lambda b, j, i: (i, 0)),
                  pl.BlockSpec((None, K, tn), lambda b, j, i: (b, 0, j)),
                  blk, blk, pl.BlockSpec((1, tn), lambda b, j, i: (0, j))],
        out_specs=[blk, blk],
        out_shape=[jax.ShapeDtypeStruct((B, M, C), F32), jax.ShapeDtypeStruct((B, M, C), BF16)],
        compiler_params=_params(("parallel", "parallel", "arbitrary"), 48),
        name="hyena_dft_inv",
    )(imat, y_spec, gate, u, bias)


def _dft_tables(L):
    H = L // 2
    lo = min(64, H)
    hi = H // lo
    g = jnp.arange(H, dtype=jnp.int32)
    theta = 2.0 * math.pi / L
    ang_hi = ((g[:, None] * (jnp.arange(hi, dtype=jnp.int32) * lo)[None, :]) % L).astype(F32) * theta
    ang_lo = ((g[:, None] * jnp.arange(lo, dtype=jnp.int32)[None, :]) % L).astype(F32) * theta
    ch, sh, cl, sl = jnp.cos(ang_hi), jnp.sin(ang_hi), jnp.cos(ang_lo), jnp.sin(ang_lo)
    cos_m = (ch[:, :, None] * cl[:, None, :] - sh[:, :, None] * sl[:, None, :]).reshape(H, H)
    sin_m = (sh[:, :, None] * cl[:, None, :] + ch[:, :, None] * sl[:, None, :]).reshape(H, H)
    alt = jnp.where(jnp.arange(H) % 2 == 0, 1.0, -1.0).astype(F32)
    sin_m = jnp.where((g == 0)[:, None], alt[None, :], sin_m)
    fwd = jnp.concatenate([cos_m, sin_m], axis=0)
    n = 2.0 * L
    w_cos = jnp.where(g == 0, 1.0 / n, 2.0 / n).astype(F32)
    inv = jnp.concatenate([cos_m * w_cos[:, None], sin_m * (2.0 / n)], axis=0).T
    ang_tw = g.astype(F32) * (math.pi / L)
    return fwd.astype(BF16), inv.astype(BF16), jnp.cos(ang_tw)[:, None], jnp.sin(ang_tw)[:, None]


def _filter_positions(L):
    t = jnp.arange(L, dtype=F32)
    t_norm = t / max(L - 1, 1)
    bands = (POS_EMB - 1) // 2
    freqs = jnp.linspace(1e-4, bands - 1, bands, dtype=F32)
    ang = (2.0 * math.pi / L) * t[:, None] * freqs[None, :]
    z = jnp.concatenate([t_norm[:, None], jnp.cos(ang), -jnp.sin(ang)], axis=-1)
    return jnp.pad(z, ((0, 0), (0, POS_PAD - POS_EMB))), t_norm[:, None]


def _hyena_deltas():
    return jnp.linspace(abs(math.log(HYENA_TARGET)) / SLOW_DECAY_PCT,
                        abs(math.log(HYENA_TARGET)) / FAST_DECAY_PCT, D_HYENA, dtype=F32)[None, :]


def _hyena_filter_spectrum(L, tables, p):
    fwd, _, twc, tws = tables
    zpos, tnorm = _filter_positions(L)
    h2 = _filt_mlp_call(zpos, p['filt_w1'], p['filt_b1'], p['filt_w2'], p['filt_b2'], p['filt_freq'])
    filt = _filt_gen_call(h2, tnorm, _hyena_deltas(), p['filt_w3'])
    raw = _dft_fwd_call(fwd, filt.reshape(8, L // 2, D_HYENA))
    return _filt_planes_call(raw, twc, tws)


def _hyena_mixer(z, B, L, tables, k_planes, p):
    fwd, inv, twc, tws = tables
    H = L // 2
    x1, x2, v, vb = _conv3_call(z, H, p['conv_w'], p['conv_b'])
    shp = (2 * B, H, D_HYENA)
    u, ub = v.reshape(shp), vb.reshape(shp)
    for n, gate in enumerate((x1, x2)):
        raw = _dft_fwd_call(fwd, ub)
        yspec = _spec_mul_call(raw, k_planes, n, twc, tws)
        u, ub = _dft_inv_call(inv, yspec, gate.reshape(shp), u, p['bias'][n:n + 1])
    return ub.reshape(2, B * H, D_HYENA)


def _rope_swap(w):
    q = QK_ROPE // 4
    return jnp.concatenate([w[..., q:2 * q], w[..., 0:q], w[..., 3 * q:4 * q], w[..., 2 * q:3 * q]], axis=-1)


def _pad_cols(w, n):
    return jnp.pad(w, [(0, 0)] * (w.ndim - 1) + [(0, n - w.shape[-1])])


def _pack_even(e, w_in_even, mu_prev, mu_next, rwkv_w0, rwkv_w2, rwkv_a0, rwkv_a2, rwkv_g2,
               rwkv_kk, rwkv_ka, rwkv_rk, rwkv_gn_w, rwkv_gn_b, mla_q_norm, mla_kv_norm,
               mla_w_qb, mla_w_kvb, w_out_even):
    n_r = 3 * D_RWKV + W_LORA + A_LORA + G_LORA
    w_in = w_in_even[e]
    w_r = _pad_cols(w_in[:, :n_r], RWKV_COLS).astype(BF16)
    w_m = w_in[:, n_r:]
    kr_cols = w_m[:, Q_RANK + KV_RANK:]
    w_m = jnp.concatenate([w_m, _rope_swap(kr_cols)], axis=-1).astype(BF16)
    small_rows = lambda w, off: jnp.pad(w, [(0, 0)] * (w.ndim - 2)
                                        + [(off, RWKV_SMALL - off - w.shape[-2]), (0, 0)]).astype(BF16)
    wq = mla_w_qb[e].reshape(Q_RANK, H_MLA, QK_NOPE + QK_ROPE)
    wq = jnp.concatenate([wq, _rope_swap(wq[..., QK_NOPE:])], axis=-1).reshape(Q_RANK, H_MLA * HEAD_SLOT)
    blk = jnp.arange(MXU_DIM) // RWKV_HEAD
    return {
        'w_r': w_r, 'w_m': w_m,
        'mu_prev': _pad_cols(mu_prev[e][None, :], RWKV_COLS),
        'mu_next': _pad_cols(mu_next[e][None, :], RWKV_COLS),
        'k_k': rwkv_kk[e][None, :], 'k_a': rwkv_ka[e][None, :],
        'r_k': rwkv_rk[e].reshape(1, D_RWKV),
        'w0': rwkv_w0[e], 'w2': small_rows(rwkv_w2[e], 0),
        'a0': rwkv_a0[e], 'a2': small_rows(rwkv_a2[e], W_LORA),
        'g2': small_rows(rwkv_g2[e], W_LORA + A_LORA),
        'gn_w': rwkv_gn_w[e][None, :], 'gn_b': rwkv_gn_b[e][None, :],
        'q_norm': mla_q_norm[e][None, :], 'kv_norm': mla_kv_norm[e][None, :],
        'w_qb': wq.astype(BF16), 'w_kvb': mla_w_kvb[e].astype(BF16),
        'w_out': w_out_even[e].astype(BF16),
        'ones_blk': (blk[:, None] == blk[None, :]).astype(BF16),
        'eye': (jnp.arange(RWKV_HEAD)[:, None] == (jnp.arange(MXU_DIM) % RWKV_HEAD)[None, :]).astype(F32),
    }


def _rope_tables(L):
    rows = L // GRID_W
    row = jnp.repeat(jnp.arange(rows, dtype=F32), GRID_W)
    col = jnp.tile(jnp.arange(GRID_W, dtype=F32), rows)
    half = QK_ROPE // 2
    inv = 1.0 / (ROPE_THETA ** (jnp.arange(0, half, 2, dtype=F32) / half))
    ar, ac = row[:, None] * inv[None, :], col[:, None] * inv[None, :]
    cos_t = jnp.concatenate([jnp.cos(ar), jnp.cos(ar), jnp.cos(ac), jnp.cos(ac)], axis=-1)
    sin_t = jnp.concatenate([-jnp.sin(ar), jnp.sin(ar), -jnp.sin(ac), jnp.sin(ac)], axis=-1)
    return _pad_cols(cos_t, LANE), _pad_cols(sin_t, LANE)


def _state_to_groups(s):
    B = s.shape[0]
    s = s.reshape(B, 2, N_GRP, H_RWKV // N_GRP, RWKV_HEAD, RWKV_HEAD)
    return jnp.swapaxes(s, 3, 4).reshape(B, 2, N_GRP, RWKV_HEAD, MXU_DIM)


def _groups_to_state(s):
    B = s.shape[0]
    s = s.reshape(B, 2, N_GRP, RWKV_HEAD, H_RWKV // N_GRP, RWKV_HEAD)
    return jnp.swapaxes(s, 3, 4).reshape(B, 2, H_RWKV, RWKV_HEAD, RWKV_HEAD)


def _even_mixer(x, mod_l, goff, B, L, gamma, p, rope, ctx):
    group_tokens = x.shape[0] if ctx is None else L
    z_r = _inproj_call(x, mod_l, goff, group_tokens, gamma, p['w_r'], RWKV_COLS // 3)
    z_m = _inproj_call(x, mod_l, goff, group_tokens, gamma, p['w_m'], MLA_COLS)
    names = ('r', 'v', 'c', 'w0', 'b0', 'k0', 'wc0', 'vk0', 'be0', 'w1', 'b1', 'k1', 'wc1', 'vk1', 'be1',
             'bonus', 'g')
    pre = dict(zip(names, _rwkv_prep_call(z_r, L, p)))
    seq = {n: pre[n].reshape(B, L, pre[n].shape[-1]) for n in names[:15]}
    if ctx is None:
        s0 = jnp.zeros((B, 2, N_GRP, RWKV_HEAD, MXU_DIM), F32)
    else:
        s0 = _state_to_groups(ctx[2].astype(F32))
    y0, y1, s_fin = _scan_call(seq, s0, p['eye'])
    y_r = _rwkv_post_call(y0.reshape(B * L, D_RWKV), y1.reshape(B * L, D_RWKV), pre['bonus'], pre['g'],
                          p['gn_w'], p['gn_b'], p['ones_blk'])

    q, k, v, ckv, krp = _mla_prep_call(z_m, rope[0], rope[1], L, p)
    q = q.reshape(B, L, H_MLA * HEAD_SLOT)
    k = k.reshape(B, L, H_MLA * HEAD_SLOT)
    v = v.reshape(B, L, H_MLA * V_HEAD)
    if ctx is not None:
        P = ctx[0].shape[1]
        k_ctx, v_ctx = _ctx_kv_call(ctx[0].reshape(B * P, KV_RANK),
                                    _pad_cols(ctx[1].reshape(B * P, QK_ROPE), LANE), p['w_kvb'])
        k = jnp.concatenate([k, k_ctx.reshape(B, P, H_MLA * HEAD_SLOT)], axis=1)
        v = jnp.concatenate([v, v_ctx.reshape(B, P, H_MLA * V_HEAD)], axis=1)
    y_m = _attn_call(q, k, v).reshape(B * L, H_MLA * V_HEAD)
    x = _outproj_call(x, mod_l, goff, group_tokens, y_r, y_m, p['w_out'])
    state = (_groups_to_state(s_fin), ckv.reshape(B, L, KV_RANK), krp[:, :QK_ROPE].reshape(B, L, QK_ROPE))
    return x, state


def _odd_mixer(x, mod_l, goff, group_tokens, B, L, gamma, tables, k_planes, p):
    z = _inproj_parity_call(x, mod_l, goff, group_tokens, gamma, p['w_in'], 1536)
    y = _hyena_mixer(z, B, L, tables, k_planes, p)
    return _outproj_parity_call(x, mod_l, goff, group_tokens, y, p['w_out'])


def kernel(x_prompt, x_sample, cache_mla_ckv, cache_mla_krope, state_rwkv, c, c_ctx,
           w_mod, b_mod, norm_g, w_ffn_in, w_ffn_out, final_norm_g,
           w_in_even, mu_prev, mu_next, rwkv_w0, rwkv_w2, rwkv_a0, rwkv_a2, rwkv_g2,
           rwkv_kk, rwkv_ka, rwkv_rk, rwkv_gn_w, rwkv_gn_b,
           mla_q_norm, mla_kv_norm, mla_w_qb, mla_w_kvb, w_out_even,
           w_in_odd, hy_conv_w, hy_conv_b, hy_filt_w1, hy_filt_b1, hy_filt_w2, hy_filt_b2,
           hy_filt_w3, hy_filt_freq, hy_bias, w_out_odd):
    Bp, Lp, D = x_prompt.shape
    Bs, Ls, _ = x_sample.shape
    depth = w_mod.shape[0]
    xp = x_prompt.reshape(Bp * Lp, D)
    xs = x_sample.reshape(Bs * Ls, D)
    Tp = Bp * Lp

    cvec = jnp.concatenate([c_ctx[None, :], c, jnp.zeros((SUBLANE - 1 - Bs, D), F32)], axis=0)
    mod = _mod_call(cvec, w_mod, b_mod)

    rope_p = (_pad_cols(jnp.ones((Lp, QK_ROPE), F32), LANE), jnp.zeros((Lp, LANE), F32))
    rope_s = _rope_tables(Ls)
    tabs_p = tabs_s = None
    w_in = w_ffn_in.astype(BF16)
    w_out = w_ffn_out.astype(BF16)
    new_ckv, new_kr, new_s = [], [], []
    for l in range(depth):
        mod_l = mod[l]
        gam = [norm_g[l, s][None, :] for s in range(3)]
        xp = _ffn_call(xp, mod_l, 0, Tp, gam[0], w_in, w_out, l, 0, 0)
        xs = _ffn_call(xs, mod_l, 1, Ls, gam[0], w_in, w_out, l, 0, 0)
        if l % 2 == 0:
            e = l // 2
            p = _pack_even(e, w_in_even, mu_prev, mu_next, rwkv_w0, rwkv_w2, rwkv_a0, rwkv_a2, rwkv_g2,
                           rwkv_kk, rwkv_ka, rwkv_rk, rwkv_gn_w, rwkv_gn_b, mla_q_norm, mla_kv_norm,
                           mla_w_qb, mla_w_kvb, w_out_even)
            ctx = (cache_mla_ckv[:, e], cache_mla_krope[:, e], state_rwkv[:, e])
            xp, st = _even_mixer(xp, mod_l, 0, Bp, Lp, gam[1], p, rope_p, None)
            xs, _ = _even_mixer(xs, mod_l, 1, Bs, Ls, gam[1], p, rope_s, ctx)
            new_s.append(st[0].astype(x_prompt.dtype))
            new_ckv.append(st[1])
            new_kr.append(st[2])
        else:
            o = l // 2
            p = {'w_in': w_in_odd[o].astype(BF16), 'conv_w': hy_conv_w[o], 'conv_b': hy_conv_b[o][None, :],
                 'filt_w1': jnp.pad(hy_filt_w1[o], ((0, POS_PAD - POS_EMB), (0, 0))),
                 'filt_b1': hy_filt_b1[o][None, :], 'filt_w2': hy_filt_w2[o],
                 'filt_b2': hy_filt_b2[o][None, :], 'filt_w3': hy_filt_w3[o],
                 'filt_freq': hy_filt_freq[o], 'bias': hy_bias[o], 'w_out': w_out_odd[o].astype(BF16)}
            if tabs_p is None:
                tabs_p, tabs_s = _dft_tables(Lp), _dft_tables(Ls)
            ks_p = _hyena_filter_spectrum(Lp, tabs_p, p)
            ks_s = _hyena_filter_spectrum(Ls, tabs_s, p)
            xp = _odd_mixer(xp, mod_l, 0, Tp, Bp, Lp, gam[1], tabs_p, ks_p, p)
            xs = _odd_mixer(xs, mod_l, 1, Ls, Bs, Ls, gam[1], tabs_s, ks_s, p)
        xp = _ffn_call(xp, mod_l, 0, Tp, gam[2], w_in, w_out, l, 1, 2)
        xs = _ffn_call(xs, mod_l, 1, Ls, gam[2], w_in, w_out, l, 1, 2)

    fg = final_norm_g[None, :]
    y_prompt = _final_norm_call(xp, fg).reshape(Bp, Lp, D)
    y_sample = _final_norm_call(xs, fg).reshape(Bs, Ls, D)
    return (y_prompt, y_sample, jnp.stack(new_ckv, axis=1), jnp.stack(new_kr, axis=1),
            jnp.stack(new_s, axis=1))
```

```python
import functools
import math

import jax
import jax.numpy as jnp
from jax import lax
from jax.experimental import pallas as pl
from jax.experimental.pallas import tpu as pltpu

F32 = jnp.float32
BF16 = jnp.bfloat16

D_MODEL = 2048
N_MOD = 9
D_FF = 5632
D_RWKV = 1024
RWKV_HEAD = 64
H_RWKV = 16
W_LORA = 64
A_LORA = 64
G_LORA = 160
RWKV_SMALL = 384
RWKV_COLS = 3 * D_RWKV + RWKV_SMALL
RWKV_GN_EPS = 64e-5
H_MLA = 8
QK_NOPE = 128
QK_ROPE = 64
V_HEAD = 128
Q_RANK = 512
KV_RANK = 256
MLA_COLS = Q_RANK + KV_RANK + 2 * QK_ROPE
HEAD_SLOT = 256
ROPE_THETA = 10000.0
ATTN_SCALE = (QK_NOPE + QK_ROPE) ** -0.5
LOG2_E = 1.0 / math.log(2.0)
GRID_W = 64
D_HYENA = 2048
POS_EMB = 33
POS_PAD = 128
FILT_HIDDEN = 64
HYENA_TARGET = 1e-2
FAST_DECAY_PCT = 0.3
SLOW_DECAY_PCT = 1.5
LANE = 128
SUBLANE = 8
MXU_DIM = 256
MIB = 1024 * 1024


def _params(sem, vmem_mib):
    return pltpu.CompilerParams(dimension_semantics=sem, vmem_limit_bytes=vmem_mib * MIB)


def _sigmoid(x):
    return 1.0 / (1.0 + jnp.exp(-x))


def _softplus(x):
    return jnp.maximum(x, 0.0) + jnp.log(1.0 + jnp.exp(-jnp.abs(x)))


def _dot(a, b):
    return jnp.dot(a, b, preferred_element_type=F32)


def _norm_mod(x, gamma, shift, scale):
    xn = x * lax.rsqrt(jnp.mean(x * x, axis=-1, keepdims=True) + 1e-6)
    return xn * (gamma * (1.0 + scale)) + shift


def _mod_kernel(c_ref, w_ref, b_ref, o_ref):
    c = c_ref[...]
    s = c * _sigmoid(c)
    o_ref[0] = _dot(s.astype(BF16), w_ref[0].astype(BF16)) + b_ref[0]


def _mod_call(cvec, w_mod, b_mod):
    L, Dm, N = w_mod.shape
    tn = 1024
    out = pl.pallas_call(
        _mod_kernel,
        grid=(L, N // tn),
        in_specs=[pl.BlockSpec((SUBLANE, Dm), lambda l, j: (0, 0)),
                  pl.BlockSpec((1, Dm, tn), lambda l, j: (l, 0, j)),
                  pl.BlockSpec((1, 1, tn), lambda l, j: (l, 0, j))],
        out_specs=pl.BlockSpec((1, SUBLANE, tn), lambda l, j: (l, 0, j)),
        out_shape=jax.ShapeDtypeStruct((L, SUBLANE, N), F32),
        compiler_params=_params(("arbitrary", "arbitrary"), 40),
        name="adaln_mod",
    )(cvec, w_mod, b_mod.reshape(L, 1, N))
    return out.reshape(L, SUBLANE, N_MOD, Dm)


def _mod_spec(goff, tiles_per_group, nargs):
    if nargs == 1:
        return pl.BlockSpec((None, N_MOD, D_MODEL), lambda i: (goff + i // tiles_per_group, 0, 0))
    return pl.BlockSpec((None, N_MOD, D_MODEL), lambda i, j: (goff + i // tiles_per_group, 0, 0))


def _ffn_kernel(x_ref, mod_ref, g_ref, wg_ref, wu_ref, wo_ref, o_ref, h_sc, acc_sc, *, sub):
    f = pl.program_id(1)

    @pl.when(f == 0)
    def _():
        h = _norm_mod(x_ref[...], g_ref[...], mod_ref[3 * sub:3 * sub + 1, :],
                      mod_ref[3 * sub + 1:3 * sub + 2, :])
        h_sc[...] = h.astype(BF16)
        acc_sc[...] = jnp.zeros_like(acc_sc)

    h = h_sc[...]
    a = _dot(h, wg_ref[...])
    u = _dot(h, wu_ref[...])
    act = (a * _sigmoid(a)) * u
    acc_sc[...] += _dot(act.astype(BF16), wo_ref[...])

    @pl.when(f == pl.num_programs(1) - 1)
    def _():
        o_ref[...] = x_ref[...] + 0.5 * mod_ref[3 * sub + 2:3 * sub + 3, :] * acc_sc[...]


def _ffn_call(x, mod_l, goff, group_tokens, gamma, w_in, w_out, l, s, sub):
    T = x.shape[0]
    tm = min(512, group_tokens)
    tf = 512
    nf = D_FF // tf
    return pl.pallas_call(
        functools.partial(_ffn_kernel, sub=sub),
        grid=(T // tm, nf),
        in_specs=[pl.BlockSpec((tm, D_MODEL), lambda i, f: (i, 0)),
                  _mod_spec(goff, group_tokens // tm, 2),
                  pl.BlockSpec((1, D_MODEL), lambda i, f: (0, 0)),
                  pl.BlockSpec((None, None, D_MODEL, tf), lambda i, f: (l, s, 0, f)),
                  pl.BlockSpec((None, None, D_MODEL, tf), lambda i, f: (l, s, 0, f + nf)),
                  pl.BlockSpec((None, None, tf, D_MODEL), lambda i, f: (l, s, f, 0))],
        out_specs=pl.BlockSpec((tm, D_MODEL), lambda i, f: (i, 0)),
        out_shape=jax.ShapeDtypeStruct((T, D_MODEL), F32),
        scratch_shapes=[pltpu.VMEM((tm, D_MODEL), BF16), pltpu.VMEM((tm, D_MODEL), F32)],
        compiler_params=_params(("parallel", "arbitrary"), 52),
        name="ffn_swiglu",
    )(x, mod_l, gamma, w_in, w_in, w_out)


def _inproj_kernel(x_ref, mod_ref, g_ref, w_ref, o_ref, h_sc, *, col_axis):
    @pl.when(pl.program_id(col_axis) == 0)
    def _():
        h = _norm_mod(x_ref[...], g_ref[...], mod_ref[3:4, :], mod_ref[4:5, :])
        h_sc[...] = h.astype(BF16)

    o_ref[...] = _dot(h_sc[...], w_ref[...])


def _inproj_call(x, mod_l, goff, group_tokens, gamma, w, tn):
    T = x.shape[0]
    N = w.shape[1]
    tm = min(512, group_tokens)
    return pl.pallas_call(
        functools.partial(_inproj_kernel, col_axis=1),
        grid=(T // tm, N // tn),
        in_specs=[pl.BlockSpec((tm, D_MODEL), lambda i, j: (i, 0)),
                  _mod_spec(goff, group_tokens // tm, 2),
                  pl.BlockSpec((1, D_MODEL), lambda i, j: (0, 0)),
                  pl.BlockSpec((D_MODEL, tn), lambda i, j: (0, j))],
        out_specs=pl.BlockSpec((tm, tn), lambda i, j: (i, j)),
        out_shape=jax.ShapeDtypeStruct((T, N), F32),
        scratch_shapes=[pltpu.VMEM((tm, D_MODEL), BF16)],
        compiler_params=_params(("parallel", "arbitrary"), 48),
        name="mixer_inproj",
    )(x, mod_l, gamma, w)


def _rows_of_parity(lane_sc, x, par):
    rows = x.shape[0]
    parts = []
    for c in range(x.shape[1] // LANE):
        lane_sc[c] = x[:, c * LANE:(c + 1) * LANE]
        parts.append(lane_sc[c, pl.ds(par, rows // 2, stride=2), :])
    return jnp.concatenate(parts, axis=1)


def _interleave_rows(lane_sc, even, odd):
    half = even.shape[0]
    parts = []
    for c in range(even.shape[1] // LANE):
        lane_sc[c, pl.ds(0, half, stride=2), :] = even[:, c * LANE:(c + 1) * LANE]
        lane_sc[c, pl.ds(1, half, stride=2), :] = odd[:, c * LANE:(c + 1) * LANE]
        parts.append(lane_sc[c])
    return jnp.concatenate(parts, axis=1)


def _inproj_parity_kernel(x_ref, mod_ref, g_ref, w_ref, o_ref, h_sc, lane_sc, *, half):
    @pl.when(pl.program_id(1) == 0)
    def _():
        h = _norm_mod(x_ref[...], g_ref[...], mod_ref[3:4, :], mod_ref[4:5, :])
        for p in range(2):
            h_sc[p * half:(p + 1) * half, :] = _rows_of_parity(lane_sc, h, p).astype(BF16)

    z = _dot(h_sc[...], w_ref[...])
    o_ref[0] = z[:half]
    o_ref[1] = z[half:]


def _inproj_parity_call(x, mod_l, goff, group_tokens, gamma, w, tn):
    T = x.shape[0]
    N = w.shape[1]
    tm = min(512, group_tokens)
    half = tm // 2
    return pl.pallas_call(
        functools.partial(_inproj_parity_kernel, half=half),
        grid=(T // tm, N // tn),
        in_specs=[pl.BlockSpec((tm, D_MODEL), lambda i, j: (i, 0)),
                  _mod_spec(goff, group_tokens // tm, 2),
                  pl.BlockSpec((1, D_MODEL), lambda i, j: (0, 0)),
                  pl.BlockSpec((D_MODEL, tn), lambda i, j: (0, j))],
        out_specs=pl.BlockSpec((2, half, tn), lambda i, j: (0, i, j)),
        out_shape=jax.ShapeDtypeStruct((2, T // 2, N), F32),
        scratch_shapes=[pltpu.VMEM((tm, D_MODEL), BF16), pltpu.VMEM((D_MODEL // LANE, tm, LANE), F32)],
        compiler_params=_params(("parallel", "arbitrary"), 48),
        name="mixer_inproj_parity",
    )(x, mod_l, gamma, w)


def _outproj_kernel(x_ref, mod_ref, a1_ref, a2_ref, w1_ref, w2_ref, o_ref):
    y = _dot(a1_ref[...], w1_ref[...]) + _dot(a2_ref[...], w2_ref[...])
    o_ref[...] = x_ref[...] + mod_ref[5:6, :] * y


def _outproj_call(x, mod_l, goff, group_tokens, a1, a2, w):
    T = x.shape[0]
    tm = min(512, group_tokens)
    half = D_MODEL // 2
    return pl.pallas_call(
        _outproj_kernel,
        grid=(T // tm,),
        in_specs=[pl.BlockSpec((tm, D_MODEL), lambda i: (i, 0)),
                  _mod_spec(goff, group_tokens // tm, 1),
                  pl.BlockSpec((tm, half), lambda i: (i, 0)),
                  pl.BlockSpec((tm, half), lambda i: (i, 0)),
                  pl.BlockSpec((half, D_MODEL), lambda i: (0, 0)),
                  pl.BlockSpec((half, D_MODEL), lambda i: (1, 0))],
        out_specs=pl.BlockSpec((tm, D_MODEL), lambda i: (i, 0)),
        out_shape=jax.ShapeDtypeStruct((T, D_MODEL), F32),
        compiler_params=_params(("parallel",), 48),
        name="mixer_outproj",
    )(x, mod_l, a1, a2, w, w)


def _outproj_parity_kernel(x_ref, mod_ref, a1_ref, a2_ref, w1_ref, w2_ref, o_ref, lane_sc):
    ys = [_dot(a1_ref[p], w1_ref[...]) + _dot(a2_ref[p], w2_ref[...]) for p in range(2)]
    o_ref[...] = x_ref[...] + mod_ref[5:6, :] * _interleave_rows(lane_sc, ys[0], ys[1])


def _outproj_parity_call(x, mod_l, goff, group_tokens, a, w):
    T = x.shape[0]
    tm = min(512, group_tokens)
    half = tm // 2
    hd = D_MODEL // 2
    return pl.pallas_call(
        _outproj_parity_kernel,
        grid=(T // tm,),
        in_specs=[pl.BlockSpec((tm, D_MODEL), lambda i: (i, 0)),
                  _mod_spec(goff, group_tokens // tm, 1),
                  pl.BlockSpec((2, half, hd), lambda i: (0, i, 0)),
                  pl.BlockSpec((2, half, hd), lambda i: (0, i, 1)),
                  pl.BlockSpec((hd, D_MODEL), lambda i: (0, 0)),
                  pl.BlockSpec((hd, D_MODEL), lambda i: (1, 0))],
        out_specs=pl.BlockSpec((tm, D_MODEL), lambda i: (i, 0)),
        out_shape=jax.ShapeDtypeStruct((T, D_MODEL), F32),
        scratch_shapes=[pltpu.VMEM((D_MODEL // LANE, tm, LANE), F32)],
        compiler_params=_params(("parallel",), 48),
        name="mixer_outproj_parity",
    )(x, mod_l, a, a, w, w)


def _final_norm_kernel(x_ref, g_ref, o_ref):
    x = x_ref[...]
    o_ref[...] = (x * lax.rsqrt(jnp.mean(x * x, axis=-1, keepdims=True) + 1e-6)) * g_ref[...]


def _final_norm_call(x, gamma):
    T = x.shape[0]
    tm = 512
    return pl.pallas_call(
        _final_norm_kernel,
        grid=(T // tm,),
        in_specs=[pl.BlockSpec((tm, D_MODEL), lambda i: (i, 0)),
                  pl.BlockSpec((1, D_MODEL), lambda i: (0, 0))],
        out_specs=pl.BlockSpec((tm, D_MODEL), lambda i: (i, 0)),
        out_shape=jax.ShapeDtypeStruct((T, D_MODEL), F32),
        compiler_params=_params(("parallel",), 32),
        name="final_norm",
    )(x, gamma)


def _shift_prev(cur, halo_prev, row0, seq_len):
    tt = cur.shape[0]
    rid = lax.broadcasted_iota(jnp.int32, (tt, 1), 0)
    pos = jnp.bitwise_and(rid + row0, seq_len - 1)
    prev = pltpu.roll(cur, 1, 0)
    prev = jnp.where(rid == 0, halo_prev[SUBLANE - 1:SUBLANE, :], prev)
    return jnp.where(pos == 0, 0.0, prev)


def _shift_next(cur, halo_next, row0, seq_len):
    tt = cur.shape[0]
    rid = lax.broadcasted_iota(jnp.int32, (tt, 1), 0)
    pos = jnp.bitwise_and(rid + row0, seq_len - 1)
    nxt = pltpu.roll(cur, tt - 1, 0)
    nxt = jnp.where(rid == tt - 1, halo_next[0:1, :], nxt)
    return jnp.where(pos == seq_len - 1, 0.0, nxt)


def _shift_prev_next(cur, halo_prev, halo_next, row0, seq_len):
    return _shift_prev(cur, halo_prev, row0, seq_len), _shift_next(cur, halo_next, row0, seq_len)


def _halo_specs(tt, width, col, total_rows):
    per = tt // SUBLANE
    last = total_rows // SUBLANE - 1
    return [pl.BlockSpec((tt, width), lambda i: (i, col)),
            pl.BlockSpec((SUBLANE, width), lambda i: (jnp.maximum(i * per - 1, 0), col)),
            pl.BlockSpec((SUBLANE, width), lambda i: (jnp.minimum((i + 1) * per, last), col))]


def _segsum(x, ones_blk):
    hi = x.astype(BF16)
    lo = (x - hi.astype(F32)).astype(BF16)
    outs = []
    for g in range(x.shape[1] // MXU_DIM):
        sl = slice(g * MXU_DIM, (g + 1) * MXU_DIM)
        outs.append(_dot(hi[:, sl], ones_blk) + _dot(lo[:, sl], ones_blk))
    return jnp.concatenate(outs, axis=1)


def _rwkv_prep_kernel(z_ref, zp_ref, zn_ref, mup_ref, mun_ref, kk_ref, ka_ref, rk_ref,
                      w0_ref, w2_ref, a0_ref, a2_ref, g2_ref, ones_ref, sel_ref,
                      r_o, v_o, c_o, w0_o, b0_o, k0_o, wc0_o, vk0_o, be0_o,
                      w1_o, b1_o, k1_o, wc1_o, vk1_o, be1_o, bonus_o, g_o, *, tt, seq_len):
    row0 = pl.program_id(0) * tt
    cur = z_ref[...]
    halo_p, halo_n = zp_ref[...], zn_ref[...]
    prev, nxt = _shift_prev_next(cur, halo_p, halo_n, row0, seq_len)
    mup, mun = mup_ref[...], mun_ref[...]
    zs = cur + mup * (prev - cur) + mun * (nxt - cur)
    r = zs[:, 0:D_RWKV]
    k = zs[:, D_RWKV:2 * D_RWKV]
    v = zs[:, 2 * D_RWKV:3 * D_RWKV]
    small = zs[:, 3 * D_RWKV:RWKV_COLS]
    ones_blk = ones_ref[...]

    def unit_keys(keys):
        kk = keys * kk_ref[...]
        return kk / jnp.maximum(jnp.sqrt(_segsum(kk * kk, ones_blk)), 1e-12)

    kk = unit_keys(k)
    c = -kk
    ks = slice(D_RWKV, 2 * D_RWKV)
    hp, hn, mp, mn = halo_p[:, ks], halo_n[:, ks], mup[:, ks], mun[:, ks]
    zc = cur[:, ks]
    k_before = hp[7:8] + mp * (hp[6:7] - hp[7:8]) + mn * (zc[0:1] - hp[7:8])
    k_after = hn[0:1] + mp * (zc[tt - 1:tt] - hn[0:1]) + mn * (hn[1:2] - hn[0:1])
    edge = -unit_keys(jnp.concatenate([k_before, k_after, jnp.zeros((SUBLANE - 2, D_RWKV), F32)], axis=0))
    c_prev = _shift_prev(c, jnp.broadcast_to(edge[0:1], (SUBLANE, D_RWKV)), row0, seq_len)
    c_next = _shift_next(c, jnp.broadcast_to(edge[1:2], (SUBLANE, D_RWKV)), row0, seq_len)

    tw = jnp.tanh(small).astype(BF16)
    sg = _sigmoid(small).astype(BF16)
    xs = small.astype(BF16)
    r_o[...] = r
    v_o[...] = v
    c_o[...] = c
    g_o[...] = _dot(sg, g2_ref[...])

    sel = sel_ref[...]
    bonus = jnp.zeros_like(r)
    outs = ((w0_o, b0_o, k0_o, wc0_o, vk0_o, be0_o, c_next), (w1_o, b1_o, k1_o, wc1_o, vk1_o, be1_o, c_prev))
    for d in range(2):
        wl = -_softplus(-(w0_ref[d:d + 1, :] + _dot(tw, w2_ref[d]))) - 0.5
        a = _sigmoid(a0_ref[d:d + 1, :] + _dot(xs, a2_ref[d]))
        kd = k * (1.0 + (a - 1.0) * ka_ref[...])
        w_o, b_o, k_o, wc_o, vk_o, be_o, c_after = outs[d]
        decay = jnp.exp(-jnp.exp(wl))
        b = kk * a
        w_o[...] = decay
        b_o[...] = b
        k_o[...] = kd
        wc_o[...] = decay * c_after
        vk_o[...] = v * _segsum(kd * c_after, ones_blk)
        bc = b * c_after
        hi = bc.astype(BF16)
        lo = (bc - hi.astype(F32)).astype(BF16)
        be_o[...] = _dot(hi, sel) + _dot(lo, sel)
        bonus = bonus + _segsum(r * kd * rk_ref[...], ones_blk) * v
    bonus_o[...] = bonus


def _rwkv_prep_call(z_r, seq_len, p):
    T = z_r.shape[0]
    tt = min(128, seq_len)
    row = lambda n: pl.BlockSpec((1, n), lambda i: (0, 0))
    full2 = lambda a, b: pl.BlockSpec((a, b), lambda i: (0, 0))
    full3 = lambda a, b, c: pl.BlockSpec((a, b, c), lambda i: (0, 0, 0))
    in_specs = _halo_specs(tt, RWKV_COLS, 0, T) + [
        row(RWKV_COLS), row(RWKV_COLS), row(D_RWKV), row(D_RWKV), row(D_RWKV),
        full2(2, D_RWKV), full3(2, RWKV_SMALL, D_RWKV), full2(2, D_RWKV), full3(2, RWKV_SMALL, D_RWKV),
        full2(RWKV_SMALL, D_RWKV), full2(MXU_DIM, MXU_DIM), full2(D_RWKV, LANE)]
    wide = (pl.BlockSpec((tt, D_RWKV), lambda i: (i, 0)), jax.ShapeDtypeStruct((T, D_RWKV), F32))
    slim = (pl.BlockSpec((tt, LANE), lambda i: (i, 0)), jax.ShapeDtypeStruct((T, LANE), F32))
    outs = [wide] * 3 + ([wide] * 5 + [slim]) * 2 + [wide] * 2
    sel = (jnp.arange(D_RWKV)[:, None] // RWKV_HEAD == jnp.arange(LANE)[None, :]).astype(BF16)
    return pl.pallas_call(
        functools.partial(_rwkv_prep_kernel, tt=tt, seq_len=seq_len),
        grid=(T // tt,),
        in_specs=in_specs,
        out_specs=[o[0] for o in outs],
        out_shape=[o[1] for o in outs],
        compiler_params=_params(("parallel",), 56),
        name="rwkv_prep",
    )(z_r, z_r, z_r, p['mu_prev'], p['mu_next'], p['k_k'], p['k_a'], p['r_k'],
      p['w0'], p['w2'], p['a0'], p['a2'], p['g2'], p['ones_blk'], sel)


N_GRP = D_RWKV // MXU_DIM
HEADS_PER_GRP = H_RWKV // N_GRP


SCAN_ROWS = 16
SCAN_UNROLL = 8


def _scan_head_mask():
    return (jnp.arange(H_RWKV)[:, None] == (jnp.arange(D_RWKV) // RWKV_HEAD)[None, :]).astype(F32)


def _scan_kernel(rf, wcf, wf, bf, kf, vf, vhf, vkf, bef, rb, wcb, wb, bb, kb, vb, vhb, vkb, beb,
                 c0_ref, v0_ref, s0_ref, eye_ref, hm_ref, y0_ref, y1_ref, sfin_ref, st, uv, *, tc):
    j = pl.program_id(1)
    eye = eye_ref[...]
    hmask = hm_ref[...]
    dirs = ((rf, wcf, wf, bf, kf, (vf, vhf), vkf, bef, y0_ref),
            (rb, wcb, wb, bb, kb, (vb, vhb), vkb, beb, y1_ref))
    nt = (((1,), (1,)), ((), ()))
    zrows = jnp.zeros((SCAN_ROWS, D_RWKV), BF16)
    zeye = jnp.zeros((SCAN_ROWS, MXU_DIM), BF16)
    seqs = s0_ref.shape[0]
    chains = [(b, d) for b in range(seqs) for d in range(2)]

    def head_rows(row):
        return (row * hmask).astype(BF16)

    def value_rows(row):
        x = row * hmask
        folded = x[:, 0:MXU_DIM]
        for g in range(1, N_GRP):
            folded = folded + x[:, g * MXU_DIM:(g + 1) * MXU_DIM]
        return folded.astype(BF16)

    def next_value_row(v_refs, b, d, t):
        v_, vh_ = v_refs
        if d == 0:
            inside = v_[b, pl.ds(jnp.minimum(t + 1, tc - 1), 1), :]
            return jnp.where(t == tc - 1, vh_[b, 0:1, :], inside)
        inside = v_[b, pl.ds(jnp.maximum(t - 1, 0), 1), :]
        return jnp.where(t == 0, vh_[b, SUBLANE - 1:SUBLANE, :], inside)

    def state_products(b, d, rows0, eye0, eye1, rows2):
        lhs = jnp.concatenate([st[b, d, g].astype(BF16) for g in range(N_GRP)] + [eye], axis=1)
        wr = jnp.concatenate([
            jnp.concatenate([rows0, eye0], axis=1),
            jnp.concatenate([zrows, eye1], axis=1),
            jnp.concatenate([rows2, zeye], axis=1),
            jnp.concatenate([zrows, zeye], axis=1)], axis=0)
        return lax.dot_general(lhs, wr, nt, preferred_element_type=F32)

    def store_y(y_, b, t, prod):
        tr = prod.T
        y_[b, pl.ds(t, 1), :, :] = tr[2 * SCAN_ROWS:3 * SCAN_ROWS, :].reshape(1, H_RWKV, RWKV_HEAD)

    def time_of(d, s):
        s = jnp.clip(s, 0, tc - 1)
        return s if d == 0 else tc - 1 - s

    @pl.when(j == 0)
    def _():
        st[...] = s0_ref[...]
        for (b, d) in chains:
            uv[b, d] = state_products(b, d, head_rows(c0_ref[b, d:d + 1, :]), zeye,
                                      value_rows(v0_ref[b, d:d + 1, :]), zrows)

    def issue(group, s):
        out = []
        for (b, d) in group:
            r_, wc_, w_, b_, k_, vn_, vk_, be_, y_ = dirs[d]
            t, tp = time_of(d, s), time_of(d, s - 1)
            prod = state_products(b, d, head_rows(wc_[b, pl.ds(t, 1), :]),
                                  value_rows(vk_[b, pl.ds(t, 1), :]),
                                  value_rows(next_value_row(vn_, b, d, t)),
                                  head_rows(r_[b, pl.ds(tp, 1), :]))
            w2 = jnp.concatenate([head_rows(b_[b, pl.ds(t, 1), :]), head_rows(k_[b, pl.ds(t, 1), :]),
                                  zrows, zrows], axis=0)
            out.append((prod, _dot(uv[b, d].astype(BF16), w2)))
        return out

    def finish(group, s, results):
        for (b, d), (prod, upd) in zip(group, results):
            r_, wc_, w_, b_, k_, vn_, vk_, be_, y_ = dirs[d]
            t, tp = time_of(d, s), time_of(d, s - 1)
            wrow = w_[b, pl.ds(t, 1), :]
            for g in range(N_GRP):
                sl = slice(g * MXU_DIM, (g + 1) * MXU_DIM)
                st[b, d, g] = st[b, d, g] * wrow[:, sl] + upd[:, sl]
            store_y(y_, b, tp, prod)
            uv[b, d] = prod + uv[b, d] * be_[b, pl.ds(t, 1), 0:4 * SCAN_ROWS]

    lead, lag = chains[:seqs], chains[seqs:]

    def body(i, carry):
        for u in range(SCAN_UNROLL):
            s = i * SCAN_UNROLL + u
            lead_results = issue(lead, s)
            lag_results = issue(lag, s)
            finish(lead, s, lead_results)
            finish(lag, s, lag_results)
        return carry

    lax.fori_loop(0, tc // SCAN_UNROLL, body, 0)

    for (b, d) in chains:
        t = time_of(d, tc - 1)
        prod = state_products(b, d, zrows, zeye, zeye, head_rows(dirs[d][0][b, pl.ds(t, 1), :]))
        store_y(dirs[d][8], b, t, prod)

    @pl.when(j == pl.num_programs(1) - 1)
    def _():
        sfin_ref[...] = st[...]


def _scan_call(pre, s0, eye):
    B, T, _ = pre['r'].shape
    nb = 4 if B % 4 == 0 else 2
    tc = min(128 if nb == 2 else 32, T)
    nj = T // tc
    fwd = pl.BlockSpec((nb, tc, D_RWKV), lambda bi, j: (bi, j, 0))
    bwd = pl.BlockSpec((nb, tc, D_RWKV), lambda bi, j: (bi, nj - 1 - j, 0))
    yfwd = pl.BlockSpec((nb, tc, H_RWKV, RWKV_HEAD), lambda bi, j: (bi, j, 0, 0))
    ybwd = pl.BlockSpec((nb, tc, H_RWKV, RWKV_HEAD), lambda bi, j: (bi, nj - 1 - j, 0, 0))
    st_spec = pl.BlockSpec((nb, 2, N_GRP, RWKV_HEAD, MXU_DIM), lambda bi, j: (bi, 0, 0, 0, 0))
    y_shape = jax.ShapeDtypeStruct((B, T, H_RWKV, RWKV_HEAD), F32)
    befwd = pl.BlockSpec((nb, tc, LANE), lambda bi, j: (bi, j, 0))
    bebwd = pl.BlockSpec((nb, tc, LANE), lambda bi, j: (bi, nj - 1 - j, 0))
    per = tc // SUBLANE
    last8 = T // SUBLANE - 1
    hfwd = pl.BlockSpec((nb, SUBLANE, D_RWKV), lambda bi, j: (bi, jnp.minimum((j + 1) * per, last8), 0))
    hbwd = pl.BlockSpec((nb, SUBLANE, D_RWKV), lambda bi, j: (bi, jnp.maximum((nj - 1 - j) * per - 1, 0), 0))
    first = pl.BlockSpec((nb, SUBLANE, D_RWKV), lambda bi, j: (bi, 0, 0))

    wc0, vk0, be0, wc1, vk1, be1 = (pre[n] for n in ('wc0', 'vk0', 'be0', 'wc1', 'vk1', 'be1'))
    v = pre['v']
    pad = jnp.zeros((B, SUBLANE - 2, D_RWKV), F32)
    c_first = jnp.concatenate([pre['c'][:, 0:1], pre['c'][:, T - 1:T], pad], axis=1)
    v_first = jnp.concatenate([v[:, 0:1], v[:, T - 1:T], pad], axis=1)
    return pl.pallas_call(
        functools.partial(_scan_kernel, tc=tc),
        grid=(B // nb, nj),
        in_specs=[fwd] * 6 + [hfwd, fwd, befwd] + [bwd] * 6 + [hbwd, bwd, bebwd] + [
            first, first,
            st_spec,
            pl.BlockSpec((RWKV_HEAD, MXU_DIM), lambda bi, j: (0, 0)),
            pl.BlockSpec((H_RWKV, D_RWKV), lambda bi, j: (0, 0))],
        out_specs=[yfwd, ybwd, st_spec],
        out_shape=[y_shape, y_shape, jax.ShapeDtypeStruct((B, 2, N_GRP, RWKV_HEAD, MXU_DIM), F32)],
        scratch_shapes=[pltpu.VMEM((nb, 2, N_GRP, RWKV_HEAD, MXU_DIM), F32),
                        pltpu.VMEM((nb, 2, RWKV_HEAD, 4 * SCAN_ROWS), F32)],
        compiler_params=_params(("arbitrary", "arbitrary"), 52),
        name="rwkv_scan",
    )(pre['r'], wc0, pre['w0'], pre['b0'], pre['k0'], v, v, vk0, be0,
      pre['r'], wc1, pre['w1'], pre['b1'], pre['k1'], v, v, vk1, be1,
      c_first, v_first, s0, eye.astype(BF16), _scan_head_mask())


def _rwkv_post_kernel(y0_ref, y1_ref, bonus_ref, g_ref, gw_ref, gb_ref, ones_ref, o_ref):
    ones_blk = ones_ref[...]
    y = y0_ref[...] + y1_ref[...]
    mu = _segsum(y, ones_blk) * (1.0 / RWKV_HEAD)
    yc = y - mu
    var = _segsum(yc * yc, ones_blk) * (1.0 / RWKV_HEAD)
    yn = yc * lax.rsqrt(var + RWKV_GN_EPS)
    out = (yn * gw_ref[...] + gb_ref[...] + bonus_ref[...]) * g_ref[...]
    o_ref[...] = out.astype(BF16)


def _rwkv_post_call(y0, y1, bonus, g, gn_w, gn_b, ones_blk):
    T = y0.shape[0]
    tt = 512
    blk = pl.BlockSpec((tt, D_RWKV), lambda i: (i, 0))
    row = pl.BlockSpec((1, D_RWKV), lambda i: (0, 0))
    return pl.pallas_call(
        _rwkv_post_kernel,
        grid=(T // tt,),
        in_specs=[blk, blk, blk, blk, row, row, pl.BlockSpec((MXU_DIM, MXU_DIM), lambda i: (0, 0))],
        out_specs=blk,
        out_shape=jax.ShapeDtypeStruct((T, D_RWKV), BF16),
        compiler_params=_params(("parallel",), 40),
        name="rwkv_post",
    )(y0, y1, bonus, g, gn_w, gn_b, ones_blk)


def _rope128(x, cos_t, sin_t):
    return x * cos_t + pltpu.roll(x, QK_ROPE, 1) * sin_t


def _pack_kv(kv, kr_rot, k_o, v_o):
    for h in range(H_MLA):
        k_o[:, h * HEAD_SLOT:h * HEAD_SLOT + QK_NOPE] = kv[:, h * HEAD_SLOT:h * HEAD_SLOT + QK_NOPE].astype(BF16)
        k_o[:, h * HEAD_SLOT + QK_NOPE:(h + 1) * HEAD_SLOT] = kr_rot.astype(BF16)
        v_o[:, h * V_HEAD:(h + 1) * V_HEAD] = kv[:, h * HEAD_SLOT + QK_NOPE:(h + 1) * HEAD_SLOT].astype(BF16)


def _mla_prep_kernel(z_ref, cos_ref, sin_ref, qn_ref, kvn_ref, wq_ref, wkv_ref,
                     q_o, k_o, v_o, ckv_o, kr_o):
    z = z_ref[...]
    cq = z[:, 0:Q_RANK]
    ckv = z[:, Q_RANK:Q_RANK + KV_RANK]
    krp = z[:, Q_RANK + KV_RANK:MLA_COLS]
    cos_t = cos_ref[...]
    sin_t = sin_ref[...]
    cq = (cq * lax.rsqrt(jnp.mean(cq * cq, axis=-1, keepdims=True) + 1e-6)) * qn_ref[...]
    ckv = (ckv * lax.rsqrt(jnp.mean(ckv * ckv, axis=-1, keepdims=True) + 1e-6)) * kvn_ref[...]
    ckv_o[...] = ckv
    kr_o[...] = krp
    q = _dot(cq.astype(BF16), wq_ref[...]) * (ATTN_SCALE * LOG2_E)
    for h in range(H_MLA):
        q_o[:, h * HEAD_SLOT:h * HEAD_SLOT + QK_NOPE] = q[:, h * HEAD_SLOT:h * HEAD_SLOT + QK_NOPE].astype(BF16)
        q_o[:, h * HEAD_SLOT + QK_NOPE:(h + 1) * HEAD_SLOT] = _rope128(
            q[:, h * HEAD_SLOT + QK_NOPE:(h + 1) * HEAD_SLOT], cos_t, sin_t).astype(BF16)
    kv = _dot(ckv.astype(BF16), wkv_ref[...])
    _pack_kv(kv, _rope128(krp, cos_t, sin_t), k_o, v_o)


def _mla_prep_call(z_m, cos_t, sin_t, seq_len, p):
    T = z_m.shape[0]
    tm = min(512, seq_len)
    per_seq = seq_len // tm
    row = lambda n: pl.BlockSpec((1, n), lambda i: (0, 0))
    blk = lambda n: pl.BlockSpec((tm, n), lambda i: (i, 0))
    tab = pl.BlockSpec((tm, LANE), lambda i: (i % per_seq, 0))
    return pl.pallas_call(
        _mla_prep_kernel,
        grid=(T // tm,),
        in_specs=[blk(MLA_COLS), tab, tab, row(Q_RANK), row(KV_RANK),
                  pl.BlockSpec((Q_RANK, H_MLA * HEAD_SLOT), lambda i: (0, 0)),
                  pl.BlockSpec((KV_RANK, H_MLA * HEAD_SLOT), lambda i: (0, 0))],
        out_specs=[blk(H_MLA * HEAD_SLOT), blk(H_MLA * HEAD_SLOT), blk(H_MLA * V_HEAD),
                   blk(KV_RANK), blk(LANE)],
        out_shape=[jax.ShapeDtypeStruct((T, H_MLA * HEAD_SLOT), BF16),
                   jax.ShapeDtypeStruct((T, H_MLA * HEAD_SLOT), BF16),
                   jax.ShapeDtypeStruct((T, H_MLA * V_HEAD), BF16),
                   jax.ShapeDtypeStruct((T, KV_RANK), F32),
                   jax.ShapeDtypeStruct((T, LANE), F32)],
        compiler_params=_params(("parallel",), 48),
        name="mla_prep",
    )(z_m, cos_t, sin_t, p['q_norm'], p['kv_norm'], p['w_qb'], p['w_kvb'])


def _ctx_kv_kernel(ckv_ref, kr_ref, wkv_ref, k_o, v_o):
    kv = _dot(ckv_ref[...].astype(BF16), wkv_ref[...])
    _pack_kv(kv, kr_ref[...], k_o, v_o)


def _ctx_kv_call(ckv_ctx, kr_ctx_pad, w_kvb):
    T = ckv_ctx.shape[0]
    tm = min(512, T)
    blk = lambda n: pl.BlockSpec((tm, n), lambda i: (i, 0))
    return pl.pallas_call(
        _ctx_kv_kernel,
        grid=(T // tm,),
        in_specs=[blk(KV_RANK), blk(LANE), pl.BlockSpec((KV_RANK, H_MLA * HEAD_SLOT), lambda i: (0, 0))],
        out_specs=[blk(H_MLA * HEAD_SLOT), blk(H_MLA * V_HEAD)],
        out_shape=[jax.ShapeDtypeStruct((T, H_MLA * HEAD_SLOT), BF16),
                   jax.ShapeDtypeStruct((T, H_MLA * V_HEAD), BF16)],
        compiler_params=_params(("parallel",), 32),
        name="mla_ctx_kv",
    )(ckv_ctx, kr_ctx_pad, w_kvb)


ATTN_ROWS = 64


def _attn_kernel(q_ref, k_ref, v_ref, o_ref, s_sc, p_sc, *, kc, heads):
    tq, tk = q_ref.shape[0], s_sc.shape[1]
    hq = tq // 2
    nt = (((1,), (1,)), ((), ()))
    units = [(hd, half) for hd in range(heads) for half in range(2)]

    def rows_of(hd, half):
        return hd * tq + half * hq

    def scores(hd, half):
        q = q_ref[half * hq:(half + 1) * hq, hd * HEAD_SLOT:(hd + 1) * HEAD_SLOT]
        r0 = rows_of(hd, half)
        for c in range(tk // kc):
            s_sc[r0:r0 + hq, c * kc:(c + 1) * kc] = lax.dot_general(
                q, k_ref[c * kc:(c + 1) * kc, hd * HEAD_SLOT:(hd + 1) * HEAD_SLOT], nt,
                preferred_element_type=F32)

    def softmax(hd, half):
        sums = []
        for r in range(hq // ATTN_ROWS):
            r0 = rows_of(hd, half) + r * ATTN_ROWS
            rows = slice(r0, r0 + ATTN_ROWS)
            mpart = s_sc[rows, 0:LANE]
            for t in range(1, tk // LANE):
                mpart = jnp.maximum(mpart, s_sc[rows, t * LANE:(t + 1) * LANE])
            m = jnp.max(mpart, axis=-1, keepdims=True)
            lpart = jnp.zeros((ATTN_ROWS, LANE), F32)
            for t in range(tk // LANE):
                p = jnp.exp2(s_sc[rows, t * LANE:(t + 1) * LANE] - m)
                lpart = lpart + p
                p_sc[rows, t * LANE:(t + 1) * LANE] = p.astype(BF16)
            sums.append(jnp.sum(lpart, axis=-1, keepdims=True))
        return jnp.concatenate(sums, axis=0)

    def weighted_values(hd, half, l):
        r0 = rows_of(hd, half)
        acc = jnp.zeros((hq, V_HEAD), F32)
        for c in range(tk // kc):
            acc = acc + _dot(p_sc[r0:r0 + hq, c * kc:(c + 1) * kc],
                             v_ref[c * kc:(c + 1) * kc, hd * V_HEAD:(hd + 1) * V_HEAD])
        o_ref[half * hq:(half + 1) * hq, hd * V_HEAD:(hd + 1) * V_HEAD] = (acc / l).astype(BF16)

    for u in units:
        scores(*u)
    for u in units:
        weighted_values(*u, softmax(*u))


def _attn_call(q, k, v):
    B, Tq, _ = q.shape
    Tk = k.shape[1]
    tq = min(512, Tq)
    heads = H_MLA if Tk <= 512 else 1
    return pl.pallas_call(
        functools.partial(_attn_kernel, kc=MXU_DIM, heads=heads),
        grid=(B, H_MLA // heads, Tq // tq),
        in_specs=[pl.BlockSpec((None, tq, heads * HEAD_SLOT), lambda b, h, i: (b, i, h)),
                  pl.BlockSpec((None, Tk, heads * HEAD_SLOT), lambda b, h, i: (b, 0, h)),
                  pl.BlockSpec((None, Tk, heads * V_HEAD), lambda b, h, i: (b, 0, h))],
        out_specs=pl.BlockSpec((None, tq, heads * V_HEAD), lambda b, h, i: (b, i, h)),
        out_shape=jax.ShapeDtypeStruct((B, Tq, H_MLA * V_HEAD), BF16),
        scratch_shapes=[pltpu.VMEM((heads * tq, Tk), F32), pltpu.VMEM((heads * tq, Tk), BF16)],
        compiler_params=_params(("parallel", "parallel", "arbitrary"), 48),
        name="mla_attention",
    )(q, k, v)


def _conv3_kernel(*refs, tt, half_len):
    ins, (cw_refs, cb_refs), outs = refs[0:12], (refs[12:15], refs[15:18]), refs[18:]
    row0 = pl.program_id(0) * tt
    for s in range(3):
        even, odd = ins[4 * s][...], ins[4 * s + 1][...]
        odd_prev = _shift_prev(odd, ins[4 * s + 2][...], row0, half_len)
        even_next = _shift_next(even, ins[4 * s + 3][...], row0, half_len)
        cw = cw_refs[s][...]
        bias = cb_refs[s][...]
        y_even = cw[0:1, :] * odd_prev + cw[1:2, :] * even + cw[2:3, :] * odd + bias
        y_odd = cw[0:1, :] * even + cw[1:2, :] * odd + cw[2:3, :] * even_next + bias
        outs[s][0] = y_even
        outs[s][1] = y_odd
        if s == 2:
            outs[3][0] = y_even.astype(BF16)
            outs[3][1] = y_odd.astype(BF16)


def _conv3_call(z, half_len, conv_w, conv_b):
    T2 = z.shape[1]
    tt = min(128, half_len)
    per = tt // SUBLANE
    last = T2 // SUBLANE - 1
    in_specs = []
    for s in range(3):
        in_specs += [
            pl.BlockSpec((None, tt, D_HYENA), lambda i, s=s: (0, i, s)),
            pl.BlockSpec((None, tt, D_HYENA), lambda i, s=s: (1, i, s)),
            pl.BlockSpec((None, SUBLANE, D_HYENA), lambda i, s=s: (1, jnp.maximum(i * per - 1, 0), s)),
            pl.BlockSpec((None, SUBLANE, D_HYENA), lambda i, s=s: (0, jnp.minimum((i + 1) * per, last), s))]
    in_specs += [pl.BlockSpec((3, D_HYENA), lambda i, s=s: (0, s)) for s in range(3)]
    in_specs += [pl.BlockSpec((1, D_HYENA), lambda i, s=s: (0, s)) for s in range(3)]
    blk = pl.BlockSpec((2, tt, D_HYENA), lambda i: (0, i, 0))
    f32s = jax.ShapeDtypeStruct((2, T2, D_HYENA), F32)
    return pl.pallas_call(
        functools.partial(_conv3_kernel, tt=tt, half_len=half_len),
        grid=(T2 // tt,),
        in_specs=in_specs,
        out_specs=[blk] * 4,
        out_shape=[f32s, f32s, f32s, jax.ShapeDtypeStruct((2, T2, D_HYENA), BF16)],
        compiler_params=_params(("parallel",), 48),
        name="hyena_conv3",
    )(*([z] * 12), conv_w, conv_w, conv_w, conv_b, conv_b, conv_b)


def _filt_mlp_kernel(z_ref, w1_ref, b1_ref, w2_ref, b2_ref, fr_ref, o_ref):
    h = jnp.sin(fr_ref[0:1, :] * (_dot(z_ref[...].astype(BF16), w1_ref[...].astype(BF16)) + b1_ref[...]))
    h = jnp.sin(fr_ref[1:2, :] * (_dot(h.astype(BF16), w2_ref[...].astype(BF16)) + b2_ref[...]))
    o_ref[...] = h.astype(BF16)


def _filt_mlp_call(zpos, w1p, b1, w2, b2, freq):
    L = zpos.shape[0]
    return pl.pallas_call(
        _filt_mlp_kernel,
        out_shape=jax.ShapeDtypeStruct((L, FILT_HIDDEN), BF16),
        compiler_params=pltpu.CompilerParams(vmem_limit_bytes=32 * MIB),
        name="hyena_filter_mlp",
    )(zpos, w1p, b1, w2, b2, freq)


def _filt_gen_kernel(h_ref, tn_ref, dl_ref, w00, w01, w10, w11, o_ref, taps_sc):
    h = h_ref[...]
    L = h.shape[0]
    win = jnp.exp(-tn_ref[...] * dl_ref[...])
    not_first = lax.broadcasted_iota(jnp.int32, (L, 1), 0) > 0
    ws = ((w00, w01), (w10, w11))

    def emit(k, taps):
        taps_sc[...] = taps
        for par in range(2):
            o_ref[k, par] = taps_sc[pl.ds(par, L // 2, stride=2), :].astype(BF16)

    for n in range(2):
        causal = _dot(h, ws[n][0][...].astype(BF16)) * win
        anti = jnp.where(not_first, _dot(h, ws[n][1][...].astype(BF16)) * win, 0.0)
        norm = (jnp.sum(jnp.abs(causal), axis=0, keepdims=True)
                + jnp.sum(jnp.abs(anti), axis=0, keepdims=True))
        emit(2 * n, causal / norm)
        emit(2 * n + 1, anti / norm)


def _filt_gen_call(h2, tnorm, deltas, w3):
    L = h2.shape[0]
    tc = 128
    nc = D_HYENA // tc
    wspec = lambda k: pl.BlockSpec((FILT_HIDDEN, tc), lambda j, k=k: (0, k * nc + j))
    return pl.pallas_call(
        _filt_gen_kernel,
        grid=(nc,),
        in_specs=[pl.BlockSpec((L, FILT_HIDDEN), lambda j: (0, 0)),
                  pl.BlockSpec((L, 1), lambda j: (0, 0)),
                  pl.BlockSpec((1, tc), lambda j: (0, j)),
                  wspec(0), wspec(1), wspec(2), wspec(3)],
        out_specs=pl.BlockSpec((4, 2, L // 2, tc), lambda j: (0, 0, 0, j)),
        out_shape=jax.ShapeDtypeStruct((4, 2, L // 2, D_HYENA), BF16),
        scratch_shapes=[pltpu.VMEM((L, tc), F32)],
        compiler_params=_params(("parallel",), 48),
        name="hyena_filter_gen",
    )(h2, tnorm, deltas, w3, w3, w3, w3)


def _dft_fwd_kernel(f_ref, u_ref, o_ref):
    o_ref[...] = _dot(f_ref[...], u_ref[...])


def _dft_fwd_call(fmat, u):
    B, K, C = u.shape
    M = fmat.shape[0]
    tm = min(512, M)
    tn = min(C, 2048)
    return pl.pallas_call(
        _dft_fwd_kernel,
        grid=(B, C // tn, M // tm),
        in_specs=[pl.BlockSpec((tm, K), lambda b, j, i: (i, 0)),
                  pl.BlockSpec((None, K, tn), lambda b, j, i: (b, 0, j))],
        out_specs=pl.BlockSpec((None, tm, tn), lambda b, j, i: (b, i, j)),
        out_shape=jax.ShapeDtypeStruct((B, M, C), F32),
        compiler_params=_params(("parallel", "parallel", "arbitrary"), 40),
        name="hyena_dft_fwd",
    )(fmat, u)


EDGE_ROWS = 16


def _butterfly(gc, gs, hc, hs, tw_c, tw_s, first):
    tc = hc * tw_c - hs * tw_s
    ts = hs * tw_c + hc * tw_s
    p0 = gc + tc
    p2 = gc - tc
    p1 = gs + ts
    p3 = ts - gs
    if first is not None:
        p1 = jnp.where(first, gs, p1)
        p3 = jnp.where(first, hs, p3)
    return p0, p1, p2, p3


def _cmul(ac, a_s, bc, bs):
    return ac * bc - a_s * bs, ac * bs + a_s * bc


def _filt_planes_kernel(ge_ref, ho_ref, twc_ref, tws_ref, o_ref, *, tr):
    def planes(rows, first):
        tw_c, tw_s = twc_ref[rows, :], tws_ref[rows, :]
        a = _butterfly(ge_ref[0, 0, rows, :], ge_ref[0, 1, rows, :], ho_ref[0, 0, rows, :], ho_ref[0, 1, rows, :],
                       tw_c, tw_s, first)
        b = _butterfly(ge_ref[1, 0, rows, :], ge_ref[1, 1, rows, :], ho_ref[1, 0, rows, :], ho_ref[1, 1, rows, :],
                       tw_c, tw_s, first)
        p1 = a[1] - b[1] if first is None else jnp.where(first, a[1] + b[1], a[1] - b[1])
        o_ref[0, rows, :] = a[0] + b[0]
        o_ref[1, rows, :] = p1
        o_ref[2, rows, :] = a[2] + b[2]
        o_ref[3, rows, :] = a[3] - b[3]

    planes(slice(0, tr), None)

    @pl.when(pl.program_id(1) == 0)
    def _():
        planes(slice(0, EDGE_ROWS), lax.broadcasted_iota(jnp.int32, (EDGE_ROWS, 1), 0) == 0)


def _filt_planes_call(raw, twc, tws):
    _, L, C = raw.shape
    H = L // 2
    tr = min(256, H)
    tc = 512
    blk = lambda par: pl.BlockSpec((None, 2, None, 2, tr, tc), lambda n, i, j: (n, 0, par, 0, i, j))
    tw = pl.BlockSpec((tr, 1), lambda n, i, j: (i, 0))
    raw6 = raw.reshape(2, 2, 2, 2, H, C)
    return pl.pallas_call(
        functools.partial(_filt_planes_kernel, tr=tr),
        grid=(2, H // tr, C // tc),
        in_specs=[blk(0), blk(1), tw, tw],
        out_specs=pl.BlockSpec((None, 4, tr, tc), lambda n, i, j: (n, 0, i, j)),
        out_shape=jax.ShapeDtypeStruct((2, 4, H, C), F32),
        compiler_params=_params(("parallel", "parallel", "parallel"), 40),
        name="hyena_filter_planes",
    )(raw6, raw6, twc, tws)


def _spec_mul_kernel(raw_ref, k_ref, twc_ref, tws_ref, o_ref, *, tr):
    def multiply(rows, first):
        tw_c, tw_s = twc_ref[rows, :], tws_ref[rows, :]
        p0, p1, p2, p3 = _butterfly(raw_ref[0, 0, rows, :], raw_ref[0, 1, rows, :],
                                    raw_ref[1, 0, rows, :], raw_ref[1, 1, rows, :], tw_c, tw_s, first)
        k0, k1, k2, k3 = (k_ref[n, rows, :] for n in range(4))
        yac, yas = _cmul(p0, p1, k0, k1)
        ybc, ybs = _cmul(p2, p3, k2, k3)
        if first is not None:
            ymc, yms = _cmul(p1, p3, k1, k3)
            yac = jnp.where(first, p0 * k0, yac)
            ybc = jnp.where(first, p2 * k2, ybc)
        dc = yac - ybc
        ds = yas + ybs
        e_s, o_c, o_s = yas - ybs, dc * tw_c + ds * tw_s, ds * tw_c - dc * tw_s
        if first is not None:
            e_s, o_c, o_s = jnp.where(first, ymc, e_s), jnp.where(first, dc, o_c), jnp.where(first, yms, o_s)
        o_ref[0, 0, rows, :] = (yac + ybc).astype(BF16)
        o_ref[0, 1, rows, :] = e_s.astype(BF16)
        o_ref[1, 0, rows, :] = o_c.astype(BF16)
        o_ref[1, 1, rows, :] = o_s.astype(BF16)

    multiply(slice(0, tr), None)

    @pl.when(pl.program_id(1) == 0)
    def _():
        multiply(slice(0, EDGE_ROWS), lax.broadcasted_iota(jnp.int32, (EDGE_ROWS, 1), 0) == 0)


def _spec_mul_call(raw, k_planes, order, twc, tws):
    B2, L, C = raw.shape
    B, H = B2 // 2, L // 2
    tr = min(256, H)
    tc = 512
    tw = pl.BlockSpec((tr, 1), lambda b, i, j: (i, 0))
    out = pl.pallas_call(
        functools.partial(_spec_mul_kernel, tr=tr),
        grid=(B, H // tr, C // tc),
        in_specs=[pl.BlockSpec((2, None, 2, tr, tc), lambda b, i, j: (0, b, 0, i, j)),
                  pl.BlockSpec((None, 4, tr, tc), lambda b, i, j: (order, 0, i, j)),
                  tw, tw],
        out_specs=pl.BlockSpec((2, None, 2, tr, tc), lambda b, i, j: (0, b, 0, i, j)),
        out_shape=jax.ShapeDtypeStruct((2, B, 2, H, C), BF16),
        compiler_params=_params(("parallel", "parallel", "parallel"), 40),
        name="hyena_spectral_mul",
    )(raw.reshape(2, B, 2, H, C), k_planes, twc, tws)
    return out.reshape(B2, L, C)


def _dft_inv_kernel(f_ref, y_ref, gate_ref, u_ref, bias_ref, o_ref, ob_ref):
    conv = _dot(f_ref[...], y_ref[...])
    out = gate_ref[...] * (conv + u_ref[...] * bias_ref[...])
    o_ref[...] = out
    ob_ref[...] = out.astype(BF16)


def _dft_inv_call(imat, y_spec, gate, u, bias):
    B, M, C = u.shape
    K = imat.shape[1]
    tm = min(512, M)
    tn = 1024 if K > 1024 else min(C, 2048)
    blk = pl.BlockSpec((None, tm, tn), lambda b, j, i: (b, i, j))
    return pl.pallas_call(
        _dft_inv_kernel,
        grid=(B, C // tn, M // tm),
        in_specs=[pl.BlockSpec((tm, K), lambda b, j, i: (i, 0)),
                  pl.BlockSpec((None, K, tn), lambda b, j, i: (b, 0, j)),
                  blk, blk, pl.BlockSpec((1, tn), lambda b, j, i: (0, j))],
        out_specs=[blk, blk],
        out_shape=[jax.ShapeDtypeStruct((B, M, C), F32), jax.ShapeDtypeStruct((B, M, C), BF16)],
        compiler_params=_params(("parallel", "parallel", "arbitrary"), 48),
        name="hyena_dft_inv",
    )(imat, y_spec, gate, u, bias)


def _dft_tables(L):
    H = L // 2
    lo = min(64, H)
    hi = H // lo
    g = jnp.arange(H, dtype=jnp.int32)
    theta = 2.0 * math.pi / L
    ang_hi = ((g[:, None] * (jnp.arange(hi, dtype=jnp.int32) * lo)[None, :]) % L).astype(F32) * theta
    ang_lo = ((g[:, None] * jnp.arange(lo, dtype=jnp.int32)[None, :]) % L).astype(F32) * theta
    ch, sh, cl, sl = jnp.cos(ang_hi), jnp.sin(ang_hi), jnp.cos(ang_lo), jnp.sin(ang_lo)
    cos_m = (ch[:, :, None] * cl[:, None, :] - sh[:, :, None] * sl[:, None, :]).reshape(H, H)
    sin_m = (sh[:, :, None] * cl[:, None, :] + ch[:, :, None] * sl[:, None, :]).reshape(H, H)
    alt = jnp.where(jnp.arange(H) % 2 == 0, 1.0, -1.0).astype(F32)
    sin_m = jnp.where((g == 0)[:, None], alt[None, :], sin_m)
    fwd = jnp.concatenate([cos_m, sin_m], axis=0)
    n = 2.0 * L
    w_cos = jnp.where(g == 0, 1.0 / n, 2.0 / n).astype(F32)
    inv = jnp.concatenate([cos_m * w_cos[:, None], sin_m * (2.0 / n)], axis=0).T
    ang_tw = g.astype(F32) * (math.pi / L)
    return fwd.astype(BF16), inv.astype(BF16), jnp.cos(ang_tw)[:, None], jnp.sin(ang_tw)[:, None]


def _filter_positions(L):
    t = jnp.arange(L, dtype=F32)
    t_norm = t / max(L - 1, 1)
    bands = (POS_EMB - 1) // 2
    freqs = jnp.linspace(1e-4, bands - 1, bands, dtype=F32)
    ang = (2.0 * math.pi / L) * t[:, None] * freqs[None, :]
    z = jnp.concatenate([t_norm[:, None], jnp.cos(ang), -jnp.sin(ang)], axis=-1)
    return jnp.pad(z, ((0, 0), (0, POS_PAD - POS_EMB))), t_norm[:, None]


def _hyena_deltas():
    return jnp.linspace(abs(math.log(HYENA_TARGET)) / SLOW_DECAY_PCT,
                        abs(math.log(HYENA_TARGET)) / FAST_DECAY_PCT, D_HYENA, dtype=F32)[None, :]


def _hyena_filter_spectrum(L, tables, p):
    fwd, _, twc, tws = tables
    zpos, tnorm = _filter_positions(L)
    h2 = _filt_mlp_call(zpos, p['filt_w1'], p['filt_b1'], p['filt_w2'], p['filt_b2'], p['filt_freq'])
    filt = _filt_gen_call(h2, tnorm, _hyena_deltas(), p['filt_w3'])
    raw = _dft_fwd_call(fwd, filt.reshape(8, L // 2, D_HYENA))
    return _filt_planes_call(raw, twc, tws)


def _hyena_mixer(z, B, L, tables, k_planes, p):
    fwd, inv, twc, tws = tables
    H = L // 2
    x1, x2, v, vb = _conv3_call(z, H, p['conv_w'], p['conv_b'])
    shp = (2 * B, H, D_HYENA)
    u, ub = v.reshape(shp), vb.reshape(shp)
    for n, gate in enumerate((x1, x2)):
        raw = _dft_fwd_call(fwd, ub)
        yspec = _spec_mul_call(raw, k_planes, n, twc, tws)
        u, ub = _dft_inv_call(inv, yspec, gate.reshape(shp), u, p['bias'][n:n + 1])
    return ub.reshape(2, B * H, D_HYENA)


def _rope_swap(w):
    q = QK_ROPE // 4
    return jnp.concatenate([w[..., q:2 * q], w[..., 0:q], w[..., 3 * q:4 * q], w[..., 2 * q:3 * q]], axis=-1)


def _pad_cols(w, n):
    return jnp.pad(w, [(0, 0)] * (w.ndim - 1) + [(0, n - w.shape[-1])])


def _pack_even(e, w_in_even, mu_prev, mu_next, rwkv_w0, rwkv_w2, rwkv_a0, rwkv_a2, rwkv_g2,
               rwkv_kk, rwkv_ka, rwkv_rk, rwkv_gn_w, rwkv_gn_b, mla_q_norm, mla_kv_norm,
               mla_w_qb, mla_w_kvb, w_out_even):
    n_r = 3 * D_RWKV + W_LORA + A_LORA + G_LORA
    w_in = w_in_even[e]
    w_r = _pad_cols(w_in[:, :n_r], RWKV_COLS).astype(BF16)
    w_m = w_in[:, n_r:]
    kr_cols = w_m[:, Q_RANK + KV_RANK:]
    w_m = jnp.concatenate([w_m, _rope_swap(kr_cols)], axis=-1).astype(BF16)
    small_rows = lambda w, off: jnp.pad(w, [(0, 0)] * (w.ndim - 2)
                                        + [(off, RWKV_SMALL - off - w.shape[-2]), (0, 0)]).astype(BF16)
    wq = mla_w_qb[e].reshape(Q_RANK, H_MLA, QK_NOPE + QK_ROPE)
    wq = jnp.concatenate([wq, _rope_swap(wq[..., QK_NOPE:])], axis=-1).reshape(Q_RANK, H_MLA * HEAD_SLOT)
    blk = jnp.arange(MXU_DIM) // RWKV_HEAD
    return {
        'w_r': w_r, 'w_m': w_m,
        'mu_prev': _pad_cols(mu_prev[e][None, :], RWKV_COLS),
        'mu_next': _pad_cols(mu_next[e][None, :], RWKV_COLS),
        'k_k': rwkv_kk[e][None, :], 'k_a': rwkv_ka[e][None, :],
        'r_k': rwkv_rk[e].reshape(1, D_RWKV),
        'w0': rwkv_w0[e], 'w2': small_rows(rwkv_w2[e], 0),
        'a0': rwkv_a0[e], 'a2': small_rows(rwkv_a2[e], W_LORA),
        'g2': small_rows(rwkv_g2[e], W_LORA + A_LORA),
        'gn_w': rwkv_gn_w[e][None, :], 'gn_b': rwkv_gn_b[e][None, :],
        'q_norm': mla_q_norm[e][None, :], 'kv_norm': mla_kv_norm[e][None, :],
        'w_qb': wq.astype(BF16), 'w_kvb': mla_w_kvb[e].astype(BF16),
        'w_out': w_out_even[e].astype(BF16),
        'ones_blk': (blk[:, None] == blk[None, :]).astype(BF16),
        'eye': (jnp.arange(RWKV_HEAD)[:, None] == (jnp.arange(MXU_DIM) % RWKV_HEAD)[None, :]).astype(F32),
    }


def _rope_tables(L):
    rows = L // GRID_W
    row = jnp.repeat(jnp.arange(rows, dtype=F32), GRID_W)
    col = jnp.tile(jnp.arange(GRID_W, dtype=F32), rows)
    half = QK_ROPE // 2
    inv = 1.0 / (ROPE_THETA ** (jnp.arange(0, half, 2, dtype=F32) / half))
    ar, ac = row[:, None] * inv[None, :], col[:, None] * inv[None, :]
    cos_t = jnp.concatenate([jnp.cos(ar), jnp.cos(ar), jnp.cos(ac), jnp.cos(ac)], axis=-1)
    sin_t = jnp.concatenate([-jnp.sin(ar), jnp.sin(ar), -jnp.sin(ac), jnp.sin(ac)], axis=-1)
    return _pad_cols(cos_t, LANE), _pad_cols(sin_t, LANE)


def _state_to_groups(s):
    B = s.shape[0]
    s = s.reshape(B, 2, N_GRP, H_RWKV // N_GRP, RWKV_HEAD, RWKV_HEAD)
    return jnp.swapaxes(s, 3, 4).reshape(B, 2, N_GRP, RWKV_HEAD, MXU_DIM)


def _groups_to_state(s):
    B = s.shape[0]
    s = s.reshape(B, 2, N_GRP, RWKV_HEAD, H_RWKV // N_GRP, RWKV_HEAD)
    return jnp.swapaxes(s, 3, 4).reshape(B, 2, H_RWKV, RWKV_HEAD, RWKV_HEAD)


def _even_mixer(x, mod_l, goff, B, L, gamma, p, rope, ctx):
    group_tokens = x.shape[0] if ctx is None else L
    z_r = _inproj_call(x, mod_l, goff, group_tokens, gamma, p['w_r'], RWKV_COLS // 3)
    z_m = _inproj_call(x, mod_l, goff, group_tokens, gamma, p['w_m'], MLA_COLS)
    names = ('r', 'v', 'c', 'w0', 'b0', 'k0', 'wc0', 'vk0', 'be0', 'w1', 'b1', 'k1', 'wc1', 'vk1', 'be1',
             'bonus', 'g')
    pre = dict(zip(names, _rwkv_prep_call(z_r, L, p)))
    seq = {n: pre[n].reshape(B, L, pre[n].shape[-1]) for n in names[:15]}
    if ctx is None:
        s0 = jnp.zeros((B, 2, N_GRP, RWKV_HEAD, MXU_DIM), F32)
    else:
        s0 = _state_to_groups(ctx[2].astype(F32))
    y0, y1, s_fin = _scan_call(seq, s0, p['eye'])
    y_r = _rwkv_post_call(y0.reshape(B * L, D_RWKV), y1.reshape(B * L, D_RWKV), pre['bonus'], pre['g'],
                          p['gn_w'], p['gn_b'], p['ones_blk'])

    q, k, v, ckv, krp = _mla_prep_call(z_m, rope[0], rope[1], L, p)
    q = q.reshape(B, L, H_MLA * HEAD_SLOT)
    k = k.reshape(B, L, H_MLA * HEAD_SLOT)
    v = v.reshape(B, L, H_MLA * V_HEAD)
    if ctx is not None:
        P = ctx[0].shape[1]
        k_ctx, v_ctx = _ctx_kv_call(ctx[0].reshape(B * P, KV_RANK),
                                    _pad_cols(ctx[1].reshape(B * P, QK_ROPE), LANE), p['w_kvb'])
        k = jnp.concatenate([k, k_ctx.reshape(B, P, H_MLA * HEAD_SLOT)], axis=1)
        v = jnp.concatenate([v, v_ctx.reshape(B, P, H_MLA * V_HEAD)], axis=1)
    y_m = _attn_call(q, k, v).reshape(B * L, H_MLA * V_HEAD)
    x = _outproj_call(x, mod_l, goff, group_tokens, y_r, y_m, p['w_out'])
    state = (_groups_to_state(s_fin), ckv.reshape(B, L, KV_RANK), krp[:, :QK_ROPE].reshape(B, L, QK_ROPE))
    return x, state


def _odd_mixer(x, mod_l, goff, group_tokens, B, L, gamma, tables, k_planes, p):
    z = _inproj_parity_call(x, mod_l, goff, group_tokens, gamma, p['w_in'], 1536)
    y = _hyena_mixer(z, B, L, tables, k_planes, p)
    return _outproj_parity_call(x, mod_l, goff, group_tokens, y, p['w_out'])


def kernel(x_prompt, x_sample, cache_mla_ckv, cache_mla_krope, state_rwkv, c, c_ctx,
           w_mod, b_mod, norm_g, w_ffn_in, w_ffn_out, final_norm_g,
           w_in_even, mu_prev, mu_next, rwkv_w0, rwkv_w2, rwkv_a0, rwkv_a2, rwkv_g2,
           rwkv_kk, rwkv_ka, rwkv_rk, rwkv_gn_w, rwkv_gn_b,
           mla_q_norm, mla_kv_norm, mla_w_qb, mla_w_kvb, w_out_even,
           w_in_odd, hy_conv_w, hy_conv_b, hy_filt_w1, hy_filt_b1, hy_filt_w2, hy_filt_b2,
           hy_filt_w3, hy_filt_freq, hy_bias, w_out_odd):
    Bp, Lp, D = x_prompt.shape
    Bs, Ls, _ = x_sample.shape
    depth = w_mod.shape[0]
    xp = x_prompt.reshape(Bp * Lp, D)
    xs = x_sample.reshape(Bs * Ls, D)
    Tp = Bp * Lp

    cvec = jnp.concatenate([c_ctx[None, :], c, jnp.zeros((SUBLANE - 1 - Bs, D), F32)], axis=0)
    mod = _mod_call(cvec, w_mod, b_mod)

    rope_p = (_pad_cols(jnp.ones((Lp, QK_ROPE), F32), LANE), jnp.zeros((Lp, LANE), F32))
    rope_s = _rope_tables(Ls)
    tabs_p = tabs_s = None
    w_in = w_ffn_in.astype(BF16)
    w_out = w_ffn_out.astype(BF16)
    new_ckv, new_kr, new_s = [], [], []
    for l in range(depth):
        mod_l = mod[l]
        gam = [norm_g[l, s][None, :] for s in range(3)]
        xp = _ffn_call(xp, mod_l, 0, Tp, gam[0], w_in, w_out, l, 0, 0)
        xs = _ffn_call(xs, mod_l, 1, Ls, gam[0], w_in, w_out, l, 0, 0)
        if l % 2 == 0:
            e = l // 2
            p = _pack_even(e, w_in_even, mu_prev, mu_next, rwkv_w0, rwkv_w2, rwkv_a0, rwkv_a2, rwkv_g2,
                           rwkv_kk, rwkv_ka, rwkv_rk, rwkv_gn_w, rwkv_gn_b, mla_q_norm, mla_kv_norm,
                           mla_w_qb, mla_w_kvb, w_out_even)
            ctx = (cache_mla_ckv[:, e], cache_mla_krope[:, e], state_rwkv[:, e])
            xp, st = _even_mixer(xp, mod_l, 0, Bp, Lp, gam[1], p, rope_p, None)
            xs, _ = _even_mixer(xs, mod_l, 1, Bs, Ls, gam[1], p, rope_s, ctx)
            new_s.append(st[0].astype(x_prompt.dtype))
            new_ckv.append(st[1])
            new_kr.append(st[2])
        else:
            o = l // 2
            p = {'w_in': w_in_odd[o].astype(BF16), 'conv_w': hy_conv_w[o], 'conv_b': hy_conv_b[o][None, :],
                 'filt_w1': jnp.pad(hy_filt_w1[o], ((0, POS_PAD - POS_EMB), (0, 0))),
                 'filt_b1': hy_filt_b1[o][None, :], 'filt_w2': hy_filt_w2[o],
                 'filt_b2': hy_filt_b2[o][None, :], 'filt_w3': hy_filt_w3[o],
                 'filt_freq': hy_filt_freq[o], 'bias': hy_bias[o], 'w_out': w_out_odd[o].astype(BF16)}
            if tabs_p is None:
                tabs_p, tabs_s = _dft_tables(Lp), _dft_tables(Ls)
            ks_p = _hyena_filter_spectrum(Lp, tabs_p, p)
            ks_s = _hyena_filter_spectrum(Ls, tabs_s, p)
            xp = _odd_mixer(xp, mod_l, 0, Tp, Bp, Lp, gam[1], tabs_p, ks_p, p)
            xs = _odd_mixer(xs, mod_l, 1, Ls, Bs, Ls, gam[1], tabs_s, ks_s, p)
        xp = _ffn_call(xp, mod_l, 0, Tp, gam[2], w_in, w_out, l, 1, 2)
        xs = _ffn_call(xs, mod_l, 1, Ls, gam[2], w_in, w_out, l, 1, 2)

    fg = final_norm_g[None, :]
    y_prompt = _final_norm_call(xp, fg).reshape(Bp, Lp, D)
    y_sample = _final_norm_call(xs, fg).reshape(Bs, Ls, D)
    return (y_prompt, y_sample, jnp.stack(new_ckv, axis=1), jnp.stack(new_kr, axis=1),
            jnp.stack(new_s, axis=1))
```

```python
import functools
import math

import jax
import jax.numpy as jnp
from jax import lax
from jax.experimental import pallas as pl
from jax.experimental.pallas import tpu as pltpu

F32 = jnp.float32
BF16 = jnp.bfloat16

D_MODEL = 2048
N_MOD = 9
D_FF = 5632
D_RWKV = 1024
RWKV_HEAD = 64
H_RWKV = 16
W_LORA = 64
A_LORA = 64
G_LORA = 160
RWKV_SMALL = 384
RWKV_COLS = 3 * D_RWKV + RWKV_SMALL
RWKV_GN_EPS = 64e-5
H_MLA = 8
QK_NOPE = 128
QK_ROPE = 64
V_HEAD = 128
Q_RANK = 512
KV_RANK = 256
MLA_COLS = Q_RANK + KV_RANK + 2 * QK_ROPE
HEAD_SLOT = 256
ROPE_THETA = 10000.0
ATTN_SCALE = (QK_NOPE + QK_ROPE) ** -0.5
LOG2_E = 1.0 / math.log(2.0)
GRID_W = 64
D_HYENA = 2048
POS_EMB = 33
POS_PAD = 128
FILT_HIDDEN = 64
HYENA_TARGET = 1e-2
FAST_DECAY_PCT = 0.3
SLOW_DECAY_PCT = 1.5
LANE = 128
SUBLANE = 8
MXU_DIM = 256
MIB = 1024 * 1024


def _params(sem, vmem_mib):
    return pltpu.CompilerParams(dimension_semantics=sem, vmem_limit_bytes=vmem_mib * MIB)


def _sigmoid(x):
    return 1.0 / (1.0 + jnp.exp(-x))


def _softplus(x):
    return jnp.maximum(x, 0.0) + jnp.log(1.0 + jnp.exp(-jnp.abs(x)))


def _dot(a, b):
    return jnp.dot(a, b, preferred_element_type=F32)


def _norm_mod(x, gamma, shift, scale):
    xn = x * lax.rsqrt(jnp.mean(x * x, axis=-1, keepdims=True) + 1e-6)
    return xn * (gamma * (1.0 + scale)) + shift


def _mod_kernel(c_ref, w_ref, b_ref, o_ref):
    c = c_ref[...]
    s = c * _sigmoid(c)
    o_ref[0] = _dot(s.astype(BF16), w_ref[0].astype(BF16)) + b_ref[0]


def _mod_call(cvec, w_mod, b_mod):
    L, Dm, N = w_mod.shape
    tn = 1024
    out = pl.pallas_call(
        _mod_kernel,
        grid=(L, N // tn),
        in_specs=[pl.BlockSpec((SUBLANE, Dm), lambda l, j: (0, 0)),
                  pl.BlockSpec((1, Dm, tn), lambda l, j: (l, 0, j)),
                  pl.BlockSpec((1, 1, tn), lambda l, j: (l, 0, j))],
        out_specs=pl.BlockSpec((1, SUBLANE, tn), lambda l, j: (l, 0, j)),
        out_shape=jax.ShapeDtypeStruct((L, SUBLANE, N), F32),
        compiler_params=_params(("arbitrary", "arbitrary"), 40),
        name="adaln_mod",
    )(cvec, w_mod, b_mod.reshape(L, 1, N))
    return out.reshape(L, SUBLANE, N_MOD, Dm)


def _mod_spec(goff, tiles_per_group, nargs):
    if nargs == 1:
        return pl.BlockSpec((None, N_MOD, D_MODEL), lambda i: (goff + i // tiles_per_group, 0, 0))
    return pl.BlockSpec((None, N_MOD, D_MODEL), lambda i, j: (goff + i // tiles_per_group, 0, 0))


def _ffn_kernel(x_ref, mod_ref, g_ref, wg_ref, wu_ref, wo_ref, o_ref, h_sc, acc_sc, *, sub):
    f = pl.program_id(1)

    @pl.when(f == 0)
    def _():
        h = _norm_mod(x_ref[...], g_ref[...], mod_ref[3 * sub:3 * sub + 1, :],
                      mod_ref[3 * sub + 1:3 * sub + 2, :])
        h_sc[...] = h.astype(BF16)
        acc_sc[...] = jnp.zeros_like(acc_sc)

    h = h_sc[...]
    a = _dot(h, wg_ref[...])
    u = _dot(h, wu_ref[...])
    act = (a * _sigmoid(a)) * u
    acc_sc[...] += _dot(act.astype(BF16), wo_ref[...])

    @pl.when(f == pl.num_programs(1) - 1)
    def _():
        o_ref[...] = x_ref[...] + 0.5 * mod_ref[3 * sub + 2:3 * sub + 3, :] * acc_sc[...]


def _ffn_call(x, mod_l, goff, group_tokens, gamma, w_in, w_out, l, s, sub):
    T = x.shape[0]
    tm = min(512, group_tokens)
    tf = 512
    nf = D_FF // tf
    return pl.pallas_call(
        functools.partial(_ffn_kernel, sub=sub),
        grid=(T // tm, nf),
        in_specs=[pl.BlockSpec((tm, D_MODEL), lambda i, f: (i, 0)),
                  _mod_spec(goff, group_tokens // tm, 2),
                  pl.BlockSpec((1, D_MODEL), lambda i, f: (0, 0)),
                  pl.BlockSpec((None, None, D_MODEL, tf), lambda i, f: (l, s, 0, f)),
                  pl.BlockSpec((None, None, D_MODEL, tf), lambda i, f: (l, s, 0, f + nf)),
                  pl.BlockSpec((None, None, tf, D_MODEL), lambda i, f: (l, s, f, 0))],
        out_specs=pl.BlockSpec((tm, D_MODEL), lambda i, f: (i, 0)),
        out_shape=jax.ShapeDtypeStruct((T, D_MODEL), F32),
        scratch_shapes=[pltpu.VMEM((tm, D_MODEL), BF16), pltpu.VMEM((tm, D_MODEL), F32)],
        compiler_params=_params(("parallel", "arbitrary"), 52),
        name="ffn_swiglu",
    )(x, mod_l, gamma, w_in, w_in, w_out)


def _inproj_kernel(x_ref, mod_ref, g_ref, w_ref, o_ref, h_sc, *, col_axis):
    @pl.when(pl.program_id(col_axis) == 0)
    def _():
        h = _norm_mod(x_ref[...], g_ref[...], mod_ref[3:4, :], mod_ref[4:5, :])
        h_sc[...] = h.astype(BF16)

    o_ref[...] = _dot(h_sc[...], w_ref[...])


def _inproj_call(x, mod_l, goff, group_tokens, gamma, w, tn):
    T = x.shape[0]
    N = w.shape[1]
    tm = min(512, group_tokens)
    return pl.pallas_call(
        functools.partial(_inproj_kernel, col_axis=1),
        grid=(T // tm, N // tn),
        in_specs=[pl.BlockSpec((tm, D_MODEL), lambda i, j: (i, 0)),
                  _mod_spec(goff, group_tokens // tm, 2),
                  pl.BlockSpec((1, D_MODEL), lambda i, j: (0, 0)),
                  pl.BlockSpec((D_MODEL, tn), lambda i, j: (0, j))],
        out_specs=pl.BlockSpec((tm, tn), lambda i, j: (i, j)),
        out_shape=jax.ShapeDtypeStruct((T, N), F32),
        scratch_shapes=[pltpu.VMEM((tm, D_MODEL), BF16)],
        compiler_params=_params(("parallel", "arbitrary"), 48),
        name="mixer_inproj",
    )(x, mod_l, gamma, w)


def _rows_of_parity(lane_sc, x, par):
    rows = x.shape[0]
    parts = []
    for c in range(x.shape[1] // LANE):
        lane_sc[c] = x[:, c * LANE:(c + 1) * LANE]
        parts.append(lane_sc[c, pl.ds(par, rows // 2, stride=2), :])
    return jnp.concatenate(parts, axis=1)


def _interleave_rows(lane_sc, even, odd):
    half = even.shape[0]
    parts = []
    for c in range(even.shape[1] // LANE):
        lane_sc[c, pl.ds(0, half, stride=2), :] = even[:, c * LANE:(c + 1) * LANE]
        lane_sc[c, pl.ds(1, half, stride=2), :] = odd[:, c * LANE:(c + 1) * LANE]
        parts.append(lane_sc[c])
    return jnp.concatenate(parts, axis=1)


def _inproj_parity_kernel(x_ref, mod_ref, g_ref, w_ref, o_ref, h_sc, lane_sc, *, half):
    @pl.when(pl.program_id(1) == 0)
    def _():
        h = _norm_mod(x_ref[...], g_ref[...], mod_ref[3:4, :], mod_ref[4:5, :])
        for p in range(2):
            h_sc[p * half:(p + 1) * half, :] = _rows_of_parity(lane_sc, h, p).astype(BF16)

    z = _dot(h_sc[...], w_ref[...])
    o_ref[0] = z[:half]
    o_ref[1] = z[half:]


def _inproj_parity_call(x, mod_l, goff, group_tokens, gamma, w, tn):
    T = x.shape[0]
    N = w.shape[1]
    tm = min(512, group_tokens)
    half = tm // 2
    return pl.pallas_call(
        functools.partial(_inproj_parity_kernel, half=half),
        grid=(T // tm, N // tn),
        in_specs=[pl.BlockSpec((tm, D_MODEL), lambda i, j: (i, 0)),
                  _mod_spec(goff, group_tokens // tm, 2),
                  pl.BlockSpec((1, D_MODEL), lambda i, j: (0, 0)),
                  pl.BlockSpec((D_MODEL, tn), lambda i, j: (0, j))],
        out_specs=pl.BlockSpec((2, half, tn), lambda i, j: (0, i, j)),
        out_shape=jax.ShapeDtypeStruct((2, T // 2, N), F32),
        scratch_shapes=[pltpu.VMEM((tm, D_MODEL), BF16), pltpu.VMEM((D_MODEL // LANE, tm, LANE), F32)],
        compiler_params=_params(("parallel", "arbitrary"), 48),
        name="mixer_inproj_parity",
    )(x, mod_l, gamma, w)


def _outproj_kernel(x_ref, mod_ref, a1_ref, a2_ref, w1_ref, w2_ref, o_ref):
    y = _dot(a1_ref[...], w1_ref[...]) + _dot(a2_ref[...], w2_ref[...])
    o_ref[...] = x_ref[...] + mod_ref[5:6, :] * y


def _outproj_call(x, mod_l, goff, group_tokens, a1, a2, w):
    T = x.shape[0]
    tm = min(512, group_tokens)
    half = D_MODEL // 2
    return pl.pallas_call(
        _outproj_kernel,
        grid=(T // tm,),
        in_specs=[pl.BlockSpec((tm, D_MODEL), lambda i: (i, 0)),
                  _mod_spec(goff, group_tokens // tm, 1),
                  pl.BlockSpec((tm, half), lambda i: (i, 0)),
                  pl.BlockSpec((tm, half), lambda i: (i, 0)),
                  pl.BlockSpec((half, D_MODEL), lambda i: (0, 0)),
                  pl.BlockSpec((half, D_MODEL), lambda i: (1, 0))],
        out_specs=pl.BlockSpec((tm, D_MODEL), lambda i: (i, 0)),
        out_shape=jax.ShapeDtypeStruct((T, D_MODEL), F32),
        compiler_params=_params(("parallel",), 48),
        name="mixer_outproj",
    )(x, mod_l, a1, a2, w, w)


def _outproj_parity_kernel(x_ref, mod_ref, a1_ref, a2_ref, w1_ref, w2_ref, o_ref, lane_sc):
    ys = [_dot(a1_ref[p], w1_ref[...]) + _dot(a2_ref[p], w2_ref[...]) for p in range(2)]
    o_ref[...] = x_ref[...] + mod_ref[5:6, :] * _interleave_rows(lane_sc, ys[0], ys[1])


def _outproj_parity_call(x, mod_l, goff, group_tokens, a, w):
    T = x.shape[0]
    tm = min(512, group_tokens)
    half = tm // 2
    hd = D_MODEL // 2
    return pl.pallas_call(
        _outproj_parity_kernel,
        grid=(T // tm,),
        in_specs=[pl.BlockSpec((tm, D_MODEL), lambda i: (i, 0)),
                  _mod_spec(goff, group_tokens // tm, 1),
                  pl.BlockSpec((2, half, hd), lambda i: (0, i, 0)),
                  pl.BlockSpec((2, half, hd), lambda i: (0, i, 1)),
                  pl.BlockSpec((hd, D_MODEL), lambda i: (0, 0)),
                  pl.BlockSpec((hd, D_MODEL), lambda i: (1, 0))],
        out_specs=pl.BlockSpec((tm, D_MODEL), lambda i: (i, 0)),
        out_shape=jax.ShapeDtypeStruct((T, D_MODEL), F32),
        scratch_shapes=[pltpu.VMEM((D_MODEL // LANE, tm, LANE), F32)],
        compiler_params=_params(("parallel",), 48),
        name="mixer_outproj_parity",
    )(x, mod_l, a, a, w, w)


def _final_norm_kernel(x_ref, g_ref, o_ref):
    x = x_ref[...]
    o_ref[...] = (x * lax.rsqrt(jnp.mean(x * x, axis=-1, keepdims=True) + 1e-6)) * g_ref[...]


def _final_norm_call(x, gamma):
    T = x.shape[0]
    tm = 512
    return pl.pallas_call(
        _final_norm_kernel,
        grid=(T // tm,),
        in_specs=[pl.BlockSpec((tm, D_MODEL), lambda i: (i, 0)),
                  pl.BlockSpec((1, D_MODEL), lambda i: (0, 0))],
        out_specs=pl.BlockSpec((tm, D_MODEL), lambda i: (i, 0)),
        out_shape=jax.ShapeDtypeStruct((T, D_MODEL), F32),
        compiler_params=_params(("parallel",), 32),
        name="final_norm",
    )(x, gamma)


def _shift_prev(cur, halo_prev, row0, seq_len):
    tt = cur.shape[0]
    rid = lax.broadcasted_iota(jnp.int32, (tt, 1), 0)
    pos = jnp.bitwise_and(rid + row0, seq_len - 1)
    prev = pltpu.roll(cur, 1, 0)
    prev = jnp.where(rid == 0, halo_prev[SUBLANE - 1:SUBLANE, :], prev)
    return jnp.where(pos == 0, 0.0, prev)


def _shift_next(cur, halo_next, row0, seq_len):
    tt = cur.shape[0]
    rid = lax.broadcasted_iota(jnp.int32, (tt, 1), 0)
    pos = jnp.bitwise_and(rid + row0, seq_len - 1)
    nxt = pltpu.roll(cur, tt - 1, 0)
    nxt = jnp.where(rid == tt - 1, halo_next[0:1, :], nxt)
    return jnp.where(pos == seq_len - 1, 0.0, nxt)


def _shift_prev_next(cur, halo_prev, halo_next, row0, seq_len):
    return _shift_prev(cur, halo_prev, row0, seq_len), _shift_next(cur, halo_next, row0, seq_len)


def _halo_specs(tt, width, col, total_rows):
    per = tt // SUBLANE
    last = total_rows // SUBLANE - 1
    return [pl.BlockSpec((tt, width), lambda i: (i, col)),
            pl.BlockSpec((SUBLANE, width), lambda i: (jnp.maximum(i * per - 1, 0), col)),
            pl.BlockSpec((SUBLANE, width), lambda i: (jnp.minimum((i + 1) * per, last), col))]


def _segsum(x, ones_blk):
    hi = x.astype(BF16)
    lo = (x - hi.astype(F32)).astype(BF16)
    outs = []
    for g in range(x.shape[1] // MXU_DIM):
        sl = slice(g * MXU_DIM, (g + 1) * MXU_DIM)
        outs.append(_dot(hi[:, sl], ones_blk) + _dot(lo[:, sl], ones_blk))
    return jnp.concatenate(outs, axis=1)


def _rwkv_prep_kernel(z_ref, zp_ref, zn_ref, mup_ref, mun_ref, kk_ref, ka_ref, rk_ref,
                      w0_ref, w2_ref, a0_ref, a2_ref, g2_ref, ones_ref, sel_ref,
                      r_o, v_o, c_o, w0_o, b0_o, k0_o, wc0_o, vk0_o, be0_o,
                      w1_o, b1_o, k1_o, wc1_o, vk1_o, be1_o, bonus_o, g_o, *, tt, seq_len):
    row0 = pl.program_id(0) * tt
    cur = z_ref[...]
    halo_p, halo_n = zp_ref[...], zn_ref[...]
    prev, nxt = _shift_prev_next(cur, halo_p, halo_n, row0, seq_len)
    mup, mun = mup_ref[...], mun_ref[...]
    zs = cur + mup * (prev - cur) + mun * (nxt - cur)
    r = zs[:, 0:D_RWKV]
    k = zs[:, D_RWKV:2 * D_RWKV]
    v = zs[:, 2 * D_RWKV:3 * D_RWKV]
    small = zs[:, 3 * D_RWKV:RWKV_COLS]
    ones_blk = ones_ref[...]

    def unit_keys(keys):
        kk = keys * kk_ref[...]
        return kk / jnp.maximum(jnp.sqrt(_segsum(kk * kk, ones_blk)), 1e-12)

    kk = unit_keys(k)
    c = -kk
    ks = slice(D_RWKV, 2 * D_RWKV)
    hp, hn, mp, mn = halo_p[:, ks], halo_n[:, ks], mup[:, ks], mun[:, ks]
    zc = cur[:, ks]
    k_before = hp[7:8] + mp * (hp[6:7] - hp[7:8]) + mn * (zc[0:1] - hp[7:8])
    k_after = hn[0:1] + mp * (zc[tt - 1:tt] - hn[0:1]) + mn * (hn[1:2] - hn[0:1])
    edge = -unit_keys(jnp.concatenate([k_before, k_after, jnp.zeros((SUBLANE - 2, D_RWKV), F32)], axis=0))
    c_prev = _shift_prev(c, jnp.broadcast_to(edge[0:1], (SUBLANE, D_RWKV)), row0, seq_len)
    c_next = _shift_next(c, jnp.broadcast_to(edge[1:2], (SUBLANE, D_RWKV)), row0, seq_len)

    tw = jnp.tanh(small).astype(BF16)
    sg = _sigmoid(small).astype(BF16)
    xs = small.astype(BF16)
    r_o[...] = r
    v_o[...] = v
    c_o[...] = c
    g_o[...] = _dot(sg, g2_ref[...])

    sel = sel_ref[...]
    bonus = jnp.zeros_like(r)
    outs = ((w0_o, b0_o, k0_o, wc0_o, vk0_o, be0_o, c_next), (w1_o, b1_o, k1_o, wc1_o, vk1_o, be1_o, c_prev))
    for d in range(2):
        wl = -_softplus(-(w0_ref[d:d + 1, :] + _dot(tw, w2_ref[d]))) - 0.5
        a = _sigmoid(a0_ref[d:d + 1, :] + _dot(xs, a2_ref[d]))
        kd = k * (1.0 + (a - 1.0) * ka_ref[...])
        w_o, b_o, k_o, wc_o, vk_o, be_o, c_after = outs[d]
        decay = jnp.exp(-jnp.exp(wl))
        b = kk * a
        w_o[...] = decay
        b_o[...] = b
        k_o[...] = kd
        wc_o[...] = decay * c_after
        vk_o[...] = v * _segsum(kd * c_after, ones_blk)
        bc = b * c_after
        hi = bc.astype(BF16)
        lo = (bc - hi.astype(F32)).astype(BF16)
        be_o[...] = _dot(hi, sel) + _dot(lo, sel)
        bonus = bonus + _segsum(r * kd * rk_ref[...], ones_blk) * v
    bonus_o[...] = bonus


def _rwkv_prep_call(z_r, seq_len, p):
    T = z_r.shape[0]
    tt = min(128, seq_len)
    row = lambda n: pl.BlockSpec((1, n), lambda i: (0, 0))
    full2 = lambda a, b: pl.BlockSpec((a, b), lambda i: (0, 0))
    full3 = lambda a, b, c: pl.BlockSpec((a, b, c), lambda i: (0, 0, 0))
    in_specs = _halo_specs(tt, RWKV_COLS, 0, T) + [
        row(RWKV_COLS), row(RWKV_COLS), row(D_RWKV), row(D_RWKV), row(D_RWKV),
        full2(2, D_RWKV), full3(2, RWKV_SMALL, D_RWKV), full2(2, D_RWKV), full3(2, RWKV_SMALL, D_RWKV),
        full2(RWKV_SMALL, D_RWKV), full2(MXU_DIM, MXU_DIM), full2(D_RWKV, LANE)]
    wide = (pl.BlockSpec((tt, D_RWKV), lambda i: (i, 0)), jax.ShapeDtypeStruct((T, D_RWKV), F32))
    slim = (pl.BlockSpec((tt, LANE), lambda i: (i, 0)), jax.ShapeDtypeStruct((T, LANE), F32))
    outs = [wide] * 3 + ([wide] * 5 + [slim]) * 2 + [wide] * 2
    sel = (jnp.arange(D_RWKV)[:, None] // RWKV_HEAD == jnp.arange(LANE)[None, :]).astype(BF16)
    return pl.pallas_call(
        functools.partial(_rwkv_prep_kernel, tt=tt, seq_len=seq_len),
        grid=(T // tt,),
        in_specs=in_specs,
        out_specs=[o[0] for o in outs],
        out_shape=[o[1] for o in outs],
        compiler_params=_params(("parallel",), 56),
        name="rwkv_prep",
    )(z_r, z_r, z_r, p['mu_prev'], p['mu_next'], p['k_k'], p['k_a'], p['r_k'],
      p['w0'], p['w2'], p['a0'], p['a2'], p['g2'], p['ones_blk'], sel)


N_GRP = D_RWKV // MXU_DIM


SCAN_ROWS = 16
SCAN_UNROLL = 8


def _scan_head_mask():
    return (jnp.arange(H_RWKV)[:, None] == (jnp.arange(D_RWKV) // RWKV_HEAD)[None, :]).astype(F32)


def _scan_kernel(rf, wcf, wf, bf, kf, vf, vhf, vkf, bef, rb, wcb, wb, bb, kb, vb, vhb, vkb, beb,
                 c0_ref, v0_ref, s0_ref, eye_ref, hm_ref, y0_ref, y1_ref, sfin_ref, st, uv, *, tc):
    j = pl.program_id(1)
    eye = eye_ref[...]
    hmask = hm_ref[...]
    dirs = ((rf, wcf, wf, bf, kf, (vf, vhf), vkf, bef, y0_ref),
            (rb, wcb, wb, bb, kb, (vb, vhb), vkb, beb, y1_ref))
    nt = (((1,), (1,)), ((), ()))
    zrows = jnp.zeros((SCAN_ROWS, D_RWKV), BF16)
    zeye = jnp.zeros((SCAN_ROWS, MXU_DIM), BF16)
    seqs = s0_ref.shape[0]
    chains = [(b, d) for b in range(seqs) for d in range(2)]

    def head_rows(row):
        return (row * hmask).astype(BF16)

    def value_rows(row):
        x = row * hmask
        folded = x[:, 0:MXU_DIM]
        for g in range(1, N_GRP):
            folded = folded + x[:, g * MXU_DIM:(g + 1) * MXU_DIM]
        return folded.astype(BF16)

    def next_value_row(v_refs, b, d, t):
        v_, vh_ = v_refs
        if d == 0:
            inside = v_[b, pl.ds(jnp.minimum(t + 1, tc - 1), 1), :]
            return jnp.where(t == tc - 1, vh_[b, 0:1, :], inside)
        inside = v_[b, pl.ds(jnp.maximum(t - 1, 0), 1), :]
        return jnp.where(t == 0, vh_[b, SUBLANE - 1:SUBLANE, :], inside)

    def state_products(b, d, rows0, eye0, eye1, rows2):
        lhs = jnp.concatenate([st[b, d, g].astype(BF16) for g in range(N_GRP)] + [eye], axis=1)
        wr = jnp.concatenate([
            jnp.concatenate([rows0, eye0], axis=1),
            jnp.concatenate([zrows, eye1], axis=1),
            jnp.concatenate([rows2, zeye], axis=1),
            jnp.concatenate([zrows, zeye], axis=1)], axis=0)
        return lax.dot_general(lhs, wr, nt, preferred_element_type=F32)

    def store_y(y_, b, t, prod):
        tr = prod.T
        y_[b, pl.ds(t, 1), :, :] = tr[2 * SCAN_ROWS:3 * SCAN_ROWS, :].reshape(1, H_RWKV, RWKV_HEAD)

    def time_of(d, s):
        s = jnp.clip(s, 0, tc - 1)
        return s if d == 0 else tc - 1 - s

    @pl.when(j == 0)
    def _():
        st[...] = s0_ref[...]
        for (b, d) in chains:
            uv[b, d] = state_products(b, d, head_rows(c0_ref[b, d:d + 1, :]), zeye,
                                      value_rows(v0_ref[b, d:d + 1, :]), zrows)

    def issue(group, s):
        out = []
        for (b, d) in group:
            r_, wc_, w_, b_, k_, vn_, vk_, be_, y_ = dirs[d]
            t, tp = time_of(d, s), time_of(d, s - 1)
            prod = state_products(b, d, head_rows(wc_[b, pl.ds(t, 1), :]),
                                  value_rows(vk_[b, pl.ds(t, 1), :]),
                                  value_rows(next_value_row(vn_, b, d, t)),
                                  head_rows(r_[b, pl.ds(tp, 1), :]))
            w2 = jnp.concatenate([head_rows(b_[b, pl.ds(t, 1), :]), head_rows(k_[b, pl.ds(t, 1), :]),
                                  zrows, zrows], axis=0)
            out.append((prod, _dot(uv[b, d].astype(BF16), w2)))
        return out

    def finish(group, s, results):
        for (b, d), (prod, upd) in zip(group, results):
            r_, wc_, w_, b_, k_, vn_, vk_, be_, y_ = dirs[d]
            t, tp = time_of(d, s), time_of(d, s - 1)
            wrow = w_[b, pl.ds(t, 1), :]
            for g in range(N_GRP):
                sl = slice(g * MXU_DIM, (g + 1) * MXU_DIM)
                st[b, d, g] = st[b, d, g] * wrow[:, sl] + upd[:, sl]
            store_y(y_, b, tp, prod)
            uv[b, d] = prod + uv[b, d] * be_[b, pl.ds(t, 1), 0:4 * SCAN_ROWS]

    lead, lag = chains[:seqs], chains[seqs:]

    unroll = SCAN_UNROLL * 4 // len(chains) * 2

    def body(i, carry):
        for u in range(unroll):
            s = i * unroll + u
            lead_results = issue(lead, s)
            lag_results = issue(lag, s)
            finish(lead, s, lead_results)
            finish(lag, s, lag_results)
        return carry

    lax.fori_loop(0, tc // unroll, body, 0)

    for (b, d) in chains:
        t = time_of(d, tc - 1)
        prod = state_products(b, d, zrows, zeye, zeye, head_rows(dirs[d][0][b, pl.ds(t, 1), :]))
        store_y(dirs[d][8], b, t, prod)

    @pl.when(j == pl.num_programs(1) - 1)
    def _():
        sfin_ref[...] = st[...]


def _scan_call(pre, s0, eye):
    B, T, _ = pre['r'].shape
    nb = 4 if B % 4 == 0 else 2
    tc = min(128 if nb == 2 else 32, T)
    nj = T // tc
    fwd = pl.BlockSpec((nb, tc, D_RWKV), lambda bi, j: (bi, j, 0))
    bwd = pl.BlockSpec((nb, tc, D_RWKV), lambda bi, j: (bi, nj - 1 - j, 0))
    yfwd = pl.BlockSpec((nb, tc, H_RWKV, RWKV_HEAD), lambda bi, j: (bi, j, 0, 0))
    ybwd = pl.BlockSpec((nb, tc, H_RWKV, RWKV_HEAD), lambda bi, j: (bi, nj - 1 - j, 0, 0))
    st_spec = pl.BlockSpec((nb, 2, N_GRP, RWKV_HEAD, MXU_DIM), lambda bi, j: (bi, 0, 0, 0, 0))
    y_shape = jax.ShapeDtypeStruct((B, T, H_RWKV, RWKV_HEAD), F32)
    befwd = pl.BlockSpec((nb, tc, LANE), lambda bi, j: (bi, j, 0))
    bebwd = pl.BlockSpec((nb, tc, LANE), lambda bi, j: (bi, nj - 1 - j, 0))
    per = tc // SUBLANE
    last8 = T // SUBLANE - 1
    hfwd = pl.BlockSpec((nb, SUBLANE, D_RWKV), lambda bi, j: (bi, jnp.minimum((j + 1) * per, last8), 0))
    hbwd = pl.BlockSpec((nb, SUBLANE, D_RWKV), lambda bi, j: (bi, jnp.maximum((nj - 1 - j) * per - 1, 0), 0))
    first = pl.BlockSpec((nb, SUBLANE, D_RWKV), lambda bi, j: (bi, 0, 0))

    wc0, vk0, be0, wc1, vk1, be1 = (pre[n] for n in ('wc0', 'vk0', 'be0', 'wc1', 'vk1', 'be1'))
    v = pre['v']
    pad = jnp.zeros((B, SUBLANE - 2, D_RWKV), F32)
    c_first = jnp.concatenate([pre['c'][:, 0:1], pre['c'][:, T - 1:T], pad], axis=1)
    v_first = jnp.concatenate([v[:, 0:1], v[:, T - 1:T], pad], axis=1)
    return pl.pallas_call(
        functools.partial(_scan_kernel, tc=tc),
        grid=(B // nb, nj),
        in_specs=[fwd] * 6 + [hfwd, fwd, befwd] + [bwd] * 6 + [hbwd, bwd, bebwd] + [
            first, first,
            st_spec,
            pl.BlockSpec((RWKV_HEAD, MXU_DIM), lambda bi, j: (0, 0)),
            pl.BlockSpec((H_RWKV, D_RWKV), lambda bi, j: (0, 0))],
        out_specs=[yfwd, ybwd, st_spec],
        out_shape=[y_shape, y_shape, jax.ShapeDtypeStruct((B, 2, N_GRP, RWKV_HEAD, MXU_DIM), F32)],
        scratch_shapes=[pltpu.VMEM((nb, 2, N_GRP, RWKV_HEAD, MXU_DIM), F32),
                        pltpu.VMEM((nb, 2, RWKV_HEAD, 4 * SCAN_ROWS), F32)],
        compiler_params=_params(("arbitrary", "arbitrary"), 52),
        name="rwkv_scan",
    )(pre['r'], wc0, pre['w0'], pre['b0'], pre['k0'], v, v, vk0, be0,
      pre['r'], wc1, pre['w1'], pre['b1'], pre['k1'], v, v, vk1, be1,
      c_first, v_first, s0, eye.astype(BF16), _scan_head_mask())


def _rwkv_post_kernel(y0_ref, y1_ref, bonus_ref, g_ref, gw_ref, gb_ref, ones_ref, o_ref):
    ones_blk = ones_ref[...]
    y = y0_ref[...] + y1_ref[...]
    mu = _segsum(y, ones_blk) * (1.0 / RWKV_HEAD)
    yc = y - mu
    var = _segsum(yc * yc, ones_blk) * (1.0 / RWKV_HEAD)
    yn = yc * lax.rsqrt(var + RWKV_GN_EPS)
    out = (yn * gw_ref[...] + gb_ref[...] + bonus_ref[...]) * g_ref[...]
    o_ref[...] = out.astype(BF16)


def _rwkv_post_call(y0, y1, bonus, g, gn_w, gn_b, ones_blk):
    T = y0.shape[0]
    tt = 512
    blk = pl.BlockSpec((tt, D_RWKV), lambda i: (i, 0))
    row = pl.BlockSpec((1, D_RWKV), lambda i: (0, 0))
    return pl.pallas_call(
        _rwkv_post_kernel,
        grid=(T // tt,),
        in_specs=[blk, blk, blk, blk, row, row, pl.BlockSpec((MXU_DIM, MXU_DIM), lambda i: (0, 0))],
        out_specs=blk,
        out_shape=jax.ShapeDtypeStruct((T, D_RWKV), BF16),
        compiler_params=_params(("parallel",), 40),
        name="rwkv_post",
    )(y0, y1, bonus, g, gn_w, gn_b, ones_blk)


def _rope128(x, cos_t, sin_t):
    return x * cos_t + pltpu.roll(x, QK_ROPE, 1) * sin_t


def _pack_kv(kv, kr_rot, k_o, v_o):
    for h in range(H_MLA):
        k_o[:, h * HEAD_SLOT:h * HEAD_SLOT + QK_NOPE] = kv[:, h * HEAD_SLOT:h * HEAD_SLOT + QK_NOPE].astype(BF16)
        k_o[:, h * HEAD_SLOT + QK_NOPE:(h + 1) * HEAD_SLOT] = kr_rot.astype(BF16)
        v_o[:, h * V_HEAD:(h + 1) * V_HEAD] = kv[:, h * HEAD_SLOT + QK_NOPE:(h + 1) * HEAD_SLOT].astype(BF16)


def _mla_prep_kernel(z_ref, cos_ref, sin_ref, qn_ref, kvn_ref, wq_ref, wkv_ref,
                     q_o, k_o, v_o, ckv_o, kr_o):
    z = z_ref[...]
    cq = z[:, 0:Q_RANK]
    ckv = z[:, Q_RANK:Q_RANK + KV_RANK]
    krp = z[:, Q_RANK + KV_RANK:MLA_COLS]
    cos_t = cos_ref[...]
    sin_t = sin_ref[...]
    cq = (cq * lax.rsqrt(jnp.mean(cq * cq, axis=-1, keepdims=True) + 1e-6)) * qn_ref[...]
    ckv = (ckv * lax.rsqrt(jnp.mean(ckv * ckv, axis=-1, keepdims=True) + 1e-6)) * kvn_ref[...]
    ckv_o[...] = ckv
    kr_o[...] = krp
    q = _dot(cq.astype(BF16), wq_ref[...]) * (ATTN_SCALE * LOG2_E)
    for h in range(H_MLA):
        q_o[:, h * HEAD_SLOT:h * HEAD_SLOT + QK_NOPE] = q[:, h * HEAD_SLOT:h * HEAD_SLOT + QK_NOPE].astype(BF16)
        q_o[:, h * HEAD_SLOT + QK_NOPE:(h + 1) * HEAD_SLOT] = _rope128(
            q[:, h * HEAD_SLOT + QK_NOPE:(h + 1) * HEAD_SLOT], cos_t, sin_t).astype(BF16)
    kv = _dot(ckv.astype(BF16), wkv_ref[...])
    _pack_kv(kv, _rope128(krp, cos_t, sin_t), k_o, v_o)


def _mla_prep_call(z_m, cos_t, sin_t, seq_len, p):
    T = z_m.shape[0]
    tm = min(512, seq_len)
    per_seq = seq_len // tm
    row = lambda n: pl.BlockSpec((1, n), lambda i: (0, 0))
    blk = lambda n: pl.BlockSpec((tm, n), lambda i: (i, 0))
    tab = pl.BlockSpec((tm, LANE), lambda i: (i % per_seq, 0))
    return pl.pallas_call(
        _mla_prep_kernel,
        grid=(T // tm,),
        in_specs=[blk(MLA_COLS), tab, tab, row(Q_RANK), row(KV_RANK),
                  pl.BlockSpec((Q_RANK, H_MLA * HEAD_SLOT), lambda i: (0, 0)),
                  pl.BlockSpec((KV_RANK, H_MLA * HEAD_SLOT), lambda i: (0, 0))],
        out_specs=[blk(H_MLA * HEAD_SLOT), blk(H_MLA * HEAD_SLOT), blk(H_MLA * V_HEAD),
                   blk(KV_RANK), blk(LANE)],
        out_shape=[jax.ShapeDtypeStruct((T, H_MLA * HEAD_SLOT), BF16),
                   jax.ShapeDtypeStruct((T, H_MLA * HEAD_SLOT), BF16),
                   jax.ShapeDtypeStruct((T, H_MLA * V_HEAD), BF16),
                   jax.ShapeDtypeStruct((T, KV_RANK), F32),
                   jax.ShapeDtypeStruct((T, LANE), F32)],
        compiler_params=_params(("parallel",), 48),
        name="mla_prep",
    )(z_m, cos_t, sin_t, p['q_norm'], p['kv_norm'], p['w_qb'], p['w_kvb'])


def _ctx_kv_kernel(ckv_ref, kr_ref, wkv_ref, k_o, v_o):
    kv = _dot(ckv_ref[...].astype(BF16), wkv_ref[...])
    _pack_kv(kv, kr_ref[...], k_o, v_o)


def _ctx_kv_call(ckv_ctx, kr_ctx_pad, w_kvb):
    T = ckv_ctx.shape[0]
    tm = min(512, T)
    blk = lambda n: pl.BlockSpec((tm, n), lambda i: (i, 0))
    return pl.pallas_call(
        _ctx_kv_kernel,
        grid=(T // tm,),
        in_specs=[blk(KV_RANK), blk(LANE), pl.BlockSpec((KV_RANK, H_MLA * HEAD_SLOT), lambda i: (0, 0))],
        out_specs=[blk(H_MLA * HEAD_SLOT), blk(H_MLA * V_HEAD)],
        out_shape=[jax.ShapeDtypeStruct((T, H_MLA * HEAD_SLOT), BF16),
                   jax.ShapeDtypeStruct((T, H_MLA * V_HEAD), BF16)],
        compiler_params=_params(("parallel",), 32),
        name="mla_ctx_kv",
    )(ckv_ctx, kr_ctx_pad, w_kvb)


ATTN_ROWS = 64


def _attn_kernel(q_ref, k_ref, v_ref, o_ref, s_sc, p_sc, *, kc, heads):
    tq, tk = q_ref.shape[0], s_sc.shape[1]
    hq = tq // 2
    nt = (((1,), (1,)), ((), ()))
    units = [(hd, half) for hd in range(heads) for half in range(2)]

    def rows_of(hd, half):
        return hd * tq + half * hq

    def scores(hd, half):
        q = q_ref[half * hq:(half + 1) * hq, hd * HEAD_SLOT:(hd + 1) * HEAD_SLOT]
        r0 = rows_of(hd, half)
        for c in range(tk // kc):
            s_sc[r0:r0 + hq, c * kc:(c + 1) * kc] = lax.dot_general(
                q, k_ref[c * kc:(c + 1) * kc, hd * HEAD_SLOT:(hd + 1) * HEAD_SLOT], nt,
                preferred_element_type=F32)

    def softmax(hd, half):
        sums = []
        for r in range(hq // ATTN_ROWS):
            r0 = rows_of(hd, half) + r * ATTN_ROWS
            rows = slice(r0, r0 + ATTN_ROWS)
            mpart = s_sc[rows, 0:LANE]
            for t in range(1, tk // LANE):
                mpart = jnp.maximum(mpart, s_sc[rows, t * LANE:(t + 1) * LANE])
            m = jnp.max(mpart, axis=-1, keepdims=True)
            lpart = jnp.zeros((ATTN_ROWS, LANE), F32)
            for t in range(tk // LANE):
                p = jnp.exp2(s_sc[rows, t * LANE:(t + 1) * LANE] - m)
                lpart = lpart + p
                p_sc[rows, t * LANE:(t + 1) * LANE] = p.astype(BF16)
            sums.append(jnp.sum(lpart, axis=-1, keepdims=True))
        return jnp.concatenate(sums, axis=0)

    def weighted_values(hd, half, l):
        r0 = rows_of(hd, half)
        acc = jnp.zeros((hq, V_HEAD), F32)
        for c in range(tk // kc):
            acc = acc + _dot(p_sc[r0:r0 + hq, c * kc:(c + 1) * kc],
                             v_ref[c * kc:(c + 1) * kc, hd * V_HEAD:(hd + 1) * V_HEAD])
        o_ref[half * hq:(half + 1) * hq, hd * V_HEAD:(hd + 1) * V_HEAD] = (acc / l).astype(BF16)

    for u in units:
        scores(*u)
    for u in units:
        weighted_values(*u, softmax(*u))


def _attn_call(q, k, v):
    B, Tq, _ = q.shape
    Tk = k.shape[1]
    tq = min(512, Tq)
    heads = H_MLA if Tk <= 512 else 1
    return pl.pallas_call(
        functools.partial(_attn_kernel, kc=MXU_DIM, heads=heads),
        grid=(B, H_MLA // heads, Tq // tq),
        in_specs=[pl.BlockSpec((None, tq, heads * HEAD_SLOT), lambda b, h, i: (b, i, h)),
                  pl.BlockSpec((None, Tk, heads * HEAD_SLOT), lambda b, h, i: (b, 0, h)),
                  pl.BlockSpec((None, Tk, heads * V_HEAD), lambda b, h, i: (b, 0, h))],
        out_specs=pl.BlockSpec((None, tq, heads * V_HEAD), lambda b, h, i: (b, i, h)),
        out_shape=jax.ShapeDtypeStruct((B, Tq, H_MLA * V_HEAD), BF16),
        scratch_shapes=[pltpu.VMEM((heads * tq, Tk), F32), pltpu.VMEM((heads * tq, Tk), BF16)],
        compiler_params=_params(("parallel", "parallel", "arbitrary"), 48),
        name="mla_attention",
    )(q, k, v)


def _conv3_kernel(*refs, tt, half_len):
    ins, (cw_refs, cb_refs), outs = refs[0:12], (refs[12:15], refs[15:18]), refs[18:]
    row0 = pl.program_id(0) * tt
    for s in range(3):
        even, odd = ins[4 * s][...], ins[4 * s + 1][...]
        odd_prev = _shift_prev(odd, ins[4 * s + 2][...], row0, half_len)
        even_next = _shift_next(even, ins[4 * s + 3][...], row0, half_len)
        cw = cw_refs[s][...]
        bias = cb_refs[s][...]
        y_even = cw[0:1, :] * odd_prev + cw[1:2, :] * even + cw[2:3, :] * odd + bias
        y_odd = cw[0:1, :] * even + cw[1:2, :] * odd + cw[2:3, :] * even_next + bias
        outs[s][0] = y_even
        outs[s][1] = y_odd
        if s == 2:
            outs[3][0] = y_even.astype(BF16)
            outs[3][1] = y_odd.astype(BF16)


def _conv3_call(z, half_len, conv_w, conv_b):
    T2 = z.shape[1]
    tt = min(128, half_len)
    per = tt // SUBLANE
    last = T2 // SUBLANE - 1
    in_specs = []
    for s in range(3):
        in_specs += [
            pl.BlockSpec((None, tt, D_HYENA), lambda i, s=s: (0, i, s)),
            pl.BlockSpec((None, tt, D_HYENA), lambda i, s=s: (1, i, s)),
            pl.BlockSpec((None, SUBLANE, D_HYENA), lambda i, s=s: (1, jnp.maximum(i * per - 1, 0), s)),
            pl.BlockSpec((None, SUBLANE, D_HYENA), lambda i, s=s: (0, jnp.minimum((i + 1) * per, last), s))]
    in_specs += [pl.BlockSpec((3, D_HYENA), lambda i, s=s: (0, s)) for s in range(3)]
    in_specs += [pl.BlockSpec((1, D_HYENA), lambda i, s=s: (0, s)) for s in range(3)]
    blk = pl.BlockSpec((2, tt, D_HYENA), lambda i: (0, i, 0))
    f32s = jax.ShapeDtypeStruct((2, T2, D_HYENA), F32)
    return pl.pallas_call(
        functools.partial(_conv3_kernel, tt=tt, half_len=half_len),
        grid=(T2 // tt,),
        in_specs=in_specs,
        out_specs=[blk] * 4,
        out_shape=[f32s, f32s, f32s, jax.ShapeDtypeStruct((2, T2, D_HYENA), BF16)],
        compiler_params=_params(("parallel",), 48),
        name="hyena_conv3",
    )(*([z] * 12), conv_w, conv_w, conv_w, conv_b, conv_b, conv_b)


def _filt_mlp_kernel(z_ref, w1_ref, b1_ref, w2_ref, b2_ref, fr_ref, o_ref):
    h = jnp.sin(fr_ref[0:1, :] * (_dot(z_ref[...].astype(BF16), w1_ref[...].astype(BF16)) + b1_ref[...]))
    h = jnp.sin(fr_ref[1:2, :] * (_dot(h.astype(BF16), w2_ref[...].astype(BF16)) + b2_ref[...]))
    o_ref[...] = h.astype(BF16)


def _filt_mlp_call(zpos, w1p, b1, w2, b2, freq):
    L = zpos.shape[0]
    return pl.pallas_call(
        _filt_mlp_kernel,
        out_shape=jax.ShapeDtypeStruct((L, FILT_HIDDEN), BF16),
        compiler_params=pltpu.CompilerParams(vmem_limit_bytes=32 * MIB),
        name="hyena_filter_mlp",
    )(zpos, w1p, b1, w2, b2, freq)


def _filt_gen_kernel(h_ref, tn_ref, dl_ref, w00, w01, w10, w11, o_ref, taps_sc):
    h = h_ref[...]
    L = h.shape[0]
    win = jnp.exp(-tn_ref[...] * dl_ref[...])
    not_first = lax.broadcasted_iota(jnp.int32, (L, 1), 0) > 0
    ws = ((w00, w01), (w10, w11))

    def emit(k, taps):
        taps_sc[...] = taps
        for par in range(2):
            o_ref[k, par] = taps_sc[pl.ds(par, L // 2, stride=2), :].astype(BF16)

    for n in range(2):
        causal = _dot(h, ws[n][0][...].astype(BF16)) * win
        anti = jnp.where(not_first, _dot(h, ws[n][1][...].astype(BF16)) * win, 0.0)
        norm = (jnp.sum(jnp.abs(causal), axis=0, keepdims=True)
                + jnp.sum(jnp.abs(anti), axis=0, keepdims=True))
        emit(2 * n, causal / norm)
        emit(2 * n + 1, anti / norm)


def _filt_gen_call(h2, tnorm, deltas, w3):
    L = h2.shape[0]
    tc = 128
    nc = D_HYENA // tc
    wspec = lambda k: pl.BlockSpec((FILT_HIDDEN, tc), lambda j, k=k: (0, k * nc + j))
    return pl.pallas_call(
        _filt_gen_kernel,
        grid=(nc,),
        in_specs=[pl.BlockSpec((L, FILT_HIDDEN), lambda j: (0, 0)),
                  pl.BlockSpec((L, 1), lambda j: (0, 0)),
                  pl.BlockSpec((1, tc), lambda j: (0, j)),
                  wspec(0), wspec(1), wspec(2), wspec(3)],
        out_specs=pl.BlockSpec((4, 2, L // 2, tc), lambda j: (0, 0, 0, j)),
        out_shape=jax.ShapeDtypeStruct((4, 2, L // 2, D_HYENA), BF16),
        scratch_shapes=[pltpu.VMEM((L, tc), F32)],
        compiler_params=_params(("parallel",), 48),
        name="hyena_filter_gen",
    )(h2, tnorm, deltas, w3, w3, w3, w3)


def _dft_fwd_kernel(f_ref, u_ref, o_ref):
    o_ref[...] = _dot(f_ref[...], u_ref[...])


def _dft_fwd_call(fmat, u):
    B, K, C = u.shape
    M = fmat.shape[0]
    tm = min(512, M)
    tn = min(C, 2048)
    return pl.pallas_call(
        _dft_fwd_kernel,
        grid=(B, C // tn, M // tm),
        in_specs=[pl.BlockSpec((tm, K), lambda b, j, i: (i, 0)),
                  pl.BlockSpec((None, K, tn), lambda b, j, i: (b, 0, j))],
        out_specs=pl.BlockSpec((None, tm, tn), lambda b, j, i: (b, i, j)),
        out_shape=jax.ShapeDtypeStruct((B, M, C), F32),
        compiler_params=_params(("parallel", "parallel", "arbitrary"), 40),
        name="hyena_dft_fwd",
    )(fmat, u)


EDGE_ROWS = 16


def _butterfly(gc, gs, hc, hs, tw_c, tw_s, first):
    tc = hc * tw_c - hs * tw_s
    ts = hs * tw_c + hc * tw_s
    p0 = gc + tc
    p2 = gc - tc
    p1 = gs + ts
    p3 = ts - gs
    if first is not None:
        p1 = jnp.where(first, gs, p1)
        p3 = jnp.where(first, hs, p3)
    return p0, p1, p2, p3


def _cmul(ac, a_s, bc, bs):
    return ac * bc - a_s * bs, ac * bs + a_s * bc


def _filt_planes_kernel(ge_ref, ho_ref, twc_ref, tws_ref, o_ref, *, tr):
    def planes(rows, first):
        tw_c, tw_s = twc_ref[rows, :], tws_ref[rows, :]
        a = _butterfly(ge_ref[0, 0, rows, :], ge_ref[0, 1, rows, :], ho_ref[0, 0, rows, :], ho_ref[0, 1, rows, :],
                       tw_c, tw_s, first)
        b = _butterfly(ge_ref[1, 0, rows, :], ge_ref[1, 1, rows, :], ho_ref[1, 0, rows, :], ho_ref[1, 1, rows, :],
                       tw_c, tw_s, first)
        p1 = a[1] - b[1] if first is None else jnp.where(first, a[1] + b[1], a[1] - b[1])
        o_ref[0, rows, :] = a[0] + b[0]
        o_ref[1, rows, :] = p1
        o_ref[2, rows, :] = a[2] + b[2]
        o_ref[3, rows, :] = a[3] - b[3]

    planes(slice(0, tr), None)

    @pl.when(pl.program_id(1) == 0)
    def _():
        planes(slice(0, EDGE_ROWS), lax.broadcasted_iota(jnp.int32, (EDGE_ROWS, 1), 0) == 0)


def _filt_planes_call(raw, twc, tws):
    _, L, C = raw.shape
    H = L // 2
    tr = min(256, H)
    tc = 512
    blk = lambda par: pl.BlockSpec((None, 2, None, 2, tr, tc), lambda n, i, j: (n, 0, par, 0, i, j))
    tw = pl.BlockSpec((tr, 1), lambda n, i, j: (i, 0))
    raw6 = raw.reshape(2, 2, 2, 2, H, C)
    return pl.pallas_call(
        functools.partial(_filt_planes_kernel, tr=tr),
        grid=(2, H // tr, C // tc),
        in_specs=[blk(0), blk(1), tw, tw],
        out_specs=pl.BlockSpec((None, 4, tr, tc), lambda n, i, j: (n, 0, i, j)),
        out_shape=jax.ShapeDtypeStruct((2, 4, H, C), F32),
        compiler_params=_params(("parallel", "parallel", "parallel"), 40),
        name="hyena_filter_planes",
    )(raw6, raw6, twc, tws)


def _spec_mul_kernel(raw_ref, k_ref, twc_ref, tws_ref, o_ref, *, tr):
    def multiply(rows, first):
        tw_c, tw_s = twc_ref[rows, :], tws_ref[rows, :]
        p0, p1, p2, p3 = _butterfly(raw_ref[0, 0, rows, :], raw_ref[0, 1, rows, :],
                                    raw_ref[1, 0, rows, :], raw_ref[1, 1, rows, :], tw_c, tw_s, first)
        k0, k1, k2, k3 = (k_ref[n, rows, :] for n in range(4))
        yac, yas = _cmul(p0, p1, k0, k1)
        ybc, ybs = _cmul(p2, p3, k2, k3)
        if first is not None:
            ymc, yms = _cmul(p1, p3, k1, k3)
            yac = jnp.where(first, p0 * k0, yac)
            ybc = jnp.where(first, p2 * k2, ybc)
        dc = yac - ybc
        ds = yas + ybs
        e_s, o_c, o_s = yas - ybs, dc * tw_c + ds * tw_s, ds * tw_c - dc * tw_s
        if first is not None:
            e_s, o_c, o_s = jnp.where(first, ymc, e_s), jnp.where(first, dc, o_c), jnp.where(first, yms, o_s)
        o_ref[0, 0, rows, :] = (yac + ybc).astype(BF16)
        o_ref[0, 1, rows, :] = e_s.astype(BF16)
        o_ref[1, 0, rows, :] = o_c.astype(BF16)
        o_ref[1, 1, rows, :] = o_s.astype(BF16)

    multiply(slice(0, tr), None)

    @pl.when(pl.program_id(1) == 0)
    def _():
        multiply(slice(0, EDGE_ROWS), lax.broadcasted_iota(jnp.int32, (EDGE_ROWS, 1), 0) == 0)


def _spec_mul_call(raw, k_planes, order, twc, tws):
    B2, L, C = raw.shape
    B, H = B2 // 2, L // 2
    tr = min(256, H)
    tc = 512
    tw = pl.BlockSpec((tr, 1), lambda b, i, j: (i, 0))
    out = pl.pallas_call(
        functools.partial(_spec_mul_kernel, tr=tr),
        grid=(B, H // tr, C // tc),
        in_specs=[pl.BlockSpec((2, None, 2, tr, tc), lambda b, i, j: (0, b, 0, i, j)),
                  pl.BlockSpec((None, 4, tr, tc), lambda b, i, j: (order, 0, i, j)),
                  tw, tw],
        out_specs=pl.BlockSpec((2, None, 2, tr, tc), lambda b, i, j: (0, b, 0, i, j)),
        out_shape=jax.ShapeDtypeStruct((2, B, 2, H, C), BF16),
        compiler_params=_params(("parallel", "parallel", "parallel"), 40),
        name="hyena_spectral_mul",
    )(raw.reshape(2, B, 2, H, C), k_planes, twc, tws)
    return out.reshape(B2, L, C)


def _dft_inv_kernel(f_ref, y_ref, gate_ref, u_ref, bias_ref, o_ref, ob_ref):
    conv = _dot(f_ref[...], y_ref[...])
    out = gate_ref[...] * (conv + u_ref[...] * bias_ref[...])
    o_ref[...] = out
    ob_ref[...] = out.astype(BF16)


def _dft_inv_call(imat, y_spec, gate, u, bias):
    B, M, C = u.shape
    K = imat.shape[1]
    tm = min(512, M)
    tn = 1024 if K > 1024 else min(C, 2048)
    blk = pl.BlockSpec((None, tm, tn), lambda b, j, i: (b, i, j))
    return pl.pallas_call(
        _dft_inv_kernel,
        grid=(B, C // tn, M // tm),
        in_specs=[pl.BlockSpec((tm, K), lambda b, j, i: (i, 0)),
                  pl.BlockSpec((None, K, tn), lambda b, j, i: (b, 0, j)),
                  blk, blk, pl.BlockSpec((1, tn), lambda b, j, i: (0, j))],
        out_specs=[blk, blk],
        out_shape=[jax.ShapeDtypeStruct((B, M, C), F32), jax.ShapeDtypeStruct((B, M, C), BF16)],
        compiler_params=_params(("parallel", "parallel", "arbitrary"), 48),
        name="hyena_dft_inv",
    )(imat, y_spec, gate, u, bias)


def _dft_tables(L):
    H = L // 2
    lo = min(64, H)
    hi = H // lo
    g = jnp.arange(H, dtype=jnp.int32)
    theta = 2.0 * math.pi / L
    ang_hi = ((g[:, None] * (jnp.arange(hi, dtype=jnp.int32) * lo)[None, :]) % L).astype(F32) * theta
    ang_lo = ((g[:, None] * jnp.arange(lo, dtype=jnp.int32)[None, :]) % L).astype(F32) * theta
    ch, sh, cl, sl = jnp.cos(ang_hi), jnp.sin(ang_hi), jnp.cos(ang_lo), jnp.sin(ang_lo)
    cos_m = (ch[:, :, None] * cl[:, None, :] - sh[:, :, None] * sl[:, None, :]).reshape(H, H)
    sin_m = (sh[:, :, None] * cl[:, None, :] + ch[:, :, None] * sl[:, None, :]).reshape(H, H)
    alt = jnp.where(jnp.arange(H) % 2 == 0, 1.0, -1.0).astype(F32)
    sin_m = jnp.where((g == 0)[:, None], alt[None, :], sin_m)
    fwd = jnp.concatenate([cos_m, sin_m], axis=0)
    n = 2.0 * L
    w_cos = jnp.where(g == 0, 1.0 / n, 2.0 / n).astype(F32)
    inv = jnp.concatenate([cos_m * w_cos[:, None], sin_m * (2.0 / n)], axis=0).T
    ang_tw = g.astype(F32) * (math.pi / L)
    return fwd.astype(BF16), inv.astype(BF16), jnp.cos(ang_tw)[:, None], jnp.sin(ang_tw)[:, None]


def _filter_positions(L):
    t = jnp.arange(L, dtype=F32)
    t_norm = t / max(L - 1, 1)
    bands = (POS_EMB - 1) // 2
    freqs = jnp.linspace(1e-4, bands - 1, bands, dtype=F32)
    ang = (2.0 * math.pi / L) * t[:, None] * freqs[None, :]
    z = jnp.concatenate([t_norm[:, None], jnp.cos(ang), -jnp.sin(ang)], axis=-1)
    return jnp.pad(z, ((0, 0), (0, POS_PAD - POS_EMB))), t_norm[:, None]


def _hyena_deltas():
    return jnp.linspace(abs(math.log(HYENA_TARGET)) / SLOW_DECAY_PCT,
                        abs(math.log(HYENA_TARGET)) / FAST_DECAY_PCT, D_HYENA, dtype=F32)[None, :]


def _hyena_filter_spectrum(L, tables, p):
    fwd, _, twc, tws = tables
    zpos, tnorm = _filter_positions(L)
    h2 = _filt_mlp_call(zpos, p['filt_w1'], p['filt_b1'], p['filt_w2'], p['filt_b2'], p['filt_freq'])
    filt = _filt_gen_call(h2, tnorm, _hyena_deltas(), p['filt_w3'])
    raw = _dft_fwd_call(fwd, filt.reshape(8, L // 2, D_HYENA))
    return _filt_planes_call(raw, twc, tws)


def _hyena_mixer(z, B, L, tables, k_planes, p):
    fwd, inv, twc, tws = tables
    H = L // 2
    x1, x2, v, vb = _conv3_call(z, H, p['conv_w'], p['conv_b'])
    shp = (2 * B, H, D_HYENA)
    u, ub = v.reshape(shp), vb.reshape(shp)
    for n, gate in enumerate((x1, x2)):
        raw = _dft_fwd_call(fwd, ub)
        yspec = _spec_mul_call(raw, k_planes, n, twc, tws)
        u, ub = _dft_inv_call(inv, yspec, gate.reshape(shp), u, p['bias'][n:n + 1])
    return ub.reshape(2, B * H, D_HYENA)


def _rope_swap(w):
    q = QK_ROPE // 4
    return jnp.concatenate([w[..., q:2 * q], w[..., 0:q], w[..., 3 * q:4 * q], w[..., 2 * q:3 * q]], axis=-1)


def _pad_cols(w, n):
    return jnp.pad(w, [(0, 0)] * (w.ndim - 1) + [(0, n - w.shape[-1])])


def _pack_even(e, w_in_even, mu_prev, mu_next, rwkv_w0, rwkv_w2, rwkv_a0, rwkv_a2, rwkv_g2,
               rwkv_kk, rwkv_ka, rwkv_rk, rwkv_gn_w, rwkv_gn_b, mla_q_norm, mla_kv_norm,
               mla_w_qb, mla_w_kvb, w_out_even):
    n_r = 3 * D_RWKV + W_LORA + A_LORA + G_LORA
    w_in = w_in_even[e]
    w_r = _pad_cols(w_in[:, :n_r], RWKV_COLS).astype(BF16)
    w_m = w_in[:, n_r:]
    kr_cols = w_m[:, Q_RANK + KV_RANK:]
    w_m = jnp.concatenate([w_m, _rope_swap(kr_cols)], axis=-1).astype(BF16)
    small_rows = lambda w, off: jnp.pad(w, [(0, 0)] * (w.ndim - 2)
                                        + [(off, RWKV_SMALL - off - w.shape[-2]), (0, 0)]).astype(BF16)
    wq = mla_w_qb[e].reshape(Q_RANK, H_MLA, QK_NOPE + QK_ROPE)
    wq = jnp.concatenate([wq, _rope_swap(wq[..., QK_NOPE:])], axis=-1).reshape(Q_RANK, H_MLA * HEAD_SLOT)
    blk = jnp.arange(MXU_DIM) // RWKV_HEAD
    return {
        'w_r': w_r, 'w_m': w_m,
        'mu_prev': _pad_cols(mu_prev[e][None, :], RWKV_COLS),
        'mu_next': _pad_cols(mu_next[e][None, :], RWKV_COLS),
        'k_k': rwkv_kk[e][None, :], 'k_a': rwkv_ka[e][None, :],
        'r_k': rwkv_rk[e].reshape(1, D_RWKV),
        'w0': rwkv_w0[e], 'w2': small_rows(rwkv_w2[e], 0),
        'a0': rwkv_a0[e], 'a2': small_rows(rwkv_a2[e], W_LORA),
        'g2': small_rows(rwkv_g2[e], W_LORA + A_LORA),
        'gn_w': rwkv_gn_w[e][None, :], 'gn_b': rwkv_gn_b[e][None, :],
        'q_norm': mla_q_norm[e][None, :], 'kv_norm': mla_kv_norm[e][None, :],
        'w_qb': wq.astype(BF16), 'w_kvb': mla_w_kvb[e].astype(BF16),
        'w_out': w_out_even[e].astype(BF16),
        'ones_blk': (blk[:, None] == blk[None, :]).astype(BF16),
        'eye': (jnp.arange(RWKV_HEAD)[:, None] == (jnp.arange(MXU_DIM) % RWKV_HEAD)[None, :]).astype(F32),
    }


def _rope_tables(L):
    rows = L // GRID_W
    row = jnp.repeat(jnp.arange(rows, dtype=F32), GRID_W)
    col = jnp.tile(jnp.arange(GRID_W, dtype=F32), rows)
    half = QK_ROPE // 2
    inv = 1.0 / (ROPE_THETA ** (jnp.arange(0, half, 2, dtype=F32) / half))
    ar, ac = row[:, None] * inv[None, :], col[:, None] * inv[None, :]
    cos_t = jnp.concatenate([jnp.cos(ar), jnp.cos(ar), jnp.cos(ac), jnp.cos(ac)], axis=-1)
    sin_t = jnp.concatenate([-jnp.sin(ar), jnp.sin(ar), -jnp.sin(ac), jnp.sin(ac)], axis=-1)
    return _pad_cols(cos_t, LANE), _pad_cols(sin_t, LANE)


def _state_to_groups(s):
    B = s.shape[0]
    s = s.reshape(B, 2, N_GRP, H_RWKV // N_GRP, RWKV_HEAD, RWKV_HEAD)
    return jnp.swapaxes(s, 3, 4).reshape(B, 2, N_GRP, RWKV_HEAD, MXU_DIM)


def _groups_to_state(s):
    B = s.shape[0]
    s = s.reshape(B, 2, N_GRP, RWKV_HEAD, H_RWKV // N_GRP, RWKV_HEAD)
    return jnp.swapaxes(s, 3, 4).reshape(B, 2, H_RWKV, RWKV_HEAD, RWKV_HEAD)


def _even_mixer(x, mod_l, goff, B, L, gamma, p, rope, ctx):
    group_tokens = x.shape[0] if ctx is None else L
    z_r = _inproj_call(x, mod_l, goff, group_tokens, gamma, p['w_r'], RWKV_COLS // 3)
    z_m = _inproj_call(x, mod_l, goff, group_tokens, gamma, p['w_m'], MLA_COLS)
    names = ('r', 'v', 'c', 'w0', 'b0', 'k0', 'wc0', 'vk0', 'be0', 'w1', 'b1', 'k1', 'wc1', 'vk1', 'be1',
             'bonus', 'g')
    pre = dict(zip(names, _rwkv_prep_call(z_r, L, p)))
    seq = {n: pre[n].reshape(B, L, pre[n].shape[-1]) for n in names[:15]}
    if ctx is None:
        s0 = jnp.zeros((B, 2, N_GRP, RWKV_HEAD, MXU_DIM), F32)
    else:
        s0 = _state_to_groups(ctx[2].astype(F32))
    y0, y1, s_fin = _scan_call(seq, s0, p['eye'])
    y_r = _rwkv_post_call(y0.reshape(B * L, D_RWKV), y1.reshape(B * L, D_RWKV), pre['bonus'], pre['g'],
                          p['gn_w'], p['gn_b'], p['ones_blk'])

    q, k, v, ckv, krp = _mla_prep_call(z_m, rope[0], rope[1], L, p)
    q = q.reshape(B, L, H_MLA * HEAD_SLOT)
    k = k.reshape(B, L, H_MLA * HEAD_SLOT)
    v = v.reshape(B, L, H_MLA * V_HEAD)
    if ctx is not None:
        P = ctx[0].shape[1]
        k_ctx, v_ctx = _ctx_kv_call(ctx[0].reshape(B * P, KV_RANK),
                                    _pad_cols(ctx[1].reshape(B * P, QK_ROPE), LANE), p['w_kvb'])
        k = jnp.concatenate([k, k_ctx.reshape(B, P, H_MLA * HEAD_SLOT)], axis=1)
        v = jnp.concatenate([v, v_ctx.reshape(B, P, H_MLA * V_HEAD)], axis=1)
    y_m = _attn_call(q, k, v).reshape(B * L, H_MLA * V_HEAD)
    x = _outproj_call(x, mod_l, goff, group_tokens, y_r, y_m, p['w_out'])
    state = (_groups_to_state(s_fin), ckv.reshape(B, L, KV_RANK), krp[:, :QK_ROPE].reshape(B, L, QK_ROPE))
    return x, state


def _odd_mixer(x, mod_l, goff, group_tokens, B, L, gamma, tables, k_planes, p):
    z = _inproj_parity_call(x, mod_l, goff, group_tokens, gamma, p['w_in'], 1536)
    y = _hyena_mixer(z, B, L, tables, k_planes, p)
    return _outproj_parity_call(x, mod_l, goff, group_tokens, y, p['w_out'])


def kernel(x_prompt, x_sample, cache_mla_ckv, cache_mla_krope, state_rwkv, c, c_ctx,
           w_mod, b_mod, norm_g, w_ffn_in, w_ffn_out, final_norm_g,
           w_in_even, mu_prev, mu_next, rwkv_w0, rwkv_w2, rwkv_a0, rwkv_a2, rwkv_g2,
           rwkv_kk, rwkv_ka, rwkv_rk, rwkv_gn_w, rwkv_gn_b,
           mla_q_norm, mla_kv_norm, mla_w_qb, mla_w_kvb, w_out_even,
           w_in_odd, hy_conv_w, hy_conv_b, hy_filt_w1, hy_filt_b1, hy_filt_w2, hy_filt_b2,
           hy_filt_w3, hy_filt_freq, hy_bias, w_out_odd):
    Bp, Lp, D = x_prompt.shape
    Bs, Ls, _ = x_sample.shape
    depth = w_mod.shape[0]
    xp = x_prompt.reshape(Bp * Lp, D)
    xs = x_sample.reshape(Bs * Ls, D)
    Tp = Bp * Lp

    cvec = jnp.concatenate([c_ctx[None, :], c, jnp.zeros((SUBLANE - 1 - Bs, D), F32)], axis=0)
    mod = _mod_call(cvec, w_mod, b_mod)

    rope_p = (_pad_cols(jnp.ones((Lp, QK_ROPE), F32), LANE), jnp.zeros((Lp, LANE), F32))
    rope_s = _rope_tables(Ls)
    tabs_p = tabs_s = None
    w_in = w_ffn_in.astype(BF16)
    w_out = w_ffn_out.astype(BF16)
    new_ckv, new_kr, new_s = [], [], []
    for l in range(depth):
        mod_l = mod[l]
        gam = [norm_g[l, s][None, :] for s in range(3)]
        xp = _ffn_call(xp, mod_l, 0, Tp, gam[0], w_in, w_out, l, 0, 0)
        xs = _ffn_call(xs, mod_l, 1, Ls, gam[0], w_in, w_out, l, 0, 0)
        if l % 2 == 0:
            e = l // 2
            p = _pack_even(e, w_in_even, mu_prev, mu_next, rwkv_w0, rwkv_w2, rwkv_a0, rwkv_a2, rwkv_g2,
                           rwkv_kk, rwkv_ka, rwkv_rk, rwkv_gn_w, rwkv_gn_b, mla_q_norm, mla_kv_norm,
                           mla_w_qb, mla_w_kvb, w_out_even)
            ctx = (cache_mla_ckv[:, e], cache_mla_krope[:, e], state_rwkv[:, e])
            xp, st = _even_mixer(xp, mod_l, 0, Bp, Lp, gam[1], p, rope_p, None)
            xs, _ = _even_mixer(xs, mod_l, 1, Bs, Ls, gam[1], p, rope_s, ctx)
            new_s.append(st[0].astype(x_prompt.dtype))
            new_ckv.append(st[1])
            new_kr.append(st[2])
        else:
            o = l // 2
            p = {'w_in': w_in_odd[o].astype(BF16), 'conv_w': hy_conv_w[o], 'conv_b': hy_conv_b[o][None, :],
                 'filt_w1': jnp.pad(hy_filt_w1[o], ((0, POS_PAD - POS_EMB), (0, 0))),
                 'filt_b1': hy_filt_b1[o][None, :], 'filt_w2': hy_filt_w2[o],
                 'filt_b2': hy_filt_b2[o][None, :], 'filt_w3': hy_filt_w3[o],
                 'filt_freq': hy_filt_freq[o], 'bias': hy_bias[o], 'w_out': w_out_odd[o].astype(BF16)}
            if tabs_p is None:
                tabs_p, tabs_s = _dft_tables(Lp), _dft_tables(Ls)
            ks_p = _hyena_filter_spectrum(Lp, tabs_p, p)
            ks_s = _hyena_filter_spectrum(Ls, tabs_s, p)
            xp = _odd_mixer(xp, mod_l, 0, Tp, Bp, Lp, gam[1], tabs_p, ks_p, p)
            xs = _odd_mixer(xs, mod_l, 1, Ls, Bs, Ls, gam[1], tabs_s, ks_s, p)
        xp = _ffn_call(xp, mod_l, 0, Tp, gam[2], w_in, w_out, l, 1, 2)
        xs = _ffn_call(xs, mod_l, 1, Ls, gam[2], w_in, w_out, l, 1, 2)

    fg = final_norm_g[None, :]
    y_prompt = _final_norm_call(xp, fg).reshape(Bp, Lp, D)
    y_sample = _final_norm_call(xs, fg).reshape(Bs, Ls, D)
    return (y_prompt, y_sample, jnp.stack(new_ckv, axis=1), jnp.stack(new_kr, axis=1),
            jnp.stack(new_s, axis=1))
```

```python
import functools
import math

import jax
import jax.numpy as jnp
from jax import lax
from jax.experimental import pallas as pl
from jax.experimental.pallas import tpu as pltpu

F32 = jnp.float32
BF16 = jnp.bfloat16

D_MODEL = 2048
N_MOD = 9
D_FF = 5632
D_RWKV = 1024
RWKV_HEAD = 64
H_RWKV = 16
W_LORA = 64
A_LORA = 64
G_LORA = 160
RWKV_SMALL = 384
RWKV_COLS = 3 * D_RWKV + RWKV_SMALL
RWKV_GN_EPS = 64e-5
H_MLA = 8
QK_NOPE = 128
QK_ROPE = 64
V_HEAD = 128
Q_RANK = 512
KV_RANK = 256
MLA_COLS = Q_RANK + KV_RANK + 2 * QK_ROPE
HEAD_SLOT = 256
ROPE_THETA = 10000.0
ATTN_SCALE = (QK_NOPE + QK_ROPE) ** -0.5
LOG2_E = 1.0 / math.log(2.0)
GRID_W = 64
D_HYENA = 2048
POS_EMB = 33
POS_PAD = 128
FILT_HIDDEN = 64
HYENA_TARGET = 1e-2
FAST_DECAY_PCT = 0.3
SLOW_DECAY_PCT = 1.5
LANE = 128
SUBLANE = 8
MXU_DIM = 256
MIB = 1024 * 1024


def _params(sem, vmem_mib):
    return pltpu.CompilerParams(dimension_semantics=sem, vmem_limit_bytes=vmem_mib * MIB)


def _sigmoid(x):
    return 1.0 / (1.0 + jnp.exp(-x))


def _softplus(x):
    return jnp.maximum(x, 0.0) + jnp.log(1.0 + jnp.exp(-jnp.abs(x)))


def _dot(a, b):
    return jnp.dot(a, b, preferred_element_type=F32)


def _norm_mod(x, gamma, shift, scale):
    xn = x * lax.rsqrt(jnp.mean(x * x, axis=-1, keepdims=True) + 1e-6)
    return xn * (gamma * (1.0 + scale)) + shift


def _mod_kernel(c_ref, w_ref, b_ref, o_ref):
    c = c_ref[...]
    s = c * _sigmoid(c)
    o_ref[0] = _dot(s.astype(BF16), w_ref[0].astype(BF16)) + b_ref[0]


def _mod_call(cvec, w_mod, b_mod):
    L, Dm, N = w_mod.shape
    tn = 1024
    out = pl.pallas_call(
        _mod_kernel,
        grid=(L, N // tn),
        in_specs=[pl.BlockSpec((SUBLANE, Dm), lambda l, j: (0, 0)),
                  pl.BlockSpec((1, Dm, tn), lambda l, j: (l, 0, j)),
                  pl.BlockSpec((1, 1, tn), lambda l, j: (l, 0, j))],
        out_specs=pl.BlockSpec((1, SUBLANE, tn), lambda l, j: (l, 0, j)),
        out_shape=jax.ShapeDtypeStruct((L, SUBLANE, N), F32),
        compiler_params=_params(("arbitrary", "arbitrary"), 40),
        name="adaln_mod",
    )(cvec, w_mod, b_mod.reshape(L, 1, N))
    return out.reshape(L, SUBLANE, N_MOD, Dm)


def _mod_spec(goff, tiles_per_group, nargs):
    if nargs == 1:
        return pl.BlockSpec((None, N_MOD, D_MODEL), lambda i: (goff + i // tiles_per_group, 0, 0))
    return pl.BlockSpec((None, N_MOD, D_MODEL), lambda i, j: (goff + i // tiles_per_group, 0, 0))


def _ffn_kernel(x_ref, mod_ref, g_ref, wg_ref, wu_ref, wo_ref, o_ref, h_sc, acc_sc, *, sub):
    f = pl.program_id(1)

    @pl.when(f == 0)
    def _():
        h = _norm_mod(x_ref[...], g_ref[...], mod_ref[3 * sub:3 * sub + 1, :],
                      mod_ref[3 * sub + 1:3 * sub + 2, :])
        h_sc[...] = h.astype(BF16)
        acc_sc[...] = jnp.zeros_like(acc_sc)

    h = h_sc[...]
    a = _dot(h, wg_ref[...])
    u = _dot(h, wu_ref[...])
    act = (a * _sigmoid(a)) * u
    acc_sc[...] += _dot(act.astype(BF16), wo_ref[...])

    @pl.when(f == pl.num_programs(1) - 1)
    def _():
        o_ref[...] = x_ref[...] + 0.5 * mod_ref[3 * sub + 2:3 * sub + 3, :] * acc_sc[...]


def _ffn_call(x, mod_l, goff, group_tokens, gamma, w_in, w_out, l, s, sub):
    T = x.shape[0]
    tm = min(512, group_tokens)
    tf = 512
    nf = D_FF // tf
    return pl.pallas_call(
        functools.partial(_ffn_kernel, sub=sub),
        grid=(T // tm, nf),
        in_specs=[pl.BlockSpec((tm, D_MODEL), lambda i, f: (i, 0)),
                  _mod_spec(goff, group_tokens // tm, 2),
                  pl.BlockSpec((1, D_MODEL), lambda i, f: (0, 0)),
                  pl.BlockSpec((None, None, D_MODEL, tf), lambda i, f: (l, s, 0, f)),
                  pl.BlockSpec((None, None, D_MODEL, tf), lambda i, f: (l, s, 0, f + nf)),
                  pl.BlockSpec((None, None, tf, D_MODEL), lambda i, f: (l, s, f, 0))],
        out_specs=pl.BlockSpec((tm, D_MODEL), lambda i, f: (i, 0)),
        out_shape=jax.ShapeDtypeStruct((T, D_MODEL), F32),
        scratch_shapes=[pltpu.VMEM((tm, D_MODEL), BF16), pltpu.VMEM((tm, D_MODEL), F32)],
        compiler_params=_params(("parallel", "arbitrary"), 52),
        name="ffn_swiglu",
    )(x, mod_l, gamma, w_in, w_in, w_out)


def _inproj_kernel(x_ref, mod_ref, g_ref, w_ref, o_ref, h_sc, *, col_axis):
    @pl.when(pl.program_id(col_axis) == 0)
    def _():
        h = _norm_mod(x_ref[...], g_ref[...], mod_ref[3:4, :], mod_ref[4:5, :])
        h_sc[...] = h.astype(BF16)

    o_ref[...] = _dot(h_sc[...], w_ref[...])


def _inproj_call(x, mod_l, goff, group_tokens, gamma, w, tn):
    T = x.shape[0]
    N = w.shape[1]
    tm = min(512, group_tokens)
    return pl.pallas_call(
        functools.partial(_inproj_kernel, col_axis=1),
        grid=(T // tm, N // tn),
        in_specs=[pl.BlockSpec((tm, D_MODEL), lambda i, j: (i, 0)),
                  _mod_spec(goff, group_tokens // tm, 2),
                  pl.BlockSpec((1, D_MODEL), lambda i, j: (0, 0)),
                  pl.BlockSpec((D_MODEL, tn), lambda i, j: (0, j))],
        out_specs=pl.BlockSpec((tm, tn), lambda i, j: (i, j)),
        out_shape=jax.ShapeDtypeStruct((T, N), F32),
        scratch_shapes=[pltpu.VMEM((tm, D_MODEL), BF16)],
        compiler_params=_params(("parallel", "arbitrary"), 48),
        name="mixer_inproj",
    )(x, mod_l, gamma, w)


def _rows_of_parity(lane_sc, x, par):
    rows = x.shape[0]
    parts = []
    for c in range(x.shape[1] // LANE):
        lane_sc[c] = x[:, c * LANE:(c + 1) * LANE]
        parts.append(lane_sc[c, pl.ds(par, rows // 2, stride=2), :])
    return jnp.concatenate(parts, axis=1)


def _interleave_rows(lane_sc, even, odd):
    half = even.shape[0]
    parts = []
    for c in range(even.shape[1] // LANE):
        lane_sc[c, pl.ds(0, half, stride=2), :] = even[:, c * LANE:(c + 1) * LANE]
        lane_sc[c, pl.ds(1, half, stride=2), :] = odd[:, c * LANE:(c + 1) * LANE]
        parts.append(lane_sc[c])
    return jnp.concatenate(parts, axis=1)


def _inproj_parity_kernel(x_ref, mod_ref, g_ref, w_ref, o_ref, h_sc, lane_sc, *, half):
    @pl.when(pl.program_id(1) == 0)
    def _():
        h = _norm_mod(x_ref[...], g_ref[...], mod_ref[3:4, :], mod_ref[4:5, :])
        for p in range(2):
            h_sc[p * half:(p + 1) * half, :] = _rows_of_parity(lane_sc, h, p).astype(BF16)

    z = _dot(h_sc[...], w_ref[...])
    o_ref[0] = z[:half]
    o_ref[1] = z[half:]


def _inproj_parity_call(x, mod_l, goff, group_tokens, gamma, w, tn):
    T = x.shape[0]
    N = w.shape[1]
    tm = min(512, group_tokens)
    half = tm // 2
    return pl.pallas_call(
        functools.partial(_inproj_parity_kernel, half=half),
        grid=(T // tm, N // tn),
        in_specs=[pl.BlockSpec((tm, D_MODEL), lambda i, j: (i, 0)),
                  _mod_spec(goff, group_tokens // tm, 2),
                  pl.BlockSpec((1, D_MODEL), lambda i, j: (0, 0)),
                  pl.BlockSpec((D_MODEL, tn), lambda i, j: (0, j))],
        out_specs=pl.BlockSpec((2, half, tn), lambda i, j: (0, i, j)),
        out_shape=jax.ShapeDtypeStruct((2, T // 2, N), F32),
        scratch_shapes=[pltpu.VMEM((tm, D_MODEL), BF16), pltpu.VMEM((D_MODEL // LANE, tm, LANE), F32)],
        compiler_params=_params(("parallel", "arbitrary"), 48),
        name="mixer_inproj_parity",
    )(x, mod_l, gamma, w)


def _outproj_kernel(x_ref, mod_ref, a1_ref, a2_ref, w1_ref, w2_ref, o_ref):
    y = _dot(a1_ref[...], w1_ref[...]) + _dot(a2_ref[...], w2_ref[...])
    o_ref[...] = x_ref[...] + mod_ref[5:6, :] * y


def _outproj_call(x, mod_l, goff, group_tokens, a1, a2, w):
    T = x.shape[0]
    tm = min(512, group_tokens)
    half = D_MODEL // 2
    return pl.pallas_call(
        _outproj_kernel,
        grid=(T // tm,),
        in_specs=[pl.BlockSpec((tm, D_MODEL), lambda i: (i, 0)),
                  _mod_spec(goff, group_tokens // tm, 1),
                  pl.BlockSpec((tm, half), lambda i: (i, 0)),
                  pl.BlockSpec((tm, half), lambda i: (i, 0)),
                  pl.BlockSpec((half, D_MODEL), lambda i: (0, 0)),
                  pl.BlockSpec((half, D_MODEL), lambda i: (1, 0))],
        out_specs=pl.BlockSpec((tm, D_MODEL), lambda i: (i, 0)),
        out_shape=jax.ShapeDtypeStruct((T, D_MODEL), F32),
        compiler_params=_params(("parallel",), 48),
        name="mixer_outproj",
    )(x, mod_l, a1, a2, w, w)


def _outproj_parity_kernel(x_ref, mod_ref, a1_ref, a2_ref, w1_ref, w2_ref, o_ref, lane_sc):
    ys = [_dot(a1_ref[p], w1_ref[...]) + _dot(a2_ref[p], w2_ref[...]) for p in range(2)]
    o_ref[...] = x_ref[...] + mod_ref[5:6, :] * _interleave_rows(lane_sc, ys[0], ys[1])


def _outproj_parity_call(x, mod_l, goff, group_tokens, a, w):
    T = x.shape[0]
    tm = min(512, group_tokens)
    half = tm // 2
    hd = D_MODEL // 2
    return pl.pallas_call(
        _outproj_parity_kernel,
        grid=(T // tm,),
        in_specs=[pl.BlockSpec((tm, D_MODEL), lambda i: (i, 0)),
                  _mod_spec(goff, group_tokens // tm, 1),
                  pl.BlockSpec((2, half, hd), lambda i: (0, i, 0)),
                  pl.BlockSpec((2, half, hd), lambda i: (0, i, 1)),
                  pl.BlockSpec((hd, D_MODEL), lambda i: (0, 0)),
                  pl.BlockSpec((hd, D_MODEL), lambda i: (1, 0))],
        out_specs=pl.BlockSpec((tm, D_MODEL), lambda i: (i, 0)),
        out_shape=jax.ShapeDtypeStruct((T, D_MODEL), F32),
        scratch_shapes=[pltpu.VMEM((D_MODEL // LANE, tm, LANE), F32)],
        compiler_params=_params(("parallel",), 48),
        name="mixer_outproj_parity",
    )(x, mod_l, a, a, w, w)


def _final_norm_kernel(x_ref, g_ref, o_ref):
    x = x_ref[...]
    o_ref[...] = (x * lax.rsqrt(jnp.mean(x * x, axis=-1, keepdims=True) + 1e-6)) * g_ref[...]


def _final_norm_call(x, gamma):
    T = x.shape[0]
    tm = 512
    return pl.pallas_call(
        _final_norm_kernel,
        grid=(T // tm,),
        in_specs=[pl.BlockSpec((tm, D_MODEL), lambda i: (i, 0)),
                  pl.BlockSpec((1, D_MODEL), lambda i: (0, 0))],
        out_specs=pl.BlockSpec((tm, D_MODEL), lambda i: (i, 0)),
        out_shape=jax.ShapeDtypeStruct((T, D_MODEL), F32),
        compiler_params=_params(("parallel",), 32),
        name="final_norm",
    )(x, gamma)


def _shift_prev(cur, halo_prev, row0, seq_len):
    tt = cur.shape[0]
    rid = lax.broadcasted_iota(jnp.int32, (tt, 1), 0)
    pos = jnp.bitwise_and(rid + row0, seq_len - 1)
    prev = pltpu.roll(cur, 1, 0)
    prev = jnp.where(rid == 0, halo_prev[SUBLANE - 1:SUBLANE, :], prev)
    return jnp.where(pos == 0, 0.0, prev)


def _shift_next(cur, halo_next, row0, seq_len):
    tt = cur.shape[0]
    rid = lax.broadcasted_iota(jnp.int32, (tt, 1), 0)
    pos = jnp.bitwise_and(rid + row0, seq_len - 1)
    nxt = pltpu.roll(cur, tt - 1, 0)
    nxt = jnp.where(rid == tt - 1, halo_next[0:1, :], nxt)
    return jnp.where(pos == seq_len - 1, 0.0, nxt)


def _shift_prev_next(cur, halo_prev, halo_next, row0, seq_len):
    return _shift_prev(cur, halo_prev, row0, seq_len), _shift_next(cur, halo_next, row0, seq_len)


def _halo_specs(tt, width, col, total_rows):
    per = tt // SUBLANE
    last = total_rows // SUBLANE - 1
    return [pl.BlockSpec((tt, width), lambda i: (i, col)),
            pl.BlockSpec((SUBLANE, width), lambda i: (jnp.maximum(i * per - 1, 0), col)),
            pl.BlockSpec((SUBLANE, width), lambda i: (jnp.minimum((i + 1) * per, last), col))]


def _segsum(x, ones_blk):
    hi = x.astype(BF16)
    lo = (x - hi.astype(F32)).astype(BF16)
    outs = []
    for g in range(x.shape[1] // MXU_DIM):
        sl = slice(g * MXU_DIM, (g + 1) * MXU_DIM)
        outs.append(_dot(hi[:, sl], ones_blk) + _dot(lo[:, sl], ones_blk))
    return jnp.concatenate(outs, axis=1)


def _rwkv_prep_kernel(z_ref, zp_ref, zn_ref, mup_ref, mun_ref, kk_ref, ka_ref, rk_ref,
                      w0_ref, w2_ref, a0_ref, a2_ref, g2_ref, ones_ref, sel_ref,
                      r_o, v_o, c_o, w0_o, b0_o, k0_o, wc0_o, vk0_o, be0_o,
                      w1_o, b1_o, k1_o, wc1_o, vk1_o, be1_o, bonus_o, g_o, *, tt, seq_len):
    row0 = pl.program_id(0) * tt
    cur = z_ref[...]
    halo_p, halo_n = zp_ref[...], zn_ref[...]
    prev, nxt = _shift_prev_next(cur, halo_p, halo_n, row0, seq_len)
    mup, mun = mup_ref[...], mun_ref[...]
    zs = cur + mup * (prev - cur) + mun * (nxt - cur)
    r = zs[:, 0:D_RWKV]
    k = zs[:, D_RWKV:2 * D_RWKV]
    v = zs[:, 2 * D_RWKV:3 * D_RWKV]
    small = zs[:, 3 * D_RWKV:RWKV_COLS]
    ones_blk = ones_ref[...]

    def unit_keys(keys):
        kk = keys * kk_ref[...]
        return kk / jnp.maximum(jnp.sqrt(_segsum(kk * kk, ones_blk)), 1e-12)

    kk = unit_keys(k)
    c = -kk
    ks = slice(D_RWKV, 2 * D_RWKV)
    hp, hn, mp, mn = halo_p[:, ks], halo_n[:, ks], mup[:, ks], mun[:, ks]
    zc = cur[:, ks]
    k_before = hp[7:8] + mp * (hp[6:7] - hp[7:8]) + mn * (zc[0:1] - hp[7:8])
    k_after = hn[0:1] + mp * (zc[tt - 1:tt] - hn[0:1]) + mn * (hn[1:2] - hn[0:1])
    edge = -unit_keys(jnp.concatenate([k_before, k_after, jnp.zeros((SUBLANE - 2, D_RWKV), F32)], axis=0))
    c_prev = _shift_prev(c, jnp.broadcast_to(edge[0:1], (SUBLANE, D_RWKV)), row0, seq_len)
    c_next = _shift_next(c, jnp.broadcast_to(edge[1:2], (SUBLANE, D_RWKV)), row0, seq_len)

    tw = jnp.tanh(small).astype(BF16)
    sg = _sigmoid(small).astype(BF16)
    xs = small.astype(BF16)
    r_o[...] = r
    v_o[...] = v
    c_o[...] = c
    g_o[...] = _dot(sg, g2_ref[...])

    sel = sel_ref[...]
    bonus = jnp.zeros_like(r)
    outs = ((w0_o, b0_o, k0_o, wc0_o, vk0_o, be0_o, c_next), (w1_o, b1_o, k1_o, wc1_o, vk1_o, be1_o, c_prev))
    for d in range(2):
        wl = -_softplus(-(w0_ref[d:d + 1, :] + _dot(tw, w2_ref[d]))) - 0.5
        a = _sigmoid(a0_ref[d:d + 1, :] + _dot(xs, a2_ref[d]))
        kd = k * (1.0 + (a - 1.0) * ka_ref[...])
        w_o, b_o, k_o, wc_o, vk_o, be_o, c_after = outs[d]
        decay = jnp.exp(-jnp.exp(wl))
        b = kk * a
        w_o[...] = decay
        b_o[...] = b
        k_o[...] = kd
        wc_o[...] = decay * c_after
        vk_o[...] = v * _segsum(kd * c_after, ones_blk)
        bc = b * c_after
        hi = bc.astype(BF16)
        lo = (bc - hi.astype(F32)).astype(BF16)
        be_o[...] = _dot(hi, sel) + _dot(lo, sel)
        bonus = bonus + _segsum(r * kd * rk_ref[...], ones_blk) * v
    bonus_o[...] = bonus


def _rwkv_prep_call(z_r, seq_len, p):
    T = z_r.shape[0]
    tt = min(128, seq_len)
    row = lambda n: pl.BlockSpec((1, n), lambda i: (0, 0))
    full2 = lambda a, b: pl.BlockSpec((a, b), lambda i: (0, 0))
    full3 = lambda a, b, c: pl.BlockSpec((a, b, c), lambda i: (0, 0, 0))
    in_specs = _halo_specs(tt, RWKV_COLS, 0, T) + [
        row(RWKV_COLS), row(RWKV_COLS), row(D_RWKV), row(D_RWKV), row(D_RWKV),
        full2(2, D_RWKV), full3(2, RWKV_SMALL, D_RWKV), full2(2, D_RWKV), full3(2, RWKV_SMALL, D_RWKV),
        full2(RWKV_SMALL, D_RWKV), full2(MXU_DIM, MXU_DIM), full2(D_RWKV, LANE)]
    wide = (pl.BlockSpec((tt, D_RWKV), lambda i: (i, 0)), jax.ShapeDtypeStruct((T, D_RWKV), F32))
    slim = (pl.BlockSpec((tt, LANE), lambda i: (i, 0)), jax.ShapeDtypeStruct((T, LANE), F32))
    outs = [wide] * 3 + ([wide] * 5 + [slim]) * 2 + [wide] * 2
    sel = (jnp.arange(D_RWKV)[:, None] // RWKV_HEAD == jnp.arange(LANE)[None, :]).astype(BF16)
    return pl.pallas_call(
        functools.partial(_rwkv_prep_kernel, tt=tt, seq_len=seq_len),
        grid=(T // tt,),
        in_specs=in_specs,
        out_specs=[o[0] for o in outs],
        out_shape=[o[1] for o in outs],
        compiler_params=_params(("parallel",), 56),
        name="rwkv_prep",
    )(z_r, z_r, z_r, p['mu_prev'], p['mu_next'], p['k_k'], p['k_a'], p['r_k'],
      p['w0'], p['w2'], p['a0'], p['a2'], p['g2'], p['ones_blk'], sel)


N_GRP = D_RWKV // MXU_DIM


SCAN_ROWS = 16
SCAN_UNROLL = 8


def _scan_head_mask():
    return (jnp.arange(H_RWKV)[:, None] == (jnp.arange(D_RWKV) // RWKV_HEAD)[None, :]).astype(F32)


def _scan_kernel(rf, wcf, wf, bf, kf, vf, vhf, vkf, bef, rb, wcb, wb, bb, kb, vb, vhb, vkb, beb,
                 c0_ref, v0_ref, s0_ref, eye_ref, hm_ref, y0_ref, y1_ref, sfin_ref, st, uv, *, tc):
    j = pl.program_id(1)
    eye = eye_ref[...]
    hmask = hm_ref[...]
    dirs = ((rf, wcf, wf, bf, kf, (vf, vhf), vkf, bef, y0_ref),
            (rb, wcb, wb, bb, kb, (vb, vhb), vkb, beb, y1_ref))
    nt = (((1,), (1,)), ((), ()))
    zrows = jnp.zeros((SCAN_ROWS, D_RWKV), BF16)
    zeye = jnp.zeros((SCAN_ROWS, MXU_DIM), BF16)
    seqs = s0_ref.shape[0]
    chains = [(b, d) for b in range(seqs) for d in range(2)]

    def head_rows(row):
        return (row * hmask).astype(BF16)

    def value_rows(row):
        x = row * hmask
        folded = x[:, 0:MXU_DIM]
        for g in range(1, N_GRP):
            folded = folded + x[:, g * MXU_DIM:(g + 1) * MXU_DIM]
        return folded.astype(BF16)

    def next_value_row(v_refs, b, d, t):
        v_, vh_ = v_refs
        if d == 0:
            inside = v_[b, pl.ds(jnp.minimum(t + 1, tc - 1), 1), :]
            return jnp.where(t == tc - 1, vh_[b, 0:1, :], inside)
        inside = v_[b, pl.ds(jnp.maximum(t - 1, 0), 1), :]
        return jnp.where(t == 0, vh_[b, SUBLANE - 1:SUBLANE, :], inside)

    def state_products(b, d, rows0, eye0, eye1, rows2):
        lhs = jnp.concatenate([st[b, d, g].astype(BF16) for g in range(N_GRP)] + [eye], axis=1)
        wr = jnp.concatenate([
            jnp.concatenate([rows0, eye0], axis=1),
            jnp.concatenate([zrows, eye1], axis=1),
            jnp.concatenate([rows2, zeye], axis=1),
            jnp.concatenate([zrows, zeye], axis=1)], axis=0)
        return lax.dot_general(lhs, wr, nt, preferred_element_type=F32)

    def store_y(y_, b, t, prod):
        tr = prod.T
        y_[b, pl.ds(t, 1), :, :] = tr[2 * SCAN_ROWS:3 * SCAN_ROWS, :].reshape(1, H_RWKV, RWKV_HEAD)

    def time_of(d, s):
        s = jnp.clip(s, 0, tc - 1)
        return s if d == 0 else tc - 1 - s

    @pl.when(j == 0)
    def _():
        st[...] = s0_ref[...]
        for (b, d) in chains:
            uv[b, d] = state_products(b, d, head_rows(c0_ref[b, d:d + 1, :]), zeye,
                                      value_rows(v0_ref[b, d:d + 1, :]), zrows)

    def issue(group, s):
        out = []
        for (b, d) in group:
            r_, wc_, w_, b_, k_, vn_, vk_, be_, y_ = dirs[d]
            t, tp = time_of(d, s), time_of(d, s - 1)
            prod = state_products(b, d, head_rows(wc_[b, pl.ds(t, 1), :]),
                                  value_rows(vk_[b, pl.ds(t, 1), :]),
                                  value_rows(next_value_row(vn_, b, d, t)),
                                  head_rows(r_[b, pl.ds(tp, 1), :]))
            w2 = jnp.concatenate([head_rows(b_[b, pl.ds(t, 1), :]), head_rows(k_[b, pl.ds(t, 1), :]),
                                  zrows, zrows], axis=0)
            out.append((prod, _dot(uv[b, d].astype(BF16), w2)))
        return out

    def finish(group, s, results):
        for (b, d), (prod, upd) in zip(group, results):
            r_, wc_, w_, b_, k_, vn_, vk_, be_, y_ = dirs[d]
            t, tp = time_of(d, s), time_of(d, s - 1)
            wrow = w_[b, pl.ds(t, 1), :]
            for g in range(N_GRP):
                sl = slice(g * MXU_DIM, (g + 1) * MXU_DIM)
                st[b, d, g] = st[b, d, g] * wrow[:, sl] + upd[:, sl]
            store_y(y_, b, tp, prod)
            uv[b, d] = prod + uv[b, d] * be_[b, pl.ds(t, 1), 0:4 * SCAN_ROWS]

    lead, lag = chains[:seqs], chains[seqs:]

    unroll = SCAN_UNROLL * 4 // len(chains) * 2

    def body(i, carry):
        for u in range(unroll):
            s = i * unroll + u
            lead_results = issue(lead, s)
            lag_results = issue(lag, s)
            finish(lead, s, lead_results)
            finish(lag, s, lag_results)
        return carry

    lax.fori_loop(0, tc // unroll, body, 0)

    for (b, d) in chains:
        t = time_of(d, tc - 1)
        prod = state_products(b, d, zrows, zeye, zeye, head_rows(dirs[d][0][b, pl.ds(t, 1), :]))
        store_y(dirs[d][8], b, t, prod)

    @pl.when(j == pl.num_programs(1) - 1)
    def _():
        sfin_ref[...] = st[...]


def _scan_call(pre, s0, eye):
    B, T, _ = pre['r'].shape
    nb = 4 if B % 4 == 0 else 2
    tc = min(128 if nb == 2 else 32, T)
    nj = T // tc
    fwd = pl.BlockSpec((nb, tc, D_RWKV), lambda bi, j: (bi, j, 0))
    bwd = pl.BlockSpec((nb, tc, D_RWKV), lambda bi, j: (bi, nj - 1 - j, 0))
    yfwd = pl.BlockSpec((nb, tc, H_RWKV, RWKV_HEAD), lambda bi, j: (bi, j, 0, 0))
    ybwd = pl.BlockSpec((nb, tc, H_RWKV, RWKV_HEAD), lambda bi, j: (bi, nj - 1 - j, 0, 0))
    st_spec = pl.BlockSpec((nb, 2, N_GRP, RWKV_HEAD, MXU_DIM), lambda bi, j: (bi, 0, 0, 0, 0))
    y_shape = jax.ShapeDtypeStruct((B, T, H_RWKV, RWKV_HEAD), F32)
    befwd = pl.BlockSpec((nb, tc, LANE), lambda bi, j: (bi, j, 0))
    bebwd = pl.BlockSpec((nb, tc, LANE), lambda bi, j: (bi, nj - 1 - j, 0))
    per = tc // SUBLANE
    last8 = T // SUBLANE - 1
    hfwd = pl.BlockSpec((nb, SUBLANE, D_RWKV), lambda bi, j: (bi, jnp.minimum((j + 1) * per, last8), 0))
    hbwd = pl.BlockSpec((nb, SUBLANE, D_RWKV), lambda bi, j: (bi, jnp.maximum((nj - 1 - j) * per - 1, 0), 0))
    first = pl.BlockSpec((nb, SUBLANE, D_RWKV), lambda bi, j: (bi, 0, 0))

    wc0, vk0, be0, wc1, vk1, be1 = (pre[n] for n in ('wc0', 'vk0', 'be0', 'wc1', 'vk1', 'be1'))
    v = pre['v']
    pad = jnp.zeros((B, SUBLANE - 2, D_RWKV), F32)
    c_first = jnp.concatenate([pre['c'][:, 0:1], pre['c'][:, T - 1:T], pad], axis=1)
    v_first = jnp.concatenate([v[:, 0:1], v[:, T - 1:T], pad], axis=1)
    return pl.pallas_call(
        functools.partial(_scan_kernel, tc=tc),
        grid=(B // nb, nj),
        in_specs=[fwd] * 6 + [hfwd, fwd, befwd] + [bwd] * 6 + [hbwd, bwd, bebwd] + [
            first, first,
            st_spec,
            pl.BlockSpec((RWKV_HEAD, MXU_DIM), lambda bi, j: (0, 0)),
            pl.BlockSpec((H_RWKV, D_RWKV), lambda bi, j: (0, 0))],
        out_specs=[yfwd, ybwd, st_spec],
        out_shape=[y_shape, y_shape, jax.ShapeDtypeStruct((B, 2, N_GRP, RWKV_HEAD, MXU_DIM), F32)],
        scratch_shapes=[pltpu.VMEM((nb, 2, N_GRP, RWKV_HEAD, MXU_DIM), F32),
                        pltpu.VMEM((nb, 2, RWKV_HEAD, 4 * SCAN_ROWS), F32)],
        compiler_params=_params(("arbitrary", "arbitrary"), 52),
        name="rwkv_scan",
    )(pre['r'], wc0, pre['w0'], pre['b0'], pre['k0'], v, v, vk0, be0,
      pre['r'], wc1, pre['w1'], pre['b1'], pre['k1'], v, v, vk1, be1,
      c_first, v_first, s0, eye.astype(BF16), _scan_head_mask())


def _rwkv_post_kernel(y0_ref, y1_ref, bonus_ref, g_ref, gw_ref, gb_ref, ones_ref, o_ref):
    ones_blk = ones_ref[...]
    y = y0_ref[...] + y1_ref[...]
    mu = _segsum(y, ones_blk) * (1.0 / RWKV_HEAD)
    yc = y - mu
    var = _segsum(yc * yc, ones_blk) * (1.0 / RWKV_HEAD)
    yn = yc * lax.rsqrt(var + RWKV_GN_EPS)
    out = (yn * gw_ref[...] + gb_ref[...] + bonus_ref[...]) * g_ref[...]
    o_ref[...] = out.astype(BF16)


def _rwkv_post_call(y0, y1, bonus, g, gn_w, gn_b, ones_blk):
    T = y0.shape[0]
    tt = 512
    blk = pl.BlockSpec((tt, D_RWKV), lambda i: (i, 0))
    row = pl.BlockSpec((1, D_RWKV), lambda i: (0, 0))
    return pl.pallas_call(
        _rwkv_post_kernel,
        grid=(T // tt,),
        in_specs=[blk, blk, blk, blk, row, row, pl.BlockSpec((MXU_DIM, MXU_DIM), lambda i: (0, 0))],
        out_specs=blk,
        out_shape=jax.ShapeDtypeStruct((T, D_RWKV), BF16),
        compiler_params=_params(("parallel",), 40),
        name="rwkv_post",
    )(y0, y1, bonus, g, gn_w, gn_b, ones_blk)


def _rope128(x, cos_t, sin_t):
    return x * cos_t + pltpu.roll(x, QK_ROPE, 1) * sin_t


def _pack_kv(kv, kr_rot, k_o, v_o):
    for h in range(H_MLA):
        k_o[:, h * HEAD_SLOT:h * HEAD_SLOT + QK_NOPE] = kv[:, h * HEAD_SLOT:h * HEAD_SLOT + QK_NOPE].astype(BF16)
        k_o[:, h * HEAD_SLOT + QK_NOPE:(h + 1) * HEAD_SLOT] = kr_rot.astype(BF16)
        v_o[:, h * V_HEAD:(h + 1) * V_HEAD] = kv[:, h * HEAD_SLOT + QK_NOPE:(h + 1) * HEAD_SLOT].astype(BF16)


def _mla_prep_kernel(z_ref, cos_ref, sin_ref, qn_ref, kvn_ref, wq_ref, wkv_ref,
                     q_o, k_o, v_o, ckv_o, kr_o):
    z = z_ref[...]
    cq = z[:, 0:Q_RANK]
    ckv = z[:, Q_RANK:Q_RANK + KV_RANK]
    krp = z[:, Q_RANK + KV_RANK:MLA_COLS]
    cos_t = cos_ref[...]
    sin_t = sin_ref[...]
    cq = (cq * lax.rsqrt(jnp.mean(cq * cq, axis=-1, keepdims=True) + 1e-6)) * qn_ref[...]
    ckv = (ckv * lax.rsqrt(jnp.mean(ckv * ckv, axis=-1, keepdims=True) + 1e-6)) * kvn_ref[...]
    ckv_o[...] = ckv
    kr_o[...] = krp
    q = _dot(cq.astype(BF16), wq_ref[...]) * (ATTN_SCALE * LOG2_E)
    for h in range(H_MLA):
        q_o[:, h * HEAD_SLOT:h * HEAD_SLOT + QK_NOPE] = q[:, h * HEAD_SLOT:h * HEAD_SLOT + QK_NOPE].astype(BF16)
        q_o[:, h * HEAD_SLOT + QK_NOPE:(h + 1) * HEAD_SLOT] = _rope128(
            q[:, h * HEAD_SLOT + QK_NOPE:(h + 1) * HEAD_SLOT], cos_t, sin_t).astype(BF16)
    kv = _dot(ckv.astype(BF16), wkv_ref[...])
    _pack_kv(kv, _rope128(krp, cos_t, sin_t), k_o, v_o)


def _mla_prep_call(z_m, cos_t, sin_t, seq_len, p):
    T = z_m.shape[0]
    tm = min(512, seq_len)
    per_seq = seq_len // tm
    row = lambda n: pl.BlockSpec((1, n), lambda i: (0, 0))
    blk = lambda n: pl.BlockSpec((tm, n), lambda i: (i, 0))
    tab = pl.BlockSpec((tm, LANE), lambda i: (i % per_seq, 0))
    return pl.pallas_call(
        _mla_prep_kernel,
        grid=(T // tm,),
        in_specs=[blk(MLA_COLS), tab, tab, row(Q_RANK), row(KV_RANK),
                  pl.BlockSpec((Q_RANK, H_MLA * HEAD_SLOT), lambda i: (0, 0)),
                  pl.BlockSpec((KV_RANK, H_MLA * HEAD_SLOT), lambda i: (0, 0))],
        out_specs=[blk(H_MLA * HEAD_SLOT), blk(H_MLA * HEAD_SLOT), blk(H_MLA * V_HEAD),
                   blk(KV_RANK), blk(LANE)],
        out_shape=[jax.ShapeDtypeStruct((T, H_MLA * HEAD_SLOT), BF16),
                   jax.ShapeDtypeStruct((T, H_MLA * HEAD_SLOT), BF16),
                   jax.ShapeDtypeStruct((T, H_MLA * V_HEAD), BF16),
                   jax.ShapeDtypeStruct((T, KV_RANK), F32),
                   jax.ShapeDtypeStruct((T, LANE), F32)],
        compiler_params=_params(("parallel",), 48),
        name="mla_prep",
    )(z_m, cos_t, sin_t, p['q_norm'], p['kv_norm'], p['w_qb'], p['w_kvb'])


def _ctx_kv_kernel(ckv_ref, kr_ref, wkv_ref, k_o, v_o):
    kv = _dot(ckv_ref[...].astype(BF16), wkv_ref[...])
    _pack_kv(kv, kr_ref[...], k_o, v_o)


def _ctx_kv_call(ckv_ctx, kr_ctx_pad, w_kvb):
    T = ckv_ctx.shape[0]
    tm = min(512, T)
    blk = lambda n: pl.BlockSpec((tm, n), lambda i: (i, 0))
    return pl.pallas_call(
        _ctx_kv_kernel,
        grid=(T // tm,),
        in_specs=[blk(KV_RANK), blk(LANE), pl.BlockSpec((KV_RANK, H_MLA * HEAD_SLOT), lambda i: (0, 0))],
        out_specs=[blk(H_MLA * HEAD_SLOT), blk(H_MLA * V_HEAD)],
        out_shape=[jax.ShapeDtypeStruct((T, H_MLA * HEAD_SLOT), BF16),
                   jax.ShapeDtypeStruct((T, H_MLA * V_HEAD), BF16)],
        compiler_params=_params(("parallel",), 32),
        name="mla_ctx_kv",
    )(ckv_ctx, kr_ctx_pad, w_kvb)


ATTN_ROWS = 64


def _attn_kernel(q_ref, k_ref, v_ref, o_ref, s_sc, p_sc, *, kc, heads):
    tq, tk = q_ref.shape[0], s_sc.shape[1]
    hq = tq // 2
    nt = (((1,), (1,)), ((), ()))
    units = [(hd, half) for hd in range(heads) for half in range(2)]

    def rows_of(hd, half):
        return hd * tq + half * hq

    def scores(hd, half):
        q = q_ref[half * hq:(half + 1) * hq, hd * HEAD_SLOT:(hd + 1) * HEAD_SLOT]
        r0 = rows_of(hd, half)
        for c in range(tk // kc):
            s_sc[r0:r0 + hq, c * kc:(c + 1) * kc] = lax.dot_general(
                q, k_ref[c * kc:(c + 1) * kc, hd * HEAD_SLOT:(hd + 1) * HEAD_SLOT], nt,
                preferred_element_type=F32)

    def softmax(hd, half):
        sums = []
        for r in range(hq // ATTN_ROWS):
            r0 = rows_of(hd, half) + r * ATTN_ROWS
            rows = slice(r0, r0 + ATTN_ROWS)
            mpart = s_sc[rows, 0:LANE]
            for t in range(1, tk // LANE):
                mpart = jnp.maximum(mpart, s_sc[rows, t * LANE:(t + 1) * LANE])
            m = jnp.max(mpart, axis=-1, keepdims=True)
            lpart = jnp.zeros((ATTN_ROWS, LANE), F32)
            for t in range(tk // LANE):
                p = jnp.exp2(s_sc[rows, t * LANE:(t + 1) * LANE] - m)
                lpart = lpart + p
                p_sc[rows, t * LANE:(t + 1) * LANE] = p.astype(BF16)
            sums.append(jnp.sum(lpart, axis=-1, keepdims=True))
        return jnp.concatenate(sums, axis=0)

    def weighted_values(hd, half, l):
        r0 = rows_of(hd, half)
        acc = jnp.zeros((hq, V_HEAD), F32)
        for c in range(tk // kc):
            acc = acc + _dot(p_sc[r0:r0 + hq, c * kc:(c + 1) * kc],
                             v_ref[c * kc:(c + 1) * kc, hd * V_HEAD:(hd + 1) * V_HEAD])
        o_ref[half * hq:(half + 1) * hq, hd * V_HEAD:(hd + 1) * V_HEAD] = (acc / l).astype(BF16)

    for u in units:
        scores(*u)
    for u in units:
        weighted_values(*u, softmax(*u))


def _attn_call(q, k, v):
    B, Tq, _ = q.shape
    Tk = k.shape[1]
    tq = min(512, Tq)
    heads = H_MLA if Tk <= 512 else 1
    return pl.pallas_call(
        functools.partial(_attn_kernel, kc=MXU_DIM, heads=heads),
        grid=(B, H_MLA // heads, Tq // tq),
        in_specs=[pl.BlockSpec((None, tq, heads * HEAD_SLOT), lambda b, h, i: (b, i, h)),
                  pl.BlockSpec((None, Tk, heads * HEAD_SLOT), lambda b, h, i: (b, 0, h)),
                  pl.BlockSpec((None, Tk, heads * V_HEAD), lambda b, h, i: (b, 0, h))],
        out_specs=pl.BlockSpec((None, tq, heads * V_HEAD), lambda b, h, i: (b, i, h)),
        out_shape=jax.ShapeDtypeStruct((B, Tq, H_MLA * V_HEAD), BF16),
        scratch_shapes=[pltpu.VMEM((heads * tq, Tk), F32), pltpu.VMEM((heads * tq, Tk), BF16)],
        compiler_params=_params(("parallel", "parallel", "arbitrary"), 48),
        name="mla_attention",
    )(q, k, v)


def _conv3_kernel(*refs, tt, half_len):
    ins, (cw_refs, cb_refs), outs = refs[0:12], (refs[12:15], refs[15:18]), refs[18:]
    row0 = pl.program_id(0) * tt
    for s in range(3):
        even, odd = ins[4 * s][...], ins[4 * s + 1][...]
        odd_prev = _shift_prev(odd, ins[4 * s + 2][...], row0, half_len)
        even_next = _shift_next(even, ins[4 * s + 3][...], row0, half_len)
        cw = cw_refs[s][...]
        bias = cb_refs[s][...]
        y_even = cw[0:1, :] * odd_prev + cw[1:2, :] * even + cw[2:3, :] * odd + bias
        y_odd = cw[0:1, :] * even + cw[1:2, :] * odd + cw[2:3, :] * even_next + bias
        outs[s][0] = y_even
        outs[s][1] = y_odd
        if s == 2:
            outs[3][0] = y_even.astype(BF16)
            outs[3][1] = y_odd.astype(BF16)


def _conv3_call(z, half_len, conv_w, conv_b):
    T2 = z.shape[1]
    tt = min(128, half_len)
    per = tt // SUBLANE
    last = T2 // SUBLANE - 1
    in_specs = []
    for s in range(3):
        in_specs += [
            pl.BlockSpec((None, tt, D_HYENA), lambda i, s=s: (0, i, s)),
            pl.BlockSpec((None, tt, D_HYENA), lambda i, s=s: (1, i, s)),
            pl.BlockSpec((None, SUBLANE, D_HYENA), lambda i, s=s: (1, jnp.maximum(i * per - 1, 0), s)),
            pl.BlockSpec((None, SUBLANE, D_HYENA), lambda i, s=s: (0, jnp.minimum((i + 1) * per, last), s))]
    in_specs += [pl.BlockSpec((3, D_HYENA), lambda i, s=s: (0, s)) for s in range(3)]
    in_specs += [pl.BlockSpec((1, D_HYENA), lambda i, s=s: (0, s)) for s in range(3)]
    blk = pl.BlockSpec((2, tt, D_HYENA), lambda i: (0, i, 0))
    f32s = jax.ShapeDtypeStruct((2, T2, D_HYENA), F32)
    return pl.pallas_call(
        functools.partial(_conv3_kernel, tt=tt, half_len=half_len),
        grid=(T2 // tt,),
        in_specs=in_specs,
        out_specs=[blk] * 4,
        out_shape=[f32s, f32s, f32s, jax.ShapeDtypeStruct((2, T2, D_HYENA), BF16)],
        compiler_params=_params(("parallel",), 48),
        name="hyena_conv3",
    )(*([z] * 12), conv_w, conv_w, conv_w, conv_b, conv_b, conv_b)


def _filt_mlp_kernel(z_ref, w1_ref, b1_ref, w2_ref, b2_ref, fr_ref, o_ref):
    h = jnp.sin(fr_ref[0:1, :] * (_dot(z_ref[...].astype(BF16), w1_ref[...].astype(BF16)) + b1_ref[...]))
    h = jnp.sin(fr_ref[1:2, :] * (_dot(h.astype(BF16), w2_ref[...].astype(BF16)) + b2_ref[...]))
    o_ref[...] = h.astype(BF16)


def _filt_mlp_call(zpos, w1p, b1, w2, b2, freq):
    L = zpos.shape[0]
    return pl.pallas_call(
        _filt_mlp_kernel,
        out_shape=jax.ShapeDtypeStruct((L, FILT_HIDDEN), BF16),
        compiler_params=pltpu.CompilerParams(vmem_limit_bytes=32 * MIB),
        name="hyena_filter_mlp",
    )(zpos, w1p, b1, w2, b2, freq)


def _filt_gen_kernel(h_ref, tn_ref, dl_ref, w00, w01, w10, w11, o_ref, taps_sc):
    h = h_ref[...]
    L = h.shape[0]
    win = jnp.exp(-tn_ref[...] * dl_ref[...])
    not_first = lax.broadcasted_iota(jnp.int32, (L, 1), 0) > 0
    ws = ((w00, w01), (w10, w11))

    def emit(k, taps):
        taps_sc[...] = taps
        for par in range(2):
            o_ref[k, par] = taps_sc[pl.ds(par, L // 2, stride=2), :].astype(BF16)

    for n in range(2):
        causal = _dot(h, ws[n][0][...].astype(BF16)) * win
        anti = jnp.where(not_first, _dot(h, ws[n][1][...].astype(BF16)) * win, 0.0)
        norm = (jnp.sum(jnp.abs(causal), axis=0, keepdims=True)
                + jnp.sum(jnp.abs(anti), axis=0, keepdims=True))
        emit(2 * n, causal / norm)
        emit(2 * n + 1, anti / norm)


def _filt_gen_call(h2, tnorm, deltas, w3):
    L = h2.shape[0]
    tc = 128
    nc = D_HYENA // tc
    wspec = lambda k: pl.BlockSpec((FILT_HIDDEN, tc), lambda j, k=k: (0, k * nc + j))
    return pl.pallas_call(
        _filt_gen_kernel,
        grid=(nc,),
        in_specs=[pl.BlockSpec((L, FILT_HIDDEN), lambda j: (0, 0)),
                  pl.BlockSpec((L, 1), lambda j: (0, 0)),
                  pl.BlockSpec((1, tc), lambda j: (0, j)),
                  wspec(0), wspec(1), wspec(2), wspec(3)],
        out_specs=pl.BlockSpec((4, 2, L // 2, tc), lambda j: (0, 0, 0, j)),
        out_shape=jax.ShapeDtypeStruct((4, 2, L // 2, D_HYENA), BF16),
        scratch_shapes=[pltpu.VMEM((L, tc), F32)],
        compiler_params=_params(("parallel",), 48),
        name="hyena_filter_gen",
    )(h2, tnorm, deltas, w3, w3, w3, w3)


def _dft_fwd_kernel(f_ref, u_ref, o_ref):
    o_ref[...] = _dot(f_ref[...], u_ref[...])


def _dft_fwd_call(fmat, u):
    B, K, C = u.shape
    M = fmat.shape[0]
    tm = min(512, M)
    tn = min(C, 2048)
    return pl.pallas_call(
        _dft_fwd_kernel,
        grid=(B, C // tn, M // tm),
        in_specs=[pl.BlockSpec((tm, K), lambda b, j, i: (i, 0)),
                  pl.BlockSpec((None, K, tn), lambda b, j, i: (b, 0, j))],
        out_specs=pl.BlockSpec((None, tm, tn), lambda b, j, i: (b, i, j)),
        out_shape=jax.ShapeDtypeStruct((B, M, C), F32),
        compiler_params=_params(("parallel", "parallel", "arbitrary"), 40),
        name="hyena_dft_fwd",
    )(fmat, u)


EDGE_ROWS = 16


def _butterfly(gc, gs, hc, hs, tw_c, tw_s, first):
    tc = hc * tw_c - hs * tw_s
    ts = hs * tw_c + hc * tw_s
    p0 = gc + tc
    p2 = gc - tc
    p1 = gs + ts
    p3 = ts - gs
    if first is not None:
        p1 = jnp.where(first, gs, p1)
        p3 = jnp.where(first, hs, p3)
    return p0, p1, p2, p3


def _cmul(ac, a_s, bc, bs):
    return ac * bc - a_s * bs, ac * bs + a_s * bc


def _filt_planes_kernel(ge_ref, ho_ref, twc_ref, tws_ref, o_ref, *, tr):
    def planes(rows, first):
        tw_c, tw_s = twc_ref[rows, :], tws_ref[rows, :]
        a = _butterfly(ge_ref[0, 0, rows, :], ge_ref[0, 1, rows, :], ho_ref[0, 0, rows, :], ho_ref[0, 1, rows, :],
                       tw_c, tw_s, first)
        b = _butterfly(ge_ref[1, 0, rows, :], ge_ref[1, 1, rows, :], ho_ref[1, 0, rows, :], ho_ref[1, 1, rows, :],
                       tw_c, tw_s, first)
        p1 = a[1] - b[1] if first is None else jnp.where(first, a[1] + b[1], a[1] - b[1])
        o_ref[0, rows, :] = a[0] + b[0]
        o_ref[1, rows, :] = p1
        o_ref[2, rows, :] = a[2] + b[2]
        o_ref[3, rows, :] = a[3] - b[3]

    planes(slice(0, tr), None)

    @pl.when(pl.program_id(1) == 0)
    def _():
        planes(slice(0, EDGE_ROWS), lax.broadcasted_iota(jnp.int32, (EDGE_ROWS, 1), 0) == 0)


def _filt_planes_call(raw, twc, tws):
    _, L, C = raw.shape
    H = L // 2
    tr = min(256, H)
    tc = 512 if H >= 256 else C
    blk = lambda par: pl.BlockSpec((None, 2, None, 2, tr, tc), lambda n, i, j: (n, 0, par, 0, i, j))
    tw = pl.BlockSpec((tr, 1), lambda n, i, j: (i, 0))
    raw6 = raw.reshape(2, 2, 2, 2, H, C)
    return pl.pallas_call(
        functools.partial(_filt_planes_kernel, tr=tr),
        grid=(2, H // tr, C // tc),
        in_specs=[blk(0), blk(1), tw, tw],
        out_specs=pl.BlockSpec((None, 4, tr, tc), lambda n, i, j: (n, 0, i, j)),
        out_shape=jax.ShapeDtypeStruct((2, 4, H, C), F32),
        compiler_params=_params(("parallel", "parallel", "parallel"), 40),
        name="hyena_filter_planes",
    )(raw6, raw6, twc, tws)


def _spec_mul_kernel(raw_ref, k_ref, twc_ref, tws_ref, o_ref, *, tr):
    def multiply(rows, first):
        tw_c, tw_s = twc_ref[rows, :], tws_ref[rows, :]
        p0, p1, p2, p3 = _butterfly(raw_ref[0, 0, rows, :], raw_ref[0, 1, rows, :],
                                    raw_ref[1, 0, rows, :], raw_ref[1, 1, rows, :], tw_c, tw_s, first)
        k0, k1, k2, k3 = (k_ref[n, rows, :] for n in range(4))
        yac, yas = _cmul(p0, p1, k0, k1)
        ybc, ybs = _cmul(p2, p3, k2, k3)
        if first is not None:
            ymc, yms = _cmul(p1, p3, k1, k3)
            yac = jnp.where(first, p0 * k0, yac)
            ybc = jnp.where(first, p2 * k2, ybc)
        dc = yac - ybc
        ds = yas + ybs
        e_s, o_c, o_s = yas - ybs, dc * tw_c + ds * tw_s, ds * tw_c - dc * tw_s
        if first is not None:
            e_s, o_c, o_s = jnp.where(first, ymc, e_s), jnp.where(first, dc, o_c), jnp.where(first, yms, o_s)
        o_ref[0, 0, rows, :] = (yac + ybc).astype(BF16)
        o_ref[0, 1, rows, :] = e_s.astype(BF16)
        o_ref[1, 0, rows, :] = o_c.astype(BF16)
        o_ref[1, 1, rows, :] = o_s.astype(BF16)

    multiply(slice(0, tr), None)

    @pl.when(pl.program_id(1) == 0)
    def _():
        multiply(slice(0, EDGE_ROWS), lax.broadcasted_iota(jnp.int32, (EDGE_ROWS, 1), 0) == 0)


def _spec_mul_call(raw, k_planes, order, twc, tws):
    B2, L, C = raw.shape
    B, H = B2 // 2, L // 2
    tr = min(256, H)
    tc = 512 if H >= 256 else C
    tw = pl.BlockSpec((tr, 1), lambda b, i, j: (i, 0))
    out = pl.pallas_call(
        functools.partial(_spec_mul_kernel, tr=tr),
        grid=(B, H // tr, C // tc),
        in_specs=[pl.BlockSpec((2, None, 2, tr, tc), lambda b, i, j: (0, b, 0, i, j)),
                  pl.BlockSpec((None, 4, tr, tc), lambda b, i, j: (order, 0, i, j)),
                  tw, tw],
        out_specs=pl.BlockSpec((2, None, 2, tr, tc), lambda b, i, j: (0, b, 0, i, j)),
        out_shape=jax.ShapeDtypeStruct((2, B, 2, H, C), BF16),
        compiler_params=_params(("parallel", "parallel", "parallel"), 40),
        name="hyena_spectral_mul",
    )(raw.reshape(2, B, 2, H, C), k_planes, twc, tws)
    return out.reshape(B2, L, C)


def _dft_inv_kernel(f_ref, y_ref, gate_ref, u_ref, bias_ref, o_ref, ob_ref):
    conv = _dot(f_ref[...], y_ref[...])
    out = gate_ref[...] * (conv + u_ref[...] * bias_ref[...])
    o_ref[...] = out
    ob_ref[...] = out.astype(BF16)


def _dft_inv_call(imat, y_spec, gate, u, bias):
    B, M, C = u.shape
    K = imat.shape[1]
    tm = min(512, M)
    tn = 1024 if K > 1024 else min(C, 2048)
    blk = pl.BlockSpec((None, tm, tn), lambda b, j, i: (b, i, j))
    return pl.pallas_call(
        _dft_inv_kernel,
        grid=(B, C // tn, M // tm),
        in_specs=[pl.BlockSpec((tm, K), lambda b, j, i: (i, 0)),
                  pl.BlockSpec((None, K, tn), lambda b, j, i: (b, 0, j)),
                  blk, blk, pl.BlockSpec((1, tn), lambda b, j, i: (0, j))],
        out_specs=[blk, blk],
        out_shape=[jax.ShapeDtypeStruct((B, M, C), F32), jax.ShapeDtypeStruct((B, M, C), BF16)],
        compiler_params=_params(("parallel", "parallel", "arbitrary"), 48),
        name="hyena_dft_inv",
    )(imat, y_spec, gate, u, bias)


def _dft_tables(L):
    H = L // 2
    lo = min(64, H)
    hi = H // lo
    g = jnp.arange(H, dtype=jnp.int32)
    theta = 2.0 * math.pi / L
    ang_hi = ((g[:, None] * (jnp.arange(hi, dtype=jnp.int32) * lo)[None, :]) % L).astype(F32) * theta
    ang_lo = ((g[:, None] * jnp.arange(lo, dtype=jnp.int32)[None, :]) % L).astype(F32) * theta
    ch, sh, cl, sl = jnp.cos(ang_hi), jnp.sin(ang_hi), jnp.cos(ang_lo), jnp.sin(ang_lo)
    cos_m = (ch[:, :, None] * cl[:, None, :] - sh[:, :, None] * sl[:, None, :]).reshape(H, H)
    sin_m = (sh[:, :, None] * cl[:, None, :] + ch[:, :, None] * sl[:, None, :]).reshape(H, H)
    alt = jnp.where(jnp.arange(H) % 2 == 0, 1.0, -1.0).astype(F32)
    sin_m = jnp.where((g == 0)[:, None], alt[None, :], sin_m)
    fwd = jnp.concatenate([cos_m, sin_m], axis=0)
    n = 2.0 * L
    w_cos = jnp.where(g == 0, 1.0 / n, 2.0 / n).astype(F32)
    inv = jnp.concatenate([cos_m * w_cos[:, None], sin_m * (2.0 / n)], axis=0).T
    ang_tw = g.astype(F32) * (math.pi / L)
    return fwd.astype(BF16), inv.astype(BF16), jnp.cos(ang_tw)[:, None], jnp.sin(ang_tw)[:, None]


def _filter_positions(L):
    t = jnp.arange(L, dtype=F32)
    t_norm = t / max(L - 1, 1)
    bands = (POS_EMB - 1) // 2
    freqs = jnp.linspace(1e-4, bands - 1, bands, dtype=F32)
    ang = (2.0 * math.pi / L) * t[:, None] * freqs[None, :]
    z = jnp.concatenate([t_norm[:, None], jnp.cos(ang), -jnp.sin(ang)], axis=-1)
    return jnp.pad(z, ((0, 0), (0, POS_PAD - POS_EMB))), t_norm[:, None]


def _hyena_deltas():
    return jnp.linspace(abs(math.log(HYENA_TARGET)) / SLOW_DECAY_PCT,
                        abs(math.log(HYENA_TARGET)) / FAST_DECAY_PCT, D_HYENA, dtype=F32)[None, :]


def _hyena_filter_spectrum(L, tables, p):
    fwd, _, twc, tws = tables
    zpos, tnorm = _filter_positions(L)
    h2 = _filt_mlp_call(zpos, p['filt_w1'], p['filt_b1'], p['filt_w2'], p['filt_b2'], p['filt_freq'])
    filt = _filt_gen_call(h2, tnorm, _hyena_deltas(), p['filt_w3'])
    raw = _dft_fwd_call(fwd, filt.reshape(8, L // 2, D_HYENA))
    return _filt_planes_call(raw, twc, tws)


def _hyena_mixer(z, B, L, tables, k_planes, p):
    fwd, inv, twc, tws = tables
    H = L // 2
    x1, x2, v, vb = _conv3_call(z, H, p['conv_w'], p['conv_b'])
    shp = (2 * B, H, D_HYENA)
    u, ub = v.reshape(shp), vb.reshape(shp)
    for n, gate in enumerate((x1, x2)):
        raw = _dft_fwd_call(fwd, ub)
        yspec = _spec_mul_call(raw, k_planes, n, twc, tws)
        u, ub = _dft_inv_call(inv, yspec, gate.reshape(shp), u, p['bias'][n:n + 1])
    return ub.reshape(2, B * H, D_HYENA)


def _rope_swap(w):
    q = QK_ROPE // 4
    return jnp.concatenate([w[..., q:2 * q], w[..., 0:q], w[..., 3 * q:4 * q], w[..., 2 * q:3 * q]], axis=-1)


def _pad_cols(w, n):
    return jnp.pad(w, [(0, 0)] * (w.ndim - 1) + [(0, n - w.shape[-1])])


def _pack_even(e, w_in_even, mu_prev, mu_next, rwkv_w0, rwkv_w2, rwkv_a0, rwkv_a2, rwkv_g2,
               rwkv_kk, rwkv_ka, rwkv_rk, rwkv_gn_w, rwkv_gn_b, mla_q_norm, mla_kv_norm,
               mla_w_qb, mla_w_kvb, w_out_even):
    n_r = 3 * D_RWKV + W_LORA + A_LORA + G_LORA
    w_in = w_in_even[e]
    w_r = _pad_cols(w_in[:, :n_r], RWKV_COLS).astype(BF16)
    w_m = w_in[:, n_r:]
    kr_cols = w_m[:, Q_RANK + KV_RANK:]
    w_m = jnp.concatenate([w_m, _rope_swap(kr_cols)], axis=-1).astype(BF16)
    small_rows = lambda w, off: jnp.pad(w, [(0, 0)] * (w.ndim - 2)
                                        + [(off, RWKV_SMALL - off - w.shape[-2]), (0, 0)]).astype(BF16)
    wq = mla_w_qb[e].reshape(Q_RANK, H_MLA, QK_NOPE + QK_ROPE)
    wq = jnp.concatenate([wq, _rope_swap(wq[..., QK_NOPE:])], axis=-1).reshape(Q_RANK, H_MLA * HEAD_SLOT)
    blk = jnp.arange(MXU_DIM) // RWKV_HEAD
    return {
        'w_r': w_r, 'w_m': w_m,
        'mu_prev': _pad_cols(mu_prev[e][None, :], RWKV_COLS),
        'mu_next': _pad_cols(mu_next[e][None, :], RWKV_COLS),
        'k_k': rwkv_kk[e][None, :], 'k_a': rwkv_ka[e][None, :],
        'r_k': rwkv_rk[e].reshape(1, D_RWKV),
        'w0': rwkv_w0[e], 'w2': small_rows(rwkv_w2[e], 0),
        'a0': rwkv_a0[e], 'a2': small_rows(rwkv_a2[e], W_LORA),
        'g2': small_rows(rwkv_g2[e], W_LORA + A_LORA),
        'gn_w': rwkv_gn_w[e][None, :], 'gn_b': rwkv_gn_b[e][None, :],
        'q_norm': mla_q_norm[e][None, :], 'kv_norm': mla_kv_norm[e][None, :],
        'w_qb': wq.astype(BF16), 'w_kvb': mla_w_kvb[e].astype(BF16),
        'w_out': w_out_even[e].astype(BF16),
        'ones_blk': (blk[:, None] == blk[None, :]).astype(BF16),
        'eye': (jnp.arange(RWKV_HEAD)[:, None] == (jnp.arange(MXU_DIM) % RWKV_HEAD)[None, :]).astype(F32),
    }


def _rope_tables(L):
    rows = L // GRID_W
    row = jnp.repeat(jnp.arange(rows, dtype=F32), GRID_W)
    col = jnp.tile(jnp.arange(GRID_W, dtype=F32), rows)
    half = QK_ROPE // 2
    inv = 1.0 / (ROPE_THETA ** (jnp.arange(0, half, 2, dtype=F32) / half))
    ar, ac = row[:, None] * inv[None, :], col[:, None] * inv[None, :]
    cos_t = jnp.concatenate([jnp.cos(ar), jnp.cos(ar), jnp.cos(ac), jnp.cos(ac)], axis=-1)
    sin_t = jnp.concatenate([-jnp.sin(ar), jnp.sin(ar), -jnp.sin(ac), jnp.sin(ac)], axis=-1)
    return _pad_cols(cos_t, LANE), _pad_cols(sin_t, LANE)


def _state_to_groups(s):
    B = s.shape[0]
    s = s.reshape(B, 2, N_GRP, H_RWKV // N_GRP, RWKV_HEAD, RWKV_HEAD)
    return jnp.swapaxes(s, 3, 4).reshape(B, 2, N_GRP, RWKV_HEAD, MXU_DIM)


def _groups_to_state(s):
    B = s.shape[0]
    s = s.reshape(B, 2, N_GRP, RWKV_HEAD, H_RWKV // N_GRP, RWKV_HEAD)
    return jnp.swapaxes(s, 3, 4).reshape(B, 2, H_RWKV, RWKV_HEAD, RWKV_HEAD)


def _even_mixer(x, mod_l, goff, B, L, gamma, p, rope, ctx):
    group_tokens = x.shape[0] if ctx is None else L
    z_r = _inproj_call(x, mod_l, goff, group_tokens, gamma, p['w_r'], RWKV_COLS // 3)
    z_m = _inproj_call(x, mod_l, goff, group_tokens, gamma, p['w_m'], MLA_COLS)
    names = ('r', 'v', 'c', 'w0', 'b0', 'k0', 'wc0', 'vk0', 'be0', 'w1', 'b1', 'k1', 'wc1', 'vk1', 'be1',
             'bonus', 'g')
    pre = dict(zip(names, _rwkv_prep_call(z_r, L, p)))
    seq = {n: pre[n].reshape(B, L, pre[n].shape[-1]) for n in names[:15]}
    if ctx is None:
        s0 = jnp.zeros((B, 2, N_GRP, RWKV_HEAD, MXU_DIM), F32)
    else:
        s0 = _state_to_groups(ctx[2].astype(F32))
    y0, y1, s_fin = _scan_call(seq, s0, p['eye'])
    y_r = _rwkv_post_call(y0.reshape(B * L, D_RWKV), y1.reshape(B * L, D_RWKV), pre['bonus'], pre['g'],
                          p['gn_w'], p['gn_b'], p['ones_blk'])

    q, k, v, ckv, krp = _mla_prep_call(z_m, rope[0], rope[1], L, p)
    q = q.reshape(B, L, H_MLA * HEAD_SLOT)
    k = k.reshape(B, L, H_MLA * HEAD_SLOT)
    v = v.reshape(B, L, H_MLA * V_HEAD)
    if ctx is not None:
        P = ctx[0].shape[1]
        k_ctx, v_ctx = _ctx_kv_call(ctx[0].reshape(B * P, KV_RANK),
                                    _pad_cols(ctx[1].reshape(B * P, QK_ROPE), LANE), p['w_kvb'])
        k = jnp.concatenate([k, k_ctx.reshape(B, P, H_MLA * HEAD_SLOT)], axis=1)
        v = jnp.concatenate([v, v_ctx.reshape(B, P, H_MLA * V_HEAD)], axis=1)
    y_m = _attn_call(q, k, v).reshape(B * L, H_MLA * V_HEAD)
    x = _outproj_call(x, mod_l, goff, group_tokens, y_r, y_m, p['w_out'])
    state = (_groups_to_state(s_fin), ckv.reshape(B, L, KV_RANK), krp[:, :QK_ROPE].reshape(B, L, QK_ROPE))
    return x, state


def _odd_mixer(x, mod_l, goff, group_tokens, B, L, gamma, tables, k_planes, p):
    z = _inproj_parity_call(x, mod_l, goff, group_tokens, gamma, p['w_in'], 1536)
    y = _hyena_mixer(z, B, L, tables, k_planes, p)
    return _outproj_parity_call(x, mod_l, goff, group_tokens, y, p['w_out'])


def kernel(x_prompt, x_sample, cache_mla_ckv, cache_mla_krope, state_rwkv, c, c_ctx,
           w_mod, b_mod, norm_g, w_ffn_in, w_ffn_out, final_norm_g,
           w_in_even, mu_prev, mu_next, rwkv_w0, rwkv_w2, rwkv_a0, rwkv_a2, rwkv_g2,
           rwkv_kk, rwkv_ka, rwkv_rk, rwkv_gn_w, rwkv_gn_b,
           mla_q_norm, mla_kv_norm, mla_w_qb, mla_w_kvb, w_out_even,
           w_in_odd, hy_conv_w, hy_conv_b, hy_filt_w1, hy_filt_b1, hy_filt_w2, hy_filt_b2,
           hy_filt_w3, hy_filt_freq, hy_bias, w_out_odd):
    Bp, Lp, D = x_prompt.shape
    Bs, Ls, _ = x_sample.shape
    depth = w_mod.shape[0]
    xp = x_prompt.reshape(Bp * Lp, D)
    xs = x_sample.reshape(Bs * Ls, D)
    Tp = Bp * Lp

    cvec = jnp.concatenate([c_ctx[None, :], c, jnp.zeros((SUBLANE - 1 - Bs, D), F32)], axis=0)
    mod = _mod_call(cvec, w_mod, b_mod)

    rope_p = (_pad_cols(jnp.ones((Lp, QK_ROPE), F32), LANE), jnp.zeros((Lp, LANE), F32))
    rope_s = _rope_tables(Ls)
    tabs_p = tabs_s = None
    w_in = w_ffn_in.astype(BF16)
    w_out = w_ffn_out.astype(BF16)
    new_ckv, new_kr, new_s = [], [], []
    for l in range(depth):
        mod_l = mod[l]
        gam = [norm_g[l, s][None, :] for s in range(3)]
        xp = _ffn_call(xp, mod_l, 0, Tp, gam[0], w_in, w_out, l, 0, 0)
        xs = _ffn_call(xs, mod_l, 1, Ls, gam[0], w_in, w_out, l, 0, 0)
        if l % 2 == 0:
            e = l // 2
            p = _pack_even(e, w_in_even, mu_prev, mu_next, rwkv_w0, rwkv_w2, rwkv_a0, rwkv_a2, rwkv_g2,
                           rwkv_kk, rwkv_ka, rwkv_rk, rwkv_gn_w, rwkv_gn_b, mla_q_norm, mla_kv_norm,
                           mla_w_qb, mla_w_kvb, w_out_even)
            ctx = (cache_mla_ckv[:, e], cache_mla_krope[:, e], state_rwkv[:, e])
            xp, st = _even_mixer(xp, mod_l, 0, Bp, Lp, gam[1], p, rope_p, None)
            xs, _ = _even_mixer(xs, mod_l, 1, Bs, Ls, gam[1], p, rope_s, ctx)
            new_s.append(st[0].astype(x_prompt.dtype))
            new_ckv.append(st[1])
            new_kr.append(st[2])
        else:
            o = l // 2
            p = {'w_in': w_in_odd[o].astype(BF16), 'conv_w': hy_conv_w[o], 'conv_b': hy_conv_b[o][None, :],
                 'filt_w1': jnp.pad(hy_filt_w1[o], ((0, POS_PAD - POS_EMB), (0, 0))),
                 'filt_b1': hy_filt_b1[o][None, :], 'filt_w2': hy_filt_w2[o],
                 'filt_b2': hy_filt_b2[o][None, :], 'filt_w3': hy_filt_w3[o],
                 'filt_freq': hy_filt_freq[o], 'bias': hy_bias[o], 'w_out': w_out_odd[o].astype(BF16)}
            if tabs_p is None:
                tabs_p, tabs_s = _dft_tables(Lp), _dft_tables(Ls)
            ks_p = _hyena_filter_spectrum(Lp, tabs_p, p)
            ks_s = _hyena_filter_spectrum(Ls, tabs_s, p)
            xp = _odd_mixer(xp, mod_l, 0, Tp, Bp, Lp, gam[1], tabs_p, ks_p, p)
            xs = _odd_mixer(xs, mod_l, 1, Ls, Bs, Ls, gam[1], tabs_s, ks_s, p)
        xp = _ffn_call(xp, mod_l, 0, Tp, gam[2], w_in, w_out, l, 1, 2)
        xs = _ffn_call(xs, mod_l, 1, Ls, gam[2], w_in, w_out, l, 1, 2)

    fg = final_norm_g[None, :]
    y_prompt = _final_norm_call(xp, fg).reshape(Bp, Lp, D)
    y_sample = _final_norm_call(xs, fg).reshape(Bs, Ls, D)
    return (y_prompt, y_sample, jnp.stack(new_ckv, axis=1), jnp.stack(new_kr, axis=1),
            jnp.stack(new_s, axis=1))
```
